```python
import jax, jax.numpy as jnp
from jax import lax
import numpy as np

D_MODEL = 1024
BATCH = 16
SEQ = 4096
DEPTH = 4

CTX_LEN = 256
GRID_W = 64
HEAD_DIM = 64
A_Q_HEADS = 8
A_KV_HEADS = 2
A_GROUP = A_Q_HEADS // A_KV_HEADS
WINDOW = 128
ATTN_BLOCK = 128
ROPE_BASE = 10000.0
GLA_HEADS = 4
GLA_DK = 64
GLA_DV = 128
GLA_RANK = 16
GLA_TAU = 16.0
GLA_CHUNK = 64
A_Q_W = A_Q_HEADS * HEAD_DIM
A_KV_W = A_KV_HEADS * HEAD_DIM
GLA_K_W = GLA_HEADS * GLA_DK
GLA_V_W = GLA_HEADS * GLA_DV
EVEN_SPLITS = (A_Q_W, A_KV_W, A_KV_W, GLA_K_W, GLA_K_W, GLA_V_W, GLA_V_W, 2 * GLA_RANK)
EVEN_IN = sum(EVEN_SPLITS)
MIX_W = A_Q_W + GLA_V_W
CONV_W = 3
N_GROUPS = 4
EXPERTS_PER_GROUP = 8
N_EXPERTS = N_GROUPS * EXPERTS_PER_GROUP
TOP_K = 2
D_EXPERT = 512
MOE_BLOCK = 128
N_EVEN = (DEPTH + 1) // 2
N_ODD = DEPTH // 2
DEEPNORM_ALPHA = (2.0 * DEPTH) ** 0.25
DEEPNORM_BETA = (8.0 * DEPTH) ** -0.25
LN_EPS = 1e-5
RMS_EPS = 1e-6

kernel_name = 'hybrid_dit_swa_gla_shortconv_hmoe'


def layer_norm(x, w, b):
    xf = x.astype(jnp.float32)
    mu = jnp.mean(xf, -1, keepdims=True)
    var = jnp.mean(jnp.square(xf - mu), -1, keepdims=True)
    return ((xf - mu) * lax.rsqrt(var + LN_EPS) * w.astype(jnp.float32) + b.astype(jnp.float32)).astype(x.dtype)


def post_norm(x, y, w, b):
    return layer_norm(DEEPNORM_ALPHA * x + y, w, b)


def ada_modulation(cond, w, b):
    m = jax.nn.silu(cond) @ w + b
    return m.reshape(m.shape[:-1] + (6, D_MODEL))


def modulate(x, shift, scale):
    return x * (1 + scale) + shift


def axial_rope_tables(n_tok):
    n_rows = n_tok // GRID_W
    row = jnp.repeat(jnp.arange(n_rows), GRID_W).astype(jnp.float32)
    col = jnp.tile(jnp.arange(GRID_W), n_rows).astype(jnp.float32)
    axis_dim = HEAD_DIM // 2
    inv_freq = ROPE_BASE ** (-jnp.arange(0, axis_dim, 2, dtype=jnp.float32) / axis_dim)
    ang = jnp.concatenate([row[:, None] * inv_freq, col[:, None] * inv_freq], -1)
    return jnp.cos(ang), jnp.sin(ang)


def apply_rope(x, cos, sin):
    nf = HEAD_DIM // 4
    xs = x.astype(jnp.float32).reshape(x.shape[:-1] + (2, 2, nf))
    x1, x2 = xs[..., 0, :], xs[..., 1, :]
    c = cos.reshape(cos.shape[0], 1, 2, nf)
    s = sin.reshape(sin.shape[0], 1, 2, nf)
    out = jnp.stack([x1 * c - x2 * s, x2 * c + x1 * s], axis=-2)
    return out.reshape(x.shape).astype(x.dtype)


def split_even(p):
    idx = [sum(EVEN_SPLITS[:n]) for n in range(1, len(EVEN_SPLITS))]
    return jnp.split(p, idx, axis=-1)


def sink_attend(qg, k, v, sink, mask):
    s = jnp.einsum('bqgrd,bkgd->bgrqk', qg, k, preferred_element_type=jnp.float32) * (HEAD_DIM ** -0.5)
    if mask is not None:
        s = jnp.where(mask, s, -jnp.inf)
    s_sink = jnp.broadcast_to(sink.astype(jnp.float32).reshape(1, A_KV_HEADS, A_GROUP, 1, 1), s.shape[:-1] + (1,))
    p = jax.nn.softmax(jnp.concatenate([s, s_sink], -1), axis=-1)[..., :-1]
    return jnp.einsum('bgrqk,bkgd->bqgrd', p.astype(v.dtype), v)


def window_attention(q, k, v, k_ctx, v_ctx, sink):
    B, S = q.shape[:2]
    L = k_ctx.shape[1]
    nb = S // ATTN_BLOCK
    qb = q.reshape(B, nb, ATTN_BLOCK, A_KV_HEADS, A_GROUP, HEAD_DIM).transpose(1, 0, 2, 3, 4, 5)
    pad = ((0, 0), (ATTN_BLOCK, ATTN_BLOCK), (0, 0), (0, 0))

    def band(t):
        tp = jnp.pad(t, pad).reshape(B, nb + 2, ATTN_BLOCK, A_KV_HEADS, HEAD_DIM)
        return jnp.concatenate([tp[:, :-2], tp[:, 1:-1], tp[:, 2:]], axis=2).transpose(1, 0, 2, 3, 4)

    kwin, vwin = band(k), band(v)
    blk = jnp.arange(nb)[:, None] * ATTN_BLOCK
    q_pos = blk + jnp.arange(ATTN_BLOCK)[None]
    k_pos = blk - ATTN_BLOCK + jnp.arange(3 * ATTN_BLOCK)[None]
    band_mask = ((jnp.abs(q_pos[:, :, None] - k_pos[:, None, :]) <= WINDOW)
                 & (k_pos[:, None, :] >= 0) & (k_pos[:, None, :] < S))
    ctx_mask = jnp.ones((ATTN_BLOCK, L), bool)

    def one_block(args):
        qi, ki, vi, mi = args
        keys = jnp.concatenate([ki, k_ctx], axis=1)
        vals = jnp.concatenate([vi, v_ctx], axis=1)
        return sink_attend(qi, keys, vals, sink, jnp.concatenate([mi, ctx_mask], -1))

    o = lax.map(one_block, (qb, kwin, vwin, band_mask))
    return o.transpose(1, 0, 2, 3, 4, 5).reshape(B, S, A_Q_W)


def gla_inputs(q, k, v, g_lr, wa2_d, ba_d, d):
    B, T = q.shape[:2]
    g = g_lr[..., d * GLA_RANK:(d + 1) * GLA_RANK] @ wa2_d + ba_d
    log_a = jax.nn.log_sigmoid(g.astype(jnp.float32)) / GLA_TAU
    shp = (B, T, GLA_HEADS, -1)
    return [q.reshape(shp).astype(jnp.float32) * (GLA_DK ** -0.5), k.reshape(shp).astype(jnp.float32),
            v.reshape(shp).astype(jnp.float32), log_a.reshape(shp)]


def gla_chunked(q, k, v, log_a, s0):
    B, T, H, _ = q.shape
    dv = v.shape[-1]
    n = T // GLA_CHUNK

    def chunks(t):
        return t.reshape(B, n, GLA_CHUNK, H, t.shape[-1]).transpose(1, 0, 2, 3, 4)

    qc, kc, vc, ac = chunks(q), chunks(k), chunks(v), chunks(log_a)
    b = jnp.cumsum(ac, axis=2)
    b_last = b[:, :, -1:]
    q_in = qc * jnp.exp(b)
    k_in = kc * jnp.exp(-b)
    k_st = kc * jnp.exp(b_last - b)
    causal = jnp.tril(jnp.ones((GLA_CHUNK, GLA_CHUNK), bool))
    attn = jnp.where(causal, jnp.einsum('nbihd,nbjhd->nbhij', q_in, k_in), 0.0)
    o_intra = jnp.einsum('nbhij,nbjhv->nbihv', attn, vc)

    def step(s, xs):
        qi, ksi, vi, dl = xs
        o_inter = jnp.einsum('bihd,bhdv->bihv', qi, s)
        s = dl[:, 0, :, :, None] * s + jnp.einsum('bjhd,bjhv->bhdv', ksi, vi)
        return s, o_inter

    s_fin, o_inter = lax.scan(step, s0, (q_in, k_st, vc, jnp.exp(b_last)))
    o = (o_intra + o_inter).transpose(1, 0, 2, 3, 4).reshape(B, T, H, dv)
    return o, s_fin


def gla_final_state(k, v, log_a):
    b = jnp.cumsum(log_a, axis=1)
    return jnp.einsum('bthd,bthv->bhdv', k * jnp.exp(b[:, -1:] - b), v)


def gla_bidirectional(lat, ctx, wa2, ba, need_ctx):
    B = lat[0].shape[0]
    s0 = jnp.zeros((B, GLA_HEADS, GLA_DK, GLA_DV), jnp.float32)
    o_lat, o_ctx = 0.0, 0.0
    for d in range(2):
        lat_in = gla_inputs(*lat, wa2[d], ba[d], d)
        ctx_in = gla_inputs(*ctx, wa2[d], ba[d], d)
        if d == 1:
            lat_in = [jnp.flip(t, 1) for t in lat_in]
            ctx_in = [jnp.flip(t, 1) for t in ctx_in]
        if need_ctx:
            oc, s_ctx = gla_chunked(*ctx_in, s0)
            o_ctx = o_ctx + (jnp.flip(oc, 1) if d == 1 else oc)
        else:
            s_ctx = gla_final_state(*ctx_in[1:])
        ol, _ = gla_chunked(*lat_in, s_ctx)
        o_lat = o_lat + (jnp.flip(ol, 1) if d == 1 else ol)
    return o_lat, o_ctx


def gla_output(o, r, norm_w):
    o = o * lax.rsqrt(jnp.mean(jnp.square(o), -1, keepdims=True) + RMS_EPS)
    o = o.reshape(o.shape[:2] + (GLA_V_W,)).astype(r.dtype) * norm_w
    return o * jax.nn.silu(r)


def attn_gla_mixer(u_lat, u_ctx, cos, sin, w_in, w_out, sink, wa2, ba, norm_w, need_ctx):
    B, S, _ = u_lat.shape
    L = u_ctx.shape[1]
    qa, ka, va, qg, kg, vg, rg, gg = split_even(u_lat @ w_in)
    qa_c, ka_c, va_c, qg_c, kg_c, vg_c, rg_c, gg_c = split_even(u_ctx @ w_in)

    def heads(t, h):
        return t.reshape(t.shape[:2] + (h, HEAD_DIM))

    k_ctx, v_ctx = heads(ka_c, A_KV_HEADS), heads(va_c, A_KV_HEADS)
    a_lat = window_attention(apply_rope(heads(qa, A_Q_HEADS), cos, sin), apply_rope(heads(ka, A_KV_HEADS), cos, sin),
                             heads(va, A_KV_HEADS), k_ctx, v_ctx, sink)
    o_lat, o_ctx = gla_bidirectional((qg, kg, vg, gg), (qg_c, kg_c, vg_c, gg_c), wa2, ba, need_ctx)
    y_lat = jnp.concatenate([a_lat, gla_output(o_lat, rg, norm_w)], -1) @ w_out
    if not need_ctx:
        return y_lat, None
    qc = qa_c.reshape(B, L, A_KV_HEADS, A_GROUP, HEAD_DIM)
    a_ctx = sink_attend(qc, k_ctx, v_ctx, sink, None).reshape(B, L, A_Q_W)
    y_ctx = jnp.concatenate([a_ctx, gla_output(o_ctx, rg_c, norm_w)], -1) @ w_out
    return y_lat, y_ctx


def short_conv_mixer(h, w_in, conv_w, conv_b, w_out):
    gb, gc, u = jnp.split(h @ w_in, 3, axis=-1)
    up = jnp.pad(gc * u, ((0, 0), (1, 1), (0, 0)))
    y = up[:, :-2] * conv_w[0] + up[:, 1:-1] * conv_w[1] + up[:, 2:] * conv_w[2] + conv_b
    return (gb * y) @ w_out


def routed_expert_ffn(h, expert, weights, w1, w3, w2):
    N, D = h.shape
    A = N * TOP_K
    flat_e = expert.reshape(A)
    flat_tok = jnp.repeat(jnp.arange(N), TOP_K)
    flat_w = weights.reshape(A)
    order = jnp.argsort(flat_e)
    e_sorted, tok_sorted, w_sorted = flat_e[order], flat_tok[order], flat_w[order]
    counts = jnp.bincount(flat_e, length=N_EXPERTS)
    start = jnp.cumsum(counts) - counts
    padded = (counts + MOE_BLOCK - 1) // MOE_BLOCK * MOE_BLOCK
    pad_end = jnp.cumsum(padded)
    pad_start = pad_end - padded
    dest = pad_start[e_sorted] + jnp.arange(A) - start[e_sorted]
    n_blocks = -(-A // MOE_BLOCK) + N_EXPERTS
    buf = jnp.zeros((n_blocks * MOE_BLOCK, D), h.dtype).at[dest].set(h[tok_sorted])
    blk_expert = jnp.minimum(jnp.searchsorted(pad_end, jnp.arange(n_blocks) * MOE_BLOCK, side='right'), N_EXPERTS - 1)

    def expert_block(args):
        xb, e = args
        return (jax.nn.silu(xb @ w1[e]) * (xb @ w3[e])) @ w2[e]

    y = lax.map(expert_block, (buf.reshape(n_blocks, MOE_BLOCK, D), blk_expert)).reshape(-1, D)
    return jnp.zeros((N, D), h.dtype).at[tok_sorted].add(y[dest] * w_sorted[:, None].astype(h.dtype))


def hier_moe(h, wg, bg, we, be, w1, w3, w2):
    N = h.shape[0]
    hf = h.astype(jnp.float32)
    g_logits = hf @ wg.astype(jnp.float32) + bg.astype(jnp.float32)
    _, g_sel = lax.top_k(g_logits, 1)
    p_group = jnp.take_along_axis(jax.nn.softmax(g_logits, -1), g_sel, -1)
    e_logits = (hf @ we.astype(jnp.float32) + be.astype(jnp.float32)).reshape(N, N_GROUPS, EXPERTS_PER_GROUP)
    e_logits = jnp.take_along_axis(e_logits, g_sel[:, :, None], 1)[:, 0]
    top_p, top_i = lax.top_k(jax.nn.softmax(e_logits, -1), TOP_K)
    weights = p_group * top_p / jnp.sum(top_p, -1, keepdims=True)
    expert = g_sel * EXPERTS_PER_GROUP + top_i
    return routed_expert_ffn(h, expert, weights, w1, w3, w2)


def setup_inputs(seed: int = 0) -> dict:
    key = jax.random.key(seed)
    ks = jax.random.split(key, 25)
    D = D_MODEL

    def nrm(k, shape, s):
        return jax.random.normal(k, shape, jnp.float32) * s

    return {
        'x': nrm(ks[0], (BATCH, SEQ, D), 1.0),
        'c': nrm(ks[1], (BATCH, D), 1.0),
        'ctx': nrm(ks[2], (BATCH, CTX_LEN, D), 1.0),
        'c_ctx': nrm(ks[3], (D,), 1.0),
        'w_in_even': nrm(ks[4], (N_EVEN, D, EVEN_IN), D ** -0.5),
        'w_out_even': nrm(ks[5], (N_EVEN, MIX_W, D), MIX_W ** -0.5 * DEEPNORM_BETA),
        'attn_sink': nrm(ks[6], (N_EVEN, A_Q_HEADS), 1.0),
        'gla_wa2': nrm(ks[7], (N_EVEN, 2, GLA_RANK, GLA_K_W), GLA_RANK ** -0.5),
        'gla_ba': nrm(ks[8], (N_EVEN, 2, GLA_K_W), 0.1),
        'gla_norm_w': 1.0 + nrm(ks[9], (N_EVEN, GLA_V_W), 0.02),
        'w_in_odd': nrm(ks[10], (N_ODD, D, 3 * D), D ** -0.5),
        'conv_w': nrm(ks[11], (N_ODD, CONV_W, D), CONV_W ** -0.5),
        'conv_b': nrm(ks[12], (N_ODD, D), 0.02),
        'w_out_odd': nrm(ks[13], (N_ODD, D, D), D ** -0.5 * DEEPNORM_BETA),
        'ada_w': nrm(ks[14], (DEPTH, D, 6 * D), 0.5 * D ** -0.5),
        'ada_b': nrm(ks[15], (DEPTH, 6 * D), 0.02),
        'ln_w': 1.0 + nrm(ks[16], (DEPTH, 2, D), 0.02),
        'ln_b': nrm(ks[17], (DEPTH, 2, D), 0.02),
        'router_wg': nrm(ks[18], (DEPTH, D, N_GROUPS), D ** -0.5),
        'router_bg': nrm(ks[19], (DEPTH, N_GROUPS), 0.01),
        'router_we': nrm(ks[20], (DEPTH, D, N_EXPERTS), D ** -0.5),
        'router_be': nrm(ks[21], (DEPTH, N_EXPERTS), 0.01),
        'moe_w1': nrm(ks[22], (DEPTH, N_EXPERTS, D, D_EXPERT), D ** -0.5),
        'moe_w3': nrm(ks[23], (DEPTH, N_EXPERTS, D, D_EXPERT), D ** -0.5),
        'moe_w2': nrm(ks[24], (DEPTH, N_EXPERTS, D_EXPERT, D), D_EXPERT ** -0.5 * DEEPNORM_BETA),
    }


def reference(x, c, ctx, c_ctx, w_in_even, w_out_even, attn_sink, gla_wa2, gla_ba, gla_norm_w,
              w_in_odd, conv_w, conv_b, w_out_odd, ada_w, ada_b, ln_w, ln_b,
              router_wg, router_bg, router_we, router_be, moe_w1, moe_w3, moe_w2):
    B, S, D = x.shape
    L = ctx.shape[1]
    cos, sin = axial_rope_tables(S)
    h_lat, h_ctx = x, ctx
    for l in range(DEPTH):
        i = l // 2
        need_ctx = any(j % 2 == 0 for j in range(l + 1, DEPTH))
        mod = ada_modulation(c, ada_w[l], ada_b[l])
        m = [mod[:, n, None] for n in range(6)]
        u_lat = modulate(h_lat, m[0], m[1])
        if need_ctx or l % 2 == 0:
            mc = ada_modulation(c_ctx, ada_w[l], ada_b[l])
            u_ctx = modulate(h_ctx, mc[0], mc[1])
        if l % 2 == 0:
            y_lat, y_ctx = attn_gla_mixer(u_lat, u_ctx, cos, sin, w_in_even[i], w_out_even[i], attn_sink[i],
                                          gla_wa2[i], gla_ba[i], gla_norm_w[i], need_ctx)
        else:
            y_lat = short_conv_mixer(u_lat, w_in_odd[i], conv_w[i], conv_b[i], w_out_odd[i])
            y_ctx = short_conv_mixer(u_ctx, w_in_odd[i], conv_w[i], conv_b[i], w_out_odd[i]) if need_ctx else None
        h_lat = post_norm(h_lat, m[2] * y_lat, ln_w[l, 0], ln_b[l, 0])
        tokens = modulate(h_lat, m[3], m[4]).reshape(B * S, D)
        if need_ctx:
            h_ctx = post_norm(h_ctx, mc[2] * y_ctx, ln_w[l, 0], ln_b[l, 0])
            tokens = jnp.concatenate([tokens, modulate(h_ctx, mc[3], mc[4]).reshape(B * L, D)], 0)
        y = hier_moe(tokens, router_wg[l], router_bg[l], router_we[l], router_be[l], moe_w1[l], moe_w3[l], moe_w2[l])
        h_lat = post_norm(h_lat, m[5] * y[:B * S].reshape(B, S, D), ln_w[l, 1], ln_b[l, 1])
        if need_ctx:
            h_ctx = post_norm(h_ctx, mc[5] * y[B * S:].reshape(B, L, D), ln_w[l, 1], ln_b[l, 1])
    return h_lat
```

```python
import functools

import numpy as np
import jax
import jax.numpy as jnp
from jax import lax
from jax.experimental import pallas as pl
from jax.experimental.pallas import tpu as pltpu

F32 = jnp.float32
BF16 = jnp.bfloat16
HIGHEST = lax.Precision.HIGHEST

D = 1024
DEPTH = 4
GRID_W = 64
HEAD_DIM = 64
A_Q_HEADS = 8
A_KV_HEADS = 2
WINDOW = 128
ROPE_BASE = 10000.0
GLA_HEADS = 4
GLA_DK = 64
GLA_DV = 128
GLA_RANK = 16
GLA_TAU = 16.0
GLA_CHUNK = 64
N_GROUPS = 4
EXPERTS_PER_GROUP = 8
N_EXPERTS = 32
TOP_K = 2
D_EXPERT = 512
ALPHA = (2.0 * DEPTH) ** 0.25
LN_EPS = 1e-5
RMS_EPS = 1e-6

LANE = 128
VMEM_LIMIT = 48 * 1024 * 1024

C_QA, C_KD, C_VD, C_VG, C_RG, C_QG, C_KG, C_GG = 0, 512, 768, 1024, 1536, 2048, 2304, 2560
P_W = 2816
ROUTE_W = 128
MOE_TM = 256
NEG = -1e30


def _cparams(sem):
    return pltpu.CompilerParams(dimension_semantics=sem, vmem_limit_bytes=VMEM_LIMIT)


def _dot(a, b):
    return jnp.dot(a, b, preferred_element_type=F32)


def _dot_nt(a, b):
    return lax.dot_general(a, b, (((1,), (1,)), ((), ())), preferred_element_type=F32)


def _dot_tn(a, b):
    return lax.dot_general(a, b, (((0,), (0,)), ((), ())), preferred_element_type=F32)


def _silu(x):
    return x * (1.0 / (1.0 + jnp.exp(-x)))


def _ada_kernel(c_ref, w_ref, b_ref, o_ref):
    s = _silu(c_ref[...])
    o_ref[0] = jnp.dot(s, w_ref[0], precision=HIGHEST, preferred_element_type=F32) + b_ref[0]


def ada_modulation_all(cc, ada_w, ada_b):
    R = cc.shape[0]
    tn = 1536
    return pl.pallas_call(
        _ada_kernel,
        out_shape=jax.ShapeDtypeStruct((DEPTH, R, 6 * D), F32),
        grid=(DEPTH, 6 * D // tn),
        in_specs=[pl.BlockSpec((R, D), lambda l, n: (0, 0)),
                  pl.BlockSpec((1, D, tn), lambda l, n: (l, 0, n)),
                  pl.BlockSpec((1, 1, tn), lambda l, n: (l, 0, n))],
        out_specs=pl.BlockSpec((1, R, tn), lambda l, n: (l, 0, n)),
        compiler_params=_cparams(("arbitrary", "arbitrary")),
        name="ada_modulation",
    )(cc, ada_w, ada_b.reshape(DEPTH, 1, 6 * D))


_EVEN_CHUNKS = ((0, 512, True), (512, 768, True), (768, 1024, False), (1024, 1536, False),
                (1536, 2048, False), (2048, 2560, False), (2560, 2816, False))


def _inproj_even_kernel(h_ref, mod_ref, w_ref, cos_ref, sin_ref, p_ref):
    tm = h_ref.shape[1]
    u = (h_ref[0] * (1.0 + mod_ref[0, 1:2, :]) + mod_ref[0, 0:1, :]).astype(BF16)
    cos = cos_ref[...]
    sin = sin_ref[...]
    lane = lax.broadcasted_iota(jnp.int32, (tm, LANE), 1)
    first_half = (lane % HEAD_DIM) < (HEAD_DIM // 2)
    for c0, c1, rope in _EVEN_CHUNKS:
        acc = _dot(u, w_ref[:, c0:c1])
        if rope:
            for i in range((c1 - c0) // LANE):
                x = acc[:, i * LANE:(i + 1) * LANE]
                partner = jnp.where(first_half, pltpu.roll(x, LANE - 32, 1), pltpu.roll(x, 32, 1))
                p_ref[0, :, c0 + i * LANE:c0 + (i + 1) * LANE] = (x * cos + partner * sin).astype(BF16)
        else:
            p_ref[0, :, c0:c1] = acc.astype(BF16)


def inproj_even(h, mod, w, cos_t, sin_t):
    B, T, _ = h.shape
    tm = min(512, T)
    mb = mod.shape[0]
    return pl.pallas_call(
        _inproj_even_kernel,
        out_shape=jax.ShapeDtypeStruct((B, T, P_W), BF16),
        grid=(B, T // tm),
        in_specs=[pl.BlockSpec((1, tm, D), lambda b, j: (b, j, 0)),
                  pl.BlockSpec((1, 6, D), (lambda b, j: (b, 0, 0)) if mb > 1 else (lambda b, j: (0, 0, 0))),
                  pl.BlockSpec((D, P_W), lambda b, j: (0, 0)),
                  pl.BlockSpec((tm, LANE), lambda b, j: (j, 0)),
                  pl.BlockSpec((tm, LANE), lambda b, j: (j, 0))],
        out_specs=pl.BlockSpec((1, tm, P_W), lambda b, j: (b, j, 0)),
        compiler_params=_cparams(("parallel", "arbitrary")),
        name="inproj_even",
    )(h, mod, w, cos_t, sin_t)


def _attn_kernel(*refs, tq, has_window):
    if has_window:
        sink_ref, q_ref, kw_ref, vw_ref, kc_ref, vc_ref, o_ref = refs
    else:
        sink_ref, q_ref, kc_ref, vc_ref, o_ref = refs
    lane = lax.broadcasted_iota(jnp.int32, (tq, LANE), 1)
    lo = lane < HEAD_DIM
    if has_window:
        S = kw_ref.shape[1]
        wk = tq + 2 * WINDOW
        q0 = pl.program_id(1) * tq
        wstart = pl.multiple_of(jnp.clip(q0 - WINDOW, 0, S - wk), LANE)
        qpos = q0 + lax.broadcasted_iota(jnp.int32, (tq, wk), 0)
        kpos = wstart + lax.broadcasted_iota(jnp.int32, (tq, wk), 1)
        band = jnp.abs(qpos - kpos) <= WINDOW
    for g in range(A_KV_HEADS):
        cols = slice(g * LANE, (g + 1) * LANE)
        kc = kc_ref[0, :, cols]
        vc = vc_ref[0, :, cols]
        if has_window:
            kw = kw_ref[0, pl.ds(wstart, wk), cols]
            vw = vw_ref[0, pl.ds(wstart, wk), cols]
        for pr in range(2):
            blk = slice((2 * g + pr) * LANE, (2 * g + pr + 1) * LANE)
            qblk = q_ref[0, :, blk]
            res = []
            for hh in range(2):
                snk = sink_ref[4 * g + 2 * pr + hh]
                qm = jnp.where(lo if hh == 0 else jnp.logical_not(lo), qblk, jnp.zeros_like(qblk))
                sc = _dot_nt(qm, kc)
                m = jnp.maximum(jnp.max(sc, axis=-1, keepdims=True), snk)
                if has_window:
                    sw = jnp.where(band, _dot_nt(qm, kw), NEG)
                    m = jnp.maximum(m, jnp.max(sw, axis=-1, keepdims=True))
                pc = jnp.exp(sc - m)
                den = jnp.sum(pc, axis=-1, keepdims=True) + jnp.exp(snk - m)
                o = _dot(pc.astype(BF16), vc)
                if has_window:
                    pw = jnp.exp(sw - m)
                    den = den + jnp.sum(pw, axis=-1, keepdims=True)
                    o = o + _dot(pw.astype(BF16), vw)
                res.append(o / den)
            o_ref[0, :, blk] = jnp.where(lo, res[0], res[1]).astype(BF16)


def attention(p_q, p_ctx, sink, has_window):
    B, T, _ = p_q.shape
    L = p_ctx.shape[1]
    tq = 128
    in_specs = [pl.BlockSpec(memory_space=pltpu.SMEM),
                pl.BlockSpec((1, tq, 512), lambda b, j: (b, j, C_QA // 512))]
    args = [sink, p_q]
    if has_window:
        in_specs += [pl.BlockSpec((1, T, 256), lambda b, j: (b, 0, C_KD // 256)),
                     pl.BlockSpec((1, T, 256), lambda b, j: (b, 0, C_VD // 256))]
        args += [p_q, p_q]
    in_specs += [pl.BlockSpec((1, L, 256), lambda b, j: (b, 0, C_KD // 256)),
                 pl.BlockSpec((1, L, 256), lambda b, j: (b, 0, C_VD // 256))]
    args += [p_ctx, p_ctx]
    return pl.pallas_call(
        functools.partial(_attn_kernel, tq=tq, has_window=has_window),
        out_shape=jax.ShapeDtypeStruct((B, T, 512), BF16),
        grid=(B, T // tq),
        in_specs=in_specs,
        out_specs=pl.BlockSpec((1, tq, 512), lambda b, j: (b, j, 0)),
        compiler_params=_cparams(("parallel", "arbitrary")),
        name="window_attention" if has_window else "context_attention",
    )(*args)


def _log_sigmoid(x):
    return jnp.minimum(x, 0.0) - jnp.log(1.0 + jnp.exp(-jnp.abs(x)))


def _gla_kernel(qf_ref, kf_ref, vf_ref, gf_ref, qb_ref, kb_ref, vb_ref, gb_ref, wa_ref, ba_ref, s0_ref,
                of_ref, ob_ref, sfin_ref, s_sc):
    j = pl.program_id(1)
    nblk = pl.num_programs(1)
    tb = qf_ref.shape[1]
    nc = tb // GLA_CHUNK

    @pl.when(j == 0)
    def _():
        s_sc[...] = s0_ref[0]

    ri = lax.broadcasted_iota(jnp.int32, (GLA_CHUNK, GLA_CHUNK), 0)
    ci = lax.broadcasted_iota(jnp.int32, (GLA_CHUNK, GLA_CHUNK), 1)
    lane = lax.broadcasted_iota(jnp.int32, (GLA_CHUNK, LANE), 1)
    lo = lane < GLA_DK
    lane_s = lax.broadcasted_iota(jnp.int32, (GLA_DV, LANE), 1)
    lo_s = lane_s < GLA_DK

    for d in range(2):
        q_ref, k_ref, v_ref, g_ref, o_ref = ((qf_ref, kf_ref, vf_ref, gf_ref, of_ref) if d == 0 else
                                             (qb_ref, kb_ref, vb_ref, gb_ref, ob_ref))
        causal = (ri >= ci) if d == 0 else (ci >= ri)
        tri = causal.astype(F32)
        g = _dot(g_ref[0, :, d * LANE:(d + 1) * LANE], wa_ref[d]) + ba_ref[d]
        log_a = _log_sigmoid(g) / GLA_TAU
        for c in (range(nc) if d == 0 else range(nc - 1, -1, -1)):
            rows = slice(c * GLA_CHUNK, (c + 1) * GLA_CHUNK)
            b = jnp.dot(tri, log_a[rows], precision=HIGHEST, preferred_element_type=F32)
            b_last = b[GLA_CHUNK - 1:GLA_CHUNK] if d == 0 else b[0:1]
            qc = q_ref[0, rows, :].astype(F32)
            kc = k_ref[0, rows, :].astype(F32)
            q_in = (qc * jnp.exp(b)).astype(BF16)
            k_in = (kc * jnp.exp(-b)).astype(BF16)
            k_st = (kc * jnp.exp(b_last - b)).astype(BF16)
            dl = jnp.exp(b_last)
            for pair in range(2):
                cols = slice(pair * LANE, (pair + 1) * LANE)
                qp, kp, ksp = q_in[:, cols], k_in[:, cols], k_st[:, cols]
                st = s_sc[d, pair]
                st_b = st.astype(BF16)
                upd = []
                for hh in range(2):
                    head = 2 * pair + hh
                    qm = jnp.where(lo if hh == 0 else jnp.logical_not(lo), qp, jnp.zeros_like(qp))
                    attn = jnp.where(causal, _dot_nt(qm, kp), 0.0)
                    vh = v_ref[0, rows, head * GLA_DV:(head + 1) * GLA_DV]
                    o_ref[0, rows, head * GLA_DV:(head + 1) * GLA_DV] = (
                        _dot(attn.astype(BF16), vh) + _dot_nt(qm, st_b)).astype(o_ref.dtype)
                    upd.append(_dot_tn(vh, ksp))
                s_sc[d, pair] = st * dl[:, cols] + jnp.where(lo_s, upd[0], upd[1])

    @pl.when(j == nblk - 1)
    def _():
        sfin_ref[0] = s_sc[...]


def gla_scan(p, wa_p, ba, s0):
    B, T, _ = p.shape
    tb = min(512, T)
    nblk = T // tb
    fwd = lambda b, j: (b, j)
    bwd = lambda b, j: (b, nblk - 1 - j)

    def specs(im):
        return [pl.BlockSpec((1, tb, 256), lambda b, j: im(b, j) + (C_QG // 256,)),
                pl.BlockSpec((1, tb, 256), lambda b, j: im(b, j) + (C_KG // 256,)),
                pl.BlockSpec((1, tb, 512), lambda b, j: im(b, j) + (C_VG // 512,)),
                pl.BlockSpec((1, tb, 256), lambda b, j: im(b, j) + (C_GG // 256,))]

    return pl.pallas_call(
        _gla_kernel,
        out_shape=(jax.ShapeDtypeStruct((B, T, 512), F32), jax.ShapeDtypeStruct((B, T, 512), F32),
                   jax.ShapeDtypeStruct(s0.shape, F32)),
        grid=(B, nblk),
        in_specs=specs(fwd) + specs(bwd) + [
            pl.BlockSpec((2, LANE, 256), lambda b, j: (0, 0, 0)),
            pl.BlockSpec((2, 1, 256), lambda b, j: (0, 0, 0)),
            pl.BlockSpec((1, 2, 2, GLA_DV, LANE), lambda b, j: (b, 0, 0, 0, 0))],
        out_specs=(pl.BlockSpec((1, tb, 512), lambda b, j: (b, j, 0)),
                   pl.BlockSpec((1, tb, 512), lambda b, j: (b, nblk - 1 - j, 0)),
                   pl.BlockSpec((1, 2, 2, GLA_DV, LANE), lambda b, j: (b, 0, 0, 0, 0))),
        scratch_shapes=[pltpu.VMEM((2, 2, GLA_DV, LANE), F32)],
        compiler_params=_cparams(("parallel", "arbitrary")),
        name="gla_scan",
    )(p, p, p, p, p, p, p, p, wa_p, ba, s0)


def _layer_norm(r, w, b):
    mu = jnp.mean(r, axis=-1, keepdims=True)
    xc = r - mu
    var = jnp.mean(xc * xc, axis=-1, keepdims=True)
    return xc * lax.rsqrt(var + LN_EPS) * w + b


def _post_norm_and_route(h, y, mod_ref, lnw_ref, lnb_ref, rwh_ref, rwl_ref, rb_ref, h1_ref, tok_ref, lg_ref):
    h1 = _layer_norm(ALPHA * h + mod_ref[0, 2:3, :] * y, lnw_ref[...], lnb_ref[...])
    h1_ref[0] = h1
    tok = h1 * (1.0 + mod_ref[0, 4:5, :]) + mod_ref[0, 3:4, :]
    tok_ref[0] = tok
    hi = tok.astype(BF16)
    lo = (tok - hi.astype(F32)).astype(BF16)
    lg_ref[0] = _dot(hi, rwh_ref[...]) + _dot(lo, rwh_ref[...]) + _dot(hi, rwl_ref[...]) + rb_ref[...]


def _epilogue_specs(tm, mb):
    mod_map = (lambda b, j: (b, 0, 0)) if mb > 1 else (lambda b, j: (0, 0, 0))
    const2 = lambda b, j: (0, 0)
    return [pl.BlockSpec((1, tm, D), lambda b, j: (b, j, 0)),
            pl.BlockSpec((1, 6, D), mod_map),
            pl.BlockSpec((1, D), const2), pl.BlockSpec((1, D), const2),
            pl.BlockSpec((D, ROUTE_W), const2), pl.BlockSpec((D, ROUTE_W), const2),
            pl.BlockSpec((1, ROUTE_W), const2)]


def _epilogue_outs(B, T, tm):
    shapes = (jax.ShapeDtypeStruct((B, T, D), F32), jax.ShapeDtypeStruct((B, T, D), F32),
              jax.ShapeDtypeStruct((B, T, ROUTE_W), F32))
    specs = (pl.BlockSpec((1, tm, D), lambda b, j: (b, j, 0)), pl.BlockSpec((1, tm, D), lambda b, j: (b, j, 0)),
             pl.BlockSpec((1, tm, ROUTE_W), lambda b, j: (b, j, 0)))
    return shapes, specs


def _outproj_even_kernel(a_ref, of_ref, ob_ref, rg_ref, nw_ref, wo_ref,
                         h_ref, mod_ref, lnw_ref, lnb_ref, rwh_ref, rwl_ref, rb_ref,
                         h1_ref, tok_ref, lg_ref):
    y = _dot(a_ref[0], wo_ref[0:512, :])
    for hd in range(GLA_HEADS):
        cols = slice(hd * GLA_DV, (hd + 1) * GLA_DV)
        o = of_ref[0, :, cols] + ob_ref[0, :, cols]
        o = o * lax.rsqrt(jnp.mean(o * o, axis=-1, keepdims=True) + RMS_EPS)
        gated = o * nw_ref[:, cols] * _silu(rg_ref[0, :, cols].astype(F32))
        y = y + _dot(gated.astype(BF16), wo_ref[512 + hd * GLA_DV:512 + (hd + 1) * GLA_DV, :])
    _post_norm_and_route(h_ref[0], y, mod_ref, lnw_ref, lnb_ref, rwh_ref, rwl_ref, rb_ref, h1_ref, tok_ref, lg_ref)


def outproj_even(a, o_f, o_b, p, norm_w, w_out, h, mod, lnw, lnb, rwh, rwl, rb):
    B, T, _ = h.shape
    tm = min(512, T)
    tile = lambda b, j: (b, j, 0)
    shapes, ospecs = _epilogue_outs(B, T, tm)
    return pl.pallas_call(
        _outproj_even_kernel,
        out_shape=shapes,
        grid=(B, T // tm),
        in_specs=[pl.BlockSpec((1, tm, 512), tile), pl.BlockSpec((1, tm, 512), tile), pl.BlockSpec((1, tm, 512), tile),
                  pl.BlockSpec((1, tm, 512), lambda b, j: (b, j, C_RG // 512)),
                  pl.BlockSpec((1, 512), lambda b, j: (0, 0)),
                  pl.BlockSpec((D, D), lambda b, j: (0, 0))] + _epilogue_specs(tm, mod.shape[0]),
        out_specs=ospecs,
        compiler_params=_cparams(("parallel", "arbitrary")),
        name="outproj_even",
    )(a, o_f, o_b, p, norm_w, w_out, h, mod, lnw, lnb, rwh, rwl, rb)


def _inproj_odd_kernel(h_ref, mod_ref, w_ref, o_ref):
    u = (h_ref[0] * (1.0 + mod_ref[0, 1:2, :]) + mod_ref[0, 0:1, :]).astype(BF16)
    o_ref[0, :, 0:D] = _dot(u, w_ref[:, 0:D]).astype(BF16)
    o_ref[0, :, D:2 * D] = (_dot(u, w_ref[:, D:2 * D]) * _dot(u, w_ref[:, 2 * D:3 * D])).astype(BF16)


def inproj_odd(h, mod, w):
    B, T, _ = h.shape
    tm = min(512, T)
    mb = mod.shape[0]
    return pl.pallas_call(
        _inproj_odd_kernel,
        out_shape=jax.ShapeDtypeStruct((B, T, 2 * D), BF16),
        grid=(B, T // tm),
        in_specs=[pl.BlockSpec((1, tm, D), lambda b, j: (b, j, 0)),
                  pl.BlockSpec((1, 6, D), (lambda b, j: (b, 0, 0)) if mb > 1 else (lambda b, j: (0, 0, 0))),
                  pl.BlockSpec((D, 3 * D), lambda b, j: (0, 0))],
        out_specs=pl.BlockSpec((1, tm, 2 * D), lambda b, j: (b, j, 0)),
        compiler_params=_cparams(("parallel", "arbitrary")),
        name="inproj_odd",
    )(h, mod, w)


HALO = 16


def _outproj_odd_kernel(gb_ref, z_ref, zp_ref, zn_ref, cw_ref, cb_ref, wo_ref,
                        h_ref, mod_ref, lnw_ref, lnb_ref, rwh_ref, rwl_ref, rb_ref,
                        h1_ref, tok_ref, lg_ref):
    j = pl.program_id(1)
    tm = z_ref.shape[1]
    z = z_ref[0].astype(F32)
    prev_row = jnp.where(j > 0, zp_ref[0, HALO - 1:HALO, :].astype(F32), 0.0)
    next_row = jnp.where(j < pl.num_programs(1) - 1, zn_ref[0, 0:1, :].astype(F32), 0.0)
    row = lax.broadcasted_iota(jnp.int32, (tm, D), 0)
    z_prev = jnp.where(row == 0, prev_row, pltpu.roll(z, 1, 0))
    z_next = jnp.where(row == tm - 1, next_row, pltpu.roll(z, tm - 1, 0))
    conv = z_prev * cw_ref[0:1, :] + z * cw_ref[1:2, :] + z_next * cw_ref[2:3, :] + cb_ref[...]
    y = _dot((gb_ref[0].astype(F32) * conv).astype(BF16), wo_ref[...])
    _post_norm_and_route(h_ref[0], y, mod_ref, lnw_ref, lnb_ref, rwh_ref, rwl_ref, rb_ref, h1_ref, tok_ref, lg_ref)


def outproj_odd(gz, conv_w, conv_b, w_out, h, mod, lnw, lnb, rwh, rwl, rb):
    B, T, _ = h.shape
    tm = min(512, T)
    r = tm // HALO
    nh = T // HALO
    shapes, ospecs = _epilogue_outs(B, T, tm)
    return pl.pallas_call(
        _outproj_odd_kernel,
        out_shape=shapes,
        grid=(B, T // tm),
        in_specs=[pl.BlockSpec((1, tm, D), lambda b, j: (b, j, 0)),
                  pl.BlockSpec((1, tm, D), lambda b, j: (b, j, 1)),
                  pl.BlockSpec((1, HALO, D), lambda b, j: (b, jnp.maximum(j * r - 1, 0), 1)),
                  pl.BlockSpec((1, HALO, D), lambda b, j: (b, jnp.minimum((j + 1) * r, nh - 1), 1)),
                  pl.BlockSpec((3, D), lambda b, j: (0, 0)),
                  pl.BlockSpec((1, D), lambda b, j: (0, 0)),
                  pl.BlockSpec((D, D), lambda b, j: (0, 0))] + _epilogue_specs(tm, mod.shape[0]),
        out_specs=ospecs,
        compiler_params=_cparams(("parallel", "arbitrary")),
        name="outproj_odd",
    )(gz, gz, gz, gz, conv_w, conv_b, w_out, h, mod, lnw, lnb, rwh, rwl, rb)


def _row_copy_wait(src_ref, dst_ref, sem):
    pltpu.make_async_copy(src_ref, dst_ref, sem).wait()


def _dispatch_kernel(*refs, aliased):
    if aliased:
        dest_ref, tok_ref, _, xs_ref, sem = refs
    else:
        dest_ref, tok_ref, xs_ref, sem = refs
    tm = tok_ref.shape[0]

    def body(r, carry):
        for k in range(TOP_K):
            pltpu.make_async_copy(tok_ref.at[pl.ds(r, 1)], xs_ref.at[pl.ds(dest_ref[0, 0, k * tm + r], 1)], sem).start()
        return carry

    lax.fori_loop(0, tm, body, 0, unroll=8)
    for k in range(TOP_K):
        _row_copy_wait(tok_ref, xs_ref.at[pl.ds(0, tm)], sem)


def moe_dispatch(tok, dest_tiles, n_rows, xs=None):
    N = tok.shape[0]
    tm = dest_tiles.shape[2] // TOP_K
    aliased = xs is not None
    in_specs = [pl.BlockSpec((1, 1, TOP_K * tm), lambda i: (i, 0, 0), memory_space=pltpu.SMEM),
                pl.BlockSpec((tm, D), lambda i: (i, 0))]
    args = [dest_tiles, tok]
    if aliased:
        in_specs.append(pl.BlockSpec(memory_space=pl.ANY))
        args.append(xs)
    return pl.pallas_call(
        functools.partial(_dispatch_kernel, aliased=aliased),
        out_shape=jax.ShapeDtypeStruct((n_rows, D), F32),
        grid=(N // tm,),
        in_specs=in_specs,
        out_specs=pl.BlockSpec(memory_space=pl.ANY),
        scratch_shapes=[pltpu.SemaphoreType.DMA(())],
        input_output_aliases={2: 0} if aliased else {},
        compiler_params=_cparams(("arbitrary",)),
        name="moe_dispatch",
    )(*args)


def _ffn_kernel(be_ref, nv_ref, x_ref, w1_ref, w3_ref, w2_ref, y_ref, w1b, w3b, w2b):
    i = pl.program_id(0)
    changed = jnp.logical_or(i == 0, be_ref[i] != be_ref[jnp.maximum(i - 1, 0)])

    @pl.when(changed)
    def _():
        w1b[...] = w1_ref[0, 0].astype(BF16)
        w3b[...] = w3_ref[0, 0].astype(BF16)
        w2b[...] = w2_ref[0, 0].astype(BF16)

    @pl.when(i < nv_ref[0])
    def _():
        x = x_ref[...].astype(BF16)
        mid = _silu(_dot(x, w1b[...])) * _dot(x, w3b[...])
        y_ref[...] = _dot(mid.astype(BF16), w2b[...])

    @pl.when(i >= nv_ref[0])
    def _():
        y_ref[...] = jnp.zeros_like(y_ref)


def moe_ffn(xs, blk_expert, n_valid, w1, w3, w2, layer):
    n_rows = xs.shape[0]
    nb = n_rows // MOE_TM
    return pl.pallas_call(
        _ffn_kernel,
        out_shape=jax.ShapeDtypeStruct((n_rows, D), F32),
        grid_spec=pltpu.PrefetchScalarGridSpec(
            num_scalar_prefetch=2,
            grid=(nb,),
            in_specs=[pl.BlockSpec((MOE_TM, D), lambda i, be, nv: (i, 0)),
                      pl.BlockSpec((1, 1, D, D_EXPERT), lambda i, be, nv: (layer, be[i], 0, 0)),
                      pl.BlockSpec((1, 1, D, D_EXPERT), lambda i, be, nv: (layer, be[i], 0, 0)),
                      pl.BlockSpec((1, 1, D_EXPERT, D), lambda i, be, nv: (layer, be[i], 0, 0))],
            out_specs=pl.BlockSpec((MOE_TM, D), lambda i, be, nv: (i, 0)),
            scratch_shapes=[pltpu.VMEM((D, D_EXPERT), BF16), pltpu.VMEM((D, D_EXPERT), BF16),
                            pltpu.VMEM((D_EXPERT, D), BF16)]),
        compiler_params=_cparams(("arbitrary",)),
        name="moe_ffn",
    )(blk_expert, n_valid, xs, w1, w3, w2)


def _combine_kernel(dest_ref, h_ref, wt_ref, mod_ref, lnw_ref, lnb_ref, y_ref, o_ref, buf, sem):
    tm = h_ref.shape[0]

    def body(r, carry):
        for k in range(TOP_K):
            pltpu.make_async_copy(y_ref.at[pl.ds(dest_ref[0, 0, k * tm + r], 1)], buf.at[k, pl.ds(r, 1)], sem).start()
        return carry

    lax.fori_loop(0, tm, body, 0, unroll=8)
    for k in range(TOP_K):
        _row_copy_wait(y_ref.at[pl.ds(0, tm)], buf.at[k], sem)
    y = wt_ref[:, 0:1] * buf[0] + wt_ref[:, 1:2] * buf[1]
    o_ref[...] = _layer_norm(ALPHA * h_ref[...] + mod_ref[0, 5:6, :] * y, lnw_ref[...], lnb_ref[...])


def moe_combine(h1, y_sorted, dest_tiles, wts, mod, lnw, lnb):
    B, T, _ = h1.shape
    N = B * T
    tm = dest_tiles.shape[2] // TOP_K
    per_b = T // tm
    mod_map = (lambda i: (i // per_b, 0, 0)) if mod.shape[0] > 1 else (lambda i: (0, 0, 0))
    out = pl.pallas_call(
        _combine_kernel,
        out_shape=jax.ShapeDtypeStruct((N, D), F32),
        grid=(N // tm,),
        in_specs=[pl.BlockSpec((1, 1, TOP_K * tm), lambda i: (i, 0, 0), memory_space=pltpu.SMEM),
                  pl.BlockSpec((tm, D), lambda i: (i, 0)),
                  pl.BlockSpec((tm, TOP_K), lambda i: (i, 0)),
                  pl.BlockSpec((1, 6, D), mod_map),
                  pl.BlockSpec((1, D), lambda i: (0, 0)), pl.BlockSpec((1, D), lambda i: (0, 0)),
                  pl.BlockSpec(memory_space=pl.ANY)],
        out_specs=pl.BlockSpec((tm, D), lambda i: (i, 0)),
        scratch_shapes=[pltpu.VMEM((TOP_K, tm, D), F32), pltpu.SemaphoreType.DMA(())],
        compiler_params=_cparams(("arbitrary",)),
        name="moe_combine",
    )(dest_tiles, h1.reshape(N, D), wts, mod, lnw, lnb, y_sorted)
    return out.reshape(B, T, D)


def _route(logits):
    N = logits.shape[0]
    g_logits = logits[:, :N_GROUPS]
    _, g_sel = lax.top_k(g_logits, 1)
    p_group = jnp.take_along_axis(jax.nn.softmax(g_logits, -1), g_sel, -1)
    e_logits = logits[:, N_GROUPS:N_GROUPS + N_EXPERTS].reshape(N, N_GROUPS, EXPERTS_PER_GROUP)
    e_logits = jnp.take_along_axis(e_logits, g_sel[:, :, None], 1)[:, 0]
    top_p, top_i = lax.top_k(jax.nn.softmax(e_logits, -1), TOP_K)
    weights = p_group * top_p / jnp.sum(top_p, -1, keepdims=True)
    return g_sel * EXPERTS_PER_GROUP + top_i, weights


def _sorted_layout(expert):
    A = expert.shape[0] * TOP_K
    flat_e = expert.reshape(A)
    onehot = (flat_e[:, None] == jnp.arange(N_EXPERTS, dtype=flat_e.dtype)[None, :]).astype(jnp.int32)
    csum = jnp.cumsum(onehot, axis=0)
    rank = jnp.take_along_axis(csum, flat_e[:, None], 1)[:, 0] - 1
    counts = csum[-1]
    padded = (counts + MOE_TM - 1) // MOE_TM * MOE_TM
    pad_end = jnp.cumsum(padded)
    dest = (pad_end - padded)[flat_e] + rank
    nb = -(-A // MOE_TM) + N_EXPERTS
    blk_start = jnp.arange(nb, dtype=jnp.int32) * MOE_TM
    blk_expert = jnp.minimum(jnp.sum((pad_end[None, :] <= blk_start[:, None]).astype(jnp.int32), axis=1), N_EXPERTS - 1)
    n_valid = (pad_end[-1] // MOE_TM).astype(jnp.int32).reshape(1)
    return dest.reshape(-1, TOP_K).astype(jnp.int32), blk_expert.astype(jnp.int32), n_valid, nb * MOE_TM


def _tile_dest(dest, tm):
    n = dest.shape[0]
    return dest.reshape(n // tm, tm, TOP_K).transpose(0, 2, 1).reshape(n // tm, 1, TOP_K * tm)


def hier_moe_and_norm(streams, w1, w3, w2, layer, lnw, lnb):
    logits = jnp.concatenate([s[2].reshape(-1, ROUTE_W) for s in streams], 0)
    expert, weights = _route(logits)
    dest, blk_expert, n_valid, n_rows = _sorted_layout(expert)
    xs, parts, off = None, [], 0
    for h1, tok, _, mod in streams:
        n = h1.shape[0] * h1.shape[1]
        tm = min(256, h1.shape[1])
        dt = _tile_dest(dest[off:off + n], tm)
        xs = moe_dispatch(tok.reshape(n, D), dt, n_rows, xs)
        parts.append((h1, dt, weights[off:off + n], mod))
        off += n
    y = moe_ffn(xs, blk_expert, n_valid, w1, w3, w2, layer)
    return [moe_combine(h1, y, dt, wt, mod, lnw, lnb) for h1, dt, wt, mod in parts]


def _even_weight_columns():
    n = np.arange(HEAD_DIM)
    perm = (n % 32) // 16 * 32 + n // 32 * 16 + n % 16
    idx = np.zeros(P_W, np.int32)
    scale = np.zeros(P_W, np.float32)
    for hd in range(A_Q_HEADS):
        idx[C_QA + hd * 64:C_QA + (hd + 1) * 64] = hd * 64 + perm
    scale[C_QA:C_QA + 512] = HEAD_DIM ** -0.5
    for g in range(A_KV_HEADS):
        for rep in range(2):
            o = g * 128 + rep * 64
            idx[C_KD + o:C_KD + o + 64] = 512 + g * 64 + perm
            idx[C_VD + o:C_VD + o + 64] = 640 + g * 64 + n
    scale[C_KD:C_VD + 256] = 1.0
    idx[C_QG:C_QG + 256] = 768 + np.arange(256)
    scale[C_QG:C_QG + 256] = GLA_DK ** -0.5
    idx[C_KG:C_KG + 256] = 1024 + np.arange(256)
    idx[C_VG:C_VG + 512] = 1280 + np.arange(512)
    idx[C_RG:C_RG + 512] = 1792 + np.arange(512)
    scale[C_KG:C_KG + 256] = 1.0
    scale[C_VG:C_RG + 512] = 1.0
    for d in range(2):
        idx[C_GG + d * 128:C_GG + d * 128 + GLA_RANK] = 2304 + d * GLA_RANK + np.arange(GLA_RANK)
        scale[C_GG + d * 128:C_GG + d * 128 + GLA_RANK] = 1.0
    return idx, scale


def _rope_tables(S):
    row = jnp.repeat(jnp.arange(S // GRID_W), GRID_W).astype(F32)
    col = jnp.tile(jnp.arange(GRID_W), S // GRID_W).astype(F32)
    axis_dim = HEAD_DIM // 2
    inv_freq = ROPE_BASE ** (-jnp.arange(0, axis_dim, 2, dtype=F32) / axis_dim)
    ang = jnp.concatenate([row[:, None] * inv_freq, col[:, None] * inv_freq], -1)
    cos, sin = jnp.cos(ang), jnp.sin(ang)
    cos_t = jnp.tile(cos, (1, 4))
    sin_t = jnp.tile(jnp.concatenate([-sin, sin], -1), (1, 2))
    return cos_t, sin_t


def _router_weights(wg, bg, we, be):
    w = jnp.zeros((D, ROUTE_W), F32).at[:, :N_GROUPS].set(wg).at[:, N_GROUPS:N_GROUPS + N_EXPERTS].set(we)
    b = jnp.zeros((1, ROUTE_W), F32).at[0, :N_GROUPS].set(bg).at[0, N_GROUPS:N_GROUPS + N_EXPERTS].set(be)
    hi = w.astype(BF16)
    return hi, (w - hi.astype(F32)).astype(BF16), b


def kernel(x, c, ctx, c_ctx, w_in_even, w_out_even, attn_sink, gla_wa2, gla_ba, gla_norm_w, w_in_odd, conv_w, conv_b, w_out_odd, ada_w, ada_b, ln_w, ln_b, router_wg, router_bg, router_we, router_be, moe_w1, moe_w3, moe_w2):
    B, S, _ = x.shape
    L = ctx.shape[1]
    cos_t, sin_t = _rope_tables(S)
    cos_c, sin_c = jnp.ones((L, LANE), F32), jnp.zeros((L, LANE), F32)
    col_idx, col_scale = _even_weight_columns()

    n_cond = -(-(B + 1) // 8) * 8
    cc = jnp.zeros((n_cond, D), F32).at[:B].set(c).at[B].set(c_ctx)
    mods = ada_modulation_all(cc, ada_w, ada_b).reshape(DEPTH, n_cond, 6, D)

    h_lat, h_ctx = x, ctx
    for l in range(DEPTH):
        i = l // 2
        need_ctx = any(j % 2 == 0 for j in range(l + 1, DEPTH))
        m_lat = mods[l, :B]
        m_ctx = mods[l, B:B + 1]
        lnw0, lnb0 = ln_w[l, 0:1], ln_b[l, 0:1]
        lnw1, lnb1 = ln_w[l, 1:2], ln_b[l, 1:2]
        rwh, rwl, rb = _router_weights(router_wg[l], router_bg[l], router_we[l], router_be[l])
        streams = []
        if l % 2 == 0:
            w_in = (w_in_even[i][:, col_idx] * col_scale[None, :]).astype(BF16)
            w_out = w_out_even[i].astype(BF16)
            wa_p = jnp.zeros((2, LANE, GLA_HEADS * GLA_DK), F32).at[:, :GLA_RANK].set(gla_wa2[i]).astype(BF16)
            ba = gla_ba[i].reshape(2, 1, -1)
            nw = gla_norm_w[i].reshape(1, -1)
            p_ctx = inproj_even(h_ctx, m_ctx, w_in, cos_c, sin_c)
            p_lat = inproj_even(h_lat, m_lat, w_in, cos_t, sin_t)
            a_lat = attention(p_lat, p_ctx, attn_sink[i], True)
            s0 = jnp.zeros((B, 2, 2, GLA_DV, LANE), F32)
            oc_f, oc_b, s_ctx = gla_scan(p_ctx, wa_p, ba, s0)
            ol_f, ol_b, _ = gla_scan(p_lat, wa_p, ba, s_ctx)
            streams.append(outproj_even(a_lat, ol_f, ol_b, p_lat, nw, w_out, h_lat, m_lat, lnw0, lnb0, rwh, rwl, rb)
                           + (m_lat,))
            if need_ctx:
                a_ctx = attention(p_ctx, p_ctx, attn_sink[i], False)
                streams.append(outproj_even(a_ctx, oc_f, oc_b, p_ctx, nw, w_out, h_ctx, m_ctx, lnw0, lnb0,
                                            rwh, rwl, rb) + (m_ctx,))
        else:
            w_in = w_in_odd[i].astype(BF16)
            w_out = w_out_odd[i].astype(BF16)
            cb = conv_b[i].reshape(1, D)
            pairs = [(h_lat, m_lat)] + ([(h_ctx, m_ctx)] if need_ctx else [])
            for h, m in pairs:
                gz = inproj_odd(h, m, w_in)
                streams.append(outproj_odd(gz, conv_w[i], cb, w_out, h, m, lnw0, lnb0, rwh, rwl, rb) + (m,))
        outs = hier_moe_and_norm(streams, moe_w1, moe_w3, moe_w2, l, lnw1, lnb1)
        h_lat = outs[0]
        if need_ctx:
            h_ctx = outs[1]
    return h_lat
```

```python
import functools

import numpy as np
import jax
import jax.numpy as jnp
from jax import lax
from jax.experimental import pallas as pl
from jax.experimental.pallas import tpu as pltpu

F32 = jnp.float32
BF16 = jnp.bfloat16
HIGHEST = lax.Precision.HIGHEST

D = 1024
DEPTH = 4
GRID_W = 64
HEAD_DIM = 64
A_Q_HEADS = 8
A_KV_HEADS = 2
WINDOW = 128
ROPE_BASE = 10000.0
GLA_HEADS = 4
GLA_DK = 64
GLA_DV = 128
GLA_RANK = 16
GLA_TAU = 16.0
GLA_CHUNK = 64
N_GROUPS = 4
EXPERTS_PER_GROUP = 8
N_EXPERTS = 32
TOP_K = 2
D_EXPERT = 512
ALPHA = (2.0 * DEPTH) ** 0.25
LN_EPS = 1e-5
RMS_EPS = 1e-6

LANE = 128
VMEM_LIMIT = 48 * 1024 * 1024

C_QA, C_KD, C_VD, C_VG, C_RG, C_QG, C_KG, C_GG = 0, 512, 768, 1024, 1536, 2048, 2304, 2560
P_W = 2816
ROUTE_W = 128
ROUTE_ROWS = 40
MOE_TM = 256
NEG = -1e30


def _cparams(sem):
    return pltpu.CompilerParams(dimension_semantics=sem, vmem_limit_bytes=VMEM_LIMIT)


def _dot(a, b):
    return jnp.dot(a, b, preferred_element_type=F32)


def _dot_nt(a, b):
    return lax.dot_general(a, b, (((1,), (1,)), ((), ())), preferred_element_type=F32)


def _dot_tn(a, b):
    return lax.dot_general(a, b, (((0,), (0,)), ((), ())), preferred_element_type=F32)


def _silu(x):
    return x * (1.0 / (1.0 + jnp.exp(-x)))


def _ada_kernel(c_ref, w_ref, b_ref, o_ref):
    s = _silu(c_ref[...])
    o_ref[0] = jnp.dot(s, w_ref[0], precision=HIGHEST, preferred_element_type=F32) + b_ref[0]


def ada_modulation_all(cc, ada_w, ada_b):
    R = cc.shape[0]
    tn = 1536
    return pl.pallas_call(
        _ada_kernel,
        out_shape=jax.ShapeDtypeStruct((DEPTH, R, 6 * D), F32),
        grid=(DEPTH, 6 * D // tn),
        in_specs=[pl.BlockSpec((R, D), lambda l, n: (0, 0)),
                  pl.BlockSpec((1, D, tn), lambda l, n: (l, 0, n)),
                  pl.BlockSpec((1, 1, tn), lambda l, n: (l, 0, n))],
        out_specs=pl.BlockSpec((1, R, tn), lambda l, n: (l, 0, n)),
        compiler_params=_cparams(("arbitrary", "arbitrary")),
        name="ada_modulation",
    )(cc, ada_w, ada_b.reshape(DEPTH, 1, 6 * D))


_EVEN_CHUNKS = ((0, 512, True), (512, 768, True), (768, 1024, False), (1024, 1536, False),
                (1536, 2048, False), (2048, 2560, False), (2560, 2816, False))


def _inproj_even_kernel(h_ref, mod_ref, w_ref, cos_ref, sin_ref, p_ref):
    tm = h_ref.shape[1]
    u = (h_ref[0] * (1.0 + mod_ref[0, 1:2, :]) + mod_ref[0, 0:1, :]).astype(BF16)
    cos = cos_ref[...]
    sin = sin_ref[...]
    lane = lax.broadcasted_iota(jnp.int32, (tm, LANE), 1)
    first_half = (lane % HEAD_DIM) < (HEAD_DIM // 2)
    for c0, c1, rope in _EVEN_CHUNKS:
        acc = _dot(u, w_ref[:, c0:c1])
        if rope:
            for i in range((c1 - c0) // LANE):
                x = acc[:, i * LANE:(i + 1) * LANE]
                partner = jnp.where(first_half, pltpu.roll(x, LANE - 32, 1), pltpu.roll(x, 32, 1))
                p_ref[0, :, c0 + i * LANE:c0 + (i + 1) * LANE] = (x * cos + partner * sin).astype(BF16)
        else:
            p_ref[0, :, c0:c1] = acc.astype(BF16)


def inproj_even(h, mod, w, cos_t, sin_t):
    B, T, _ = h.shape
    tm = min(512, T)
    mb = mod.shape[0]
    return pl.pallas_call(
        _inproj_even_kernel,
        out_shape=jax.ShapeDtypeStruct((B, T, P_W), BF16),
        grid=(B, T // tm),
        in_specs=[pl.BlockSpec((1, tm, D), lambda b, j: (b, j, 0)),
                  pl.BlockSpec((1, 6, D), (lambda b, j: (b, 0, 0)) if mb > 1 else (lambda b, j: (0, 0, 0))),
                  pl.BlockSpec((D, P_W), lambda b, j: (0, 0)),
                  pl.BlockSpec((tm, LANE), lambda b, j: (j, 0)),
                  pl.BlockSpec((tm, LANE), lambda b, j: (j, 0))],
        out_specs=pl.BlockSpec((1, tm, P_W), lambda b, j: (b, j, 0)),
        compiler_params=_cparams(("parallel", "arbitrary")),
        name="inproj_even",
    )(h, mod, w, cos_t, sin_t)


def _attn_kernel(*refs, tq, has_window):
    if has_window:
        sink_ref, q_ref, kw_ref, vw_ref, kc_ref, vc_ref, o_ref = refs
    else:
        sink_ref, q_ref, kc_ref, vc_ref, o_ref = refs
    lane = lax.broadcasted_iota(jnp.int32, (tq, LANE), 1)
    lo = lane < HEAD_DIM
    if has_window:
        S = kw_ref.shape[1]
        wk = tq + 2 * WINDOW
        q0 = pl.program_id(1) * tq
        wstart = pl.multiple_of(jnp.clip(q0 - WINDOW, 0, S - wk), LANE)
        qpos = q0 + lax.broadcasted_iota(jnp.int32, (tq, wk), 0)
        kpos = wstart + lax.broadcasted_iota(jnp.int32, (tq, wk), 1)
        band = jnp.abs(qpos - kpos) <= WINDOW
    for g in range(A_KV_HEADS):
        cols = slice(g * LANE, (g + 1) * LANE)
        kc = kc_ref[0, :, cols]
        vc = vc_ref[0, :, cols]
        if has_window:
            kw = kw_ref[0, pl.ds(wstart, wk), cols]
            vw = vw_ref[0, pl.ds(wstart, wk), cols]
        for pr in range(2):
            blk = slice((2 * g + pr) * LANE, (2 * g + pr + 1) * LANE)
            qblk = q_ref[0, :, blk]
            res = []
            for hh in range(2):
                snk = sink_ref[4 * g + 2 * pr + hh]
                qm = jnp.where(lo if hh == 0 else jnp.logical_not(lo), qblk, jnp.zeros_like(qblk))
                sc = _dot_nt(qm, kc)
                m = jnp.maximum(jnp.max(sc, axis=-1, keepdims=True), snk)
                if has_window:
                    sw = jnp.where(band, _dot_nt(qm, kw), NEG)
                    m = jnp.maximum(m, jnp.max(sw, axis=-1, keepdims=True))
                pc = jnp.exp(sc - m)
                den = jnp.sum(pc, axis=-1, keepdims=True) + jnp.exp(snk - m)
                o = _dot(pc.astype(BF16), vc)
                if has_window:
                    pw = jnp.exp(sw - m)
                    den = den + jnp.sum(pw, axis=-1, keepdims=True)
                    o = o + _dot(pw.astype(BF16), vw)
                res.append(o / den)
            o_ref[0, :, blk] = jnp.where(lo, res[0], res[1]).astype(BF16)


def attention(p_q, p_ctx, sink, has_window):
    B, T, _ = p_q.shape
    L = p_ctx.shape[1]
    tq = 128
    in_specs = [pl.BlockSpec(memory_space=pltpu.SMEM),
                pl.BlockSpec((1, tq, 512), lambda b, j: (b, j, C_QA // 512))]
    args = [sink, p_q]
    if has_window:
        in_specs += [pl.BlockSpec((1, T, 256), lambda b, j: (b, 0, C_KD // 256)),
                     pl.BlockSpec((1, T, 256), lambda b, j: (b, 0, C_VD // 256))]
        args += [p_q, p_q]
    in_specs += [pl.BlockSpec((1, L, 256), lambda b, j: (b, 0, C_KD // 256)),
                 pl.BlockSpec((1, L, 256), lambda b, j: (b, 0, C_VD // 256))]
    args += [p_ctx, p_ctx]
    return pl.pallas_call(
        functools.partial(_attn_kernel, tq=tq, has_window=has_window),
        out_shape=jax.ShapeDtypeStruct((B, T, 512), BF16),
        grid=(B, T // tq),
        in_specs=in_specs,
        out_specs=pl.BlockSpec((1, tq, 512), lambda b, j: (b, j, 0)),
        compiler_params=_cparams(("parallel", "arbitrary")),
        name="window_attention" if has_window else "context_attention",
    )(*args)


def _log_sigmoid(x):
    return jnp.minimum(x, 0.0) - jnp.log(1.0 + jnp.exp(-jnp.abs(x)))


def _gla_kernel(qf_ref, kf_ref, vf_ref, gf_ref, qb_ref, kb_ref, vb_ref, gb_ref, wa_ref, ba_ref, s0_ref,
                of_ref, ob_ref, sfin_ref, s_sc):
    j = pl.program_id(1)
    nblk = pl.num_programs(1)
    tb = qf_ref.shape[1]
    nc = tb // GLA_CHUNK

    @pl.when(j == 0)
    def _():
        s_sc[...] = s0_ref[0]

    ri = lax.broadcasted_iota(jnp.int32, (GLA_CHUNK, GLA_CHUNK), 0)
    ci = lax.broadcasted_iota(jnp.int32, (GLA_CHUNK, GLA_CHUNK), 1)
    lane = lax.broadcasted_iota(jnp.int32, (GLA_CHUNK, LANE), 1)
    lo = lane < GLA_DK
    lane_s = lax.broadcasted_iota(jnp.int32, (GLA_DV, LANE), 1)
    lo_s = lane_s < GLA_DK

    for d in range(2):
        q_ref, k_ref, v_ref, g_ref, o_ref = ((qf_ref, kf_ref, vf_ref, gf_ref, of_ref) if d == 0 else
                                             (qb_ref, kb_ref, vb_ref, gb_ref, ob_ref))
        causal = (ri >= ci) if d == 0 else (ci >= ri)
        tri = causal.astype(F32)
        g = _dot(g_ref[0, :, d * LANE:(d + 1) * LANE], wa_ref[d]) + ba_ref[d]
        log_a = _log_sigmoid(g) / GLA_TAU
        for c in (range(nc) if d == 0 else range(nc - 1, -1, -1)):
            rows = slice(c * GLA_CHUNK, (c + 1) * GLA_CHUNK)
            b = jnp.dot(tri, log_a[rows], precision=HIGHEST, preferred_element_type=F32)
            b_last = b[GLA_CHUNK - 1:GLA_CHUNK] if d == 0 else b[0:1]
            qc = q_ref[0, rows, :].astype(F32)
            kc = k_ref[0, rows, :].astype(F32)
            q_in = (qc * jnp.exp(b)).astype(BF16)
            k_in = (kc * jnp.exp(-b)).astype(BF16)
            k_st = (kc * jnp.exp(b_last - b)).astype(BF16)
            dl = jnp.exp(b_last)
            for pair in range(2):
                cols = slice(pair * LANE, (pair + 1) * LANE)
                qp, kp, ksp = q_in[:, cols], k_in[:, cols], k_st[:, cols]
                st = s_sc[d, pair]
                st_b = st.astype(BF16)
                upd = []
                for hh in range(2):
                    head = 2 * pair + hh
                    qm = jnp.where(lo if hh == 0 else jnp.logical_not(lo), qp, jnp.zeros_like(qp))
                    attn = jnp.where(causal, _dot_nt(qm, kp), 0.0)
                    vh = v_ref[0, rows, head * GLA_DV:(head + 1) * GLA_DV]
                    o_ref[0, rows, head * GLA_DV:(head + 1) * GLA_DV] = (
                        _dot(attn.astype(BF16), vh) + _dot_nt(qm, st_b)).astype(o_ref.dtype)
                    upd.append(_dot_tn(vh, ksp))
                s_sc[d, pair] = st * dl[:, cols] + jnp.where(lo_s, upd[0], upd[1])

    @pl.when(j == nblk - 1)
    def _():
        sfin_ref[0] = s_sc[...]


def gla_scan(p, wa_p, ba, s0):
    B, T, _ = p.shape
    tb = min(512, T)
    nblk = T // tb
    fwd = lambda b, j: (b, j)
    bwd = lambda b, j: (b, nblk - 1 - j)

    def specs(im):
        return [pl.BlockSpec((1, tb, 256), lambda b, j: im(b, j) + (C_QG // 256,)),
                pl.BlockSpec((1, tb, 256), lambda b, j: im(b, j) + (C_KG // 256,)),
                pl.BlockSpec((1, tb, 512), lambda b, j: im(b, j) + (C_VG // 512,)),
                pl.BlockSpec((1, tb, 256), lambda b, j: im(b, j) + (C_GG // 256,))]

    return pl.pallas_call(
        _gla_kernel,
        out_shape=(jax.ShapeDtypeStruct((B, T, 512), F32), jax.ShapeDtypeStruct((B, T, 512), F32),
                   jax.ShapeDtypeStruct(s0.shape, F32)),
        grid=(B, nblk),
        in_specs=specs(fwd) + specs(bwd) + [
            pl.BlockSpec((2, LANE, 256), lambda b, j: (0, 0, 0)),
            pl.BlockSpec((2, 1, 256), lambda b, j: (0, 0, 0)),
            pl.BlockSpec((1, 2, 2, GLA_DV, LANE), lambda b, j: (b, 0, 0, 0, 0))],
        out_specs=(pl.BlockSpec((1, tb, 512), lambda b, j: (b, j, 0)),
                   pl.BlockSpec((1, tb, 512), lambda b, j: (b, nblk - 1 - j, 0)),
                   pl.BlockSpec((1, 2, 2, GLA_DV, LANE), lambda b, j: (b, 0, 0, 0, 0))),
        scratch_shapes=[pltpu.VMEM((2, 2, GLA_DV, LANE), F32)],
        compiler_params=_cparams(("parallel", "arbitrary")),
        name="gla_scan",
    )(p, p, p, p, p, p, p, p, wa_p, ba, s0)


def _layer_norm(r, w, b):
    mu = jnp.mean(r, axis=-1, keepdims=True)
    xc = r - mu
    var = jnp.mean(xc * xc, axis=-1, keepdims=True)
    return xc * lax.rsqrt(var + LN_EPS) * w + b


def _post_norm_and_route(h, y, mod_ref, lnw_ref, lnb_ref, rwh_ref, rwl_ref, rb_ref, h1_ref, tok_ref, lg_ref):
    h1 = _layer_norm(ALPHA * h + mod_ref[0, 2:3, :] * y, lnw_ref[...], lnb_ref[...])
    h1_ref[0] = h1
    tok = h1 * (1.0 + mod_ref[0, 4:5, :]) + mod_ref[0, 3:4, :]
    tok_ref[0] = tok
    hi = tok.astype(BF16)
    lo = (tok - hi.astype(F32)).astype(BF16)
    lg = _dot(hi, rwh_ref[...]) + _dot(lo, rwh_ref[...]) + _dot(hi, rwl_ref[...]) + rb_ref[...]
    lg_ref[...] = lg.T[0:ROUTE_ROWS, :]


def _epilogue_specs(tm, mb):
    mod_map = (lambda b, j: (b, 0, 0)) if mb > 1 else (lambda b, j: (0, 0, 0))
    const2 = lambda b, j: (0, 0)
    return [pl.BlockSpec((1, tm, D), lambda b, j: (b, j, 0)),
            pl.BlockSpec((1, 6, D), mod_map),
            pl.BlockSpec((1, D), const2), pl.BlockSpec((1, D), const2),
            pl.BlockSpec((D, ROUTE_W), const2), pl.BlockSpec((D, ROUTE_W), const2),
            pl.BlockSpec((1, ROUTE_W), const2)]


def _epilogue_outs(B, T, tm):
    nj = T // tm
    shapes = (jax.ShapeDtypeStruct((B, T, D), F32), jax.ShapeDtypeStruct((B, T, D), F32),
              jax.ShapeDtypeStruct((ROUTE_ROWS, B * T), F32))
    specs = (pl.BlockSpec((1, tm, D), lambda b, j: (b, j, 0)), pl.BlockSpec((1, tm, D), lambda b, j: (b, j, 0)),
             pl.BlockSpec((ROUTE_ROWS, tm), lambda b, j: (0, b * nj + j)))
    return shapes, specs


def _outproj_even_kernel(a_ref, of_ref, ob_ref, rg_ref, nw_ref, wo_ref,
                         h_ref, mod_ref, lnw_ref, lnb_ref, rwh_ref, rwl_ref, rb_ref,
                         h1_ref, tok_ref, lg_ref):
    y = _dot(a_ref[0], wo_ref[0:512, :])
    for hd in range(GLA_HEADS):
        cols = slice(hd * GLA_DV, (hd + 1) * GLA_DV)
        o = of_ref[0, :, cols] + ob_ref[0, :, cols]
        o = o * lax.rsqrt(jnp.mean(o * o, axis=-1, keepdims=True) + RMS_EPS)
        gated = o * nw_ref[:, cols] * _silu(rg_ref[0, :, cols].astype(F32))
        y = y + _dot(gated.astype(BF16), wo_ref[512 + hd * GLA_DV:512 + (hd + 1) * GLA_DV, :])
    _post_norm_and_route(h_ref[0], y, mod_ref, lnw_ref, lnb_ref, rwh_ref, rwl_ref, rb_ref, h1_ref, tok_ref, lg_ref)


def outproj_even(a, o_f, o_b, p, norm_w, w_out, h, mod, lnw, lnb, rwh, rwl, rb):
    B, T, _ = h.shape
    tm = min(512, T)
    tile = lambda b, j: (b, j, 0)
    shapes, ospecs = _epilogue_outs(B, T, tm)
    return pl.pallas_call(
        _outproj_even_kernel,
        out_shape=shapes,
        grid=(B, T // tm),
        in_specs=[pl.BlockSpec((1, tm, 512), tile), pl.BlockSpec((1, tm, 512), tile), pl.BlockSpec((1, tm, 512), tile),
                  pl.BlockSpec((1, tm, 512), lambda b, j: (b, j, C_RG // 512)),
                  pl.BlockSpec((1, 512), lambda b, j: (0, 0)),
                  pl.BlockSpec((D, D), lambda b, j: (0, 0))] + _epilogue_specs(tm, mod.shape[0]),
        out_specs=ospecs,
        compiler_params=_cparams(("parallel", "arbitrary")),
        name="outproj_even",
    )(a, o_f, o_b, p, norm_w, w_out, h, mod, lnw, lnb, rwh, rwl, rb)


def _inproj_odd_kernel(h_ref, mod_ref, w_ref, o_ref):
    u = (h_ref[0] * (1.0 + mod_ref[0, 1:2, :]) + mod_ref[0, 0:1, :]).astype(BF16)
    o_ref[0, :, 0:D] = _dot(u, w_ref[:, 0:D]).astype(BF16)
    o_ref[0, :, D:2 * D] = (_dot(u, w_ref[:, D:2 * D]) * _dot(u, w_ref[:, 2 * D:3 * D])).astype(BF16)


def inproj_odd(h, mod, w):
    B, T, _ = h.shape
    tm = min(512, T)
    mb = mod.shape[0]
    return pl.pallas_call(
        _inproj_odd_kernel,
        out_shape=jax.ShapeDtypeStruct((B, T, 2 * D), BF16),
        grid=(B, T // tm),
        in_specs=[pl.BlockSpec((1, tm, D), lambda b, j: (b, j, 0)),
                  pl.BlockSpec((1, 6, D), (lambda b, j: (b, 0, 0)) if mb > 1 else (lambda b, j: (0, 0, 0))),
                  pl.BlockSpec((D, 3 * D), lambda b, j: (0, 0))],
        out_specs=pl.BlockSpec((1, tm, 2 * D), lambda b, j: (b, j, 0)),
        compiler_params=_cparams(("parallel", "arbitrary")),
        name="inproj_odd",
    )(h, mod, w)


HALO = 16


def _outproj_odd_kernel(gb_ref, z_ref, zp_ref, zn_ref, cw_ref, cb_ref, wo_ref,
                        h_ref, mod_ref, lnw_ref, lnb_ref, rwh_ref, rwl_ref, rb_ref,
                        h1_ref, tok_ref, lg_ref):
    j = pl.program_id(1)
    tm = z_ref.shape[1]
    z = z_ref[0].astype(F32)
    prev_row = jnp.where(j > 0, zp_ref[0, HALO - 1:HALO, :].astype(F32), 0.0)
    next_row = jnp.where(j < pl.num_programs(1) - 1, zn_ref[0, 0:1, :].astype(F32), 0.0)
    row = lax.broadcasted_iota(jnp.int32, (tm, D), 0)
    z_prev = jnp.where(row == 0, prev_row, pltpu.roll(z, 1, 0))
    z_next = jnp.where(row == tm - 1, next_row, pltpu.roll(z, tm - 1, 0))
    conv = z_prev * cw_ref[0:1, :] + z * cw_ref[1:2, :] + z_next * cw_ref[2:3, :] + cb_ref[...]
    y = _dot((gb_ref[0].astype(F32) * conv).astype(BF16), wo_ref[...])
    _post_norm_and_route(h_ref[0], y, mod_ref, lnw_ref, lnb_ref, rwh_ref, rwl_ref, rb_ref, h1_ref, tok_ref, lg_ref)


def outproj_odd(gz, conv_w, conv_b, w_out, h, mod, lnw, lnb, rwh, rwl, rb):
    B, T, _ = h.shape
    tm = min(512, T)
    r = tm // HALO
    nh = T // HALO
    shapes, ospecs = _epilogue_outs(B, T, tm)
    return pl.pallas_call(
        _outproj_odd_kernel,
        out_shape=shapes,
        grid=(B, T // tm),
        in_specs=[pl.BlockSpec((1, tm, D), lambda b, j: (b, j, 0)),
                  pl.BlockSpec((1, tm, D), lambda b, j: (b, j, 1)),
                  pl.BlockSpec((1, HALO, D), lambda b, j: (b, jnp.maximum(j * r - 1, 0), 1)),
                  pl.BlockSpec((1, HALO, D), lambda b, j: (b, jnp.minimum((j + 1) * r, nh - 1), 1)),
                  pl.BlockSpec((3, D), lambda b, j: (0, 0)),
                  pl.BlockSpec((1, D), lambda b, j: (0, 0)),
                  pl.BlockSpec((D, D), lambda b, j: (0, 0))] + _epilogue_specs(tm, mod.shape[0]),
        out_specs=ospecs,
        compiler_params=_cparams(("parallel", "arbitrary")),
        name="outproj_odd",
    )(gz, gz, gz, gz, conv_w, conv_b, w_out, h, mod, lnw, lnb, rwh, rwl, rb)


def _row_copy_wait(src_ref, dst_ref, sem):
    pltpu.make_async_copy(src_ref, dst_ref, sem).wait()


def _dispatch_kernel(dest_ref, fill_ref, *rest, tiles):
    tok_refs, (xs_ref, zero_sc, sem) = rest[:len(tiles)], rest[len(tiles):]
    i = pl.program_id(0)
    nf = zero_sc.shape[0]

    @pl.when(i == 0)
    def _():
        zero_sc[...] = jnp.zeros_like(zero_sc)

    first = 0
    for s, (tok_ref, nt) in enumerate(zip(tok_refs, tiles)):
        tm = tok_ref.shape[0]

        @pl.when(jnp.logical_and(i >= first, i < first + nt))
        def _(tok_ref=tok_ref, tm=tm, s=s):
            def body(r, carry):
                for k in range(TOP_K):
                    pltpu.make_async_copy(tok_ref.at[pl.ds(r, 1)], xs_ref.at[pl.ds(dest_ref[k, r], 1)], sem).start()
                return carry

            lax.fori_loop(0, tm, body, 0, unroll=8)
            if s == 0:
                def fill(r, carry):
                    pltpu.make_async_copy(zero_sc.at[pl.ds(r, 1)], xs_ref.at[pl.ds(fill_ref[0, 0, r], 1)], sem).start()
                    return carry

                lax.fori_loop(0, nf, fill, 0, unroll=8)
                _row_copy_wait(zero_sc, xs_ref.at[pl.ds(0, nf)], sem)
            for k in range(TOP_K):
                _row_copy_wait(tok_ref, xs_ref.at[pl.ds(0, tm)], sem)

        first += nt


def moe_dispatch(toks, dest, n_rows, fill_rows):
    tm = min(256, toks[0].shape[0])
    tiles = tuple(t.shape[0] // tm for t in toks)
    starts = tuple(sum(tiles[:s]) for s in range(len(tiles)))
    nf = fill_rows.shape[0] // tiles[0]
    tok_specs = [pl.BlockSpec((tm, D), lambda i, s0=s0, nt=nt: (jnp.clip(i - s0, 0, nt - 1), 0))
                 for s0, nt in zip(starts, tiles)]
    return pl.pallas_call(
        functools.partial(_dispatch_kernel, tiles=tiles),
        out_shape=jax.ShapeDtypeStruct((n_rows, D), F32),
        grid=(sum(tiles),),
        in_specs=[pl.BlockSpec((TOP_K, tm), lambda i: (0, i), memory_space=pltpu.SMEM),
                  pl.BlockSpec((1, 1, nf), lambda i: (jnp.minimum(i, tiles[0] - 1), 0, 0), memory_space=pltpu.SMEM)]
                 + tok_specs,
        out_specs=pl.BlockSpec(memory_space=pl.ANY),
        scratch_shapes=[pltpu.VMEM((nf, D), F32), pltpu.SemaphoreType.DMA(())],
        compiler_params=_cparams(("arbitrary",)),
        name="moe_dispatch",
    )(dest, fill_rows.reshape(tiles[0], 1, nf), *toks)


def _ffn_kernel(be_ref, nv_ref, x_ref, w1_ref, w3_ref, w2_ref, y_ref, w1b, w3b, w2b):
    i = pl.program_id(0)
    changed = jnp.logical_or(i == 0, be_ref[i] != be_ref[jnp.maximum(i - 1, 0)])

    @pl.when(changed)
    def _():
        w1b[...] = w1_ref[0, 0].astype(BF16)
        w3b[...] = w3_ref[0, 0].astype(BF16)
        w2b[...] = w2_ref[0, 0].astype(BF16)

    @pl.when(i < nv_ref[0])
    def _():
        x = x_ref[...].astype(BF16)
        mid = _silu(_dot(x, w1b[...])) * _dot(x, w3b[...])
        y_ref[...] = _dot(mid.astype(BF16), w2b[...])

    @pl.when(i >= nv_ref[0])
    def _():
        y_ref[...] = jnp.zeros_like(y_ref)


def moe_ffn(xs, blk_expert, n_valid, w1, w3, w2, layer):
    n_rows = xs.shape[0]
    nb = n_rows // MOE_TM
    return pl.pallas_call(
        _ffn_kernel,
        out_shape=jax.ShapeDtypeStruct((n_rows, D), F32),
        grid_spec=pltpu.PrefetchScalarGridSpec(
            num_scalar_prefetch=2,
            grid=(nb,),
            in_specs=[pl.BlockSpec((MOE_TM, D), lambda i, be, nv: (i, 0)),
                      pl.BlockSpec((1, 1, D, D_EXPERT), lambda i, be, nv: (layer, be[i], 0, 0)),
                      pl.BlockSpec((1, 1, D, D_EXPERT), lambda i, be, nv: (layer, be[i], 0, 0)),
                      pl.BlockSpec((1, 1, D_EXPERT, D), lambda i, be, nv: (layer, be[i], 0, 0))],
            out_specs=pl.BlockSpec((MOE_TM, D), lambda i, be, nv: (i, 0)),
            scratch_shapes=[pltpu.VMEM((D, D_EXPERT), BF16), pltpu.VMEM((D, D_EXPERT), BF16),
                            pltpu.VMEM((D_EXPERT, D), BF16)]),
        compiler_params=_cparams(("arbitrary",)),
        name="moe_ffn",
    )(blk_expert, n_valid, xs, w1, w3, w2)


def _combine_kernel(dest_ref, h_ref, wt_ref, mod_ref, lnw_ref, lnb_ref, y_ref, o_ref, buf, sem):
    tm = h_ref.shape[0]

    def body(r, carry):
        for k in range(TOP_K):
            pltpu.make_async_copy(y_ref.at[pl.ds(dest_ref[k, r], 1)], buf.at[k, pl.ds(r, 1)], sem).start()
        return carry

    lax.fori_loop(0, tm, body, 0, unroll=8)
    for k in range(TOP_K):
        _row_copy_wait(y_ref.at[pl.ds(0, tm)], buf.at[k], sem)
    y = wt_ref[:, 0:1] * buf[0] + wt_ref[:, 1:2] * buf[1]
    o_ref[...] = _layer_norm(ALPHA * h_ref[...] + mod_ref[0, 5:6, :] * y, lnw_ref[...], lnb_ref[...])


def moe_combine(h1, y_sorted, dest, wts, tile_off, mod, lnw, lnb):
    B, T, _ = h1.shape
    N = B * T
    tm = min(256, N)
    per_b = T // tm
    mod_map = (lambda i: (i // per_b, 0, 0)) if mod.shape[0] > 1 else (lambda i: (0, 0, 0))
    out = pl.pallas_call(
        _combine_kernel,
        out_shape=jax.ShapeDtypeStruct((N, D), F32),
        grid=(N // tm,),
        in_specs=[pl.BlockSpec((TOP_K, tm), lambda i: (0, i + tile_off), memory_space=pltpu.SMEM),
                  pl.BlockSpec((tm, D), lambda i: (i, 0)),
                  pl.BlockSpec((tm, TOP_K), lambda i: (i + tile_off, 0)),
                  pl.BlockSpec((1, 6, D), mod_map),
                  pl.BlockSpec((1, D), lambda i: (0, 0)), pl.BlockSpec((1, D), lambda i: (0, 0)),
                  pl.BlockSpec(memory_space=pl.ANY)],
        out_specs=pl.BlockSpec((tm, D), lambda i: (i, 0)),
        scratch_shapes=[pltpu.VMEM((TOP_K, tm, D), F32), pltpu.SemaphoreType.DMA(())],
        compiler_params=_cparams(("arbitrary",)),
        name="moe_combine",
    )(dest, h1.reshape(N, D), wts, mod, lnw, lnb, y_sorted)
    return out.reshape(B, T, D)


def _route_kernel(lg_ref, dest_ref, wt_ref, cnt_ref, tri_sc, start_sc, run_sc):
    ph, i = pl.program_id(0), pl.program_id(1)
    tr = lg_ref.shape[1]

    @pl.when(jnp.logical_and(ph == 0, i == 0))
    def _():
        r = lax.broadcasted_iota(jnp.int32, (tr, tr), 0)
        c = lax.broadcasted_iota(jnp.int32, (tr, tr), 1)
        tri_sc[...] = (r < c).astype(BF16)
        start_sc[...] = jnp.zeros_like(start_sc)
        run_sc[...] = jnp.zeros_like(run_sc)

    @pl.when(jnp.logical_and(ph == 1, i == 0))
    def _():
        cnt = run_sc[...].astype(jnp.int32)
        cnt_ref[...] = cnt
        padded = jnp.bitwise_and(cnt + (MOE_TM - 1), -MOE_TM)
        row = lax.broadcasted_iota(jnp.int32, padded.shape, 0)
        acc = padded
        for s in (1, 2, 4, 8, 16):
            acc = acc + jnp.where(row >= s, pltpu.roll(acc, s, 0), 0)
        start_sc[...] = (acc - padded).astype(F32)
        run_sc[...] = jnp.zeros_like(run_sc)

    lg = lg_ref[...]
    gl = lg[N_EXPERTS:N_EXPERTS + N_GROUPS]
    gmax = jnp.max(gl, axis=0, keepdims=True)
    sub4 = lax.broadcasted_iota(jnp.int32, gl.shape, 0)
    g_sel = jnp.min(jnp.where(gl == gmax, sub4, N_GROUPS), axis=0, keepdims=True)
    p_group = 1.0 / jnp.sum(jnp.exp(gl - gmax), axis=0, keepdims=True)
    el = lg[0:EXPERTS_PER_GROUP]
    for g in range(1, N_GROUPS):
        el = jnp.where(g_sel == g, lg[g * EXPERTS_PER_GROUP:(g + 1) * EXPERTS_PER_GROUP], el)
    sub8 = lax.broadcasted_iota(jnp.int32, el.shape, 0)
    e1 = jnp.max(el, axis=0, keepdims=True)
    i1 = jnp.min(jnp.where(el == e1, sub8, EXPERTS_PER_GROUP), axis=0, keepdims=True)
    rest = jnp.where(sub8 == i1, -jnp.inf, el)
    e2 = jnp.max(rest, axis=0, keepdims=True)
    i2 = jnp.min(jnp.where(rest == e2, sub8, EXPERTS_PER_GROUP), axis=0, keepdims=True)
    den = jnp.sum(jnp.exp(el - e1), axis=0, keepdims=True)
    p1 = 1.0 / den
    p2 = jnp.exp(e2 - e1) / den
    wt_ref[0:1, :] = p_group * p1 / (p1 + p2)
    wt_ref[1:2, :] = p_group * p2 / (p1 + p2)

    sub32 = lax.broadcasted_iota(jnp.int32, (N_EXPERTS, tr), 0)
    oh = [(sub32 == g_sel * EXPERTS_PER_GROUP + ix).astype(F32) for ix in (i1, i2)]
    cnt = [jnp.sum(o, axis=1, keepdims=True) for o in oh]
    before = start_sc[:, 0:1] + run_sc[:, 0:1]
    for k in range(TOP_K):
        prior = _dot(oh[k].astype(BF16), tri_sc[...]) + before + (cnt[0] if k == 1 else 0.0)
        dest_ref[k:k + 1, :] = jnp.sum(oh[k] * prior, axis=0, keepdims=True).astype(jnp.int32)
    run_sc[...] = run_sc[...] + (cnt[0] + cnt[1])


def moe_route(logits_t):
    N = logits_t.shape[1]
    tr = 512 if N % 512 == 0 else 256
    return pl.pallas_call(
        _route_kernel,
        out_shape=(jax.ShapeDtypeStruct((TOP_K, N), jnp.int32), jax.ShapeDtypeStruct((TOP_K, N), F32),
                   jax.ShapeDtypeStruct((N_EXPERTS, LANE), jnp.int32)),
        grid=(2, N // tr),
        in_specs=[pl.BlockSpec((ROUTE_ROWS, tr), lambda p, i: (0, i))],
        out_specs=(pl.BlockSpec((TOP_K, tr), lambda p, i: (0, i * p)), pl.BlockSpec((TOP_K, tr), lambda p, i: (0, i * p)),
                   pl.BlockSpec((N_EXPERTS, LANE), lambda p, i: (0, 0))),
        scratch_shapes=[pltpu.VMEM((tr, tr), BF16), pltpu.VMEM((N_EXPERTS, LANE), F32),
                        pltpu.VMEM((N_EXPERTS, LANE), F32)],
        compiler_params=_cparams(("arbitrary", "arbitrary")),
        name="moe_route",
    )(logits_t)


def _block_tables(counts, n_assign):
    padded = (counts + MOE_TM - 1) // MOE_TM * MOE_TM
    pad_end = jnp.cumsum(padded)
    pad_start = pad_end - padded
    nb = -(-n_assign // MOE_TM) + N_EXPERTS
    blk_start = jnp.arange(nb, dtype=jnp.int32) * MOE_TM
    blk_expert = jnp.minimum(jnp.sum((pad_end[None, :] <= blk_start[:, None]).astype(jnp.int32), axis=1), N_EXPERTS - 1)
    n_valid = (pad_end[-1] // MOE_TM).astype(jnp.int32).reshape(1)
    n_fill = nb * MOE_TM - n_assign
    gap = padded - counts
    gap_end = jnp.cumsum(gap)
    k = jnp.arange(n_fill, dtype=jnp.int32)[:, None]
    sel = jnp.logical_and(k >= (gap_end - gap)[None, :], k < gap_end[None, :])
    in_gap = jnp.sum(jnp.where(sel, (pad_start + counts - (gap_end - gap))[None, :] + k, 0), axis=1)
    fill = jnp.where(k[:, 0] < gap_end[-1], in_gap, pad_end[-1] + k[:, 0] - gap_end[-1])
    return blk_expert.astype(jnp.int32), n_valid, fill.astype(jnp.int32), nb * MOE_TM


def hier_moe_and_norm(streams, w1, w3, w2, layer, lnw, lnb):
    logits_t = streams[0][2] if len(streams) == 1 else jnp.concatenate([s[2] for s in streams], 1)
    dest, wts_t, counts = moe_route(logits_t)
    wts = wts_t.T
    blk_expert, n_valid, fill_rows, n_rows = _block_tables(counts[:, 0], TOP_K * logits_t.shape[1])
    offs, off = [], 0
    for h1, _, _, _ in streams:
        n = h1.shape[0] * h1.shape[1]
        offs.append(off // min(256, n))
        off += n
    xs = moe_dispatch([s[1].reshape(-1, D) for s in streams], dest, n_rows, fill_rows)
    y = moe_ffn(xs, blk_expert, n_valid, w1, w3, w2, layer)
    return [moe_combine(h1, y, dest, wts, o, mod, lnw, lnb) for (h1, _, _, mod), o in zip(streams, offs)]


def _even_weight_columns():
    n = np.arange(HEAD_DIM)
    perm = (n % 32) // 16 * 32 + n // 32 * 16 + n % 16
    idx = np.zeros(P_W, np.int32)
    scale = np.zeros(P_W, np.float32)
    for hd in range(A_Q_HEADS):
        idx[C_QA + hd * 64:C_QA + (hd + 1) * 64] = hd * 64 + perm
    scale[C_QA:C_QA + 512] = HEAD_DIM ** -0.5
    for g in range(A_KV_HEADS):
        for rep in range(2):
            o = g * 128 + rep * 64
            idx[C_KD + o:C_KD + o + 64] = 512 + g * 64 + perm
            idx[C_VD + o:C_VD + o + 64] = 640 + g * 64 + n
    scale[C_KD:C_VD + 256] = 1.0
    idx[C_QG:C_QG + 256] = 768 + np.arange(256)
    scale[C_QG:C_QG + 256] = GLA_DK ** -0.5
    idx[C_KG:C_KG + 256] = 1024 + np.arange(256)
    idx[C_VG:C_VG + 512] = 1280 + np.arange(512)
    idx[C_RG:C_RG + 512] = 1792 + np.arange(512)
    scale[C_KG:C_KG + 256] = 1.0
    scale[C_VG:C_RG + 512] = 1.0
    for d in range(2):
        idx[C_GG + d * 128:C_GG + d * 128 + GLA_RANK] = 2304 + d * GLA_RANK + np.arange(GLA_RANK)
        scale[C_GG + d * 128:C_GG + d * 128 + GLA_RANK] = 1.0
    return idx, scale


def _rope_tables(S):
    row = jnp.repeat(jnp.arange(S // GRID_W), GRID_W).astype(F32)
    col = jnp.tile(jnp.arange(GRID_W), S // GRID_W).astype(F32)
    axis_dim = HEAD_DIM // 2
    inv_freq = ROPE_BASE ** (-jnp.arange(0, axis_dim, 2, dtype=F32) / axis_dim)
    ang = jnp.concatenate([row[:, None] * inv_freq, col[:, None] * inv_freq], -1)
    cos, sin = jnp.cos(ang), jnp.sin(ang)
    cos_t = jnp.tile(cos, (1, 4))
    sin_t = jnp.tile(jnp.concatenate([-sin, sin], -1), (1, 2))
    return cos_t, sin_t


def _router_weights(wg, bg, we, be):
    w = jnp.zeros((D, ROUTE_W), F32).at[:, :N_EXPERTS].set(we).at[:, N_EXPERTS:N_EXPERTS + N_GROUPS].set(wg)
    b = jnp.zeros((1, ROUTE_W), F32).at[0, :N_EXPERTS].set(be).at[0, N_EXPERTS:N_EXPERTS + N_GROUPS].set(bg)
    hi = w.astype(BF16)
    return hi, (w - hi.astype(F32)).astype(BF16), b


def kernel(x, c, ctx, c_ctx, w_in_even, w_out_even, attn_sink, gla_wa2, gla_ba, gla_norm_w, w_in_odd, conv_w, conv_b, w_out_odd, ada_w, ada_b, ln_w, ln_b, router_wg, router_bg, router_we, router_be, moe_w1, moe_w3, moe_w2):
    B, S, _ = x.shape
    L = ctx.shape[1]
    cos_t, sin_t = _rope_tables(S)
    cos_c, sin_c = jnp.ones((L, LANE), F32), jnp.zeros((L, LANE), F32)
    col_idx, col_scale = _even_weight_columns()

    n_cond = -(-(B + 1) // 8) * 8
    cc = jnp.zeros((n_cond, D), F32).at[:B].set(c).at[B].set(c_ctx)
    mods = ada_modulation_all(cc, ada_w, ada_b).reshape(DEPTH, n_cond, 6, D)

    h_lat, h_ctx = x, ctx
    for l in range(DEPTH):
        i = l // 2
        need_ctx = any(j % 2 == 0 for j in range(l + 1, DEPTH))
        m_lat = mods[l, :B]
        m_ctx = mods[l, B:B + 1]
        lnw0, lnb0 = ln_w[l, 0:1], ln_b[l, 0:1]
        lnw1, lnb1 = ln_w[l, 1:2], ln_b[l, 1:2]
        rwh, rwl, rb = _router_weights(router_wg[l], router_bg[l], router_we[l], router_be[l])
        streams = []
        if l % 2 == 0:
            w_in = (w_in_even[i][:, col_idx] * col_scale[None, :]).astype(BF16)
            w_out = w_out_even[i].astype(BF16)
            wa_p = jnp.zeros((2, LANE, GLA_HEADS * GLA_DK), F32).at[:, :GLA_RANK].set(gla_wa2[i]).astype(BF16)
            ba = gla_ba[i].reshape(2, 1, -1)
            nw = gla_norm_w[i].reshape(1, -1)
            p_ctx = inproj_even(h_ctx, m_ctx, w_in, cos_c, sin_c)
            p_lat = inproj_even(h_lat, m_lat, w_in, cos_t, sin_t)
            a_lat = attention(p_lat, p_ctx, attn_sink[i], True)
            s0 = jnp.zeros((B, 2, 2, GLA_DV, LANE), F32)
            oc_f, oc_b, s_ctx = gla_scan(p_ctx, wa_p, ba, s0)
            ol_f, ol_b, _ = gla_scan(p_lat, wa_p, ba, s_ctx)
            streams.append(outproj_even(a_lat, ol_f, ol_b, p_lat, nw, w_out, h_lat, m_lat, lnw0, lnb0, rwh, rwl, rb)
                           + (m_lat,))
            if need_ctx:
                a_ctx = attention(p_ctx, p_ctx, attn_sink[i], False)
                streams.append(outproj_even(a_ctx, oc_f, oc_b, p_ctx, nw, w_out, h_ctx, m_ctx, lnw0, lnb0,
                                            rwh, rwl, rb) + (m_ctx,))
        else:
            w_in = w_in_odd[i].astype(BF16)
            w_out = w_out_odd[i].astype(BF16)
            cb = conv_b[i].reshape(1, D)
            pairs = [(h_lat, m_lat)] + ([(h_ctx, m_ctx)] if need_ctx else [])
            for h, m in pairs:
                gz = inproj_odd(h, m, w_in)
                streams.append(outproj_odd(gz, conv_w[i], cb, w_out, h, m, lnw0, lnb0, rwh, rwl, rb) + (m,))
        outs = hier_moe_and_norm(streams, moe_w1, moe_w3, moe_w2, l, lnw1, lnb1)
        h_lat = outs[0]
        if need_ctx:
            h_ctx = outs[1]
    return h_lat
```

```python
import functools

import numpy as np
import jax
import jax.numpy as jnp
from jax import lax
from jax.experimental import pallas as pl
from jax.experimental.pallas import tpu as pltpu
from jax.experimental.pallas import tpu_sc as plsc

F32 = jnp.float32
BF16 = jnp.bfloat16
HIGHEST = lax.Precision.HIGHEST

D = 1024
DEPTH = 4
GRID_W = 64
HEAD_DIM = 64
A_Q_HEADS = 8
A_KV_HEADS = 2
WINDOW = 128
ROPE_BASE = 10000.0
GLA_HEADS = 4
GLA_DK = 64
GLA_DV = 128
GLA_RANK = 16
GLA_TAU = 16.0
GLA_CHUNK = 64
N_GROUPS = 4
EXPERTS_PER_GROUP = 8
N_EXPERTS = 32
TOP_K = 2
D_EXPERT = 512
ALPHA = (2.0 * DEPTH) ** 0.25
LN_EPS = 1e-5
RMS_EPS = 1e-6

LANE = 128
VMEM_LIMIT = 48 * 1024 * 1024

C_QA, C_KD, C_VD, C_VG, C_RG, C_QG, C_KG, C_GG = 0, 512, 768, 1024, 1536, 2048, 2304, 2560
P_W = 2816
ROUTE_W = 128
ROUTE_ROWS = 40
MOE_TM = 256
SC_CORES, SC_SUBCORES = 2, 16
SC_WORKERS = SC_CORES * SC_SUBCORES
SC_CHUNK = 64
NEG = -1e30


def _cparams(sem):
    return pltpu.CompilerParams(dimension_semantics=sem, vmem_limit_bytes=VMEM_LIMIT)


def _dot(a, b):
    return jnp.dot(a, b, preferred_element_type=F32)


def _dot_nt(a, b):
    return lax.dot_general(a, b, (((1,), (1,)), ((), ())), preferred_element_type=F32)


def _dot_tn(a, b):
    return lax.dot_general(a, b, (((0,), (0,)), ((), ())), preferred_element_type=F32)


def _silu(x):
    return x * (1.0 / (1.0 + jnp.exp(-x)))


def _ada_kernel(c_ref, w_ref, b_ref, o_ref):
    s = _silu(c_ref[...])
    o_ref[0] = jnp.dot(s, w_ref[0], precision=HIGHEST, preferred_element_type=F32) + b_ref[0]


def ada_modulation_all(cc, ada_w, ada_b):
    R = cc.shape[0]
    tn = 1536
    return pl.pallas_call(
        _ada_kernel,
        out_shape=jax.ShapeDtypeStruct((DEPTH, R, 6 * D), F32),
        grid=(DEPTH, 6 * D // tn),
        in_specs=[pl.BlockSpec((R, D), lambda l, n: (0, 0)),
                  pl.BlockSpec((1, D, tn), lambda l, n: (l, 0, n)),
                  pl.BlockSpec((1, 1, tn), lambda l, n: (l, 0, n))],
        out_specs=pl.BlockSpec((1, R, tn), lambda l, n: (l, 0, n)),
        compiler_params=_cparams(("arbitrary", "arbitrary")),
        name="ada_modulation",
    )(cc, ada_w, ada_b.reshape(DEPTH, 1, 6 * D))


_EVEN_CHUNKS = ((0, 512, True), (512, 768, True), (768, 1024, False), (1024, 1536, False),
                (1536, 2048, False), (2048, 2560, False), (2560, 2816, False))


def _inproj_even_kernel(h_ref, mod_ref, w_ref, cos_ref, sin_ref, p_ref):
    tm = h_ref.shape[1]
    u = (h_ref[0] * (1.0 + mod_ref[0, 1:2, :]) + mod_ref[0, 0:1, :]).astype(BF16)
    cos = cos_ref[...]
    sin = sin_ref[...]
    lane = lax.broadcasted_iota(jnp.int32, (tm, LANE), 1)
    first_half = (lane % HEAD_DIM) < (HEAD_DIM // 2)
    for c0, c1, rope in _EVEN_CHUNKS:
        acc = _dot(u, w_ref[:, c0:c1])
        if rope:
            for i in range((c1 - c0) // LANE):
                x = acc[:, i * LANE:(i + 1) * LANE]
                partner = jnp.where(first_half, pltpu.roll(x, LANE - 32, 1), pltpu.roll(x, 32, 1))
                p_ref[0, :, c0 + i * LANE:c0 + (i + 1) * LANE] = (x * cos + partner * sin).astype(BF16)
        else:
            p_ref[0, :, c0:c1] = acc.astype(BF16)


def inproj_even(h, mod, w, cos_t, sin_t):
    B, T, _ = h.shape
    tm = min(512, T)
    mb = mod.shape[0]
    return pl.pallas_call(
        _inproj_even_kernel,
        out_shape=jax.ShapeDtypeStruct((B, T, P_W), BF16),
        grid=(B, T // tm),
        in_specs=[pl.BlockSpec((1, tm, D), lambda b, j: (b, j, 0)),
                  pl.BlockSpec((1, 6, D), (lambda b, j: (b, 0, 0)) if mb > 1 else (lambda b, j: (0, 0, 0))),
                  pl.BlockSpec((D, P_W), lambda b, j: (0, 0)),
                  pl.BlockSpec((tm, LANE), lambda b, j: (j, 0)),
                  pl.BlockSpec((tm, LANE), lambda b, j: (j, 0))],
        out_specs=pl.BlockSpec((1, tm, P_W), lambda b, j: (b, j, 0)),
        compiler_params=_cparams(("parallel", "arbitrary")),
        name="inproj_even",
    )(h, mod, w, cos_t, sin_t)


def _attn_kernel(*refs, tq, has_window):
    if has_window:
        sink_ref, q_ref, kw_ref, vw_ref, kc_ref, vc_ref, o_ref = refs
    else:
        sink_ref, q_ref, kc_ref, vc_ref, o_ref = refs
    lane = lax.broadcasted_iota(jnp.int32, (tq, LANE), 1)
    lo = lane < HEAD_DIM
    if has_window:
        S = kw_ref.shape[1]
        wk = tq + 2 * WINDOW
        q0 = pl.program_id(1) * tq
        wstart = pl.multiple_of(jnp.clip(q0 - WINDOW, 0, S - wk), LANE)
        qpos = q0 + lax.broadcasted_iota(jnp.int32, (tq, wk), 0)
        kpos = wstart + lax.broadcasted_iota(jnp.int32, (tq, wk), 1)
        band = jnp.abs(qpos - kpos) <= WINDOW
    for g in range(A_KV_HEADS):
        cols = slice(g * LANE, (g + 1) * LANE)
        kc = kc_ref[0, :, cols]
        vc = vc_ref[0, :, cols]
        if has_window:
            kw = kw_ref[0, pl.ds(wstart, wk), cols]
            vw = vw_ref[0, pl.ds(wstart, wk), cols]
        for pr in range(2):
            blk = slice((2 * g + pr) * LANE, (2 * g + pr + 1) * LANE)
            qblk = q_ref[0, :, blk]
            res = []
            for hh in range(2):
                snk = sink_ref[4 * g + 2 * pr + hh]
                qm = jnp.where(lo if hh == 0 else jnp.logical_not(lo), qblk, jnp.zeros_like(qblk))
                sc = _dot_nt(qm, kc)
                m = jnp.maximum(jnp.max(sc, axis=-1, keepdims=True), snk)
                if has_window:
                    sw = jnp.where(band, _dot_nt(qm, kw), NEG)
                    m = jnp.maximum(m, jnp.max(sw, axis=-1, keepdims=True))
                pc = jnp.exp(sc - m)
                den = jnp.sum(pc, axis=-1, keepdims=True) + jnp.exp(snk - m)
                o = _dot(pc.astype(BF16), vc)
                if has_window:
                    pw = jnp.exp(sw - m)
                    den = den + jnp.sum(pw, axis=-1, keepdims=True)
                    o = o + _dot(pw.astype(BF16), vw)
                res.append(o / den)
            o_ref[0, :, blk] = jnp.where(lo, res[0], res[1]).astype(BF16)


def attention(p_q, p_ctx, sink, has_window):
    B, T, _ = p_q.shape
    L = p_ctx.shape[1]
    tq = 128
    in_specs = [pl.BlockSpec(memory_space=pltpu.SMEM),
                pl.BlockSpec((1, tq, 512), lambda b, j: (b, j, C_QA // 512))]
    args = [sink, p_q]
    if has_window:
        in_specs += [pl.BlockSpec((1, T, 256), lambda b, j: (b, 0, C_KD // 256)),
                     pl.BlockSpec((1, T, 256), lambda b, j: (b, 0, C_VD // 256))]
        args += [p_q, p_q]
    in_specs += [pl.BlockSpec((1, L, 256), lambda b, j: (b, 0, C_KD // 256)),
                 pl.BlockSpec((1, L, 256), lambda b, j: (b, 0, C_VD // 256))]
    args += [p_ctx, p_ctx]
    return pl.pallas_call(
        functools.partial(_attn_kernel, tq=tq, has_window=has_window),
        out_shape=jax.ShapeDtypeStruct((B, T, 512), BF16),
        grid=(B, T // tq),
        in_specs=in_specs,
        out_specs=pl.BlockSpec((1, tq, 512), lambda b, j: (b, j, 0)),
        compiler_params=_cparams(("parallel", "arbitrary")),
        name="window_attention" if has_window else "context_attention",
    )(*args)


def _log_sigmoid(x):
    return jnp.minimum(x, 0.0) - jnp.log(1.0 + jnp.exp(-jnp.abs(x)))


def _gla_kernel(qf_ref, kf_ref, vf_ref, gf_ref, qb_ref, kb_ref, vb_ref, gb_ref, wa_ref, ba_ref, s0_ref,
                of_ref, ob_ref, sfin_ref, s_sc):
    j = pl.program_id(1)
    nblk = pl.num_programs(1)
    tb = qf_ref.shape[1]
    nc = tb // GLA_CHUNK

    @pl.when(j == 0)
    def _():
        s_sc[...] = s0_ref[0]

    C = GLA_CHUNK
    ri = lax.broadcasted_iota(jnp.int32, (2 * C, C), 0) % C
    ci = lax.broadcasted_iota(jnp.int32, (2 * C, C), 1)
    lo = lax.broadcasted_iota(jnp.int32, (C, LANE), 1) < GLA_DK

    def per_head(x):
        zero = jnp.zeros_like(x)
        return jnp.concatenate([jnp.where(lo, x, zero), jnp.where(lo, zero, x)], axis=0)

    for d in range(2):
        q_ref, k_ref, v_ref, g_ref, o_ref = ((qf_ref, kf_ref, vf_ref, gf_ref, of_ref) if d == 0 else
                                             (qb_ref, kb_ref, vb_ref, gb_ref, ob_ref))
        causal = (ri >= ci) if d == 0 else (ci >= ri)
        tri = causal[0:C].astype(BF16)
        g = _dot(g_ref[0, :, d * LANE:(d + 1) * LANE], wa_ref[d]) + ba_ref[d]
        log_a = _log_sigmoid(g) / GLA_TAU
        la1 = log_a.astype(BF16)
        rem = log_a - la1.astype(F32)
        la2 = rem.astype(BF16)
        la3 = (rem - la2.astype(F32)).astype(BF16)
        for c in (range(nc) if d == 0 else range(nc - 1, -1, -1)):
            rows = slice(c * C, (c + 1) * C)
            b = _dot(tri, la1[rows]) + _dot(tri, la2[rows]) + _dot(tri, la3[rows])
            b_last = b[C - 1:C] if d == 0 else b[0:1]
            qc = q_ref[0, rows, :].astype(F32)
            kc = k_ref[0, rows, :].astype(F32)
            q_in = (qc * jnp.exp(b)).astype(BF16)
            k_in = (kc * jnp.exp(-b)).astype(BF16)
            k_st = (kc * jnp.exp(b_last - b)).astype(BF16)
            dl = jnp.exp(b_last)
            for pair in range(2):
                cols = slice(pair * LANE, (pair + 1) * LANE)
                st = s_sc[d, pair]
                q2 = per_head(q_in[:, cols])
                attn = jnp.where(causal, _dot_nt(q2, k_in[:, cols]), 0.0).astype(BF16)
                inter = _dot_nt(q2, st.astype(BF16))
                vs = [v_ref[0, rows, (2 * pair + hh) * GLA_DV:(2 * pair + hh + 1) * GLA_DV] for hh in range(2)]
                for hh in range(2):
                    hr = slice(hh * C, (hh + 1) * C)
                    o_ref[0, rows, (2 * pair + hh) * GLA_DV:(2 * pair + hh + 1) * GLA_DV] = (
                        _dot(attn[hr], vs[hh]) + inter[hr]).astype(o_ref.dtype)
                upd = _dot_tn(jnp.concatenate(vs, axis=0), per_head(k_st[:, cols]))
                s_sc[d, pair] = st * dl[:, cols] + upd

    @pl.when(j == nblk - 1)
    def _():
        sfin_ref[0] = s_sc[...]


def gla_scan(p, wa_p, ba, s0):
    B, T, _ = p.shape
    tb = min(512, T)
    nblk = T // tb
    fwd = lambda b, j: (b, j)
    bwd = lambda b, j: (b, nblk - 1 - j)

    def specs(im):
        return [pl.BlockSpec((1, tb, 256), lambda b, j: im(b, j) + (C_QG // 256,)),
                pl.BlockSpec((1, tb, 256), lambda b, j: im(b, j) + (C_KG // 256,)),
                pl.BlockSpec((1, tb, 512), lambda b, j: im(b, j) + (C_VG // 512,)),
                pl.BlockSpec((1, tb, 256), lambda b, j: im(b, j) + (C_GG // 256,))]

    return pl.pallas_call(
        _gla_kernel,
        out_shape=(jax.ShapeDtypeStruct((B, T, 512), F32), jax.ShapeDtypeStruct((B, T, 512), F32),
                   jax.ShapeDtypeStruct(s0.shape, F32)),
        grid=(B, nblk),
        in_specs=specs(fwd) + specs(bwd) + [
            pl.BlockSpec((2, LANE, 256), lambda b, j: (0, 0, 0)),
            pl.BlockSpec((2, 1, 256), lambda b, j: (0, 0, 0)),
            pl.BlockSpec((1, 2, 2, GLA_DV, LANE), lambda b, j: (b, 0, 0, 0, 0))],
        out_specs=(pl.BlockSpec((1, tb, 512), lambda b, j: (b, j, 0)),
                   pl.BlockSpec((1, tb, 512), lambda b, j: (b, nblk - 1 - j, 0)),
                   pl.BlockSpec((1, 2, 2, GLA_DV, LANE), lambda b, j: (b, 0, 0, 0, 0))),
        scratch_shapes=[pltpu.VMEM((2, 2, GLA_DV, LANE), F32)],
        compiler_params=_cparams(("parallel", "arbitrary")),
        name="gla_scan",
    )(p, p, p, p, p, p, p, p, wa_p, ba, s0)


def _layer_norm(r, w, b):
    mu = jnp.mean(r, axis=-1, keepdims=True)
    xc = r - mu
    var = jnp.mean(xc * xc, axis=-1, keepdims=True)
    return xc * lax.rsqrt(var + LN_EPS) * w + b


def _pack_bf16_pairs(x):
    half = x.shape[1] // 2
    lo = lax.bitcast_convert_type(x[:, :half].astype(BF16).astype(F32), jnp.uint32)
    hi = lax.bitcast_convert_type(x[:, half:].astype(BF16).astype(F32), jnp.uint32)
    return jnp.bitwise_or(hi, lax.shift_right_logical(lo, jnp.uint32(16)))


def _unpack_bf16_pairs(p):
    lo = lax.bitcast_convert_type(lax.shift_left(p, jnp.uint32(16)), F32)
    hi = lax.bitcast_convert_type(jnp.bitwise_and(p, jnp.uint32(0xFFFF0000)), F32)
    return lo, hi


def _post_norm_and_route(h, y, mod_ref, lnw_ref, lnb_ref, rwh_ref, rwl_ref, rb_ref, h1_ref, tok_ref, lg_ref):
    h1 = _layer_norm(ALPHA * h + mod_ref[0, 2:3, :] * y, lnw_ref[...], lnb_ref[...])
    h1_ref[0] = h1
    tok = h1 * (1.0 + mod_ref[0, 4:5, :]) + mod_ref[0, 3:4, :]
    tok_ref[0] = _pack_bf16_pairs(tok)
    hi = tok.astype(BF16)
    lo = (tok - hi.astype(F32)).astype(BF16)
    lg = _dot(hi, rwh_ref[...]) + _dot(lo, rwh_ref[...]) + _dot(hi, rwl_ref[...]) + rb_ref[...]
    lg_ref[...] = lg.T[0:ROUTE_ROWS, :]


def _epilogue_specs(tm, mb):
    mod_map = (lambda b, j: (b, 0, 0)) if mb > 1 else (lambda b, j: (0, 0, 0))
    const2 = lambda b, j: (0, 0)
    return [pl.BlockSpec((1, tm, D), lambda b, j: (b, j, 0)),
            pl.BlockSpec((1, 6, D), mod_map),
            pl.BlockSpec((1, D), const2), pl.BlockSpec((1, D), const2),
            pl.BlockSpec((D, ROUTE_W), const2), pl.BlockSpec((D, ROUTE_W), const2),
            pl.BlockSpec((1, ROUTE_W), const2)]


def _epilogue_outs(B, T, tm):
    nj = T // tm
    shapes = (jax.ShapeDtypeStruct((B, T, D), F32), jax.ShapeDtypeStruct((B, T, D // 2), jnp.uint32),
              jax.ShapeDtypeStruct((ROUTE_ROWS, B * T), F32))
    specs = (pl.BlockSpec((1, tm, D), lambda b, j: (b, j, 0)), pl.BlockSpec((1, tm, D // 2), lambda b, j: (b, j, 0)),
             pl.BlockSpec((ROUTE_ROWS, tm), lambda b, j: (0, b * nj + j)))
    return shapes, specs


def _outproj_even_kernel(a_ref, of_ref, ob_ref, rg_ref, nw_ref, wo_ref,
                         h_ref, mod_ref, lnw_ref, lnb_ref, rwh_ref, rwl_ref, rb_ref,
                         h1_ref, tok_ref, lg_ref):
    y = _dot(a_ref[0], wo_ref[0:512, :])
    for hd in range(GLA_HEADS):
        cols = slice(hd * GLA_DV, (hd + 1) * GLA_DV)
        o = of_ref[0, :, cols] + ob_ref[0, :, cols]
        o = o * lax.rsqrt(jnp.mean(o * o, axis=-1, keepdims=True) + RMS_EPS)
        gated = o * nw_ref[:, cols] * _silu(rg_ref[0, :, cols].astype(F32))
        y = y + _dot(gated.astype(BF16), wo_ref[512 + hd * GLA_DV:512 + (hd + 1) * GLA_DV, :])
    _post_norm_and_route(h_ref[0], y, mod_ref, lnw_ref, lnb_ref, rwh_ref, rwl_ref, rb_ref, h1_ref, tok_ref, lg_ref)


def outproj_even(a, o_f, o_b, p, norm_w, w_out, h, mod, lnw, lnb, rwh, rwl, rb):
    B, T, _ = h.shape
    tm = min(512, T)
    tile = lambda b, j: (b, j, 0)
    shapes, ospecs = _epilogue_outs(B, T, tm)
    return pl.pallas_call(
        _outproj_even_kernel,
        out_shape=shapes,
        grid=(B, T // tm),
        in_specs=[pl.BlockSpec((1, tm, 512), tile), pl.BlockSpec((1, tm, 512), tile), pl.BlockSpec((1, tm, 512), tile),
                  pl.BlockSpec((1, tm, 512), lambda b, j: (b, j, C_RG // 512)),
                  pl.BlockSpec((1, 512), lambda b, j: (0, 0)),
                  pl.BlockSpec((D, D), lambda b, j: (0, 0))] + _epilogue_specs(tm, mod.shape[0]),
        out_specs=ospecs,
        compiler_params=_cparams(("parallel", "arbitrary")),
        name="outproj_even",
    )(a, o_f, o_b, p, norm_w, w_out, h, mod, lnw, lnb, rwh, rwl, rb)


def _inproj_odd_kernel(h_ref, mod_ref, w_ref, o_ref):
    u = (h_ref[0] * (1.0 + mod_ref[0, 1:2, :]) + mod_ref[0, 0:1, :]).astype(BF16)
    o_ref[0, :, 0:D] = _dot(u, w_ref[:, 0:D]).astype(BF16)
    o_ref[0, :, D:2 * D] = (_dot(u, w_ref[:, D:2 * D]) * _dot(u, w_ref[:, 2 * D:3 * D])).astype(BF16)


def inproj_odd(h, mod, w):
    B, T, _ = h.shape
    tm = min(512, T)
    mb = mod.shape[0]
    return pl.pallas_call(
        _inproj_odd_kernel,
        out_shape=jax.ShapeDtypeStruct((B, T, 2 * D), BF16),
        grid=(B, T // tm),
        in_specs=[pl.BlockSpec((1, tm, D), lambda b, j: (b, j, 0)),
                  pl.BlockSpec((1, 6, D), (lambda b, j: (b, 0, 0)) if mb > 1 else (lambda b, j: (0, 0, 0))),
                  pl.BlockSpec((D, 3 * D), lambda b, j: (0, 0))],
        out_specs=pl.BlockSpec((1, tm, 2 * D), lambda b, j: (b, j, 0)),
        compiler_params=_cparams(("parallel", "arbitrary")),
        name="inproj_odd",
    )(h, mod, w)


HALO = 16


def _outproj_odd_kernel(gb_ref, z_ref, zp_ref, zn_ref, cw_ref, cb_ref, wo_ref,
                        h_ref, mod_ref, lnw_ref, lnb_ref, rwh_ref, rwl_ref, rb_ref,
                        h1_ref, tok_ref, lg_ref):
    j = pl.program_id(1)
    tm = z_ref.shape[1]
    z = z_ref[0].astype(F32)
    prev_row = jnp.where(j > 0, zp_ref[0, HALO - 1:HALO, :].astype(F32), 0.0)
    next_row = jnp.where(j < pl.num_programs(1) - 1, zn_ref[0, 0:1, :].astype(F32), 0.0)
    row = lax.broadcasted_iota(jnp.int32, (tm, D), 0)
    z_prev = jnp.where(row == 0, prev_row, pltpu.roll(z, 1, 0))
    z_next = jnp.where(row == tm - 1, next_row, pltpu.roll(z, tm - 1, 0))
    conv = z_prev * cw_ref[0:1, :] + z * cw_ref[1:2, :] + z_next * cw_ref[2:3, :] + cb_ref[...]
    y = _dot((gb_ref[0].astype(F32) * conv).astype(BF16), wo_ref[...])
    _post_norm_and_route(h_ref[0], y, mod_ref, lnw_ref, lnb_ref, rwh_ref, rwl_ref, rb_ref, h1_ref, tok_ref, lg_ref)


def outproj_odd(gz, conv_w, conv_b, w_out, h, mod, lnw, lnb, rwh, rwl, rb):
    B, T, _ = h.shape
    tm = min(512, T)
    r = tm // HALO
    nh = T // HALO
    shapes, ospecs = _epilogue_outs(B, T, tm)
    return pl.pallas_call(
        _outproj_odd_kernel,
        out_shape=shapes,
        grid=(B, T // tm),
        in_specs=[pl.BlockSpec((1, tm, D), lambda b, j: (b, j, 0)),
                  pl.BlockSpec((1, tm, D), lambda b, j: (b, j, 1)),
                  pl.BlockSpec((1, HALO, D), lambda b, j: (b, jnp.maximum(j * r - 1, 0), 1)),
                  pl.BlockSpec((1, HALO, D), lambda b, j: (b, jnp.minimum((j + 1) * r, nh - 1), 1)),
                  pl.BlockSpec((3, D), lambda b, j: (0, 0)),
                  pl.BlockSpec((1, D), lambda b, j: (0, 0)),
                  pl.BlockSpec((D, D), lambda b, j: (0, 0))] + _epilogue_specs(tm, mod.shape[0]),
        out_specs=ospecs,
        compiler_params=_cparams(("parallel", "arbitrary")),
        name="outproj_odd",
    )(gz, gz, gz, gz, conv_w, conv_b, w_out, h, mod, lnw, lnb, rwh, rwl, rb)


def _sc_mesh():
    return plsc.VectorSubcoreMesh(core_axis_name="c", subcore_axis_name="s")


def sc_gather_rows(table, idx):
    n = idx.shape[0]
    width = table.shape[1]
    per_w = n // SC_WORKERS
    n_chunks = per_w // SC_CHUNK

    @functools.partial(
        pl.kernel, mesh=_sc_mesh(),
        out_type=jax.ShapeDtypeStruct((n, width), table.dtype),
        scratch_types=[pltpu.VMEM((n_chunks, SC_CHUNK), jnp.int32),
                       pltpu.VMEM((SC_CHUNK, width), table.dtype),
                       pltpu.SemaphoreType.DMA],
    )
    def gather_kernel(table_hbm, idx_hbm, out_hbm, idx_v, rows_v, sem):
        wid = lax.axis_index("s") * SC_CORES + lax.axis_index("c")
        pltpu.sync_copy(idx_hbm.at[wid], idx_v)

        @pl.loop(0, n_chunks)
        def _(j):
            pltpu.async_copy(table_hbm.at[idx_v.at[j]], rows_v, sem).wait()
            pltpu.sync_copy(rows_v, out_hbm.at[pl.ds(wid * per_w + j * SC_CHUNK, SC_CHUNK)])

    return gather_kernel(table, idx.reshape(SC_WORKERS, n_chunks, SC_CHUNK))


def sc_scatter_rows(srcs, idxs, n_rows):
    width, dt = srcs[0].shape[1], srcs[0].dtype
    plans, args = [], []
    for src, idx in zip(srcs, idxs):
        per_w = src.shape[0] // SC_WORKERS
        chunk = min(SC_CHUNK, per_w)
        plans.append((per_w, chunk, per_w // chunk, idx.shape[0]))
        args += [src, idx.reshape(idx.shape[0], SC_WORKERS, per_w // chunk, chunk)]
    max_chunk = max(p[1] for p in plans)
    scratch = [pltpu.VMEM((max_chunk, width), dt)]
    scratch += [pltpu.VMEM((lists, n_chunks, chunk), jnp.int32) for _, chunk, n_chunks, lists in plans]

    @functools.partial(pl.kernel, mesh=_sc_mesh(), out_type=jax.ShapeDtypeStruct((n_rows, width), dt),
                       scratch_types=scratch)
    def scatter_kernel(*refs):
        ins, out_hbm, rows_v, idx_vs = refs[:2 * len(plans)], refs[2 * len(plans)], refs[2 * len(plans) + 1], \
            refs[2 * len(plans) + 2:]
        wid = lax.axis_index("s") * SC_CORES + lax.axis_index("c")
        for s, (per_w, chunk, n_chunks, lists) in enumerate(plans):
            src_hbm, idx_hbm, idx_v = ins[2 * s], ins[2 * s + 1], idx_vs[s]
            for k in range(lists):
                pltpu.sync_copy(idx_hbm.at[k, wid], idx_v.at[k])
            buf = rows_v if chunk == max_chunk else rows_v.at[pl.ds(0, chunk)]

            @pl.loop(0, n_chunks)
            def _(j, src_hbm=src_hbm, idx_v=idx_v, buf=buf, per_w=per_w, chunk=chunk, lists=lists):
                pltpu.sync_copy(src_hbm.at[pl.ds(wid * per_w + j * chunk, chunk)], buf)
                for k in range(lists):
                    pltpu.sync_copy(buf, out_hbm.at[idx_v.at[k, j]])

    return scatter_kernel(*args)


def _ffn_kernel(be_ref, nv_ref, x_ref, w1_ref, w3_ref, w2_ref, y_ref, w1b, w3b, w2b):
    i = pl.program_id(0)
    changed = jnp.logical_or(i == 0, be_ref[i] != be_ref[jnp.maximum(i - 1, 0)])

    @pl.when(changed)
    def _():
        w1b[...] = w1_ref[0, 0].astype(BF16)
        w3b[...] = w3_ref[0, 0].astype(BF16)
        w2b[...] = w2_ref[0, 0].astype(BF16)

    @pl.when(i < nv_ref[0])
    def _():
        half = D // 2
        x_lo, x_hi = (v.astype(BF16) for v in _unpack_bf16_pairs(x_ref[...]))
        gate = _dot(x_lo, w1b[0:half, :]) + _dot(x_hi, w1b[half:D, :])
        up = _dot(x_lo, w3b[0:half, :]) + _dot(x_hi, w3b[half:D, :])
        y_ref[...] = _pack_bf16_pairs(_dot((_silu(gate) * up).astype(BF16), w2b[...]))

    @pl.when(i >= nv_ref[0])
    def _():
        y_ref[...] = jnp.zeros_like(y_ref)


def moe_ffn(xs, blk_expert, n_valid, w1, w3, w2, layer):
    n_rows = xs.shape[0]
    nb = n_rows // MOE_TM
    return pl.pallas_call(
        _ffn_kernel,
        out_shape=jax.ShapeDtypeStruct((n_rows, D // 2), jnp.uint32),
        grid_spec=pltpu.PrefetchScalarGridSpec(
            num_scalar_prefetch=2,
            grid=(nb,),
            in_specs=[pl.BlockSpec((MOE_TM, D // 2), lambda i, be, nv: (i, 0)),
                      pl.BlockSpec((1, 1, D, D_EXPERT), lambda i, be, nv: (layer, be[i], 0, 0)),
                      pl.BlockSpec((1, 1, D, D_EXPERT), lambda i, be, nv: (layer, be[i], 0, 0)),
                      pl.BlockSpec((1, 1, D_EXPERT, D), lambda i, be, nv: (layer, be[i], 0, 0))],
            out_specs=pl.BlockSpec((MOE_TM, D // 2), lambda i, be, nv: (i, 0)),
            scratch_shapes=[pltpu.VMEM((D, D_EXPERT), BF16), pltpu.VMEM((D, D_EXPERT), BF16),
                            pltpu.VMEM((D_EXPERT, D), BF16)]),
        compiler_params=_cparams(("arbitrary",)),
        name="moe_ffn",
    )(blk_expert, n_valid, xs, w1, w3, w2)


def _combine_kernel(h_ref, y0_ref, y1_ref, wt_ref, mod_ref, lnw_ref, lnb_ref, o_ref):
    half = D // 2
    lo0, hi0 = _unpack_bf16_pairs(y0_ref[0])
    lo1, hi1 = _unpack_bf16_pairs(y1_ref[0])
    w0, w1 = wt_ref[:, 0:1], wt_ref[:, 1:2]
    r_lo = ALPHA * h_ref[:, 0:half] + mod_ref[0, 5:6, 0:half] * (w0 * lo0 + w1 * lo1)
    r_hi = ALPHA * h_ref[:, half:D] + mod_ref[0, 5:6, half:D] * (w0 * hi0 + w1 * hi1)
    mu = (jnp.sum(r_lo, axis=-1, keepdims=True) + jnp.sum(r_hi, axis=-1, keepdims=True)) * (1.0 / D)
    c_lo, c_hi = r_lo - mu, r_hi - mu
    var = (jnp.sum(c_lo * c_lo, axis=-1, keepdims=True) + jnp.sum(c_hi * c_hi, axis=-1, keepdims=True)) * (1.0 / D)
    inv = lax.rsqrt(var + LN_EPS)
    o_ref[:, 0:half] = c_lo * inv * lnw_ref[:, 0:half] + lnb_ref[:, 0:half]
    o_ref[:, half:D] = c_hi * inv * lnw_ref[:, half:D] + lnb_ref[:, half:D]


def moe_combine(h1, y_rows, wts, tile_off, mod, lnw, lnb):
    B, T, _ = h1.shape
    N = B * T
    tm = min(512, N)
    per_b = T // tm
    mod_map = (lambda i: (i // per_b, 0, 0)) if mod.shape[0] > 1 else (lambda i: (0, 0, 0))
    out = pl.pallas_call(
        _combine_kernel,
        out_shape=jax.ShapeDtypeStruct((N, D), F32),
        grid=(N // tm,),
        in_specs=[pl.BlockSpec((tm, D), lambda i: (i, 0)),
                  pl.BlockSpec((1, tm, D // 2), lambda i: (0, i + tile_off, 0)),
                  pl.BlockSpec((1, tm, D // 2), lambda i: (1, i + tile_off, 0)),
                  pl.BlockSpec((tm, TOP_K), lambda i: (i + tile_off, 0)),
                  pl.BlockSpec((1, 6, D), mod_map),
                  pl.BlockSpec((1, D), lambda i: (0, 0)), pl.BlockSpec((1, D), lambda i: (0, 0))],
        out_specs=pl.BlockSpec((tm, D), lambda i: (i, 0)),
        compiler_params=_cparams(("parallel",)),
        name="moe_combine",
    )(h1.reshape(N, D), y_rows, y_rows, wts, mod, lnw, lnb)
    return out.reshape(B, T, D)


def _route_kernel(lg_ref, dest_ref, wt_ref, cnt_ref, tri_sc, start_sc, run_sc):
    ph, i = pl.program_id(0), pl.program_id(1)
    tr = lg_ref.shape[1]

    @pl.when(jnp.logical_and(ph == 0, i == 0))
    def _():
        r = lax.broadcasted_iota(jnp.int32, (tr, tr), 0)
        c = lax.broadcasted_iota(jnp.int32, (tr, tr), 1)
        tri_sc[...] = (r < c).astype(BF16)
        start_sc[...] = jnp.zeros_like(start_sc)
        run_sc[...] = jnp.zeros_like(run_sc)

    @pl.when(jnp.logical_and(ph == 1, i == 0))
    def _():
        cnt = run_sc[...].astype(jnp.int32)
        cnt_ref[...] = cnt
        padded = jnp.bitwise_and(cnt + (MOE_TM - 1), -MOE_TM)
        row = lax.broadcasted_iota(jnp.int32, padded.shape, 0)
        acc = padded
        for s in (1, 2, 4, 8, 16):
            acc = acc + jnp.where(row >= s, pltpu.roll(acc, s, 0), 0)
        start_sc[...] = (acc - padded).astype(F32)
        run_sc[...] = jnp.zeros_like(run_sc)

    lg = lg_ref[...]
    gl = lg[N_EXPERTS:N_EXPERTS + N_GROUPS]
    gmax = jnp.max(gl, axis=0, keepdims=True)
    sub4 = lax.broadcasted_iota(jnp.int32, gl.shape, 0)
    g_sel = jnp.min(jnp.where(gl == gmax, sub4, N_GROUPS), axis=0, keepdims=True)
    p_group = 1.0 / jnp.sum(jnp.exp(gl - gmax), axis=0, keepdims=True)
    el = lg[0:EXPERTS_PER_GROUP]
    for g in range(1, N_GROUPS):
        el = jnp.where(g_sel == g, lg[g * EXPERTS_PER_GROUP:(g + 1) * EXPERTS_PER_GROUP], el)
    sub8 = lax.broadcasted_iota(jnp.int32, el.shape, 0)
    e1 = jnp.max(el, axis=0, keepdims=True)
    i1 = jnp.min(jnp.where(el == e1, sub8, EXPERTS_PER_GROUP), axis=0, keepdims=True)
    rest = jnp.where(sub8 == i1, -jnp.inf, el)
    e2 = jnp.max(rest, axis=0, keepdims=True)
    i2 = jnp.min(jnp.where(rest == e2, sub8, EXPERTS_PER_GROUP), axis=0, keepdims=True)
    den = jnp.sum(jnp.exp(el - e1), axis=0, keepdims=True)
    p1 = 1.0 / den
    p2 = jnp.exp(e2 - e1) / den
    wt_ref[0:1, :] = p_group * p1 / (p1 + p2)
    wt_ref[1:2, :] = p_group * p2 / (p1 + p2)

    sub32 = lax.broadcasted_iota(jnp.int32, (N_EXPERTS, tr), 0)
    oh = [(sub32 == g_sel * EXPERTS_PER_GROUP + ix).astype(F32) for ix in (i1, i2)]
    cnt = [jnp.sum(o, axis=1, keepdims=True) for o in oh]
    before = start_sc[:, 0:1] + run_sc[:, 0:1]
    for k in range(TOP_K):
        prior = _dot(oh[k].astype(BF16), tri_sc[...]) + before + (cnt[0] if k == 1 else 0.0)
        dest_ref[k:k + 1, :] = jnp.sum(oh[k] * prior, axis=0, keepdims=True).astype(jnp.int32)
    run_sc[...] = run_sc[...] + (cnt[0] + cnt[1])


def moe_route(logits_t):
    N = logits_t.shape[1]
    tr = 512 if N % 512 == 0 else 256
    return pl.pallas_call(
        _route_kernel,
        out_shape=(jax.ShapeDtypeStruct((TOP_K, N), jnp.int32), jax.ShapeDtypeStruct((TOP_K, N), F32),
                   jax.ShapeDtypeStruct((N_EXPERTS, LANE), jnp.int32)),
        grid=(2, N // tr),
        in_specs=[pl.BlockSpec((ROUTE_ROWS, tr), lambda p, i: (0, i))],
        out_specs=(pl.BlockSpec((TOP_K, tr), lambda p, i: (0, i * p)), pl.BlockSpec((TOP_K, tr), lambda p, i: (0, i * p)),
                   pl.BlockSpec((N_EXPERTS, LANE), lambda p, i: (0, 0))),
        scratch_shapes=[pltpu.VMEM((tr, tr), BF16), pltpu.VMEM((N_EXPERTS, LANE), F32),
                        pltpu.VMEM((N_EXPERTS, LANE), F32)],
        compiler_params=_cparams(("arbitrary", "arbitrary")),
        name="moe_route",
    )(logits_t)


def _block_tables(counts, n_assign):
    padded = (counts + MOE_TM - 1) // MOE_TM * MOE_TM
    pad_end = jnp.cumsum(padded)
    pad_start = pad_end - padded
    nb = -(-n_assign // MOE_TM) + N_EXPERTS
    blk_start = jnp.arange(nb, dtype=jnp.int32) * MOE_TM
    blk_expert = jnp.minimum(jnp.sum((pad_end[None, :] <= blk_start[:, None]).astype(jnp.int32), axis=1), N_EXPERTS - 1)
    n_valid = (pad_end[-1] // MOE_TM).astype(jnp.int32).reshape(1)
    n_fill = nb * MOE_TM - n_assign
    gap = padded - counts
    gap_end = jnp.cumsum(gap)
    k = jnp.arange(n_fill, dtype=jnp.int32)[:, None]
    sel = jnp.logical_and(k >= (gap_end - gap)[None, :], k < gap_end[None, :])
    in_gap = jnp.sum(jnp.where(sel, (pad_start + counts - (gap_end - gap))[None, :] + k, 0), axis=1)
    fill = jnp.where(k[:, 0] < gap_end[-1], in_gap, pad_end[-1] + k[:, 0] - gap_end[-1])
    return blk_expert.astype(jnp.int32), n_valid, fill.astype(jnp.int32), nb * MOE_TM


def hier_moe_and_norm(streams, w1, w3, w2, layer, lnw, lnb):
    logits_t = streams[0][2] if len(streams) == 1 else jnp.concatenate([s[2] for s in streams], 1)
    dest, wts_t, counts = moe_route(logits_t)
    wts = wts_t.T
    blk_expert, n_valid, fill_rows, n_rows = _block_tables(counts[:, 0], TOP_K * logits_t.shape[1])
    n_all = logits_t.shape[1]
    toks, idxs, offs, off = [], [], [], 0
    for h1, tok, _, _ in streams:
        n = h1.shape[0] * h1.shape[1]
        toks.append(tok.reshape(n, D // 2))
        idxs.append(dest[:, off:off + n])
        offs.append(off // min(512, n))
        off += n
    toks.append(jnp.zeros((fill_rows.shape[0], D // 2), jnp.uint32))
    idxs.append(fill_rows.reshape(1, -1))
    xs = sc_scatter_rows(toks, idxs, n_rows)
    y = moe_ffn(xs, blk_expert, n_valid, w1, w3, w2, layer)
    y_rows = sc_gather_rows(y, dest.reshape(-1)).reshape(TOP_K, n_all, D // 2)
    return [moe_combine(h1, y_rows, wts, o, mod, lnw, lnb) for (h1, _, _, mod), o in zip(streams, offs)]


def _even_weight_columns():
    n = np.arange(HEAD_DIM)
    perm = (n % 32) // 16 * 32 + n // 32 * 16 + n % 16
    idx = np.zeros(P_W, np.int32)
    scale = np.zeros(P_W, np.float32)
    for hd in range(A_Q_HEADS):
        idx[C_QA + hd * 64:C_QA + (hd + 1) * 64] = hd * 64 + perm
    scale[C_QA:C_QA + 512] = HEAD_DIM ** -0.5
    for g in range(A_KV_HEADS):
        for rep in range(2):
            o = g * 128 + rep * 64
            idx[C_KD + o:C_KD + o + 64] = 512 + g * 64 + perm
            idx[C_VD + o:C_VD + o + 64] = 640 + g * 64 + n
    scale[C_KD:C_VD + 256] = 1.0
    idx[C_QG:C_QG + 256] = 768 + np.arange(256)
    scale[C_QG:C_QG + 256] = GLA_DK ** -0.5
    idx[C_KG:C_KG + 256] = 1024 + np.arange(256)
    idx[C_VG:C_VG + 512] = 1280 + np.arange(512)
    idx[C_RG:C_RG + 512] = 1792 + np.arange(512)
    scale[C_KG:C_KG + 256] = 1.0
    scale[C_VG:C_RG + 512] = 1.0
    for d in range(2):
        idx[C_GG + d * 128:C_GG + d * 128 + GLA_RANK] = 2304 + d * GLA_RANK + np.arange(GLA_RANK)
        scale[C_GG + d * 128:C_GG + d * 128 + GLA_RANK] = 1.0
    return idx, scale


def _rope_tables(S):
    row = jnp.repeat(jnp.arange(S // GRID_W), GRID_W).astype(F32)
    col = jnp.tile(jnp.arange(GRID_W), S // GRID_W).astype(F32)
    axis_dim = HEAD_DIM // 2
    inv_freq = ROPE_BASE ** (-jnp.arange(0, axis_dim, 2, dtype=F32) / axis_dim)
    ang = jnp.concatenate([row[:, None] * inv_freq, col[:, None] * inv_freq], -1)
    cos, sin = jnp.cos(ang), jnp.sin(ang)
    cos_t = jnp.tile(cos, (1, 4))
    sin_t = jnp.tile(jnp.concatenate([-sin, sin], -1), (1, 2))
    return cos_t, sin_t


def _router_weights(wg, bg, we, be):
    w = jnp.zeros((D, ROUTE_W), F32).at[:, :N_EXPERTS].set(we).at[:, N_EXPERTS:N_EXPERTS + N_GROUPS].set(wg)
    b = jnp.zeros((1, ROUTE_W), F32).at[0, :N_EXPERTS].set(be).at[0, N_EXPERTS:N_EXPERTS + N_GROUPS].set(bg)
    hi = w.astype(BF16)
    return hi, (w - hi.astype(F32)).astype(BF16), b


def kernel(x, c, ctx, c_ctx, w_in_even, w_out_even, attn_sink, gla_wa2, gla_ba, gla_norm_w, w_in_odd, conv_w, conv_b, w_out_odd, ada_w, ada_b, ln_w, ln_b, router_wg, router_bg, router_we, router_be, moe_w1, moe_w3, moe_w2):
    B, S, _ = x.shape
    L = ctx.shape[1]
    cos_t, sin_t = _rope_tables(S)
    cos_c, sin_c = jnp.ones((L, LANE), F32), jnp.zeros((L, LANE), F32)
    col_idx, col_scale = _even_weight_columns()

    n_cond = -(-(B + 1) // 8) * 8
    cc = jnp.zeros((n_cond, D), F32).at[:B].set(c).at[B].set(c_ctx)
    mods = ada_modulation_all(cc, ada_w, ada_b).reshape(DEPTH, n_cond, 6, D)

    h_lat, h_ctx = x, ctx
    for l in range(DEPTH):
        i = l // 2
        need_ctx = any(j % 2 == 0 for j in range(l + 1, DEPTH))
        m_lat = mods[l, :B]
        m_ctx = mods[l, B:B + 1]
        lnw0, lnb0 = ln_w[l, 0:1], ln_b[l, 0:1]
        lnw1, lnb1 = ln_w[l, 1:2], ln_b[l, 1:2]
        rwh, rwl, rb = _router_weights(router_wg[l], router_bg[l], router_we[l], router_be[l])
        streams = []
        if l % 2 == 0:
            w_in = (w_in_even[i][:, col_idx] * col_scale[None, :]).astype(BF16)
            w_out = w_out_even[i].astype(BF16)
            wa_p = jnp.zeros((2, LANE, GLA_HEADS * GLA_DK), F32).at[:, :GLA_RANK].set(gla_wa2[i]).astype(BF16)
            ba = gla_ba[i].reshape(2, 1, -1)
            nw = gla_norm_w[i].reshape(1, -1)
            p_ctx = inproj_even(h_ctx, m_ctx, w_in, cos_c, sin_c)
            p_lat = inproj_even(h_lat, m_lat, w_in, cos_t, sin_t)
            a_lat = attention(p_lat, p_ctx, attn_sink[i], True)
            s0 = jnp.zeros((B, 2, 2, GLA_DV, LANE), F32)
            oc_f, oc_b, s_ctx = gla_scan(p_ctx, wa_p, ba, s0)
            ol_f, ol_b, _ = gla_scan(p_lat, wa_p, ba, s_ctx)
            streams.append(outproj_even(a_lat, ol_f, ol_b, p_lat, nw, w_out, h_lat, m_lat, lnw0, lnb0, rwh, rwl, rb)
                           + (m_lat,))
            if need_ctx:
                a_ctx = attention(p_ctx, p_ctx, attn_sink[i], False)
                streams.append(outproj_even(a_ctx, oc_f, oc_b, p_ctx, nw, w_out, h_ctx, m_ctx, lnw0, lnb0,
                                            rwh, rwl, rb) + (m_ctx,))
        else:
            w_in = w_in_odd[i].astype(BF16)
            w_out = w_out_odd[i].astype(BF16)
            cb = conv_b[i].reshape(1, D)
            pairs = [(h_lat, m_lat)] + ([(h_ctx, m_ctx)] if need_ctx else [])
            for h, m in pairs:
                gz = inproj_odd(h, m, w_in)
                streams.append(outproj_odd(gz, conv_w[i], cb, w_out, h, m, lnw0, lnb0, rwh, rwl, rb) + (m,))
        outs = hier_moe_and_norm(streams, moe_w1, moe_w3, moe_w2, l, lnw1, lnb1)
        h_lat = outs[0]
        if need_ctx:
            h_ctx = outs[1]
    return h_lat
```

```python
import functools

import numpy as np
import jax
import jax.numpy as jnp
from jax import lax
from jax.experimental import pallas as pl
from jax.experimental.pallas import tpu as pltpu
from jax.experimental.pallas import tpu_sc as plsc

F32 = jnp.float32
BF16 = jnp.bfloat16
HIGHEST = lax.Precision.HIGHEST

D = 1024
DEPTH = 4
GRID_W = 64
HEAD_DIM = 64
A_Q_HEADS = 8
A_KV_HEADS = 2
WINDOW = 128
ROPE_BASE = 10000.0
GLA_HEADS = 4
GLA_DK = 64
GLA_DV = 128
GLA_RANK = 16
GLA_TAU = 16.0
GLA_CHUNK = 64
N_GROUPS = 4
EXPERTS_PER_GROUP = 8
N_EXPERTS = 32
TOP_K = 2
D_EXPERT = 512
ALPHA = (2.0 * DEPTH) ** 0.25
LN_EPS = 1e-5
RMS_EPS = 1e-6

LANE = 128
VMEM_LIMIT = 48 * 1024 * 1024

C_QA, C_KD, C_VD, C_VG, C_RG, C_QG, C_KG, C_GG = 0, 512, 768, 1024, 1536, 2048, 2304, 2560
P_W = 2816
ROUTE_W = 128
ROUTE_ROWS = 40
MOE_TM = 512
SC_CORES, SC_SUBCORES = 2, 16
SC_WORKERS = SC_CORES * SC_SUBCORES
SC_CHUNK = 64
NEG = -1e30


def _cparams(sem):
    return pltpu.CompilerParams(dimension_semantics=sem, vmem_limit_bytes=VMEM_LIMIT)


def _dot(a, b):
    return jnp.dot(a, b, preferred_element_type=F32)


def _dot_nt(a, b):
    return lax.dot_general(a, b, (((1,), (1,)), ((), ())), preferred_element_type=F32)


def _dot_tn(a, b):
    return lax.dot_general(a, b, (((0,), (0,)), ((), ())), preferred_element_type=F32)


def _silu(x):
    return x * (1.0 / (1.0 + jnp.exp(-x)))


def _ada_kernel(c_ref, w_ref, b_ref, o_ref):
    s = _silu(c_ref[...])
    o_ref[0] = jnp.dot(s, w_ref[0], precision=HIGHEST, preferred_element_type=F32) + b_ref[0]


def ada_modulation_all(cc, ada_w, ada_b):
    R = cc.shape[0]
    tn = 1536
    return pl.pallas_call(
        _ada_kernel,
        out_shape=jax.ShapeDtypeStruct((DEPTH, R, 6 * D), F32),
        grid=(DEPTH, 6 * D // tn),
        in_specs=[pl.BlockSpec((R, D), lambda l, n: (0, 0)),
                  pl.BlockSpec((1, D, tn), lambda l, n: (l, 0, n)),
                  pl.BlockSpec((1, 1, tn), lambda l, n: (l, 0, n))],
        out_specs=pl.BlockSpec((1, R, tn), lambda l, n: (l, 0, n)),
        compiler_params=_cparams(("arbitrary", "arbitrary")),
        name="ada_modulation",
    )(cc, ada_w, ada_b.reshape(DEPTH, 1, 6 * D))


_EVEN_CHUNKS = ((0, 512, True), (512, 768, True), (768, 1024, False), (1024, 1536, False),
                (1536, 2048, False), (2048, 2560, False), (2560, 2816, False))


def _inproj_even_kernel(h_ref, mod_ref, w_ref, cos_ref, sin_ref, p_ref):
    tm = h_ref.shape[1]
    u = (h_ref[0] * (1.0 + mod_ref[0, 1:2, :]) + mod_ref[0, 0:1, :]).astype(BF16)
    cos = cos_ref[...]
    sin = sin_ref[...]
    lane = lax.broadcasted_iota(jnp.int32, (tm, LANE), 1)
    first_half = (lane % HEAD_DIM) < (HEAD_DIM // 2)
    for c0, c1, rope in _EVEN_CHUNKS:
        acc = _dot(u, w_ref[:, c0:c1])
        if rope:
            for i in range((c1 - c0) // LANE):
                x = acc[:, i * LANE:(i + 1) * LANE]
                partner = jnp.where(first_half, pltpu.roll(x, LANE - 32, 1), pltpu.roll(x, 32, 1))
                p_ref[0, :, c0 + i * LANE:c0 + (i + 1) * LANE] = (x * cos + partner * sin).astype(BF16)
        else:
            p_ref[0, :, c0:c1] = acc.astype(BF16)


def inproj_even(h, mod, w, cos_t, sin_t):
    B, T, _ = h.shape
    tm = min(512, T)
    mb = mod.shape[0]
    return pl.pallas_call(
        _inproj_even_kernel,
        out_shape=jax.ShapeDtypeStruct((B, T, P_W), BF16),
        grid=(B, T // tm),
        in_specs=[pl.BlockSpec((1, tm, D), lambda b, j: (b, j, 0)),
                  pl.BlockSpec((1, 6, D), (lambda b, j: (b, 0, 0)) if mb > 1 else (lambda b, j: (0, 0, 0))),
                  pl.BlockSpec((D, P_W), lambda b, j: (0, 0)),
                  pl.BlockSpec((tm, LANE), lambda b, j: (j, 0)),
                  pl.BlockSpec((tm, LANE), lambda b, j: (j, 0))],
        out_specs=pl.BlockSpec((1, tm, P_W), lambda b, j: (b, j, 0)),
        compiler_params=_cparams(("parallel", "arbitrary")),
        name="inproj_even",
    )(h, mod, w, cos_t, sin_t)


def _attn_kernel(*refs, tq, has_window):
    if has_window:
        sink_ref, q_ref, kw_ref, vw_ref, kc_ref, vc_ref, o_ref = refs
    else:
        sink_ref, q_ref, kc_ref, vc_ref, o_ref = refs
    group = A_Q_HEADS // A_KV_HEADS
    rows = group * tq
    lo = lax.broadcasted_iota(jnp.int32, (tq, LANE), 1) < HEAD_DIM
    den_lanes = lax.broadcasted_iota(jnp.int32, (rows, LANE), 1) >= HEAD_DIM

    def with_ones(v):
        return jnp.where(lax.broadcasted_iota(jnp.int32, v.shape, 1) < HEAD_DIM, v, jnp.ones_like(v))

    if has_window:
        S = kw_ref.shape[1]
        wk = tq + 2 * WINDOW
        q0 = pl.program_id(1) * tq
        wstart = pl.multiple_of(jnp.clip(q0 - WINDOW, 0, S - wk), LANE)
        qpos = q0 + lax.broadcasted_iota(jnp.int32, (rows, wk), 0) % tq
        kpos = wstart + lax.broadcasted_iota(jnp.int32, (rows, wk), 1)
        band = jnp.abs(qpos - kpos) <= WINDOW
    groups = range(A_KV_HEADS)
    cols = [slice(g * LANE, (g + 1) * LANE) for g in groups]
    q4, snk, sc, sw, m, outs = [], [], [], [], [], []
    for g in groups:
        qs = []
        for pr in range(group // 2):
            qblk = q_ref[0, :, (2 * g + pr) * LANE:(2 * g + pr + 1) * LANE]
            zero = jnp.zeros_like(qblk)
            qs += [jnp.where(lo, qblk, zero), jnp.where(lo, zero, qblk)]
        q4.append(jnp.concatenate(qs, axis=0))
        snk.append(jnp.concatenate([jnp.full((tq, 1), sink_ref[group * g + i], F32) for i in range(group)], axis=0))
    for g in groups:
        sc.append(_dot_nt(q4[g], kc_ref[0, :, cols[g]]))
        if has_window:
            sw.append(jnp.where(band, _dot_nt(q4[g], kw_ref[0, pl.ds(wstart, wk), cols[g]]), NEG))
    for g in groups:
        mg = jnp.maximum(jnp.max(sc[g], axis=-1, keepdims=True), snk[g])
        if has_window:
            mg = jnp.maximum(mg, jnp.max(sw[g], axis=-1, keepdims=True))
        m.append(mg)
    for g in groups:
        o = _dot(jnp.exp(sc[g] - m[g]).astype(BF16), with_ones(vc_ref[0, :, cols[g]]))
        if has_window:
            o = o + _dot(jnp.exp(sw[g] - m[g]).astype(BF16), with_ones(vw_ref[0, pl.ds(wstart, wk), cols[g]]))
        outs.append(o + jnp.where(den_lanes, jnp.exp(snk[g] - m[g]), 0.0))
    for g in groups:
        o = outs[g]
        swapped = pltpu.roll(o, HEAD_DIM, 1)
        for pr in range(group // 2):
            ev = slice(2 * pr * tq, (2 * pr + 1) * tq)
            od = slice((2 * pr + 1) * tq, (2 * pr + 2) * tq)
            res = jnp.where(lo, o[ev] / swapped[ev], swapped[od] / o[od])
            o_ref[0, :, (2 * g + pr) * LANE:(2 * g + pr + 1) * LANE] = res.astype(BF16)


def attention(p_q, p_ctx, sink, has_window):
    B, T, _ = p_q.shape
    L = p_ctx.shape[1]
    tq = 128
    in_specs = [pl.BlockSpec(memory_space=pltpu.SMEM),
                pl.BlockSpec((1, tq, 512), lambda b, j: (b, j, C_QA // 512))]
    args = [sink, p_q]
    if has_window:
        in_specs += [pl.BlockSpec((1, T, 256), lambda b, j: (b, 0, C_KD // 256)),
                     pl.BlockSpec((1, T, 256), lambda b, j: (b, 0, C_VD // 256))]
        args += [p_q, p_q]
    in_specs += [pl.BlockSpec((1, L, 256), lambda b, j: (b, 0, C_KD // 256)),
                 pl.BlockSpec((1, L, 256), lambda b, j: (b, 0, C_VD // 256))]
    args += [p_ctx, p_ctx]
    return pl.pallas_call(
        functools.partial(_attn_kernel, tq=tq, has_window=has_window),
        out_shape=jax.ShapeDtypeStruct((B, T, 512), BF16),
        grid=(B, T // tq),
        in_specs=in_specs,
        out_specs=pl.BlockSpec((1, tq, 512), lambda b, j: (b, j, 0)),
        compiler_params=_cparams(("parallel", "arbitrary")),
        name="window_attention" if has_window else "context_attention",
    )(*args)


def _log_sigmoid(x):
    return jnp.minimum(x, 0.0) - jnp.log(1.0 + jnp.exp(-jnp.abs(x)))


def _gla_kernel(qf_ref, kf_ref, vf_ref, gf_ref, qb_ref, kb_ref, vb_ref, gb_ref, wa_ref, ba_ref, s0_ref,
                of_ref, ob_ref, sfin_ref, s_sc):
    j = pl.program_id(1)
    nblk = pl.num_programs(1)
    tb = qf_ref.shape[1]
    nc = tb // GLA_CHUNK

    @pl.when(j == 0)
    def _():
        s_sc[...] = s0_ref[0]

    C = GLA_CHUNK
    ri = lax.broadcasted_iota(jnp.int32, (2 * C, C), 0) % C
    ci = lax.broadcasted_iota(jnp.int32, (2 * C, C), 1)
    lo = lax.broadcasted_iota(jnp.int32, (C, LANE), 1) < GLA_DK

    def per_head(x):
        zero = jnp.zeros_like(x)
        return jnp.concatenate([jnp.where(lo, x, zero), jnp.where(lo, zero, x)], axis=0)

    for d in range(2):
        q_ref, k_ref, v_ref, g_ref, o_ref = ((qf_ref, kf_ref, vf_ref, gf_ref, of_ref) if d == 0 else
                                             (qb_ref, kb_ref, vb_ref, gb_ref, ob_ref))
        causal = (ri >= ci) if d == 0 else (ci >= ri)
        tri = causal[0:C].astype(BF16)
        g = _dot(g_ref[0, :, d * LANE:(d + 1) * LANE], wa_ref[d]) + ba_ref[d]
        log_a = _log_sigmoid(g) / GLA_TAU
        la1 = log_a.astype(BF16)
        rem = log_a - la1.astype(F32)
        la2 = rem.astype(BF16)
        la3 = (rem - la2.astype(F32)).astype(BF16)
        for c in (range(nc) if d == 0 else range(nc - 1, -1, -1)):
            rows = slice(c * C, (c + 1) * C)
            b = _dot(tri, la1[rows]) + _dot(tri, la2[rows]) + _dot(tri, la3[rows])
            b_last = b[C - 1:C] if d == 0 else b[0:1]
            qc = q_ref[0, rows, :].astype(F32)
            kc = k_ref[0, rows, :].astype(F32)
            q_in = (qc * jnp.exp(b)).astype(BF16)
            k_in = (kc * jnp.exp(-b)).astype(BF16)
            k_st = (kc * jnp.exp(b_last - b)).astype(BF16)
            dl = jnp.exp(b_last)
            for pair in range(2):
                cols = slice(pair * LANE, (pair + 1) * LANE)
                st = s_sc[d, pair]
                q2 = per_head(q_in[:, cols])
                attn = jnp.where(causal, _dot_nt(q2, k_in[:, cols]), 0.0).astype(BF16)
                inter = _dot_nt(q2, st.astype(BF16))
                vs = [v_ref[0, rows, (2 * pair + hh) * GLA_DV:(2 * pair + hh + 1) * GLA_DV] for hh in range(2)]
                for hh in range(2):
                    hr = slice(hh * C, (hh + 1) * C)
                    o_ref[0, rows, (2 * pair + hh) * GLA_DV:(2 * pair + hh + 1) * GLA_DV] = (
                        _dot(attn[hr], vs[hh]) + inter[hr]).astype(o_ref.dtype)
                upd = _dot_tn(jnp.concatenate(vs, axis=0), per_head(k_st[:, cols]))
                s_sc[d, pair] = st * dl[:, cols] + upd

    @pl.when(j == nblk - 1)
    def _():
        sfin_ref[0] = s_sc[...]


def gla_scan(p, wa_p, ba, s0):
    B, T, _ = p.shape
    tb = min(512, T)
    nblk = T // tb
    fwd = lambda b, j: (b, j)
    bwd = lambda b, j: (b, nblk - 1 - j)

    def specs(im):
        return [pl.BlockSpec((1, tb, 256), lambda b, j: im(b, j) + (C_QG // 256,)),
                pl.BlockSpec((1, tb, 256), lambda b, j: im(b, j) + (C_KG // 256,)),
                pl.BlockSpec((1, tb, 512), lambda b, j: im(b, j) + (C_VG // 512,)),
                pl.BlockSpec((1, tb, 256), lambda b, j: im(b, j) + (C_GG // 256,))]

    return pl.pallas_call(
        _gla_kernel,
        out_shape=(jax.ShapeDtypeStruct((B, T, 512), F32), jax.ShapeDtypeStruct((B, T, 512), F32),
                   jax.ShapeDtypeStruct(s0.shape, F32)),
        grid=(B, nblk),
        in_specs=specs(fwd) + specs(bwd) + [
            pl.BlockSpec((2, LANE, 256), lambda b, j: (0, 0, 0)),
            pl.BlockSpec((2, 1, 256), lambda b, j: (0, 0, 0)),
            pl.BlockSpec((1, 2, 2, GLA_DV, LANE), lambda b, j: (b, 0, 0, 0, 0))],
        out_specs=(pl.BlockSpec((1, tb, 512), lambda b, j: (b, j, 0)),
                   pl.BlockSpec((1, tb, 512), lambda b, j: (b, nblk - 1 - j, 0)),
                   pl.BlockSpec((1, 2, 2, GLA_DV, LANE), lambda b, j: (b, 0, 0, 0, 0))),
        scratch_shapes=[pltpu.VMEM((2, 2, GLA_DV, LANE), F32)],
        compiler_params=_cparams(("parallel", "arbitrary")),
        name="gla_scan",
    )(p, p, p, p, p, p, p, p, wa_p, ba, s0)


def _layer_norm(r, w, b):
    mu = jnp.mean(r, axis=-1, keepdims=True)
    xc = r - mu
    var = jnp.mean(xc * xc, axis=-1, keepdims=True)
    return xc * lax.rsqrt(var + LN_EPS) * w + b


def _pack_bf16_pairs(x):
    half = x.shape[1] // 2
    lo = lax.bitcast_convert_type(x[:, :half].astype(BF16).astype(F32), jnp.uint32)
    hi = lax.bitcast_convert_type(x[:, half:].astype(BF16).astype(F32), jnp.uint32)
    return jnp.bitwise_or(hi, lax.shift_right_logical(lo, jnp.uint32(16)))


def _unpack_bf16_pairs(p):
    lo = lax.bitcast_convert_type(lax.shift_left(p, jnp.uint32(16)), F32)
    hi = lax.bitcast_convert_type(jnp.bitwise_and(p, jnp.uint32(0xFFFF0000)), F32)
    return lo, hi


def _post_norm_and_route(h, y, mod_ref, lnw_ref, lnb_ref, rwh_ref, rwl_ref, rb_ref, h1_ref, tok_ref, lg_ref):
    h1 = _layer_norm(ALPHA * h + mod_ref[0, 2:3, :] * y, lnw_ref[...], lnb_ref[...])
    h1_ref[0] = h1
    tok = h1 * (1.0 + mod_ref[0, 4:5, :]) + mod_ref[0, 3:4, :]
    tok_ref[0] = _pack_bf16_pairs(tok)
    hi = tok.astype(BF16)
    lo = (tok - hi.astype(F32)).astype(BF16)
    lg = _dot(hi, rwh_ref[...]) + _dot(lo, rwh_ref[...]) + _dot(hi, rwl_ref[...]) + rb_ref[...]
    lg_ref[...] = lg.T[0:ROUTE_ROWS, :]


def _epilogue_specs(tm, mb):
    mod_map = (lambda b, j: (b, 0, 0)) if mb > 1 else (lambda b, j: (0, 0, 0))
    const2 = lambda b, j: (0, 0)
    return [pl.BlockSpec((1, tm, D), lambda b, j: (b, j, 0)),
            pl.BlockSpec((1, 6, D), mod_map),
            pl.BlockSpec((1, D), const2), pl.BlockSpec((1, D), const2),
            pl.BlockSpec((D, ROUTE_W), const2), pl.BlockSpec((D, ROUTE_W), const2),
            pl.BlockSpec((1, ROUTE_W), const2)]


def _epilogue_outs(B, T, tm):
    nj = T // tm
    shapes = (jax.ShapeDtypeStruct((B, T, D), F32), jax.ShapeDtypeStruct((B, T, D // 2), jnp.uint32),
              jax.ShapeDtypeStruct((ROUTE_ROWS, B * T), F32))
    specs = (pl.BlockSpec((1, tm, D), lambda b, j: (b, j, 0)), pl.BlockSpec((1, tm, D // 2), lambda b, j: (b, j, 0)),
             pl.BlockSpec((ROUTE_ROWS, tm), lambda b, j: (0, b * nj + j)))
    return shapes, specs


def _outproj_even_kernel(a_ref, of_ref, ob_ref, rg_ref, nw_ref, wo_ref,
                         h_ref, mod_ref, lnw_ref, lnb_ref, rwh_ref, rwl_ref, rb_ref,
                         h1_ref, tok_ref, lg_ref):
    y = _dot(a_ref[0], wo_ref[0:512, :])
    for hd in range(GLA_HEADS):
        cols = slice(hd * GLA_DV, (hd + 1) * GLA_DV)
        o = of_ref[0, :, cols] + ob_ref[0, :, cols]
        o = o * lax.rsqrt(jnp.mean(o * o, axis=-1, keepdims=True) + RMS_EPS)
        gated = o * nw_ref[:, cols] * _silu(rg_ref[0, :, cols].astype(F32))
        y = y + _dot(gated.astype(BF16), wo_ref[512 + hd * GLA_DV:512 + (hd + 1) * GLA_DV, :])
    _post_norm_and_route(h_ref[0], y, mod_ref, lnw_ref, lnb_ref, rwh_ref, rwl_ref, rb_ref, h1_ref, tok_ref, lg_ref)


def outproj_even(a, o_f, o_b, p, norm_w, w_out, h, mod, lnw, lnb, rwh, rwl, rb):
    B, T, _ = h.shape
    tm = min(512, T)
    tile = lambda b, j: (b, j, 0)
    shapes, ospecs = _epilogue_outs(B, T, tm)
    return pl.pallas_call(
        _outproj_even_kernel,
        out_shape=shapes,
        grid=(B, T // tm),
        in_specs=[pl.BlockSpec((1, tm, 512), tile), pl.BlockSpec((1, tm, 512), tile), pl.BlockSpec((1, tm, 512), tile),
                  pl.BlockSpec((1, tm, 512), lambda b, j: (b, j, C_RG // 512)),
                  pl.BlockSpec((1, 512), lambda b, j: (0, 0)),
                  pl.BlockSpec((D, D), lambda b, j: (0, 0))] + _epilogue_specs(tm, mod.shape[0]),
        out_specs=ospecs,
        compiler_params=_cparams(("parallel", "arbitrary")),
        name="outproj_even",
    )(a, o_f, o_b, p, norm_w, w_out, h, mod, lnw, lnb, rwh, rwl, rb)


def _inproj_odd_kernel(h_ref, mod_ref, w_ref, o_ref):
    u = (h_ref[0] * (1.0 + mod_ref[0, 1:2, :]) + mod_ref[0, 0:1, :]).astype(BF16)
    o_ref[0, :, 0:D] = _dot(u, w_ref[:, 0:D]).astype(BF16)
    o_ref[0, :, D:2 * D] = (_dot(u, w_ref[:, D:2 * D]) * _dot(u, w_ref[:, 2 * D:3 * D])).astype(BF16)


def inproj_odd(h, mod, w):
    B, T, _ = h.shape
    tm = min(512, T)
    mb = mod.shape[0]
    return pl.pallas_call(
        _inproj_odd_kernel,
        out_shape=jax.ShapeDtypeStruct((B, T, 2 * D), BF16),
        grid=(B, T // tm),
        in_specs=[pl.BlockSpec((1, tm, D), lambda b, j: (b, j, 0)),
                  pl.BlockSpec((1, 6, D), (lambda b, j: (b, 0, 0)) if mb > 1 else (lambda b, j: (0, 0, 0))),
                  pl.BlockSpec((D, 3 * D), lambda b, j: (0, 0))],
        out_specs=pl.BlockSpec((1, tm, 2 * D), lambda b, j: (b, j, 0)),
        compiler_params=_cparams(("parallel", "arbitrary")),
        name="inproj_odd",
    )(h, mod, w)


HALO = 16


def _outproj_odd_kernel(gb_ref, z_ref, zp_ref, zn_ref, cw_ref, cb_ref, wo_ref,
                        h_ref, mod_ref, lnw_ref, lnb_ref, rwh_ref, rwl_ref, rb_ref,
                        h1_ref, tok_ref, lg_ref):
    j = pl.program_id(1)
    tm = z_ref.shape[1]
    z = z_ref[0].astype(F32)
    prev_row = jnp.where(j > 0, zp_ref[0, HALO - 1:HALO, :].astype(F32), 0.0)
    next_row = jnp.where(j < pl.num_programs(1) - 1, zn_ref[0, 0:1, :].astype(F32), 0.0)
    row = lax.broadcasted_iota(jnp.int32, (tm, D), 0)
    z_prev = jnp.where(row == 0, prev_row, pltpu.roll(z, 1, 0))
    z_next = jnp.where(row == tm - 1, next_row, pltpu.roll(z, tm - 1, 0))
    conv = z_prev * cw_ref[0:1, :] + z * cw_ref[1:2, :] + z_next * cw_ref[2:3, :] + cb_ref[...]
    y = _dot((gb_ref[0].astype(F32) * conv).astype(BF16), wo_ref[...])
    _post_norm_and_route(h_ref[0], y, mod_ref, lnw_ref, lnb_ref, rwh_ref, rwl_ref, rb_ref, h1_ref, tok_ref, lg_ref)


def outproj_odd(gz, conv_w, conv_b, w_out, h, mod, lnw, lnb, rwh, rwl, rb):
    B, T, _ = h.shape
    tm = min(512, T)
    r = tm // HALO
    nh = T // HALO
    shapes, ospecs = _epilogue_outs(B, T, tm)
    return pl.pallas_call(
        _outproj_odd_kernel,
        out_shape=shapes,
        grid=(B, T // tm),
        in_specs=[pl.BlockSpec((1, tm, D), lambda b, j: (b, j, 0)),
                  pl.BlockSpec((1, tm, D), lambda b, j: (b, j, 1)),
                  pl.BlockSpec((1, HALO, D), lambda b, j: (b, jnp.maximum(j * r - 1, 0), 1)),
                  pl.BlockSpec((1, HALO, D), lambda b, j: (b, jnp.minimum((j + 1) * r, nh - 1), 1)),
                  pl.BlockSpec((3, D), lambda b, j: (0, 0)),
                  pl.BlockSpec((1, D), lambda b, j: (0, 0)),
                  pl.BlockSpec((D, D), lambda b, j: (0, 0))] + _epilogue_specs(tm, mod.shape[0]),
        out_specs=ospecs,
        compiler_params=_cparams(("parallel", "arbitrary")),
        name="outproj_odd",
    )(gz, gz, gz, gz, conv_w, conv_b, w_out, h, mod, lnw, lnb, rwh, rwl, rb)


def _sc_mesh():
    return plsc.VectorSubcoreMesh(core_axis_name="c", subcore_axis_name="s")


def sc_gather_rows(table, idx):
    n = idx.shape[0]
    width = table.shape[1]
    per_w = n // SC_WORKERS
    n_chunks = per_w // SC_CHUNK

    @functools.partial(
        pl.kernel, mesh=_sc_mesh(),
        out_type=jax.ShapeDtypeStruct((n, width), table.dtype),
        scratch_types=[pltpu.VMEM((n_chunks, SC_CHUNK), jnp.int32),
                       pltpu.VMEM((SC_CHUNK, width), table.dtype),
                       pltpu.SemaphoreType.DMA],
    )
    def gather_kernel(table_hbm, idx_hbm, out_hbm, idx_v, rows_v, sem):
        wid = lax.axis_index("s") * SC_CORES + lax.axis_index("c")
        pltpu.sync_copy(idx_hbm.at[wid], idx_v)

        @pl.loop(0, n_chunks)
        def _(j):
            pltpu.async_copy(table_hbm.at[idx_v.at[j]], rows_v, sem).wait()
            pltpu.sync_copy(rows_v, out_hbm.at[pl.ds(wid * per_w + j * SC_CHUNK, SC_CHUNK)])

    return gather_kernel(table, idx.reshape(SC_WORKERS, n_chunks, SC_CHUNK))


def sc_scatter_rows(srcs, idxs, n_rows):
    width, dt = srcs[0].shape[1], srcs[0].dtype
    plans, args = [], []
    for src, idx in zip(srcs, idxs):
        per_w = src.shape[0] // SC_WORKERS
        chunk = min(SC_CHUNK, per_w)
        plans.append((per_w, chunk, per_w // chunk, idx.shape[0]))
        args += [src, idx.reshape(idx.shape[0], SC_WORKERS, per_w // chunk, chunk)]
    max_chunk = max(p[1] for p in plans)
    scratch = [pltpu.VMEM((max_chunk, width), dt)]
    scratch += [pltpu.VMEM((lists, n_chunks, chunk), jnp.int32) for _, chunk, n_chunks, lists in plans]

    @functools.partial(pl.kernel, mesh=_sc_mesh(), out_type=jax.ShapeDtypeStruct((n_rows, width), dt),
                       scratch_types=scratch)
    def scatter_kernel(*refs):
        ins, out_hbm, rows_v, idx_vs = refs[:2 * len(plans)], refs[2 * len(plans)], refs[2 * len(plans) + 1], \
            refs[2 * len(plans) + 2:]
        wid = lax.axis_index("s") * SC_CORES + lax.axis_index("c")
        for s, (per_w, chunk, n_chunks, lists) in enumerate(plans):
            src_hbm, idx_hbm, idx_v = ins[2 * s], ins[2 * s + 1], idx_vs[s]
            for k in range(lists):
                pltpu.sync_copy(idx_hbm.at[k, wid], idx_v.at[k])
            buf = rows_v if chunk == max_chunk else rows_v.at[pl.ds(0, chunk)]

            @pl.loop(0, n_chunks)
            def _(j, src_hbm=src_hbm, idx_v=idx_v, buf=buf, per_w=per_w, chunk=chunk, lists=lists):
                pltpu.sync_copy(src_hbm.at[pl.ds(wid * per_w + j * chunk, chunk)], buf)
                for k in range(lists):
                    pltpu.sync_copy(buf, out_hbm.at[idx_v.at[k, j]])

    return scatter_kernel(*args)


def _ffn_kernel(be_ref, nv_ref, x_ref, w1_ref, w3_ref, w2_ref, y_ref, w1b, w3b, w2b):
    i = pl.program_id(0)
    changed = jnp.logical_or(i == 0, be_ref[i] != be_ref[jnp.maximum(i - 1, 0)])

    @pl.when(changed)
    def _():
        w1b[...] = w1_ref[0, 0].astype(BF16)
        w3b[...] = w3_ref[0, 0].astype(BF16)
        w2b[...] = w2_ref[0, 0].astype(BF16)

    @pl.when(i < nv_ref[0])
    def _():
        half = D // 2
        x_lo, x_hi = (v.astype(BF16) for v in _unpack_bf16_pairs(x_ref[...]))
        gate = _dot(x_lo, w1b[0:half, :]) + _dot(x_hi, w1b[half:D, :])
        up = _dot(x_lo, w3b[0:half, :]) + _dot(x_hi, w3b[half:D, :])
        y_ref[...] = _pack_bf16_pairs(_dot((_silu(gate) * up).astype(BF16), w2b[...]))

    @pl.when(i >= nv_ref[0])
    def _():
        y_ref[...] = jnp.zeros_like(y_ref)


def moe_ffn(xs, blk_expert, n_valid, w1, w3, w2, layer):
    n_rows = xs.shape[0]
    nb = n_rows // MOE_TM
    return pl.pallas_call(
        _ffn_kernel,
        out_shape=jax.ShapeDtypeStruct((n_rows, D // 2), jnp.uint32),
        grid_spec=pltpu.PrefetchScalarGridSpec(
            num_scalar_prefetch=2,
            grid=(nb,),
            in_specs=[pl.BlockSpec((MOE_TM, D // 2), lambda i, be, nv: (i, 0)),
                      pl.BlockSpec((1, 1, D, D_EXPERT), lambda i, be, nv: (layer, be[i], 0, 0)),
                      pl.BlockSpec((1, 1, D, D_EXPERT), lambda i, be, nv: (layer, be[i], 0, 0)),
                      pl.BlockSpec((1, 1, D_EXPERT, D), lambda i, be, nv: (layer, be[i], 0, 0))],
            out_specs=pl.BlockSpec((MOE_TM, D // 2), lambda i, be, nv: (i, 0)),
            scratch_shapes=[pltpu.VMEM((D, D_EXPERT), BF16), pltpu.VMEM((D, D_EXPERT), BF16),
                            pltpu.VMEM((D_EXPERT, D), BF16)]),
        compiler_params=_cparams(("arbitrary",)),
        name="moe_ffn",
    )(blk_expert, n_valid, xs, w1, w3, w2)


def _combine_kernel(h_ref, y0_ref, y1_ref, wt_ref, mod_ref, lnw_ref, lnb_ref, o_ref):
    half = D // 2
    lo0, hi0 = _unpack_bf16_pairs(y0_ref[0])
    lo1, hi1 = _unpack_bf16_pairs(y1_ref[0])
    w0, w1 = wt_ref[:, 0:1], wt_ref[:, 1:2]
    r_lo = ALPHA * h_ref[:, 0:half] + mod_ref[0, 5:6, 0:half] * (w0 * lo0 + w1 * lo1)
    r_hi = ALPHA * h_ref[:, half:D] + mod_ref[0, 5:6, half:D] * (w0 * hi0 + w1 * hi1)
    mu = (jnp.sum(r_lo, axis=-1, keepdims=True) + jnp.sum(r_hi, axis=-1, keepdims=True)) * (1.0 / D)
    c_lo, c_hi = r_lo - mu, r_hi - mu
    var = (jnp.sum(c_lo * c_lo, axis=-1, keepdims=True) + jnp.sum(c_hi * c_hi, axis=-1, keepdims=True)) * (1.0 / D)
    inv = lax.rsqrt(var + LN_EPS)
    o_ref[:, 0:half] = c_lo * inv * lnw_ref[:, 0:half] + lnb_ref[:, 0:half]
    o_ref[:, half:D] = c_hi * inv * lnw_ref[:, half:D] + lnb_ref[:, half:D]


def moe_combine(h1, y_rows, wts, tile_off, mod, lnw, lnb):
    B, T, _ = h1.shape
    N = B * T
    tm = min(512, N)
    per_b = T // tm
    mod_map = (lambda i: (i // per_b, 0, 0)) if mod.shape[0] > 1 else (lambda i: (0, 0, 0))
    out = pl.pallas_call(
        _combine_kernel,
        out_shape=jax.ShapeDtypeStruct((N, D), F32),
        grid=(N // tm,),
        in_specs=[pl.BlockSpec((tm, D), lambda i: (i, 0)),
                  pl.BlockSpec((1, tm, D // 2), lambda i: (0, i + tile_off, 0)),
                  pl.BlockSpec((1, tm, D // 2), lambda i: (1, i + tile_off, 0)),
                  pl.BlockSpec((tm, TOP_K), lambda i: (i + tile_off, 0)),
                  pl.BlockSpec((1, 6, D), mod_map),
                  pl.BlockSpec((1, D), lambda i: (0, 0)), pl.BlockSpec((1, D), lambda i: (0, 0))],
        out_specs=pl.BlockSpec((tm, D), lambda i: (i, 0)),
        compiler_params=_cparams(("parallel",)),
        name="moe_combine",
    )(h1.reshape(N, D), y_rows, y_rows, wts, mod, lnw, lnb)
    return out.reshape(B, T, D)


def _route_kernel(lg_ref, dest_ref, wt_ref, cnt_ref, tri_sc, start_sc, run_sc):
    ph, i = pl.program_id(0), pl.program_id(1)
    tr = lg_ref.shape[1]

    @pl.when(jnp.logical_and(ph == 0, i == 0))
    def _():
        r = lax.broadcasted_iota(jnp.int32, (tr, tr), 0)
        c = lax.broadcasted_iota(jnp.int32, (tr, tr), 1)
        tri_sc[...] = (r < c).astype(BF16)
        start_sc[...] = jnp.zeros_like(start_sc)
        run_sc[...] = jnp.zeros_like(run_sc)

    @pl.when(jnp.logical_and(ph == 1, i == 0))
    def _():
        cnt = run_sc[...].astype(jnp.int32)
        cnt_ref[...] = cnt
        padded = jnp.bitwise_and(cnt + (MOE_TM - 1), -MOE_TM)
        row = lax.broadcasted_iota(jnp.int32, padded.shape, 0)
        acc = padded
        for s in (1, 2, 4, 8, 16):
            acc = acc + jnp.where(row >= s, pltpu.roll(acc, s, 0), 0)
        start_sc[...] = (acc - padded).astype(F32)
        run_sc[...] = jnp.zeros_like(run_sc)

    lg = lg_ref[...]
    gl = lg[N_EXPERTS:N_EXPERTS + N_GROUPS]
    gmax = jnp.max(gl, axis=0, keepdims=True)
    sub4 = lax.broadcasted_iota(jnp.int32, gl.shape, 0)
    g_sel = jnp.min(jnp.where(gl == gmax, sub4, N_GROUPS), axis=0, keepdims=True)
    p_group = 1.0 / jnp.sum(jnp.exp(gl - gmax), axis=0, keepdims=True)
    el = lg[0:EXPERTS_PER_GROUP]
    for g in range(1, N_GROUPS):
        el = jnp.where(g_sel == g, lg[g * EXPERTS_PER_GROUP:(g + 1) * EXPERTS_PER_GROUP], el)
    sub8 = lax.broadcasted_iota(jnp.int32, el.shape, 0)
    e1 = jnp.max(el, axis=0, keepdims=True)
    i1 = jnp.min(jnp.where(el == e1, sub8, EXPERTS_PER_GROUP), axis=0, keepdims=True)
    rest = jnp.where(sub8 == i1, -jnp.inf, el)
    e2 = jnp.max(rest, axis=0, keepdims=True)
    i2 = jnp.min(jnp.where(rest == e2, sub8, EXPERTS_PER_GROUP), axis=0, keepdims=True)
    den = jnp.sum(jnp.exp(el - e1), axis=0, keepdims=True)
    p1 = 1.0 / den
    p2 = jnp.exp(e2 - e1) / den
    wt_ref[0:1, :] = p_group * p1 / (p1 + p2)
    wt_ref[1:2, :] = p_group * p2 / (p1 + p2)

    sub32 = lax.broadcasted_iota(jnp.int32, (N_EXPERTS, tr), 0)
    oh = [(sub32 == g_sel * EXPERTS_PER_GROUP + ix).astype(F32) for ix in (i1, i2)]
    cnt = [jnp.sum(o, axis=1, keepdims=True) for o in oh]
    before = start_sc[:, 0:1] + run_sc[:, 0:1]
    for k in range(TOP_K):
        prior = _dot(oh[k].astype(BF16), tri_sc[...]) + before + (cnt[0] if k == 1 else 0.0)
        dest_ref[k:k + 1, :] = jnp.sum(oh[k] * prior, axis=0, keepdims=True).astype(jnp.int32)
    run_sc[...] = run_sc[...] + (cnt[0] + cnt[1])


def moe_route(logits_t):
    N = logits_t.shape[1]
    tr = next(t for t in (1024, 512, 256) if N % t == 0)
    return pl.pallas_call(
        _route_kernel,
        out_shape=(jax.ShapeDtypeStruct((TOP_K, N), jnp.int32), jax.ShapeDtypeStruct((TOP_K, N), F32),
                   jax.ShapeDtypeStruct((N_EXPERTS, LANE), jnp.int32)),
        grid=(2, N // tr),
        in_specs=[pl.BlockSpec((ROUTE_ROWS, tr), lambda p, i: (0, i))],
        out_specs=(pl.BlockSpec((TOP_K, tr), lambda p, i: (0, i * p)), pl.BlockSpec((TOP_K, tr), lambda p, i: (0, i * p)),
                   pl.BlockSpec((N_EXPERTS, LANE), lambda p, i: (0, 0))),
        scratch_shapes=[pltpu.VMEM((tr, tr), BF16), pltpu.VMEM((N_EXPERTS, LANE), F32),
                        pltpu.VMEM((N_EXPERTS, LANE), F32)],
        compiler_params=_cparams(("arbitrary", "arbitrary")),
        name="moe_route",
    )(logits_t)


def _block_tables(counts, n_assign):
    padded = (counts + MOE_TM - 1) // MOE_TM * MOE_TM
    pad_end = jnp.cumsum(padded)
    pad_start = pad_end - padded
    nb = -(-n_assign // MOE_TM) + N_EXPERTS
    blk_start = jnp.arange(nb, dtype=jnp.int32) * MOE_TM
    blk_expert = jnp.minimum(jnp.sum((pad_end[None, :] <= blk_start[:, None]).astype(jnp.int32), axis=1), N_EXPERTS - 1)
    n_valid = (pad_end[-1] // MOE_TM).astype(jnp.int32).reshape(1)
    n_fill = nb * MOE_TM - n_assign
    gap = padded - counts
    gap_end = jnp.cumsum(gap)
    k = jnp.arange(n_fill, dtype=jnp.int32)[:, None]
    sel = jnp.logical_and(k >= (gap_end - gap)[None, :], k < gap_end[None, :])
    in_gap = jnp.sum(jnp.where(sel, (pad_start + counts - (gap_end - gap))[None, :] + k, 0), axis=1)
    fill = jnp.where(k[:, 0] < gap_end[-1], in_gap, pad_end[-1] + k[:, 0] - gap_end[-1])
    return blk_expert.astype(jnp.int32), n_valid, fill.astype(jnp.int32), nb * MOE_TM


def hier_moe_and_norm(streams, w1, w3, w2, layer, lnw, lnb):
    logits_t = streams[0][2] if len(streams) == 1 else jnp.concatenate([s[2] for s in streams], 1)
    dest, wts_t, counts = moe_route(logits_t)
    wts = wts_t.T
    blk_expert, n_valid, fill_rows, n_rows = _block_tables(counts[:, 0], TOP_K * logits_t.shape[1])
    n_all = logits_t.shape[1]
    toks, idxs, offs, off = [], [], [], 0
    for h1, tok, _, _ in streams:
        n = h1.shape[0] * h1.shape[1]
        toks.append(tok.reshape(n, D // 2))
        idxs.append(dest[:, off:off + n])
        offs.append(off // min(512, n))
        off += n
    toks.append(jnp.zeros((fill_rows.shape[0], D // 2), jnp.uint32))
    idxs.append(fill_rows.reshape(1, -1))
    xs = sc_scatter_rows(toks, idxs, n_rows)
    y = moe_ffn(xs, blk_expert, n_valid, w1, w3, w2, layer)
    y_rows = sc_gather_rows(y, dest.reshape(-1)).reshape(TOP_K, n_all, D // 2)
    return [moe_combine(h1, y_rows, wts, o, mod, lnw, lnb) for (h1, _, _, mod), o in zip(streams, offs)]


def _even_weight_columns():
    n = np.arange(HEAD_DIM)
    perm = (n % 32) // 16 * 32 + n // 32 * 16 + n % 16
    idx = np.zeros(P_W, np.int32)
    scale = np.zeros(P_W, np.float32)
    for hd in range(A_Q_HEADS):
        idx[C_QA + hd * 64:C_QA + (hd + 1) * 64] = hd * 64 + perm
    scale[C_QA:C_QA + 512] = HEAD_DIM ** -0.5
    for g in range(A_KV_HEADS):
        for rep in range(2):
            o = g * 128 + rep * 64
            idx[C_KD + o:C_KD + o + 64] = 512 + g * 64 + perm
            idx[C_VD + o:C_VD + o + 64] = 640 + g * 64 + n
    scale[C_KD:C_VD + 256] = 1.0
    idx[C_QG:C_QG + 256] = 768 + np.arange(256)
    scale[C_QG:C_QG + 256] = GLA_DK ** -0.5
    idx[C_KG:C_KG + 256] = 1024 + np.arange(256)
    idx[C_VG:C_VG + 512] = 1280 + np.arange(512)
    idx[C_RG:C_RG + 512] = 1792 + np.arange(512)
    scale[C_KG:C_KG + 256] = 1.0
    scale[C_VG:C_RG + 512] = 1.0
    for d in range(2):
        idx[C_GG + d * 128:C_GG + d * 128 + GLA_RANK] = 2304 + d * GLA_RANK + np.arange(GLA_RANK)
        scale[C_GG + d * 128:C_GG + d * 128 + GLA_RANK] = 1.0
    return idx, scale


def _rope_tables(S):
    row = jnp.repeat(jnp.arange(S // GRID_W), GRID_W).astype(F32)
    col = jnp.tile(jnp.arange(GRID_W), S // GRID_W).astype(F32)
    axis_dim = HEAD_DIM // 2
    inv_freq = ROPE_BASE ** (-jnp.arange(0, axis_dim, 2, dtype=F32) / axis_dim)
    ang = jnp.concatenate([row[:, None] * inv_freq, col[:, None] * inv_freq], -1)
    cos, sin = jnp.cos(ang), jnp.sin(ang)
    cos_t = jnp.tile(cos, (1, 4))
    sin_t = jnp.tile(jnp.concatenate([-sin, sin], -1), (1, 2))
    return cos_t, sin_t


def _router_weights(wg, bg, we, be):
    w = jnp.zeros((D, ROUTE_W), F32).at[:, :N_EXPERTS].set(we).at[:, N_EXPERTS:N_EXPERTS + N_GROUPS].set(wg)
    b = jnp.zeros((1, ROUTE_W), F32).at[0, :N_EXPERTS].set(be).at[0, N_EXPERTS:N_EXPERTS + N_GROUPS].set(bg)
    hi = w.astype(BF16)
    return hi, (w - hi.astype(F32)).astype(BF16), b


def kernel(x, c, ctx, c_ctx, w_in_even, w_out_even, attn_sink, gla_wa2, gla_ba, gla_norm_w, w_in_odd, conv_w, conv_b, w_out_odd, ada_w, ada_b, ln_w, ln_b, router_wg, router_bg, router_we, router_be, moe_w1, moe_w3, moe_w2):
    B, S, _ = x.shape
    L = ctx.shape[1]
    cos_t, sin_t = _rope_tables(S)
    cos_c, sin_c = jnp.ones((L, LANE), F32), jnp.zeros((L, LANE), F32)
    col_idx, col_scale = _even_weight_columns()

    n_cond = -(-(B + 1) // 8) * 8
    cc = jnp.zeros((n_cond, D), F32).at[:B].set(c).at[B].set(c_ctx)
    mods = ada_modulation_all(cc, ada_w, ada_b).reshape(DEPTH, n_cond, 6, D)

    h_lat, h_ctx = x, ctx
    for l in range(DEPTH):
        i = l // 2
        need_ctx = any(j % 2 == 0 for j in range(l + 1, DEPTH))
        m_lat = mods[l, :B]
        m_ctx = mods[l, B:B + 1]
        lnw0, lnb0 = ln_w[l, 0:1], ln_b[l, 0:1]
        lnw1, lnb1 = ln_w[l, 1:2], ln_b[l, 1:2]
        rwh, rwl, rb = _router_weights(router_wg[l], router_bg[l], router_we[l], router_be[l])
        streams = []
        if l % 2 == 0:
            w_in = (w_in_even[i][:, col_idx] * col_scale[None, :]).astype(BF16)
            w_out = w_out_even[i].astype(BF16)
            wa_p = jnp.zeros((2, LANE, GLA_HEADS * GLA_DK), F32).at[:, :GLA_RANK].set(gla_wa2[i]).astype(BF16)
            ba = gla_ba[i].reshape(2, 1, -1)
            nw = gla_norm_w[i].reshape(1, -1)
            p_ctx = inproj_even(h_ctx, m_ctx, w_in, cos_c, sin_c)
            p_lat = inproj_even(h_lat, m_lat, w_in, cos_t, sin_t)
            a_lat = attention(p_lat, p_ctx, attn_sink[i], True)
            s0 = jnp.zeros((B, 2, 2, GLA_DV, LANE), F32)
            oc_f, oc_b, s_ctx = gla_scan(p_ctx, wa_p, ba, s0)
            ol_f, ol_b, _ = gla_scan(p_lat, wa_p, ba, s_ctx)
            streams.append(outproj_even(a_lat, ol_f, ol_b, p_lat, nw, w_out, h_lat, m_lat, lnw0, lnb0, rwh, rwl, rb)
                           + (m_lat,))
            if need_ctx:
                a_ctx = attention(p_ctx, p_ctx, attn_sink[i], False)
                streams.append(outproj_even(a_ctx, oc_f, oc_b, p_ctx, nw, w_out, h_ctx, m_ctx, lnw0, lnb0,
                                            rwh, rwl, rb) + (m_ctx,))
        else:
            w_in = w_in_odd[i].astype(BF16)
            w_out = w_out_odd[i].astype(BF16)
            cb = conv_b[i].reshape(1, D)
            pairs = [(h_lat, m_lat)] + ([(h_ctx, m_ctx)] if need_ctx else [])
            for h, m in pairs:
                gz = inproj_odd(h, m, w_in)
                streams.append(outproj_odd(gz, conv_w[i], cb, w_out, h, m, lnw0, lnb0, rwh, rwl, rb) + (m,))
        outs = hier_moe_and_norm(streams, moe_w1, moe_w3, moe_w2, l, lnw1, lnb1)
        h_lat = outs[0]
        if need_ctx:
            h_ctx = outs[1]
    return h_lat
```

```python
import functools

import numpy as np
import jax
import jax.numpy as jnp
from jax import lax
from jax.experimental import pallas as pl
from jax.experimental.pallas import tpu as pltpu
from jax.experimental.pallas import tpu_sc as plsc

F32 = jnp.float32
BF16 = jnp.bfloat16
HIGHEST = lax.Precision.HIGHEST

D = 1024
DEPTH = 4
GRID_W = 64
HEAD_DIM = 64
A_Q_HEADS = 8
A_KV_HEADS = 2
WINDOW = 128
ROPE_BASE = 10000.0
GLA_HEADS = 4
GLA_DK = 64
GLA_DV = 128
GLA_RANK = 16
GLA_TAU = 16.0
GLA_CHUNK = 64
N_GROUPS = 4
EXPERTS_PER_GROUP = 8
N_EXPERTS = 32
TOP_K = 2
D_EXPERT = 512
ALPHA = (2.0 * DEPTH) ** 0.25
LN_EPS = 1e-5
RMS_EPS = 1e-6

LANE = 128
VMEM_LIMIT = 48 * 1024 * 1024

C_QA, C_KD, C_VD, C_VG, C_RG, C_QG, C_KG, C_GG = 0, 512, 768, 1024, 1536, 2048, 2304, 2560
P_W = 2816
ROUTE_W = 128
ROUTE_ROWS = 40
MOE_TM = 512
SC_CORES, SC_SUBCORES = 2, 16
SC_WORKERS = SC_CORES * SC_SUBCORES
SC_CHUNK = 64
NEG = -1e30
LOG2_E = 1.4426950408889634


def _cparams(sem):
    return pltpu.CompilerParams(dimension_semantics=sem, vmem_limit_bytes=VMEM_LIMIT)


def _dot(a, b):
    return jnp.dot(a, b, preferred_element_type=F32)


def _dot_nt(a, b):
    return lax.dot_general(a, b, (((1,), (1,)), ((), ())), preferred_element_type=F32)


def _dot_tn(a, b):
    return lax.dot_general(a, b, (((0,), (0,)), ((), ())), preferred_element_type=F32)


def _silu(x):
    return x * (1.0 / (1.0 + jnp.exp(-x)))


def _ada_kernel(c_ref, w_ref, b_ref, o_ref):
    s = _silu(c_ref[...])
    o_ref[0] = jnp.dot(s, w_ref[0], precision=HIGHEST, preferred_element_type=F32) + b_ref[0]


def ada_modulation_all(cc, ada_w, ada_b):
    R = cc.shape[0]
    tn = 1536
    return pl.pallas_call(
        _ada_kernel,
        out_shape=jax.ShapeDtypeStruct((DEPTH, R, 6 * D), F32),
        grid=(DEPTH, 6 * D // tn),
        in_specs=[pl.BlockSpec((R, D), lambda l, n: (0, 0)),
                  pl.BlockSpec((1, D, tn), lambda l, n: (l, 0, n)),
                  pl.BlockSpec((1, 1, tn), lambda l, n: (l, 0, n))],
        out_specs=pl.BlockSpec((1, R, tn), lambda l, n: (l, 0, n)),
        compiler_params=_cparams(("arbitrary", "arbitrary")),
        name="ada_modulation",
    )(cc, ada_w, ada_b.reshape(DEPTH, 1, 6 * D))


_EVEN_CHUNKS = ((0, 512, True), (512, 768, True), (768, 1024, False), (1024, 1536, False),
                (1536, 2048, False), (2048, 2560, False), (2560, 2816, False))


def _inproj_even_kernel(h_ref, mod_ref, w_ref, cos_ref, sin_ref, p_ref):
    tm = h_ref.shape[1]
    u = (h_ref[0] * (1.0 + mod_ref[0, 1:2, :]) + mod_ref[0, 0:1, :]).astype(BF16)
    cos = cos_ref[...]
    sin = sin_ref[...]
    lane = lax.broadcasted_iota(jnp.int32, (tm, LANE), 1)
    first_half = (lane % HEAD_DIM) < (HEAD_DIM // 2)
    for c0, c1, rope in _EVEN_CHUNKS:
        acc = _dot(u, w_ref[:, c0:c1])
        if rope:
            for i in range((c1 - c0) // LANE):
                x = acc[:, i * LANE:(i + 1) * LANE]
                partner = jnp.where(first_half, pltpu.roll(x, LANE - 32, 1), pltpu.roll(x, 32, 1))
                p_ref[0, :, c0 + i * LANE:c0 + (i + 1) * LANE] = (x * cos + partner * sin).astype(BF16)
        else:
            p_ref[0, :, c0:c1] = acc.astype(BF16)


def inproj_even(h, mod, w, cos_t, sin_t):
    B, T, _ = h.shape
    tm = min(512, T)
    mb = mod.shape[0]
    return pl.pallas_call(
        _inproj_even_kernel,
        out_shape=jax.ShapeDtypeStruct((B, T, P_W), BF16),
        grid=(B, T // tm),
        in_specs=[pl.BlockSpec((1, tm, D), lambda b, j: (b, j, 0)),
                  pl.BlockSpec((1, 6, D), (lambda b, j: (b, 0, 0)) if mb > 1 else (lambda b, j: (0, 0, 0))),
                  pl.BlockSpec((D, P_W), lambda b, j: (0, 0)),
                  pl.BlockSpec((tm, LANE), lambda b, j: (j, 0)),
                  pl.BlockSpec((tm, LANE), lambda b, j: (j, 0))],
        out_specs=pl.BlockSpec((1, tm, P_W), lambda b, j: (b, j, 0)),
        compiler_params=_cparams(("parallel", "arbitrary")),
        name="inproj_even",
    )(h, mod, w, cos_t, sin_t)


def _attn_kernel(*refs, tq, has_window):
    if has_window:
        sink_ref, q_ref, kw_ref, vw_ref, kc_ref, vc_ref, o_ref = refs
    else:
        sink_ref, q_ref, kc_ref, vc_ref, o_ref = refs
    group = A_Q_HEADS // A_KV_HEADS
    rows = group * tq
    lo = lax.broadcasted_iota(jnp.int32, (tq, LANE), 1) < HEAD_DIM
    den_lanes = lax.broadcasted_iota(jnp.int32, (rows, LANE), 1) >= HEAD_DIM

    def with_ones(v):
        return jnp.where(lax.broadcasted_iota(jnp.int32, v.shape, 1) < HEAD_DIM, v, jnp.ones_like(v))

    if has_window:
        S = kw_ref.shape[1]
        wk = tq + 2 * WINDOW
        q0 = pl.program_id(1) * tq
        wstart = pl.multiple_of(jnp.clip(q0 - WINDOW, 0, S - wk), LANE)
        qpos = q0 + lax.broadcasted_iota(jnp.int32, (tq, wk), 0)
        kpos = wstart + lax.broadcasted_iota(jnp.int32, (tq, wk), 1)
        band = jnp.tile(jnp.where(jnp.abs(qpos - kpos) <= WINDOW, 0.0, NEG), (group, 1))
    groups = range(A_KV_HEADS)
    cols = [slice(g * LANE, (g + 1) * LANE) for g in groups]
    q4, snk, sc, sw, m, outs = [], [], [], [], [], []
    for g in groups:
        qs = []
        for pr in range(group // 2):
            qblk = q_ref[0, :, (2 * g + pr) * LANE:(2 * g + pr + 1) * LANE]
            zero = jnp.zeros_like(qblk)
            qs += [jnp.where(lo, qblk, zero), jnp.where(lo, zero, qblk)]
        q4.append(jnp.concatenate(qs, axis=0))
        snk.append(jnp.concatenate([jnp.full((tq, 1), sink_ref[group * g + i] * LOG2_E, F32) for i in range(group)],
                                   axis=0))
    for g in groups:
        sc.append(_dot_nt(q4[g], kc_ref[0, :, cols[g]]))
        if has_window:
            sw.append(_dot_nt(q4[g], kw_ref[0, pl.ds(wstart, wk), cols[g]]) + band)
    for g in groups:
        mg = jnp.maximum(jnp.max(sc[g], axis=-1, keepdims=True), snk[g])
        if has_window:
            mg = jnp.maximum(mg, jnp.max(sw[g], axis=-1, keepdims=True))
        m.append(mg)
    for g in groups:
        o = _dot(jnp.exp2(sc[g] - m[g]).astype(BF16), with_ones(vc_ref[0, :, cols[g]]))
        if has_window:
            o = o + _dot(jnp.exp2(sw[g] - m[g]).astype(BF16), with_ones(vw_ref[0, pl.ds(wstart, wk), cols[g]]))
        outs.append(o + jnp.where(den_lanes, jnp.exp2(snk[g] - m[g]), 0.0))
    for g in groups:
        o = outs[g]
        swapped = pltpu.roll(o, HEAD_DIM, 1)
        for pr in range(group // 2):
            ev = slice(2 * pr * tq, (2 * pr + 1) * tq)
            od = slice((2 * pr + 1) * tq, (2 * pr + 2) * tq)
            res = jnp.where(lo, o[ev] / swapped[ev], swapped[od] / o[od])
            o_ref[0, :, (2 * g + pr) * LANE:(2 * g + pr + 1) * LANE] = res.astype(BF16)


def attention(p_q, p_ctx, sink, has_window):
    B, T, _ = p_q.shape
    L = p_ctx.shape[1]
    tq = 128
    in_specs = [pl.BlockSpec(memory_space=pltpu.SMEM),
                pl.BlockSpec((1, tq, 512), lambda b, j: (b, j, C_QA // 512))]
    args = [sink, p_q]
    if has_window:
        in_specs += [pl.BlockSpec((1, T, 256), lambda b, j: (b, 0, C_KD // 256)),
                     pl.BlockSpec((1, T, 256), lambda b, j: (b, 0, C_VD // 256))]
        args += [p_q, p_q]
    in_specs += [pl.BlockSpec((1, L, 256), lambda b, j: (b, 0, C_KD // 256)),
                 pl.BlockSpec((1, L, 256), lambda b, j: (b, 0, C_VD // 256))]
    args += [p_ctx, p_ctx]
    return pl.pallas_call(
        functools.partial(_attn_kernel, tq=tq, has_window=has_window),
        out_shape=jax.ShapeDtypeStruct((B, T, 512), BF16),
        grid=(B, T // tq),
        in_specs=in_specs,
        out_specs=pl.BlockSpec((1, tq, 512), lambda b, j: (b, j, 0)),
        compiler_params=_cparams(("parallel", "arbitrary")),
        name="window_attention" if has_window else "context_attention",
    )(*args)


def _log_sigmoid(x):
    return jnp.minimum(x, 0.0) - jnp.log(1.0 + jnp.exp(-jnp.abs(x)))


def _gla_kernel(qf_ref, kf_ref, vf_ref, gf_ref, qb_ref, kb_ref, vb_ref, gb_ref, wa_ref, ba_ref, s0_ref,
                of_ref, ob_ref, sfin_ref, s_sc):
    j = pl.program_id(1)
    nblk = pl.num_programs(1)
    tb = qf_ref.shape[1]
    nc = tb // GLA_CHUNK

    @pl.when(j == 0)
    def _():
        s_sc[...] = s0_ref[0]

    C = GLA_CHUNK
    ri = lax.broadcasted_iota(jnp.int32, (2 * C, C), 0) % C
    ci = lax.broadcasted_iota(jnp.int32, (2 * C, C), 1)
    lo = lax.broadcasted_iota(jnp.int32, (C, LANE), 1) < GLA_DK

    def per_head(x):
        zero = jnp.zeros_like(x)
        return jnp.concatenate([jnp.where(lo, x, zero), jnp.where(lo, zero, x)], axis=0)

    io = ((qf_ref, kf_ref, vf_ref, gf_ref, of_ref), (qb_ref, kb_ref, vb_ref, gb_ref, ob_ref))
    causal = ((ri >= ci), (ci >= ri))
    tri = [cm[0:C].astype(BF16) for cm in causal]
    pieces = []
    for d in range(2):
        g = _dot(io[d][3][0, :, d * LANE:(d + 1) * LANE], wa_ref[d]) + ba_ref[d]
        log_a = _log_sigmoid(g) / GLA_TAU
        la1 = log_a.astype(BF16)
        rem = log_a - la1.astype(F32)
        la2 = rem.astype(BF16)
        pieces.append((la1, la2, (rem - la2.astype(F32)).astype(BF16)))
    state = [[s_sc[d, pair] for pair in range(2)] for d in range(2)]
    for step in range(nc):
        for d in range(2):
            q_ref, k_ref, v_ref, _, o_ref = io[d]
            c = step if d == 0 else nc - 1 - step
            rows = slice(c * C, (c + 1) * C)
            b = sum(_dot(tri[d], la[rows]) for la in pieces[d])
            b_last = b[C - 1:C] if d == 0 else b[0:1]
            qc = q_ref[0, rows, :].astype(F32)
            kc = k_ref[0, rows, :].astype(F32)
            q_in = (qc * jnp.exp(b)).astype(BF16)
            k_in = (kc * jnp.exp(-b)).astype(BF16)
            k_st = (kc * jnp.exp(b_last - b)).astype(BF16)
            dl = jnp.exp(b_last)
            for pair in range(2):
                cols = slice(pair * LANE, (pair + 1) * LANE)
                st = state[d][pair]
                q2 = per_head(q_in[:, cols])
                attn = jnp.where(causal[d], _dot_nt(q2, k_in[:, cols]), 0.0).astype(BF16)
                inter = _dot_nt(q2, st.astype(BF16))
                vs = [v_ref[0, rows, (2 * pair + hh) * GLA_DV:(2 * pair + hh + 1) * GLA_DV] for hh in range(2)]
                for hh in range(2):
                    hr = slice(hh * C, (hh + 1) * C)
                    o_ref[0, rows, (2 * pair + hh) * GLA_DV:(2 * pair + hh + 1) * GLA_DV] = (
                        _dot(attn[hr], vs[hh]) + inter[hr]).astype(o_ref.dtype)
                upd = _dot_tn(jnp.concatenate(vs, axis=0), per_head(k_st[:, cols]))
                state[d][pair] = st * dl[:, cols] + upd
    for d in range(2):
        for pair in range(2):
            s_sc[d, pair] = state[d][pair]

    @pl.when(j == nblk - 1)
    def _():
        sfin_ref[0] = s_sc[...]


def gla_scan(p, wa_p, ba, s0):
    B, T, _ = p.shape
    tb = min(512, T)
    nblk = T // tb
    fwd = lambda b, j: (b, j)
    bwd = lambda b, j: (b, nblk - 1 - j)

    def specs(im):
        return [pl.BlockSpec((1, tb, 256), lambda b, j: im(b, j) + (C_QG // 256,)),
                pl.BlockSpec((1, tb, 256), lambda b, j: im(b, j) + (C_KG // 256,)),
                pl.BlockSpec((1, tb, 512), lambda b, j: im(b, j) + (C_VG // 512,)),
                pl.BlockSpec((1, tb, 256), lambda b, j: im(b, j) + (C_GG // 256,))]

    return pl.pallas_call(
        _gla_kernel,
        out_shape=(jax.ShapeDtypeStruct((B, T, 512), BF16), jax.ShapeDtypeStruct((B, T, 512), BF16),
                   jax.ShapeDtypeStruct(s0.shape, F32)),
        grid=(B, nblk),
        in_specs=specs(fwd) + specs(bwd) + [
            pl.BlockSpec((2, LANE, 256), lambda b, j: (0, 0, 0)),
            pl.BlockSpec((2, 1, 256), lambda b, j: (0, 0, 0)),
            pl.BlockSpec((1, 2, 2, GLA_DV, LANE), lambda b, j: (b, 0, 0, 0, 0))],
        out_specs=(pl.BlockSpec((1, tb, 512), lambda b, j: (b, j, 0)),
                   pl.BlockSpec((1, tb, 512), lambda b, j: (b, nblk - 1 - j, 0)),
                   pl.BlockSpec((1, 2, 2, GLA_DV, LANE), lambda b, j: (b, 0, 0, 0, 0))),
        scratch_shapes=[pltpu.VMEM((2, 2, GLA_DV, LANE), F32)],
        compiler_params=_cparams(("parallel", "arbitrary")),
        name="gla_scan",
    )(p, p, p, p, p, p, p, p, wa_p, ba, s0)


def _layer_norm(r, w, b):
    mu = jnp.mean(r, axis=-1, keepdims=True)
    xc = r - mu
    var = jnp.mean(xc * xc, axis=-1, keepdims=True)
    return xc * lax.rsqrt(var + LN_EPS) * w + b


def _pack_bf16_pairs(x):
    half = x.shape[1] // 2
    lo = lax.bitcast_convert_type(x[:, :half].astype(BF16).astype(F32), jnp.uint32)
    hi = lax.bitcast_convert_type(x[:, half:].astype(BF16).astype(F32), jnp.uint32)
    return jnp.bitwise_or(hi, lax.shift_right_logical(lo, jnp.uint32(16)))


def _unpack_bf16_pairs(p):
    lo = lax.bitcast_convert_type(lax.shift_left(p, jnp.uint32(16)), F32)
    hi = lax.bitcast_convert_type(jnp.bitwise_and(p, jnp.uint32(0xFFFF0000)), F32)
    return lo, hi


def _post_norm_and_route(h, y, mod_ref, lnw_ref, lnb_ref, rwh_ref, rwl_ref, rb_ref, h1_ref, tok_ref, lg_ref):
    h1 = _layer_norm(ALPHA * h + mod_ref[0, 2:3, :] * y, lnw_ref[...], lnb_ref[...])
    h1_ref[0] = h1
    tok = h1 * (1.0 + mod_ref[0, 4:5, :]) + mod_ref[0, 3:4, :]
    tok_ref[0] = _pack_bf16_pairs(tok)
    hi = tok.astype(BF16)
    lo = (tok - hi.astype(F32)).astype(BF16)
    lg = _dot(hi, rwh_ref[...]) + _dot(lo, rwh_ref[...]) + _dot(hi, rwl_ref[...]) + rb_ref[...]
    lg_ref[...] = lg.T[0:ROUTE_ROWS, :]


def _epilogue_specs(tm, mb):
    mod_map = (lambda b, j: (b, 0, 0)) if mb > 1 else (lambda b, j: (0, 0, 0))
    const2 = lambda b, j: (0, 0)
    return [pl.BlockSpec((1, tm, D), lambda b, j: (b, j, 0)),
            pl.BlockSpec((1, 6, D), mod_map),
            pl.BlockSpec((1, D), const2), pl.BlockSpec((1, D), const2),
            pl.BlockSpec((D, ROUTE_W), const2), pl.BlockSpec((D, ROUTE_W), const2),
            pl.BlockSpec((1, ROUTE_W), const2)]


def _epilogue_outs(B, T, tm):
    nj = T // tm
    shapes = (jax.ShapeDtypeStruct((B, T, D), F32), jax.ShapeDtypeStruct((B, T, D // 2), jnp.uint32),
              jax.ShapeDtypeStruct((ROUTE_ROWS, B * T), F32))
    specs = (pl.BlockSpec((1, tm, D), lambda b, j: (b, j, 0)), pl.BlockSpec((1, tm, D // 2), lambda b, j: (b, j, 0)),
             pl.BlockSpec((ROUTE_ROWS, tm), lambda b, j: (0, b * nj + j)))
    return shapes, specs


def _outproj_even_kernel(a_ref, of_ref, ob_ref, rg_ref, nw_ref, wo_ref,
                         h_ref, mod_ref, lnw_ref, lnb_ref, rwh_ref, rwl_ref, rb_ref,
                         h1_ref, tok_ref, lg_ref):
    y = _dot(a_ref[0], wo_ref[0:512, :])
    for hd in range(GLA_HEADS):
        cols = slice(hd * GLA_DV, (hd + 1) * GLA_DV)
        o = of_ref[0, :, cols].astype(F32) + ob_ref[0, :, cols].astype(F32)
        o = o * lax.rsqrt(jnp.mean(o * o, axis=-1, keepdims=True) + RMS_EPS)
        gated = o * nw_ref[:, cols] * _silu(rg_ref[0, :, cols].astype(F32))
        y = y + _dot(gated.astype(BF16), wo_ref[512 + hd * GLA_DV:512 + (hd + 1) * GLA_DV, :])
    _post_norm_and_route(h_ref[0], y, mod_ref, lnw_ref, lnb_ref, rwh_ref, rwl_ref, rb_ref, h1_ref, tok_ref, lg_ref)


def outproj_even(a, o_f, o_b, p, norm_w, w_out, h, mod, lnw, lnb, rwh, rwl, rb):
    B, T, _ = h.shape
    tm = min(512, T)
    tile = lambda b, j: (b, j, 0)
    shapes, ospecs = _epilogue_outs(B, T, tm)
    return pl.pallas_call(
        _outproj_even_kernel,
        out_shape=shapes,
        grid=(B, T // tm),
        in_specs=[pl.BlockSpec((1, tm, 512), tile), pl.BlockSpec((1, tm, 512), tile), pl.BlockSpec((1, tm, 512), tile),
                  pl.BlockSpec((1, tm, 512), lambda b, j: (b, j, C_RG // 512)),
                  pl.BlockSpec((1, 512), lambda b, j: (0, 0)),
                  pl.BlockSpec((D, D), lambda b, j: (0, 0))] + _epilogue_specs(tm, mod.shape[0]),
        out_specs=ospecs,
        compiler_params=_cparams(("parallel", "arbitrary")),
        name="outproj_even",
    )(a, o_f, o_b, p, norm_w, w_out, h, mod, lnw, lnb, rwh, rwl, rb)


def _inproj_odd_kernel(h_ref, mod_ref, w_ref, o_ref):
    u = (h_ref[0] * (1.0 + mod_ref[0, 1:2, :]) + mod_ref[0, 0:1, :]).astype(BF16)
    o_ref[0, :, 0:D] = _dot(u, w_ref[:, 0:D]).astype(BF16)
    o_ref[0, :, D:2 * D] = (_dot(u, w_ref[:, D:2 * D]) * _dot(u, w_ref[:, 2 * D:3 * D])).astype(BF16)


def inproj_odd(h, mod, w):
    B, T, _ = h.shape
    tm = min(512, T)
    mb = mod.shape[0]
    return pl.pallas_call(
        _inproj_odd_kernel,
        out_shape=jax.ShapeDtypeStruct((B, T, 2 * D), BF16),
        grid=(B, T // tm),
        in_specs=[pl.BlockSpec((1, tm, D), lambda b, j: (b, j, 0)),
                  pl.BlockSpec((1, 6, D), (lambda b, j: (b, 0, 0)) if mb > 1 else (lambda b, j: (0, 0, 0))),
                  pl.BlockSpec((D, 3 * D), lambda b, j: (0, 0))],
        out_specs=pl.BlockSpec((1, tm, 2 * D), lambda b, j: (b, j, 0)),
        compiler_params=_cparams(("parallel", "arbitrary")),
        name="inproj_odd",
    )(h, mod, w)


HALO = 16


def _outproj_odd_kernel(gb_ref, z_ref, zp_ref, zn_ref, cw_ref, cb_ref, wo_ref,
                        h_ref, mod_ref, lnw_ref, lnb_ref, rwh_ref, rwl_ref, rb_ref,
                        h1_ref, tok_ref, lg_ref):
    j = pl.program_id(1)
    tm = z_ref.shape[1]
    z = z_ref[0].astype(F32)
    prev_row = jnp.where(j > 0, zp_ref[0, HALO - 1:HALO, :].astype(F32), 0.0)
    next_row = jnp.where(j < pl.num_programs(1) - 1, zn_ref[0, 0:1, :].astype(F32), 0.0)
    row = lax.broadcasted_iota(jnp.int32, (tm, D), 0)
    z_prev = jnp.where(row == 0, prev_row, pltpu.roll(z, 1, 0))
    z_next = jnp.where(row == tm - 1, next_row, pltpu.roll(z, tm - 1, 0))
    conv = z_prev * cw_ref[0:1, :] + z * cw_ref[1:2, :] + z_next * cw_ref[2:3, :] + cb_ref[...]
    y = _dot((gb_ref[0].astype(F32) * conv).astype(BF16), wo_ref[...])
    _post_norm_and_route(h_ref[0], y, mod_ref, lnw_ref, lnb_ref, rwh_ref, rwl_ref, rb_ref, h1_ref, tok_ref, lg_ref)


def outproj_odd(gz, conv_w, conv_b, w_out, h, mod, lnw, lnb, rwh, rwl, rb):
    B, T, _ = h.shape
    tm = min(512, T)
    r = tm // HALO
    nh = T // HALO
    shapes, ospecs = _epilogue_outs(B, T, tm)
    return pl.pallas_call(
        _outproj_odd_kernel,
        out_shape=shapes,
        grid=(B, T // tm),
        in_specs=[pl.BlockSpec((1, tm, D), lambda b, j: (b, j, 0)),
                  pl.BlockSpec((1, tm, D), lambda b, j: (b, j, 1)),
                  pl.BlockSpec((1, HALO, D), lambda b, j: (b, jnp.maximum(j * r - 1, 0), 1)),
                  pl.BlockSpec((1, HALO, D), lambda b, j: (b, jnp.minimum((j + 1) * r, nh - 1), 1)),
                  pl.BlockSpec((3, D), lambda b, j: (0, 0)),
                  pl.BlockSpec((1, D), lambda b, j: (0, 0)),
                  pl.BlockSpec((D, D), lambda b, j: (0, 0))] + _epilogue_specs(tm, mod.shape[0]),
        out_specs=ospecs,
        compiler_params=_cparams(("parallel", "arbitrary")),
        name="outproj_odd",
    )(gz, gz, gz, gz, conv_w, conv_b, w_out, h, mod, lnw, lnb, rwh, rwl, rb)


def _sc_mesh():
    return plsc.VectorSubcoreMesh(core_axis_name="c", subcore_axis_name="s")


def sc_gather_rows(table, idx):
    n = idx.shape[0]
    width = table.shape[1]
    per_w = n // SC_WORKERS
    n_chunks = per_w // SC_CHUNK
    assert n_chunks % 2 == 0

    @functools.partial(
        pl.kernel, mesh=_sc_mesh(),
        out_type=jax.ShapeDtypeStruct((n, width), table.dtype),
        scratch_types=[pltpu.VMEM((n_chunks, SC_CHUNK), jnp.int32),
                       pltpu.VMEM((SC_CHUNK, width), table.dtype), pltpu.VMEM((SC_CHUNK, width), table.dtype),
                       pltpu.SemaphoreType.DMA, pltpu.SemaphoreType.DMA],
    )
    def gather_kernel(table_hbm, idx_hbm, out_hbm, idx_v, buf0, buf1, sem0, sem1):
        wid = lax.axis_index("s") * SC_CORES + lax.axis_index("c")
        pltpu.sync_copy(idx_hbm.at[wid], idx_v)

        def fetch(j, buf, sem):
            return pltpu.make_async_copy(table_hbm.at[idx_v.at[j]], buf, sem)

        def flush(j, buf):
            pltpu.sync_copy(buf, out_hbm.at[pl.ds(wid * per_w + j * SC_CHUNK, SC_CHUNK)])

        fetch(0, buf0, sem0).start()

        @pl.loop(0, n_chunks, step=2)
        def _(j):
            fetch(j + 1, buf1, sem1).start()
            fetch(j, buf0, sem0).wait()
            flush(j, buf0)

            @pl.when(j + 2 < n_chunks)
            def _():
                fetch(j + 2, buf0, sem0).start()

            fetch(j + 1, buf1, sem1).wait()
            flush(j + 1, buf1)

    return gather_kernel(table, idx.reshape(SC_WORKERS, n_chunks, SC_CHUNK))


def sc_scatter_rows(srcs, idxs, n_rows):
    width, dt = srcs[0].shape[1], srcs[0].dtype
    plans, args = [], []
    for src, idx in zip(srcs, idxs):
        per_w = src.shape[0] // SC_WORKERS
        chunk = min(SC_CHUNK, per_w // 2)
        assert (per_w // chunk) % 2 == 0
        plans.append((per_w, chunk, per_w // chunk, idx.shape[0]))
        args += [src, idx.reshape(idx.shape[0], SC_WORKERS, per_w // chunk, chunk)]
    max_chunk = max(p[1] for p in plans)
    scratch = [pltpu.VMEM((max_chunk, width), dt), pltpu.VMEM((max_chunk, width), dt),
               pltpu.SemaphoreType.DMA, pltpu.SemaphoreType.DMA]
    scratch += [pltpu.VMEM((lists, n_chunks, chunk), jnp.int32) for _, chunk, n_chunks, lists in plans]

    @functools.partial(pl.kernel, mesh=_sc_mesh(), out_type=jax.ShapeDtypeStruct((n_rows, width), dt),
                       scratch_types=scratch)
    def scatter_kernel(*refs):
        ins, out_hbm = refs[:2 * len(plans)], refs[2 * len(plans)]
        rows0, rows1, sem0, sem1 = refs[2 * len(plans) + 1:2 * len(plans) + 5]
        idx_vs = refs[2 * len(plans) + 5:]
        wid = lax.axis_index("s") * SC_CORES + lax.axis_index("c")
        for s, (per_w, chunk, n_chunks, lists) in enumerate(plans):
            src_hbm, idx_hbm, idx_v = ins[2 * s], ins[2 * s + 1], idx_vs[s]
            for k in range(lists):
                pltpu.sync_copy(idx_hbm.at[k, wid], idx_v.at[k])
            buf0 = rows0 if chunk == max_chunk else rows0.at[pl.ds(0, chunk)]
            buf1 = rows1 if chunk == max_chunk else rows1.at[pl.ds(0, chunk)]

            def load(j, buf, sem, src_hbm=src_hbm, per_w=per_w, chunk=chunk):
                return pltpu.make_async_copy(src_hbm.at[pl.ds(wid * per_w + j * chunk, chunk)], buf, sem)

            def spread(j, buf, idx_v=idx_v, lists=lists):
                for k in range(lists):
                    pltpu.sync_copy(buf, out_hbm.at[idx_v.at[k, j]])

            load(0, buf0, sem0).start()

            @pl.loop(0, n_chunks, step=2)
            def _(j, load=load, spread=spread, buf0=buf0, buf1=buf1, n_chunks=n_chunks):
                load(j + 1, buf1, sem1).start()
                load(j, buf0, sem0).wait()
                spread(j, buf0)

                @pl.when(j + 2 < n_chunks)
                def _():
                    load(j + 2, buf0, sem0).start()

                load(j + 1, buf1, sem1).wait()
                spread(j + 1, buf1)

    return scatter_kernel(*args)


def _ffn_kernel(be_ref, nv_ref, x_ref, w1_ref, w3_ref, w2_ref, y_ref, w1b, w3b, w2b):
    i = pl.program_id(0)
    changed = jnp.logical_or(i == 0, be_ref[i] != be_ref[jnp.maximum(i - 1, 0)])

    @pl.when(changed)
    def _():
        w1b[...] = w1_ref[0, 0].astype(BF16)
        w3b[...] = w3_ref[0, 0].astype(BF16)
        w2b[...] = w2_ref[0, 0].astype(BF16)

    @pl.when(i < nv_ref[0])
    def _():
        half = D // 2
        x_lo, x_hi = (v.astype(BF16) for v in _unpack_bf16_pairs(x_ref[...]))
        gate = _dot(x_lo, w1b[0:half, :]) + _dot(x_hi, w1b[half:D, :])
        up = _dot(x_lo, w3b[0:half, :]) + _dot(x_hi, w3b[half:D, :])
        y_ref[...] = _pack_bf16_pairs(_dot((_silu(gate) * up).astype(BF16), w2b[...]))

    @pl.when(i >= nv_ref[0])
    def _():
        y_ref[...] = jnp.zeros_like(y_ref)


def moe_ffn(xs, blk_expert, n_valid, w1, w3, w2, layer):
    n_rows = xs.shape[0]
    nb = n_rows // MOE_TM
    return pl.pallas_call(
        _ffn_kernel,
        out_shape=jax.ShapeDtypeStruct((n_rows, D // 2), jnp.uint32),
        grid_spec=pltpu.PrefetchScalarGridSpec(
            num_scalar_prefetch=2,
            grid=(nb,),
            in_specs=[pl.BlockSpec((MOE_TM, D // 2), lambda i, be, nv: (i, 0)),
                      pl.BlockSpec((1, 1, D, D_EXPERT), lambda i, be, nv: (layer, be[i], 0, 0)),
                      pl.BlockSpec((1, 1, D, D_EXPERT), lambda i, be, nv: (layer, be[i], 0, 0)),
                      pl.BlockSpec((1, 1, D_EXPERT, D), lambda i, be, nv: (layer, be[i], 0, 0))],
            out_specs=pl.BlockSpec((MOE_TM, D // 2), lambda i, be, nv: (i, 0)),
            scratch_shapes=[pltpu.VMEM((D, D_EXPERT), BF16), pltpu.VMEM((D, D_EXPERT), BF16),
                            pltpu.VMEM((D_EXPERT, D), BF16)]),
        compiler_params=_cparams(("arbitrary",)),
        name="moe_ffn",
    )(blk_expert, n_valid, xs, w1, w3, w2)


def _combine_kernel(h_ref, y0_ref, y1_ref, wt_ref, mod_ref, lnw_ref, lnb_ref, o_ref):
    half = D // 2
    lo0, hi0 = _unpack_bf16_pairs(y0_ref[0])
    lo1, hi1 = _unpack_bf16_pairs(y1_ref[0])
    w0, w1 = wt_ref[:, 0:1], wt_ref[:, 1:2]
    r_lo = ALPHA * h_ref[:, 0:half] + mod_ref[0, 5:6, 0:half] * (w0 * lo0 + w1 * lo1)
    r_hi = ALPHA * h_ref[:, half:D] + mod_ref[0, 5:6, half:D] * (w0 * hi0 + w1 * hi1)
    mu = (jnp.sum(r_lo, axis=-1, keepdims=True) + jnp.sum(r_hi, axis=-1, keepdims=True)) * (1.0 / D)
    c_lo, c_hi = r_lo - mu, r_hi - mu
    var = (jnp.sum(c_lo * c_lo, axis=-1, keepdims=True) + jnp.sum(c_hi * c_hi, axis=-1, keepdims=True)) * (1.0 / D)
    inv = lax.rsqrt(var + LN_EPS)
    o_ref[:, 0:half] = c_lo * inv * lnw_ref[:, 0:half] + lnb_ref[:, 0:half]
    o_ref[:, half:D] = c_hi * inv * lnw_ref[:, half:D] + lnb_ref[:, half:D]


def moe_combine(h1, y_rows, wts, tile_off, mod, lnw, lnb):
    B, T, _ = h1.shape
    N = B * T
    tm = min(512, N)
    per_b = T // tm
    mod_map = (lambda i: (i // per_b, 0, 0)) if mod.shape[0] > 1 else (lambda i: (0, 0, 0))
    out = pl.pallas_call(
        _combine_kernel,
        out_shape=jax.ShapeDtypeStruct((N, D), F32),
        grid=(N // tm,),
        in_specs=[pl.BlockSpec((tm, D), lambda i: (i, 0)),
                  pl.BlockSpec((1, tm, D // 2), lambda i: (0, i + tile_off, 0)),
                  pl.BlockSpec((1, tm, D // 2), lambda i: (1, i + tile_off, 0)),
                  pl.BlockSpec((tm, TOP_K), lambda i: (i + tile_off, 0)),
                  pl.BlockSpec((1, 6, D), mod_map),
                  pl.BlockSpec((1, D), lambda i: (0, 0)), pl.BlockSpec((1, D), lambda i: (0, 0))],
        out_specs=pl.BlockSpec((tm, D), lambda i: (i, 0)),
        compiler_params=_cparams(("parallel",)),
        name="moe_combine",
    )(h1.reshape(N, D), y_rows, y_rows, wts, mod, lnw, lnb)
    return out.reshape(B, T, D)


def _route_kernel(lg_ref, dest_ref, wt_ref, cnt_ref, tri_sc, start_sc, run_sc):
    ph, i = pl.program_id(0), pl.program_id(1)
    tr = lg_ref.shape[1]

    @pl.when(jnp.logical_and(ph == 0, i == 0))
    def _():
        r = lax.broadcasted_iota(jnp.int32, (tr, tr), 0)
        c = lax.broadcasted_iota(jnp.int32, (tr, tr), 1)
        tri_sc[...] = (r < c).astype(BF16)
        start_sc[...] = jnp.zeros_like(start_sc)
        run_sc[...] = jnp.zeros_like(run_sc)

    @pl.when(jnp.logical_and(ph == 1, i == 0))
    def _():
        cnt = run_sc[...].astype(jnp.int32)
        cnt_ref[...] = cnt
        padded = jnp.bitwise_and(cnt + (MOE_TM - 1), -MOE_TM)
        row = lax.broadcasted_iota(jnp.int32, padded.shape, 0)
        acc = padded
        for s in (1, 2, 4, 8, 16):
            acc = acc + jnp.where(row >= s, pltpu.roll(acc, s, 0), 0)
        start_sc[...] = (acc - padded).astype(F32)
        run_sc[...] = jnp.zeros_like(run_sc)

    lg = lg_ref[...]
    gl = lg[N_EXPERTS:N_EXPERTS + N_GROUPS]
    gmax = jnp.max(gl, axis=0, keepdims=True)
    sub4 = lax.broadcasted_iota(jnp.int32, gl.shape, 0)
    g_sel = jnp.min(jnp.where(gl == gmax, sub4, N_GROUPS), axis=0, keepdims=True)
    p_group = 1.0 / jnp.sum(jnp.exp(gl - gmax), axis=0, keepdims=True)
    el = lg[0:EXPERTS_PER_GROUP]
    for g in range(1, N_GROUPS):
        el = jnp.where(g_sel == g, lg[g * EXPERTS_PER_GROUP:(g + 1) * EXPERTS_PER_GROUP], el)
    sub8 = lax.broadcasted_iota(jnp.int32, el.shape, 0)
    e1 = jnp.max(el, axis=0, keepdims=True)
    i1 = jnp.min(jnp.where(el == e1, sub8, EXPERTS_PER_GROUP), axis=0, keepdims=True)
    rest = jnp.where(sub8 == i1, -jnp.inf, el)
    e2 = jnp.max(rest, axis=0, keepdims=True)
    i2 = jnp.min(jnp.where(rest == e2, sub8, EXPERTS_PER_GROUP), axis=0, keepdims=True)
    den = jnp.sum(jnp.exp(el - e1), axis=0, keepdims=True)
    p1 = 1.0 / den
    p2 = jnp.exp(e2 - e1) / den
    wt_ref[0:1, :] = p_group * p1 / (p1 + p2)
    wt_ref[1:2, :] = p_group * p2 / (p1 + p2)

    sub32 = lax.broadcasted_iota(jnp.int32, (N_EXPERTS, tr), 0)
    oh = [(sub32 == g_sel * EXPERTS_PER_GROUP + ix).astype(F32) for ix in (i1, i2)]
    cnt = [jnp.sum(o, axis=1, keepdims=True) for o in oh]
    before = start_sc[:, 0:1] + run_sc[:, 0:1]
    for k in range(TOP_K):
        prior = _dot(oh[k].astype(BF16), tri_sc[...]) + before + (cnt[0] if k == 1 else 0.0)
        dest_ref[k:k + 1, :] = jnp.sum(oh[k] * prior, axis=0, keepdims=True).astype(jnp.int32)
    run_sc[...] = run_sc[...] + (cnt[0] + cnt[1])


def moe_route(logits_t):
    N = logits_t.shape[1]
    tr = next(t for t in (1024, 512, 256) if N % t == 0)
    return pl.pallas_call(
        _route_kernel,
        out_shape=(jax.ShapeDtypeStruct((TOP_K, N), jnp.int32), jax.ShapeDtypeStruct((TOP_K, N), F32),
                   jax.ShapeDtypeStruct((N_EXPERTS, LANE), jnp.int32)),
        grid=(2, N // tr),
        in_specs=[pl.BlockSpec((ROUTE_ROWS, tr), lambda p, i: (0, i))],
        out_specs=(pl.BlockSpec((TOP_K, tr), lambda p, i: (0, i * p)), pl.BlockSpec((TOP_K, tr), lambda p, i: (0, i * p)),
                   pl.BlockSpec((N_EXPERTS, LANE), lambda p, i: (0, 0))),
        scratch_shapes=[pltpu.VMEM((tr, tr), BF16), pltpu.VMEM((N_EXPERTS, LANE), F32),
                        pltpu.VMEM((N_EXPERTS, LANE), F32)],
        compiler_params=_cparams(("arbitrary", "arbitrary")),
        name="moe_route",
    )(logits_t)


def _block_tables(counts, n_assign):
    padded = (counts + MOE_TM - 1) // MOE_TM * MOE_TM
    pad_end = jnp.cumsum(padded)
    pad_start = pad_end - padded
    nb = -(-n_assign // MOE_TM) + N_EXPERTS
    blk_start = jnp.arange(nb, dtype=jnp.int32) * MOE_TM
    blk_expert = jnp.minimum(jnp.sum((pad_end[None, :] <= blk_start[:, None]).astype(jnp.int32), axis=1), N_EXPERTS - 1)
    n_valid = (pad_end[-1] // MOE_TM).astype(jnp.int32).reshape(1)
    n_fill = nb * MOE_TM - n_assign
    gap = padded - counts
    gap_end = jnp.cumsum(gap)
    k = jnp.arange(n_fill, dtype=jnp.int32)[:, None]
    sel = jnp.logical_and(k >= (gap_end - gap)[None, :], k < gap_end[None, :])
    in_gap = jnp.sum(jnp.where(sel, (pad_start + counts - (gap_end - gap))[None, :] + k, 0), axis=1)
    fill = jnp.where(k[:, 0] < gap_end[-1], in_gap, pad_end[-1] + k[:, 0] - gap_end[-1])
    return blk_expert.astype(jnp.int32), n_valid, fill.astype(jnp.int32), nb * MOE_TM


def hier_moe_and_norm(streams, w1, w3, w2, layer, lnw, lnb):
    logits_t = streams[0][2] if len(streams) == 1 else jnp.concatenate([s[2] for s in streams], 1)
    dest, wts_t, counts = moe_route(logits_t)
    wts = wts_t.T
    blk_expert, n_valid, fill_rows, n_rows = _block_tables(counts[:, 0], TOP_K * logits_t.shape[1])
    n_all = logits_t.shape[1]
    toks, idxs, offs, off = [], [], [], 0
    for h1, tok, _, _ in streams:
        n = h1.shape[0] * h1.shape[1]
        toks.append(tok.reshape(n, D // 2))
        idxs.append(dest[:, off:off + n])
        offs.append(off // min(512, n))
        off += n
    toks.append(jnp.zeros((fill_rows.shape[0], D // 2), jnp.uint32))
    idxs.append(fill_rows.reshape(1, -1))
    xs = sc_scatter_rows(toks, idxs, n_rows)
    y = moe_ffn(xs, blk_expert, n_valid, w1, w3, w2, layer)
    y_rows = sc_gather_rows(y, dest.reshape(-1)).reshape(TOP_K, n_all, D // 2)
    return [moe_combine(h1, y_rows, wts, o, mod, lnw, lnb) for (h1, _, _, mod), o in zip(streams, offs)]


def _even_weight_columns():
    n = np.arange(HEAD_DIM)
    perm = (n % 32) // 16 * 32 + n // 32 * 16 + n % 16
    idx = np.zeros(P_W, np.int32)
    scale = np.zeros(P_W, np.float32)
    for hd in range(A_Q_HEADS):
        idx[C_QA + hd * 64:C_QA + (hd + 1) * 64] = hd * 64 + perm
    scale[C_QA:C_QA + 512] = HEAD_DIM ** -0.5 * LOG2_E
    for g in range(A_KV_HEADS):
        for rep in range(2):
            o = g * 128 + rep * 64
            idx[C_KD + o:C_KD + o + 64] = 512 + g * 64 + perm
            idx[C_VD + o:C_VD + o + 64] = 640 + g * 64 + n
    scale[C_KD:C_VD + 256] = 1.0
    idx[C_QG:C_QG + 256] = 768 + np.arange(256)
    scale[C_QG:C_QG + 256] = GLA_DK ** -0.5
    idx[C_KG:C_KG + 256] = 1024 + np.arange(256)
    idx[C_VG:C_VG + 512] = 1280 + np.arange(512)
    idx[C_RG:C_RG + 512] = 1792 + np.arange(512)
    scale[C_KG:C_KG + 256] = 1.0
    scale[C_VG:C_RG + 512] = 1.0
    for d in range(2):
        idx[C_GG + d * 128:C_GG + d * 128 + GLA_RANK] = 2304 + d * GLA_RANK + np.arange(GLA_RANK)
        scale[C_GG + d * 128:C_GG + d * 128 + GLA_RANK] = 1.0
    return idx, scale


def _rope_tables(S):
    row = jnp.repeat(jnp.arange(S // GRID_W), GRID_W).astype(F32)
    col = jnp.tile(jnp.arange(GRID_W), S // GRID_W).astype(F32)
    axis_dim = HEAD_DIM // 2
    inv_freq = ROPE_BASE ** (-jnp.arange(0, axis_dim, 2, dtype=F32) / axis_dim)
    ang = jnp.concatenate([row[:, None] * inv_freq, col[:, None] * inv_freq], -1)
    cos, sin = jnp.cos(ang), jnp.sin(ang)
    cos_t = jnp.tile(cos, (1, 4))
    sin_t = jnp.tile(jnp.concatenate([-sin, sin], -1), (1, 2))
    return cos_t, sin_t


def _router_weights(wg, bg, we, be):
    w = jnp.zeros((D, ROUTE_W), F32).at[:, :N_EXPERTS].set(we).at[:, N_EXPERTS:N_EXPERTS + N_GROUPS].set(wg)
    b = jnp.zeros((1, ROUTE_W), F32).at[0, :N_EXPERTS].set(be).at[0, N_EXPERTS:N_EXPERTS + N_GROUPS].set(bg)
    hi = w.astype(BF16)
    return hi, (w - hi.astype(F32)).astype(BF16), b


def kernel(x, c, ctx, c_ctx, w_in_even, w_out_even, attn_sink, gla_wa2, gla_ba, gla_norm_w, w_in_odd, conv_w, conv_b, w_out_odd, ada_w, ada_b, ln_w, ln_b, router_wg, router_bg, router_we, router_be, moe_w1, moe_w3, moe_w2):
    B, S, _ = x.shape
    L = ctx.shape[1]
    cos_t, sin_t = _rope_tables(S)
    cos_c, sin_c = jnp.ones((L, LANE), F32), jnp.zeros((L, LANE), F32)
    col_idx, col_scale = _even_weight_columns()

    n_cond = -(-(B + 1) // 8) * 8
    cc = jnp.zeros((n_cond, D), F32).at[:B].set(c).at[B].set(c_ctx)
    mods = ada_modulation_all(cc, ada_w, ada_b).reshape(DEPTH, n_cond, 6, D)

    h_lat, h_ctx = x, ctx
    for l in range(DEPTH):
        i = l // 2
        need_ctx = any(j % 2 == 0 for j in range(l + 1, DEPTH))
        m_lat = mods[l, :B]
        m_ctx = mods[l, B:B + 1]
        lnw0, lnb0 = ln_w[l, 0:1], ln_b[l, 0:1]
        lnw1, lnb1 = ln_w[l, 1:2], ln_b[l, 1:2]
        rwh, rwl, rb = _router_weights(router_wg[l], router_bg[l], router_we[l], router_be[l])
        streams = []
        if l % 2 == 0:
            w_in = (w_in_even[i][:, col_idx] * col_scale[None, :]).astype(BF16)
            w_out = w_out_even[i].astype(BF16)
            wa_p = jnp.zeros((2, LANE, GLA_HEADS * GLA_DK), F32).at[:, :GLA_RANK].set(gla_wa2[i]).astype(BF16)
            ba = gla_ba[i].reshape(2, 1, -1)
            nw = gla_norm_w[i].reshape(1, -1)
            p_ctx = inproj_even(h_ctx, m_ctx, w_in, cos_c, sin_c)
            p_lat = inproj_even(h_lat, m_lat, w_in, cos_t, sin_t)
            a_lat = attention(p_lat, p_ctx, attn_sink[i], True)
            s0 = jnp.zeros((B, 2, 2, GLA_DV, LANE), F32)
            oc_f, oc_b, s_ctx = gla_scan(p_ctx, wa_p, ba, s0)
            ol_f, ol_b, _ = gla_scan(p_lat, wa_p, ba, s_ctx)
            streams.append(outproj_even(a_lat, ol_f, ol_b, p_lat, nw, w_out, h_lat, m_lat, lnw0, lnb0, rwh, rwl, rb)
                           + (m_lat,))
            if need_ctx:
                a_ctx = attention(p_ctx, p_ctx, attn_sink[i], False)
                streams.append(outproj_even(a_ctx, oc_f, oc_b, p_ctx, nw, w_out, h_ctx, m_ctx, lnw0, lnb0,
                                            rwh, rwl, rb) + (m_ctx,))
        else:
            w_in = w_in_odd[i].astype(BF16)
            w_out = w_out_odd[i].astype(BF16)
            cb = conv_b[i].reshape(1, D)
            pairs = [(h_lat, m_lat)] + ([(h_ctx, m_ctx)] if need_ctx else [])
            for h, m in pairs:
                gz = inproj_odd(h, m, w_in)
                streams.append(outproj_odd(gz, conv_w[i], cb, w_out, h, m, lnw0, lnb0, rwh, rwl, rb) + (m,))
        outs = hier_moe_and_norm(streams, moe_w1, moe_w3, moe_w2, l, lnw1, lnb1)
        h_lat = outs[0]
        if need_ctx:
            h_ctx = outs[1]
    return h_lat
```

```python
import functools

import numpy as np
import jax
import jax.numpy as jnp
from jax import lax
from jax.experimental import pallas as pl
from jax.experimental.pallas import tpu as pltpu
from jax.experimental.pallas import tpu_sc as plsc

F32 = jnp.float32
BF16 = jnp.bfloat16
HIGHEST = lax.Precision.HIGHEST

D = 1024
DEPTH = 4
GRID_W = 64
HEAD_DIM = 64
A_Q_HEADS = 8
A_KV_HEADS = 2
WINDOW = 128
ROPE_BASE = 10000.0
GLA_HEADS = 4
GLA_DK = 64
GLA_DV = 128
GLA_RANK = 16
GLA_TAU = 16.0
GLA_CHUNK = 64
N_GROUPS = 4
EXPERTS_PER_GROUP = 8
N_EXPERTS = 32
TOP_K = 2
D_EXPERT = 512
ALPHA = (2.0 * DEPTH) ** 0.25
LN_EPS = 1e-5
RMS_EPS = 1e-6

LANE = 128
VMEM_LIMIT = 48 * 1024 * 1024

C_QA, C_KD, C_VD, C_VG, C_RG, C_QG, C_KG, C_GG = 0, 512, 768, 1024, 1536, 2048, 2304, 2560
P_W = 2816
ROUTE_W = 128
ROUTE_ROWS = 40
MOE_TM = 512
SC_CORES, SC_SUBCORES = 2, 16
SC_WORKERS = SC_CORES * SC_SUBCORES
SC_CHUNK = 64
NEG = -1e30
LOG2_E = 1.4426950408889634


def _cparams(sem):
    return pltpu.CompilerParams(dimension_semantics=sem, vmem_limit_bytes=VMEM_LIMIT)


def _dot(a, b):
    return jnp.dot(a, b, preferred_element_type=F32)


def _dot_nt(a, b):
    return lax.dot_general(a, b, (((1,), (1,)), ((), ())), preferred_element_type=F32)


def _dot_tn(a, b):
    return lax.dot_general(a, b, (((0,), (0,)), ((), ())), preferred_element_type=F32)


def _silu(x):
    return x * (1.0 / (1.0 + jnp.exp(-x)))


def _ada_kernel(c_ref, w_ref, b_ref, o_ref):
    s = _silu(c_ref[...])
    o_ref[0] = jnp.dot(s, w_ref[0], precision=HIGHEST, preferred_element_type=F32) + b_ref[0]


def ada_modulation_all(cc, ada_w, ada_b):
    R = cc.shape[0]
    tn = 1536
    return pl.pallas_call(
        _ada_kernel,
        out_shape=jax.ShapeDtypeStruct((DEPTH, R, 6 * D), F32),
        grid=(DEPTH, 6 * D // tn),
        in_specs=[pl.BlockSpec((R, D), lambda l, n: (0, 0)),
                  pl.BlockSpec((1, D, tn), lambda l, n: (l, 0, n)),
                  pl.BlockSpec((1, 1, tn), lambda l, n: (l, 0, n))],
        out_specs=pl.BlockSpec((1, R, tn), lambda l, n: (l, 0, n)),
        compiler_params=_cparams(("arbitrary", "arbitrary")),
        name="ada_modulation",
    )(cc, ada_w, ada_b.reshape(DEPTH, 1, 6 * D))


_EVEN_CHUNKS = ((0, 512, True), (512, 768, True), (768, 1024, False), (1024, 1536, False),
                (1536, 2048, False), (2048, 2560, False), (2560, 2816, False))


def _inproj_even_kernel(h_ref, mod_ref, w_ref, cos_ref, sin_ref, p_ref):
    tm = h_ref.shape[1]
    u = (h_ref[0] * (1.0 + mod_ref[0, 1:2, :]) + mod_ref[0, 0:1, :]).astype(BF16)
    cos = cos_ref[...]
    sin = sin_ref[...]
    lane = lax.broadcasted_iota(jnp.int32, (tm, LANE), 1)
    first_half = (lane % HEAD_DIM) < (HEAD_DIM // 2)
    for c0, c1, rope in _EVEN_CHUNKS:
        acc = _dot(u, w_ref[:, c0:c1])
        if rope:
            for i in range((c1 - c0) // LANE):
                x = acc[:, i * LANE:(i + 1) * LANE]
                partner = jnp.where(first_half, pltpu.roll(x, LANE - 32, 1), pltpu.roll(x, 32, 1))
                p_ref[0, :, c0 + i * LANE:c0 + (i + 1) * LANE] = (x * cos + partner * sin).astype(BF16)
        else:
            p_ref[0, :, c0:c1] = acc.astype(BF16)


def inproj_even(h, mod, w, cos_t, sin_t):
    B, T, _ = h.shape
    tm = min(512, T)
    mb = mod.shape[0]
    return pl.pallas_call(
        _inproj_even_kernel,
        out_shape=jax.ShapeDtypeStruct((B, T, P_W), BF16),
        grid=(B, T // tm),
        in_specs=[pl.BlockSpec((1, tm, D), lambda b, j: (b, j, 0)),
                  pl.BlockSpec((1, 6, D), (lambda b, j: (b, 0, 0)) if mb > 1 else (lambda b, j: (0, 0, 0))),
                  pl.BlockSpec((D, P_W), lambda b, j: (0, 0)),
                  pl.BlockSpec((tm, LANE), lambda b, j: (j, 0)),
                  pl.BlockSpec((tm, LANE), lambda b, j: (j, 0))],
        out_specs=pl.BlockSpec((1, tm, P_W), lambda b, j: (b, j, 0)),
        compiler_params=_cparams(("parallel", "arbitrary")),
        name="inproj_even",
    )(h, mod, w, cos_t, sin_t)


def _attn_kernel(*refs, tq, has_window):
    if has_window:
        sink_ref, q_ref, kw_ref, vw_ref, kc_ref, vc_ref, o_ref = refs
    else:
        sink_ref, q_ref, kc_ref, vc_ref, o_ref = refs
    group = A_Q_HEADS // A_KV_HEADS
    rows = group * tq
    lo = lax.broadcasted_iota(jnp.int32, (tq, LANE), 1) < HEAD_DIM
    den_lanes = lax.broadcasted_iota(jnp.int32, (rows, LANE), 1) >= HEAD_DIM

    def with_ones(v):
        return jnp.where(lax.broadcasted_iota(jnp.int32, v.shape, 1) < HEAD_DIM, v, jnp.ones_like(v))

    if has_window:
        S = kw_ref.shape[1]
        wk = tq + 2 * WINDOW
        q0 = pl.program_id(1) * tq
        wstart = pl.multiple_of(jnp.clip(q0 - WINDOW, 0, S - wk), LANE)
        qpos = q0 + lax.broadcasted_iota(jnp.int32, (tq, wk), 0)
        kpos = wstart + lax.broadcasted_iota(jnp.int32, (tq, wk), 1)
        band = jnp.tile(jnp.where(jnp.abs(qpos - kpos) <= WINDOW, 0.0, NEG), (group, 1))
    groups = range(A_KV_HEADS)
    cols = [slice(g * LANE, (g + 1) * LANE) for g in groups]
    q4, snk, sc, sw, m, outs = [], [], [], [], [], []
    for g in groups:
        qs = []
        for pr in range(group // 2):
            qblk = q_ref[0, :, (2 * g + pr) * LANE:(2 * g + pr + 1) * LANE]
            zero = jnp.zeros_like(qblk)
            qs += [jnp.where(lo, qblk, zero), jnp.where(lo, zero, qblk)]
        q4.append(jnp.concatenate(qs, axis=0))
        snk.append(jnp.concatenate([jnp.full((tq, 1), sink_ref[group * g + i] * LOG2_E, F32) for i in range(group)],
                                   axis=0))
    for g in groups:
        sc.append(_dot_nt(q4[g], kc_ref[0, :, cols[g]]))
        if has_window:
            sw.append(_dot_nt(q4[g], kw_ref[0, pl.ds(wstart, wk), cols[g]]) + band)
    for g in groups:
        mg = jnp.maximum(jnp.max(sc[g], axis=-1, keepdims=True), snk[g])
        if has_window:
            mg = jnp.maximum(mg, jnp.max(sw[g], axis=-1, keepdims=True))
        m.append(mg)
    for g in groups:
        o = _dot(jnp.exp2(sc[g] - m[g]).astype(BF16), with_ones(vc_ref[0, :, cols[g]]))
        if has_window:
            o = o + _dot(jnp.exp2(sw[g] - m[g]).astype(BF16), with_ones(vw_ref[0, pl.ds(wstart, wk), cols[g]]))
        outs.append(o + jnp.where(den_lanes, jnp.exp2(snk[g] - m[g]), 0.0))
    for g in groups:
        o = outs[g]
        swapped = pltpu.roll(o, HEAD_DIM, 1)
        for pr in range(group // 2):
            ev = slice(2 * pr * tq, (2 * pr + 1) * tq)
            od = slice((2 * pr + 1) * tq, (2 * pr + 2) * tq)
            res = jnp.where(lo, o[ev] / swapped[ev], swapped[od] / o[od])
            o_ref[0, :, (2 * g + pr) * LANE:(2 * g + pr + 1) * LANE] = res.astype(BF16)


def attention(p_q, p_ctx, sink, has_window):
    B, T, _ = p_q.shape
    L = p_ctx.shape[1]
    tq = 128
    in_specs = [pl.BlockSpec(memory_space=pltpu.SMEM),
                pl.BlockSpec((1, tq, 512), lambda b, j: (b, j, C_QA // 512))]
    args = [sink, p_q]
    if has_window:
        in_specs += [pl.BlockSpec((1, T, 256), lambda b, j: (b, 0, C_KD // 256)),
                     pl.BlockSpec((1, T, 256), lambda b, j: (b, 0, C_VD // 256))]
        args += [p_q, p_q]
    in_specs += [pl.BlockSpec((1, L, 256), lambda b, j: (b, 0, C_KD // 256)),
                 pl.BlockSpec((1, L, 256), lambda b, j: (b, 0, C_VD // 256))]
    args += [p_ctx, p_ctx]
    return pl.pallas_call(
        functools.partial(_attn_kernel, tq=tq, has_window=has_window),
        out_shape=jax.ShapeDtypeStruct((B, T, 512), BF16),
        grid=(B, T // tq),
        in_specs=in_specs,
        out_specs=pl.BlockSpec((1, tq, 512), lambda b, j: (b, j, 0)),
        compiler_params=_cparams(("parallel", "arbitrary")),
        name="window_attention" if has_window else "context_attention",
    )(*args)


def _log_sigmoid(x):
    return jnp.minimum(x, 0.0) - jnp.log(1.0 + jnp.exp(-jnp.abs(x)))


def _gla_kernel(qf_ref, kf_ref, vf_ref, gf_ref, qb_ref, kb_ref, vb_ref, gb_ref, wa_ref, ba_ref, s0_ref,
                of_ref, ob_ref, sfin_ref, s_sc):
    j = pl.program_id(1)
    nblk = pl.num_programs(1)
    tb = qf_ref.shape[1]
    nc = tb // GLA_CHUNK

    @pl.when(j == 0)
    def _():
        s_sc[...] = s0_ref[0]

    C = GLA_CHUNK
    ri = lax.broadcasted_iota(jnp.int32, (2 * C, 2 * C), 0) % C
    ci = lax.broadcasted_iota(jnp.int32, (2 * C, 2 * C), 1) % C
    rb = lax.broadcasted_iota(jnp.int32, (tb, tb), 0)
    cb = lax.broadcasted_iota(jnp.int32, (tb, tb), 1)
    same_chunk = (rb // C) == (cb // C)
    lo = lax.broadcasted_iota(jnp.int32, (C, LANE), 1) < GLA_DK

    def per_head(x):
        zero = jnp.zeros_like(x)
        return jnp.concatenate([jnp.where(lo, x, zero), jnp.where(lo, zero, x)], axis=0)

    io = ((qf_ref, kf_ref, vf_ref, gf_ref, of_ref), (qb_ref, kb_ref, vb_ref, gb_ref, ob_ref))
    causal = ((ri >= ci), (ci >= ri))
    tri = (jnp.logical_and(same_chunk, rb >= cb).astype(BF16), jnp.logical_and(same_chunk, cb >= rb).astype(BF16))
    b_all = []
    for d in range(2):
        g = _dot(io[d][3][0, :, d * LANE:(d + 1) * LANE], wa_ref[d]) + ba_ref[d]
        log_a = _log_sigmoid(g) / GLA_TAU
        la1 = log_a.astype(BF16)
        rem = log_a - la1.astype(F32)
        la2 = rem.astype(BF16)
        la3 = (rem - la2.astype(F32)).astype(BF16)
        b_all.append(_dot(tri[d], la1) + _dot(tri[d], la2) + _dot(tri[d], la3))
    state = [[s_sc[d, pair] for pair in range(2)] for d in range(2)]
    for step in range(nc):
        for d in range(2):
            q_ref, k_ref, v_ref, _, o_ref = io[d]
            c = step if d == 0 else nc - 1 - step
            rows = slice(c * C, (c + 1) * C)
            b = b_all[d][rows]
            b_last = b[C - 1:C] if d == 0 else b[0:1]
            qc = q_ref[0, rows, :].astype(F32)
            kc = k_ref[0, rows, :].astype(F32)
            q_in = (qc * jnp.exp(b)).astype(BF16)
            k_in = (kc * jnp.exp(-b)).astype(BF16)
            k_st = (kc * jnp.exp(b_last - b)).astype(BF16)
            dl = jnp.exp(b_last)
            for pair in range(2):
                cols = slice(pair * LANE, (pair + 1) * LANE)
                st = state[d][pair]
                q2 = per_head(q_in[:, cols])
                res = _dot_nt(q2, jnp.concatenate([st.astype(BF16), per_head(k_in[:, cols])], axis=0))
                attn = jnp.where(causal[d], res[:, GLA_DV:], 0.0).astype(BF16)
                v2 = jnp.concatenate([v_ref[0, rows, (2 * pair + hh) * GLA_DV:(2 * pair + hh + 1) * GLA_DV]
                                      for hh in range(2)], axis=0)
                o2 = _dot(attn, v2) + res[:, 0:GLA_DV]
                for hh in range(2):
                    o_ref[0, rows, (2 * pair + hh) * GLA_DV:(2 * pair + hh + 1) * GLA_DV] = (
                        o2[hh * C:(hh + 1) * C].astype(o_ref.dtype))
                upd = _dot_tn(v2, per_head(k_st[:, cols]))
                state[d][pair] = st * dl[:, cols] + upd
    for d in range(2):
        for pair in range(2):
            s_sc[d, pair] = state[d][pair]

    @pl.when(j == nblk - 1)
    def _():
        sfin_ref[0] = s_sc[...]


def gla_scan(p, wa_p, ba, s0):
    B, T, _ = p.shape
    tb = min(512, T)
    nblk = T // tb
    fwd = lambda b, j: (b, j)
    bwd = lambda b, j: (b, nblk - 1 - j)

    def specs(im):
        return [pl.BlockSpec((1, tb, 256), lambda b, j: im(b, j) + (C_QG // 256,)),
                pl.BlockSpec((1, tb, 256), lambda b, j: im(b, j) + (C_KG // 256,)),
                pl.BlockSpec((1, tb, 512), lambda b, j: im(b, j) + (C_VG // 512,)),
                pl.BlockSpec((1, tb, 256), lambda b, j: im(b, j) + (C_GG // 256,))]

    return pl.pallas_call(
        _gla_kernel,
        out_shape=(jax.ShapeDtypeStruct((B, T, 512), BF16), jax.ShapeDtypeStruct((B, T, 512), BF16),
                   jax.ShapeDtypeStruct(s0.shape, F32)),
        grid=(B, nblk),
        in_specs=specs(fwd) + specs(bwd) + [
            pl.BlockSpec((2, LANE, 256), lambda b, j: (0, 0, 0)),
            pl.BlockSpec((2, 1, 256), lambda b, j: (0, 0, 0)),
            pl.BlockSpec((1, 2, 2, GLA_DV, LANE), lambda b, j: (b, 0, 0, 0, 0))],
        out_specs=(pl.BlockSpec((1, tb, 512), lambda b, j: (b, j, 0)),
                   pl.BlockSpec((1, tb, 512), lambda b, j: (b, nblk - 1 - j, 0)),
                   pl.BlockSpec((1, 2, 2, GLA_DV, LANE), lambda b, j: (b, 0, 0, 0, 0))),
        scratch_shapes=[pltpu.VMEM((2, 2, GLA_DV, LANE), F32)],
        compiler_params=_cparams(("parallel", "arbitrary")),
        name="gla_scan",
    )(p, p, p, p, p, p, p, p, wa_p, ba, s0)


def _layer_norm(r, w, b):
    mu = jnp.mean(r, axis=-1, keepdims=True)
    xc = r - mu
    var = jnp.mean(xc * xc, axis=-1, keepdims=True)
    return xc * lax.rsqrt(var + LN_EPS) * w + b


def _pack_bf16_pairs(x):
    half = x.shape[1] // 2
    lo = lax.bitcast_convert_type(x[:, :half].astype(BF16).astype(F32), jnp.uint32)
    hi = lax.bitcast_convert_type(x[:, half:].astype(BF16).astype(F32), jnp.uint32)
    return jnp.bitwise_or(hi, lax.shift_right_logical(lo, jnp.uint32(16)))


def _unpack_bf16_pairs(p):
    lo = lax.bitcast_convert_type(lax.shift_left(p, jnp.uint32(16)), F32)
    hi = lax.bitcast_convert_type(jnp.bitwise_and(p, jnp.uint32(0xFFFF0000)), F32)
    return lo, hi


def _post_norm_and_route(h, y, mod_ref, lnw_ref, lnb_ref, rwh_ref, rwl_ref, rb_ref, h1_ref, tok_ref, lg_ref):
    h1 = _layer_norm(ALPHA * h + mod_ref[0, 2:3, :] * y, lnw_ref[...], lnb_ref[...])
    h1_ref[0] = h1
    tok = h1 * (1.0 + mod_ref[0, 4:5, :]) + mod_ref[0, 3:4, :]
    tok_ref[0] = _pack_bf16_pairs(tok)
    hi = tok.astype(BF16)
    lo = (tok - hi.astype(F32)).astype(BF16)
    lg = _dot(hi, rwh_ref[...]) + _dot(lo, rwh_ref[...]) + _dot(hi, rwl_ref[...]) + rb_ref[...]
    lg_ref[...] = lg.T[0:ROUTE_ROWS, :]


def _epilogue_specs(tm, mb):
    mod_map = (lambda b, j: (b, 0, 0)) if mb > 1 else (lambda b, j: (0, 0, 0))
    const2 = lambda b, j: (0, 0)
    return [pl.BlockSpec((1, tm, D), lambda b, j: (b, j, 0)),
            pl.BlockSpec((1, 6, D), mod_map),
            pl.BlockSpec((1, D), const2), pl.BlockSpec((1, D), const2),
            pl.BlockSpec((D, ROUTE_W), const2), pl.BlockSpec((D, ROUTE_W), const2),
            pl.BlockSpec((1, ROUTE_W), const2)]


def _epilogue_outs(B, T, tm):
    nj = T // tm
    shapes = (jax.ShapeDtypeStruct((B, T, D), F32), jax.ShapeDtypeStruct((B, T, D // 2), jnp.uint32),
              jax.ShapeDtypeStruct((ROUTE_ROWS, B * T), F32))
    specs = (pl.BlockSpec((1, tm, D), lambda b, j: (b, j, 0)), pl.BlockSpec((1, tm, D // 2), lambda b, j: (b, j, 0)),
             pl.BlockSpec((ROUTE_ROWS, tm), lambda b, j: (0, b * nj + j)))
    return shapes, specs


def _outproj_even_kernel(a_ref, of_ref, ob_ref, rg_ref, nw_ref, wo_ref,
                         h_ref, mod_ref, lnw_ref, lnb_ref, rwh_ref, rwl_ref, rb_ref,
                         h1_ref, tok_ref, lg_ref):
    y = _dot(a_ref[0], wo_ref[0:512, :])
    for hd in range(GLA_HEADS):
        cols = slice(hd * GLA_DV, (hd + 1) * GLA_DV)
        o = of_ref[0, :, cols].astype(F32) + ob_ref[0, :, cols].astype(F32)
        o = o * lax.rsqrt(jnp.mean(o * o, axis=-1, keepdims=True) + RMS_EPS)
        gated = o * nw_ref[:, cols] * _silu(rg_ref[0, :, cols].astype(F32))
        y = y + _dot(gated.astype(BF16), wo_ref[512 + hd * GLA_DV:512 + (hd + 1) * GLA_DV, :])
    _post_norm_and_route(h_ref[0], y, mod_ref, lnw_ref, lnb_ref, rwh_ref, rwl_ref, rb_ref, h1_ref, tok_ref, lg_ref)


def outproj_even(a, o_f, o_b, p, norm_w, w_out, h, mod, lnw, lnb, rwh, rwl, rb):
    B, T, _ = h.shape
    tm = min(512, T)
    tile = lambda b, j: (b, j, 0)
    shapes, ospecs = _epilogue_outs(B, T, tm)
    return pl.pallas_call(
        _outproj_even_kernel,
        out_shape=shapes,
        grid=(B, T // tm),
        in_specs=[pl.BlockSpec((1, tm, 512), tile), pl.BlockSpec((1, tm, 512), tile), pl.BlockSpec((1, tm, 512), tile),
                  pl.BlockSpec((1, tm, 512), lambda b, j: (b, j, C_RG // 512)),
                  pl.BlockSpec((1, 512), lambda b, j: (0, 0)),
                  pl.BlockSpec((D, D), lambda b, j: (0, 0))] + _epilogue_specs(tm, mod.shape[0]),
        out_specs=ospecs,
        compiler_params=_cparams(("parallel", "arbitrary")),
        name="outproj_even",
    )(a, o_f, o_b, p, norm_w, w_out, h, mod, lnw, lnb, rwh, rwl, rb)


def _inproj_odd_kernel(h_ref, mod_ref, w_ref, o_ref):
    u = (h_ref[0] * (1.0 + mod_ref[0, 1:2, :]) + mod_ref[0, 0:1, :]).astype(BF16)
    o_ref[0, :, 0:D] = _dot(u, w_ref[:, 0:D]).astype(BF16)
    o_ref[0, :, D:2 * D] = (_dot(u, w_ref[:, D:2 * D]) * _dot(u, w_ref[:, 2 * D:3 * D])).astype(BF16)


def inproj_odd(h, mod, w):
    B, T, _ = h.shape
    tm = min(512, T)
    mb = mod.shape[0]
    return pl.pallas_call(
        _inproj_odd_kernel,
        out_shape=jax.ShapeDtypeStruct((B, T, 2 * D), BF16),
        grid=(B, T // tm),
        in_specs=[pl.BlockSpec((1, tm, D), lambda b, j: (b, j, 0)),
                  pl.BlockSpec((1, 6, D), (lambda b, j: (b, 0, 0)) if mb > 1 else (lambda b, j: (0, 0, 0))),
                  pl.BlockSpec((D, 3 * D), lambda b, j: (0, 0))],
        out_specs=pl.BlockSpec((1, tm, 2 * D), lambda b, j: (b, j, 0)),
        compiler_params=_cparams(("parallel", "arbitrary")),
        name="inproj_odd",
    )(h, mod, w)


HALO = 16


def _outproj_odd_kernel(gb_ref, z_ref, zp_ref, zn_ref, cw_ref, cb_ref, wo_ref,
                        h_ref, mod_ref, lnw_ref, lnb_ref, rwh_ref, rwl_ref, rb_ref,
                        h1_ref, tok_ref, lg_ref):
    j = pl.program_id(1)
    tm = z_ref.shape[1]
    z = z_ref[0].astype(F32)
    prev_row = jnp.where(j > 0, zp_ref[0, HALO - 1:HALO, :].astype(F32), 0.0)
    next_row = jnp.where(j < pl.num_programs(1) - 1, zn_ref[0, 0:1, :].astype(F32), 0.0)
    row = lax.broadcasted_iota(jnp.int32, (tm, D), 0)
    z_prev = jnp.where(row == 0, prev_row, pltpu.roll(z, 1, 0))
    z_next = jnp.where(row == tm - 1, next_row, pltpu.roll(z, tm - 1, 0))
    conv = z_prev * cw_ref[0:1, :] + z * cw_ref[1:2, :] + z_next * cw_ref[2:3, :] + cb_ref[...]
    y = _dot((gb_ref[0].astype(F32) * conv).astype(BF16), wo_ref[...])
    _post_norm_and_route(h_ref[0], y, mod_ref, lnw_ref, lnb_ref, rwh_ref, rwl_ref, rb_ref, h1_ref, tok_ref, lg_ref)


def outproj_odd(gz, conv_w, conv_b, w_out, h, mod, lnw, lnb, rwh, rwl, rb):
    B, T, _ = h.shape
    tm = min(512, T)
    r = tm // HALO
    nh = T // HALO
    shapes, ospecs = _epilogue_outs(B, T, tm)
    return pl.pallas_call(
        _outproj_odd_kernel,
        out_shape=shapes,
        grid=(B, T // tm),
        in_specs=[pl.BlockSpec((1, tm, D), lambda b, j: (b, j, 0)),
                  pl.BlockSpec((1, tm, D), lambda b, j: (b, j, 1)),
                  pl.BlockSpec((1, HALO, D), lambda b, j: (b, jnp.maximum(j * r - 1, 0), 1)),
                  pl.BlockSpec((1, HALO, D), lambda b, j: (b, jnp.minimum((j + 1) * r, nh - 1), 1)),
                  pl.BlockSpec((3, D), lambda b, j: (0, 0)),
                  pl.BlockSpec((1, D), lambda b, j: (0, 0)),
                  pl.BlockSpec((D, D), lambda b, j: (0, 0))] + _epilogue_specs(tm, mod.shape[0]),
        out_specs=ospecs,
        compiler_params=_cparams(("parallel", "arbitrary")),
        name="outproj_odd",
    )(gz, gz, gz, gz, conv_w, conv_b, w_out, h, mod, lnw, lnb, rwh, rwl, rb)


def _sc_mesh():
    return plsc.VectorSubcoreMesh(core_axis_name="c", subcore_axis_name="s")


def sc_gather_rows(table, idx):
    n = idx.shape[0]
    width = table.shape[1]
    per_w = n // SC_WORKERS
    n_chunks = per_w // SC_CHUNK
    assert n_chunks % 2 == 0

    @functools.partial(
        pl.kernel, mesh=_sc_mesh(),
        out_type=jax.ShapeDtypeStruct((n, width), table.dtype),
        scratch_types=[pltpu.VMEM((n_chunks, SC_CHUNK), jnp.int32),
                       pltpu.VMEM((SC_CHUNK, width), table.dtype), pltpu.VMEM((SC_CHUNK, width), table.dtype),
                       pltpu.SemaphoreType.DMA, pltpu.SemaphoreType.DMA],
    )
    def gather_kernel(table_hbm, idx_hbm, out_hbm, idx_v, buf0, buf1, sem0, sem1):
        wid = lax.axis_index("s") * SC_CORES + lax.axis_index("c")
        pltpu.sync_copy(idx_hbm.at[wid], idx_v)

        def fetch(j, buf, sem):
            return pltpu.make_async_copy(table_hbm.at[idx_v.at[j]], buf, sem)

        def flush(j, buf):
            pltpu.sync_copy(buf, out_hbm.at[pl.ds(wid * per_w + j * SC_CHUNK, SC_CHUNK)])

        fetch(0, buf0, sem0).start()

        @pl.loop(0, n_chunks, step=2)
        def _(j):
            fetch(j + 1, buf1, sem1).start()
            fetch(j, buf0, sem0).wait()
            flush(j, buf0)

            @pl.when(j + 2 < n_chunks)
            def _():
                fetch(j + 2, buf0, sem0).start()

            fetch(j + 1, buf1, sem1).wait()
            flush(j + 1, buf1)

    return gather_kernel(table, idx.reshape(SC_WORKERS, n_chunks, SC_CHUNK))


def sc_scatter_rows(srcs, idxs, n_rows):
    width, dt = srcs[0].shape[1], srcs[0].dtype
    plans, args = [], []
    for src, idx in zip(srcs, idxs):
        per_w = src.shape[0] // SC_WORKERS
        chunk = min(SC_CHUNK, per_w // 2)
        assert (per_w // chunk) % 2 == 0
        plans.append((per_w, chunk, per_w // chunk, idx.shape[0]))
        args += [src, idx.reshape(idx.shape[0], SC_WORKERS, per_w // chunk, chunk)]
    max_chunk = max(p[1] for p in plans)
    scratch = [pltpu.VMEM((max_chunk, width), dt), pltpu.VMEM((max_chunk, width), dt),
               pltpu.SemaphoreType.DMA, pltpu.SemaphoreType.DMA]
    scratch += [pltpu.VMEM((lists, n_chunks, chunk), jnp.int32) for _, chunk, n_chunks, lists in plans]

    @functools.partial(pl.kernel, mesh=_sc_mesh(), out_type=jax.ShapeDtypeStruct((n_rows, width), dt),
                       scratch_types=scratch)
    def scatter_kernel(*refs):
        ins, out_hbm = refs[:2 * len(plans)], refs[2 * len(plans)]
        rows0, rows1, sem0, sem1 = refs[2 * len(plans) + 1:2 * len(plans) + 5]
        idx_vs = refs[2 * len(plans) + 5:]
        wid = lax.axis_index("s") * SC_CORES + lax.axis_index("c")
        for s, (per_w, chunk, n_chunks, lists) in enumerate(plans):
            src_hbm, idx_hbm, idx_v = ins[2 * s], ins[2 * s + 1], idx_vs[s]
            for k in range(lists):
                pltpu.sync_copy(idx_hbm.at[k, wid], idx_v.at[k])
            buf0 = rows0 if chunk == max_chunk else rows0.at[pl.ds(0, chunk)]
            buf1 = rows1 if chunk == max_chunk else rows1.at[pl.ds(0, chunk)]

            def load(j, buf, sem, src_hbm=src_hbm, per_w=per_w, chunk=chunk):
                return pltpu.make_async_copy(src_hbm.at[pl.ds(wid * per_w + j * chunk, chunk)], buf, sem)

            def spread(j, buf, idx_v=idx_v, lists=lists):
                for k in range(lists):
                    pltpu.sync_copy(buf, out_hbm.at[idx_v.at[k, j]])

            load(0, buf0, sem0).start()

            @pl.loop(0, n_chunks, step=2)
            def _(j, load=load, spread=spread, buf0=buf0, buf1=buf1, n_chunks=n_chunks):
                load(j + 1, buf1, sem1).start()
                load(j, buf0, sem0).wait()
                spread(j, buf0)

                @pl.when(j + 2 < n_chunks)
                def _():
                    load(j + 2, buf0, sem0).start()

                load(j + 1, buf1, sem1).wait()
                spread(j + 1, buf1)

    return scatter_kernel(*args)


def _ffn_kernel(be_ref, nv_ref, x_ref, w1_ref, w3_ref, w2_ref, y_ref, w1b, w3b, w2b):
    i = pl.program_id(0)
    changed = jnp.logical_or(i == 0, be_ref[i] != be_ref[jnp.maximum(i - 1, 0)])

    @pl.when(changed)
    def _():
        w1b[...] = w1_ref[0, 0].astype(BF16)
        w3b[...] = w3_ref[0, 0].astype(BF16)
        w2b[...] = w2_ref[0, 0].astype(BF16)

    @pl.when(i < nv_ref[0])
    def _():
        half = D // 2
        x_lo, x_hi = (v.astype(BF16) for v in _unpack_bf16_pairs(x_ref[...]))
        gate = _dot(x_lo, w1b[0:half, :]) + _dot(x_hi, w1b[half:D, :])
        up = _dot(x_lo, w3b[0:half, :]) + _dot(x_hi, w3b[half:D, :])
        y_ref[...] = _pack_bf16_pairs(_dot((_silu(gate) * up).astype(BF16), w2b[...]))

    @pl.when(i >= nv_ref[0])
    def _():
        y_ref[...] = jnp.zeros_like(y_ref)


def moe_ffn(xs, blk_expert, n_valid, w1, w3, w2, layer):
    n_rows = xs.shape[0]
    nb = n_rows // MOE_TM
    return pl.pallas_call(
        _ffn_kernel,
        out_shape=jax.ShapeDtypeStruct((n_rows, D // 2), jnp.uint32),
        grid_spec=pltpu.PrefetchScalarGridSpec(
            num_scalar_prefetch=2,
            grid=(nb,),
            in_specs=[pl.BlockSpec((MOE_TM, D // 2), lambda i, be, nv: (i, 0)),
                      pl.BlockSpec((1, 1, D, D_EXPERT), lambda i, be, nv: (layer, be[i], 0, 0)),
                      pl.BlockSpec((1, 1, D, D_EXPERT), lambda i, be, nv: (layer, be[i], 0, 0)),
                      pl.BlockSpec((1, 1, D_EXPERT, D), lambda i, be, nv: (layer, be[i], 0, 0))],
            out_specs=pl.BlockSpec((MOE_TM, D // 2), lambda i, be, nv: (i, 0)),
            scratch_shapes=[pltpu.VMEM((D, D_EXPERT), BF16), pltpu.VMEM((D, D_EXPERT), BF16),
                            pltpu.VMEM((D_EXPERT, D), BF16)]),
        compiler_params=_cparams(("arbitrary",)),
        name="moe_ffn",
    )(blk_expert, n_valid, xs, w1, w3, w2)


def _combine_kernel(h_ref, y0_ref, y1_ref, wt_ref, mod_ref, lnw_ref, lnb_ref, o_ref):
    half = D // 2
    lo0, hi0 = _unpack_bf16_pairs(y0_ref[0])
    lo1, hi1 = _unpack_bf16_pairs(y1_ref[0])
    w0, w1 = wt_ref[:, 0:1], wt_ref[:, 1:2]
    r_lo = ALPHA * h_ref[:, 0:half] + mod_ref[0, 5:6, 0:half] * (w0 * lo0 + w1 * lo1)
    r_hi = ALPHA * h_ref[:, half:D] + mod_ref[0, 5:6, half:D] * (w0 * hi0 + w1 * hi1)
    mu = (jnp.sum(r_lo, axis=-1, keepdims=True) + jnp.sum(r_hi, axis=-1, keepdims=True)) * (1.0 / D)
    c_lo, c_hi = r_lo - mu, r_hi - mu
    var = (jnp.sum(c_lo * c_lo, axis=-1, keepdims=True) + jnp.sum(c_hi * c_hi, axis=-1, keepdims=True)) * (1.0 / D)
    inv = lax.rsqrt(var + LN_EPS)
    o_ref[:, 0:half] = c_lo * inv * lnw_ref[:, 0:half] + lnb_ref[:, 0:half]
    o_ref[:, half:D] = c_hi * inv * lnw_ref[:, half:D] + lnb_ref[:, half:D]


def moe_combine(h1, y_rows, wts, tile_off, mod, lnw, lnb):
    B, T, _ = h1.shape
    N = B * T
    tm = min(512, N)
    per_b = T // tm
    mod_map = (lambda i: (i // per_b, 0, 0)) if mod.shape[0] > 1 else (lambda i: (0, 0, 0))
    out = pl.pallas_call(
        _combine_kernel,
        out_shape=jax.ShapeDtypeStruct((N, D), F32),
        grid=(N // tm,),
        in_specs=[pl.BlockSpec((tm, D), lambda i: (i, 0)),
                  pl.BlockSpec((1, tm, D // 2), lambda i: (0, i + tile_off, 0)),
                  pl.BlockSpec((1, tm, D // 2), lambda i: (1, i + tile_off, 0)),
                  pl.BlockSpec((tm, TOP_K), lambda i: (i + tile_off, 0)),
                  pl.BlockSpec((1, 6, D), mod_map),
                  pl.BlockSpec((1, D), lambda i: (0, 0)), pl.BlockSpec((1, D), lambda i: (0, 0))],
        out_specs=pl.BlockSpec((tm, D), lambda i: (i, 0)),
        compiler_params=_cparams(("parallel",)),
        name="moe_combine",
    )(h1.reshape(N, D), y_rows, y_rows, wts, mod, lnw, lnb)
    return out.reshape(B, T, D)


def _route_kernel(lg_ref, dest_ref, wt_ref, cnt_ref, tri_sc, start_sc, run_sc):
    ph, i = pl.program_id(0), pl.program_id(1)
    tr = lg_ref.shape[1]

    @pl.when(jnp.logical_and(ph == 0, i == 0))
    def _():
        r = lax.broadcasted_iota(jnp.int32, (tr, tr), 0)
        c = lax.broadcasted_iota(jnp.int32, (tr, tr), 1)
        tri_sc[...] = (r < c).astype(BF16)
        start_sc[...] = jnp.zeros_like(start_sc)
        run_sc[...] = jnp.zeros_like(run_sc)

    @pl.when(jnp.logical_and(ph == 1, i == 0))
    def _():
        cnt = run_sc[...].astype(jnp.int32)
        cnt_ref[...] = cnt
        padded = jnp.bitwise_and(cnt + (MOE_TM - 1), -MOE_TM)
        row = lax.broadcasted_iota(jnp.int32, padded.shape, 0)
        acc = padded
        for s in (1, 2, 4, 8, 16):
            acc = acc + jnp.where(row >= s, pltpu.roll(acc, s, 0), 0)
        start_sc[...] = (acc - padded).astype(F32)
        run_sc[...] = jnp.zeros_like(run_sc)

    lg = lg_ref[...]
    gl = lg[N_EXPERTS:N_EXPERTS + N_GROUPS]
    gmax = jnp.max(gl, axis=0, keepdims=True)
    sub4 = lax.broadcasted_iota(jnp.int32, gl.shape, 0)
    g_sel = jnp.min(jnp.where(gl == gmax, sub4, N_GROUPS), axis=0, keepdims=True)
    p_group = 1.0 / jnp.sum(jnp.exp(gl - gmax), axis=0, keepdims=True)
    el = lg[0:EXPERTS_PER_GROUP]
    for g in range(1, N_GROUPS):
        el = jnp.where(g_sel == g, lg[g * EXPERTS_PER_GROUP:(g + 1) * EXPERTS_PER_GROUP], el)
    sub8 = lax.broadcasted_iota(jnp.int32, el.shape, 0)
    e1 = jnp.max(el, axis=0, keepdims=True)
    i1 = jnp.min(jnp.where(el == e1, sub8, EXPERTS_PER_GROUP), axis=0, keepdims=True)
    rest = jnp.where(sub8 == i1, -jnp.inf, el)
    e2 = jnp.max(rest, axis=0, keepdims=True)
    i2 = jnp.min(jnp.where(rest == e2, sub8, EXPERTS_PER_GROUP), axis=0, keepdims=True)
    den = jnp.sum(jnp.exp(el - e1), axis=0, keepdims=True)
    p1 = 1.0 / den
    p2 = jnp.exp(e2 - e1) / den
    wt_ref[0:1, :] = p_group * p1 / (p1 + p2)
    wt_ref[1:2, :] = p_group * p2 / (p1 + p2)

    sub32 = lax.broadcasted_iota(jnp.int32, (N_EXPERTS, tr), 0)
    oh = [(sub32 == g_sel * EXPERTS_PER_GROUP + ix).astype(F32) for ix in (i1, i2)]
    cnt = [jnp.sum(o, axis=1, keepdims=True) for o in oh]
    @pl.when(ph == 0)
    def _():
        dest_ref[...] = jnp.zeros_like(dest_ref)

    @pl.when(ph == 1)
    def _():
        before = start_sc[:, 0:1] + run_sc[:, 0:1]
        for k in range(TOP_K):
            prior = _dot(oh[k].astype(BF16), tri_sc[...]) + before + (cnt[0] if k == 1 else 0.0)
            dest_ref[k:k + 1, :] = jnp.sum(oh[k] * prior, axis=0, keepdims=True).astype(jnp.int32)

    run_sc[...] = run_sc[...] + (cnt[0] + cnt[1])


def moe_route(logits_t):
    N = logits_t.shape[1]
    tr = next(t for t in (1024, 512, 256) if N % t == 0)
    return pl.pallas_call(
        _route_kernel,
        out_shape=(jax.ShapeDtypeStruct((TOP_K, N), jnp.int32), jax.ShapeDtypeStruct((TOP_K, N), F32),
                   jax.ShapeDtypeStruct((N_EXPERTS, LANE), jnp.int32)),
        grid=(2, N // tr),
        in_specs=[pl.BlockSpec((ROUTE_ROWS, tr), lambda p, i: (0, i))],
        out_specs=(pl.BlockSpec((TOP_K, tr), lambda p, i: (0, i * p)), pl.BlockSpec((TOP_K, tr), lambda p, i: (0, i * p)),
                   pl.BlockSpec((N_EXPERTS, LANE), lambda p, i: (0, 0))),
        scratch_shapes=[pltpu.VMEM((tr, tr), BF16), pltpu.VMEM((N_EXPERTS, LANE), F32),
                        pltpu.VMEM((N_EXPERTS, LANE), F32)],
        compiler_params=_cparams(("arbitrary", "arbitrary")),
        name="moe_route",
    )(logits_t)


def _block_tables(counts, n_assign):
    padded = (counts + MOE_TM - 1) // MOE_TM * MOE_TM
    pad_end = jnp.cumsum(padded)
    pad_start = pad_end - padded
    nb = -(-n_assign // MOE_TM) + N_EXPERTS
    blk_start = jnp.arange(nb, dtype=jnp.int32) * MOE_TM
    blk_expert = jnp.minimum(jnp.sum((pad_end[None, :] <= blk_start[:, None]).astype(jnp.int32), axis=1), N_EXPERTS - 1)
    n_valid = (pad_end[-1] // MOE_TM).astype(jnp.int32).reshape(1)
    n_fill = nb * MOE_TM - n_assign
    gap = padded - counts
    gap_end = jnp.cumsum(gap)
    k = jnp.arange(n_fill, dtype=jnp.int32)[:, None]
    sel = jnp.logical_and(k >= (gap_end - gap)[None, :], k < gap_end[None, :])
    in_gap = jnp.sum(jnp.where(sel, (pad_start + counts - (gap_end - gap))[None, :] + k, 0), axis=1)
    fill = jnp.where(k[:, 0] < gap_end[-1], in_gap, pad_end[-1] + k[:, 0] - gap_end[-1])
    return blk_expert.astype(jnp.int32), n_valid, fill.astype(jnp.int32), nb * MOE_TM


def hier_moe_and_norm(streams, w1, w3, w2, layer, lnw, lnb):
    logits_t = streams[0][2] if len(streams) == 1 else jnp.concatenate([s[2] for s in streams], 1)
    dest, wts_t, counts = moe_route(logits_t)
    wts = wts_t.T
    blk_expert, n_valid, fill_rows, n_rows = _block_tables(counts[:, 0], TOP_K * logits_t.shape[1])
    n_all = logits_t.shape[1]
    toks, idxs, offs, off = [], [], [], 0
    for h1, tok, _, _ in streams:
        n = h1.shape[0] * h1.shape[1]
        toks.append(tok.reshape(n, D // 2))
        idxs.append(dest[:, off:off + n])
        offs.append(off // min(512, n))
        off += n
    toks.append(jnp.zeros((fill_rows.shape[0], D // 2), jnp.uint32))
    idxs.append(fill_rows.reshape(1, -1))
    xs = sc_scatter_rows(toks, idxs, n_rows)
    y = moe_ffn(xs, blk_expert, n_valid, w1, w3, w2, layer)
    y_rows = sc_gather_rows(y, dest.reshape(-1)).reshape(TOP_K, n_all, D // 2)
    return [moe_combine(h1, y_rows, wts, o, mod, lnw, lnb) for (h1, _, _, mod), o in zip(streams, offs)]


def _even_weight_columns():
    n = np.arange(HEAD_DIM)
    perm = (n % 32) // 16 * 32 + n // 32 * 16 + n % 16
    idx = np.zeros(P_W, np.int32)
    scale = np.zeros(P_W, np.float32)
    for hd in range(A_Q_HEADS):
        idx[C_QA + hd * 64:C_QA + (hd + 1) * 64] = hd * 64 + perm
    scale[C_QA:C_QA + 512] = HEAD_DIM ** -0.5 * LOG2_E
    for g in range(A_KV_HEADS):
        for rep in range(2):
            o = g * 128 + rep * 64
            idx[C_KD + o:C_KD + o + 64] = 512 + g * 64 + perm
            idx[C_VD + o:C_VD + o + 64] = 640 + g * 64 + n
    scale[C_KD:C_VD + 256] = 1.0
    idx[C_QG:C_QG + 256] = 768 + np.arange(256)
    scale[C_QG:C_QG + 256] = GLA_DK ** -0.5
    idx[C_KG:C_KG + 256] = 1024 + np.arange(256)
    idx[C_VG:C_VG + 512] = 1280 + np.arange(512)
    idx[C_RG:C_RG + 512] = 1792 + np.arange(512)
    scale[C_KG:C_KG + 256] = 1.0
    scale[C_VG:C_RG + 512] = 1.0
    for d in range(2):
        idx[C_GG + d * 128:C_GG + d * 128 + GLA_RANK] = 2304 + d * GLA_RANK + np.arange(GLA_RANK)
        scale[C_GG + d * 128:C_GG + d * 128 + GLA_RANK] = 1.0
    return idx, scale


def _rope_tables(S):
    row = jnp.repeat(jnp.arange(S // GRID_W), GRID_W).astype(F32)
    col = jnp.tile(jnp.arange(GRID_W), S // GRID_W).astype(F32)
    axis_dim = HEAD_DIM // 2
    inv_freq = ROPE_BASE ** (-jnp.arange(0, axis_dim, 2, dtype=F32) / axis_dim)
    ang = jnp.concatenate([row[:, None] * inv_freq, col[:, None] * inv_freq], -1)
    cos, sin = jnp.cos(ang), jnp.sin(ang)
    cos_t = jnp.tile(cos, (1, 4))
    sin_t = jnp.tile(jnp.concatenate([-sin, sin], -1), (1, 2))
    return cos_t, sin_t


def _router_weights(wg, bg, we, be):
    w = jnp.zeros((D, ROUTE_W), F32).at[:, :N_EXPERTS].set(we).at[:, N_EXPERTS:N_EXPERTS + N_GROUPS].set(wg)
    b = jnp.zeros((1, ROUTE_W), F32).at[0, :N_EXPERTS].set(be).at[0, N_EXPERTS:N_EXPERTS + N_GROUPS].set(bg)
    hi = w.astype(BF16)
    return hi, (w - hi.astype(F32)).astype(BF16), b


def kernel(x, c, ctx, c_ctx, w_in_even, w_out_even, attn_sink, gla_wa2, gla_ba, gla_norm_w, w_in_odd, conv_w, conv_b, w_out_odd, ada_w, ada_b, ln_w, ln_b, router_wg, router_bg, router_we, router_be, moe_w1, moe_w3, moe_w2):
    B, S, _ = x.shape
    L = ctx.shape[1]
    cos_t, sin_t = _rope_tables(S)
    cos_c, sin_c = jnp.ones((L, LANE), F32), jnp.zeros((L, LANE), F32)
    col_idx, col_scale = _even_weight_columns()

    n_cond = -(-(B + 1) // 8) * 8
    cc = jnp.zeros((n_cond, D), F32).at[:B].set(c).at[B].set(c_ctx)
    mods = ada_modulation_all(cc, ada_w, ada_b).reshape(DEPTH, n_cond, 6, D)

    h_lat, h_ctx = x, ctx
    for l in range(DEPTH):
        i = l // 2
        need_ctx = any(j % 2 == 0 for j in range(l + 1, DEPTH))
        m_lat = mods[l, :B]
        m_ctx = mods[l, B:B + 1]
        lnw0, lnb0 = ln_w[l, 0:1], ln_b[l, 0:1]
        lnw1, lnb1 = ln_w[l, 1:2], ln_b[l, 1:2]
        rwh, rwl, rb = _router_weights(router_wg[l], router_bg[l], router_we[l], router_be[l])
        streams = []
        if l % 2 == 0:
            w_in = (w_in_even[i][:, col_idx] * col_scale[None, :]).astype(BF16)
            w_out = w_out_even[i].astype(BF16)
            wa_p = jnp.zeros((2, LANE, GLA_HEADS * GLA_DK), F32).at[:, :GLA_RANK].set(gla_wa2[i]).astype(BF16)
            ba = gla_ba[i].reshape(2, 1, -1)
            nw = gla_norm_w[i].reshape(1, -1)
            p_ctx = inproj_even(h_ctx, m_ctx, w_in, cos_c, sin_c)
            p_lat = inproj_even(h_lat, m_lat, w_in, cos_t, sin_t)
            a_lat = attention(p_lat, p_ctx, attn_sink[i], True)
            s0 = jnp.zeros((B, 2, 2, GLA_DV, LANE), F32)
            oc_f, oc_b, s_ctx = gla_scan(p_ctx, wa_p, ba, s0)
            ol_f, ol_b, _ = gla_scan(p_lat, wa_p, ba, s_ctx)
            streams.append(outproj_even(a_lat, ol_f, ol_b, p_lat, nw, w_out, h_lat, m_lat, lnw0, lnb0, rwh, rwl, rb)
                           + (m_lat,))
            if need_ctx:
                a_ctx = attention(p_ctx, p_ctx, attn_sink[i], False)
                streams.append(outproj_even(a_ctx, oc_f, oc_b, p_ctx, nw, w_out, h_ctx, m_ctx, lnw0, lnb0,
                                            rwh, rwl, rb) + (m_ctx,))
        else:
            w_in = w_in_odd[i].astype(BF16)
            w_out = w_out_odd[i].astype(BF16)
            cb = conv_b[i].reshape(1, D)
            pairs = [(h_lat, m_lat)] + ([(h_ctx, m_ctx)] if need_ctx else [])
            for h, m in pairs:
                gz = inproj_odd(h, m, w_in)
                streams.append(outproj_odd(gz, conv_w[i], cb, w_out, h, m, lnw0, lnb0, rwh, rwl, rb) + (m,))
        outs = hier_moe_and_norm(streams, moe_w1, moe_w3, moe_w2, l, lnw1, lnb1)
        h_lat = outs[0]
        if need_ctx:
            h_ctx = outs[1]
    return h_lat
```

```python
import functools

import numpy as np
import jax
import jax.numpy as jnp
from jax import lax
from jax.experimental import pallas as pl
from jax.experimental.pallas import tpu as pltpu
from jax.experimental.pallas import tpu_sc as plsc

F32 = jnp.float32
BF16 = jnp.bfloat16
HIGHEST = lax.Precision.HIGHEST

D = 1024
DEPTH = 4
GRID_W = 64
HEAD_DIM = 64
A_Q_HEADS = 8
A_KV_HEADS = 2
WINDOW = 128
ROPE_BASE = 10000.0
GLA_HEADS = 4
GLA_DK = 64
GLA_DV = 128
GLA_RANK = 16
GLA_TAU = 16.0
GLA_CHUNK = 64
N_GROUPS = 4
EXPERTS_PER_GROUP = 8
N_EXPERTS = 32
TOP_K = 2
D_EXPERT = 512
ALPHA = (2.0 * DEPTH) ** 0.25
LN_EPS = 1e-5
RMS_EPS = 1e-6

LANE = 128
VMEM_LIMIT = 48 * 1024 * 1024

C_QA, C_KD, C_VD, C_VG, C_RG, C_QG, C_KG, C_GG = 0, 512, 768, 1024, 1536, 2048, 2304, 2560
P_W = 2816
ROUTE_W = 128
WT_ROWS = 8
ROUTE_ROWS = 40
MOE_TM = 512
SC_CORES, SC_SUBCORES = 2, 16
SC_WORKERS = SC_CORES * SC_SUBCORES
SC_CHUNK = 64
NEG = -1e30
LOG2_E = 1.4426950408889634


def _cparams(sem):
    return pltpu.CompilerParams(dimension_semantics=sem, vmem_limit_bytes=VMEM_LIMIT)


def _dot(a, b):
    return jnp.dot(a, b, preferred_element_type=F32)


def _dot_nt(a, b):
    return lax.dot_general(a, b, (((1,), (1,)), ((), ())), preferred_element_type=F32)


def _dot_tn(a, b):
    return lax.dot_general(a, b, (((0,), (0,)), ((), ())), preferred_element_type=F32)


def _silu(x):
    return x * (1.0 / (1.0 + jnp.exp(-x)))


def _ada_kernel(c_ref, w_ref, b_ref, o_ref):
    s = _silu(c_ref[...])
    o_ref[0] = jnp.dot(s, w_ref[0], precision=HIGHEST, preferred_element_type=F32) + b_ref[0]


def ada_modulation_all(cc, ada_w, ada_b):
    R = cc.shape[0]
    tn = 1536
    return pl.pallas_call(
        _ada_kernel,
        out_shape=jax.ShapeDtypeStruct((DEPTH, R, 6 * D), F32),
        grid=(DEPTH, 6 * D // tn),
        in_specs=[pl.BlockSpec((R, D), lambda l, n: (0, 0)),
                  pl.BlockSpec((1, D, tn), lambda l, n: (l, 0, n)),
                  pl.BlockSpec((1, 1, tn), lambda l, n: (l, 0, n))],
        out_specs=pl.BlockSpec((1, R, tn), lambda l, n: (l, 0, n)),
        compiler_params=_cparams(("arbitrary", "arbitrary")),
        name="ada_modulation",
    )(cc, ada_w, ada_b.reshape(DEPTH, 1, 6 * D))


_EVEN_CHUNKS = ((0, 512, True), (512, 768, True), (768, 1024, False), (1024, 1536, False),
                (1536, 2048, False), (2048, 2560, False), (2560, 2816, False))


def _inproj_even_kernel(h_ref, mod_ref, w_ref, cos_ref, sin_ref, p_ref):
    tm = h_ref.shape[1]
    u = (h_ref[0] * (1.0 + mod_ref[0, 1:2, :]) + mod_ref[0, 0:1, :]).astype(BF16)
    cos = cos_ref[...]
    sin = sin_ref[...]
    lane = lax.broadcasted_iota(jnp.int32, (tm, LANE), 1)
    first_half = (lane % HEAD_DIM) < (HEAD_DIM // 2)
    for c0, c1, rope in _EVEN_CHUNKS:
        acc = _dot(u, w_ref[:, c0:c1])
        if rope:
            for i in range((c1 - c0) // LANE):
                x = acc[:, i * LANE:(i + 1) * LANE]
                partner = jnp.where(first_half, pltpu.roll(x, LANE - 32, 1), pltpu.roll(x, 32, 1))
                p_ref[0, :, c0 + i * LANE:c0 + (i + 1) * LANE] = (x * cos + partner * sin).astype(BF16)
        else:
            p_ref[0, :, c0:c1] = acc.astype(BF16)


def inproj_even(h, mod, w, cos_t, sin_t):
    B, T, _ = h.shape
    tm = min(512, T)
    mb = mod.shape[0]
    return pl.pallas_call(
        _inproj_even_kernel,
        out_shape=jax.ShapeDtypeStruct((B, T, P_W), BF16),
        grid=(B, T // tm),
        in_specs=[pl.BlockSpec((1, tm, D), lambda b, j: (b, j, 0)),
                  pl.BlockSpec((1, 6, D), (lambda b, j: (b, 0, 0)) if mb > 1 else (lambda b, j: (0, 0, 0))),
                  pl.BlockSpec((D, P_W), lambda b, j: (0, 0)),
                  pl.BlockSpec((tm, LANE), lambda b, j: (j, 0)),
                  pl.BlockSpec((tm, LANE), lambda b, j: (j, 0))],
        out_specs=pl.BlockSpec((1, tm, P_W), lambda b, j: (b, j, 0)),
        compiler_params=_cparams(("parallel", "arbitrary")),
        name="inproj_even",
    )(h, mod, w, cos_t, sin_t)


def _attn_kernel(*refs, tq, has_window):
    if has_window:
        sink_ref, q_ref, kw_ref, vw_ref, kc_ref, vc_ref, o_ref = refs
    else:
        sink_ref, q_ref, kc_ref, vc_ref, o_ref = refs
    group = A_Q_HEADS // A_KV_HEADS
    rows = group * tq
    lo = lax.broadcasted_iota(jnp.int32, (tq, LANE), 1) < HEAD_DIM
    den_lanes = lax.broadcasted_iota(jnp.int32, (rows, LANE), 1) >= HEAD_DIM

    def with_ones(v):
        return jnp.where(lax.broadcasted_iota(jnp.int32, v.shape, 1) < HEAD_DIM, v, jnp.ones_like(v))

    if has_window:
        S = kw_ref.shape[1]
        wk = tq + 2 * WINDOW
        q0 = pl.program_id(1) * tq
        wstart = pl.multiple_of(jnp.clip(q0 - WINDOW, 0, S - wk), LANE)
        qpos = q0 + lax.broadcasted_iota(jnp.int32, (tq, wk), 0)
        kpos = wstart + lax.broadcasted_iota(jnp.int32, (tq, wk), 1)
        band = jnp.tile(jnp.where(jnp.abs(qpos - kpos) <= WINDOW, 0.0, NEG), (group, 1))
    groups = range(A_KV_HEADS)
    cols = [slice(g * LANE, (g + 1) * LANE) for g in groups]
    q4, snk, sc, sw, m, outs = [], [], [], [], [], []
    for g in groups:
        qs = []
        for pr in range(group // 2):
            qblk = q_ref[0, :, (2 * g + pr) * LANE:(2 * g + pr + 1) * LANE]
            zero = jnp.zeros_like(qblk)
            qs += [jnp.where(lo, qblk, zero), jnp.where(lo, zero, qblk)]
        q4.append(jnp.concatenate(qs, axis=0))
        snk.append(jnp.concatenate([jnp.full((tq, 1), sink_ref[group * g + i] * LOG2_E, F32) for i in range(group)],
                                   axis=0))
    for g in groups:
        if has_window:
            keys = jnp.concatenate([kw_ref[0, pl.ds(wstart, wk), cols[g]], kc_ref[0, :, cols[g]]], axis=0)
            s = _dot_nt(q4[g], keys)
            sc.append(jnp.concatenate([s[:, 0:wk] + band, s[:, wk:]], axis=1))
        else:
            sc.append(_dot_nt(q4[g], kc_ref[0, :, cols[g]]))
    for g in groups:
        m.append(jnp.maximum(jnp.max(sc[g], axis=-1, keepdims=True), snk[g]))
    for g in groups:
        if has_window:
            vals = jnp.concatenate([vw_ref[0, pl.ds(wstart, wk), cols[g]], vc_ref[0, :, cols[g]]], axis=0)
        else:
            vals = vc_ref[0, :, cols[g]]
        o = _dot(jnp.exp2((sc[g] - m[g]).astype(BF16)), with_ones(vals))
        outs.append(o + jnp.where(den_lanes, jnp.exp2(snk[g] - m[g]), 0.0))
    for g in groups:
        o = outs[g]
        swapped = pltpu.roll(o, HEAD_DIM, 1)
        for pr in range(group // 2):
            ev = slice(2 * pr * tq, (2 * pr + 1) * tq)
            od = slice((2 * pr + 1) * tq, (2 * pr + 2) * tq)
            res = jnp.where(lo, o[ev] / swapped[ev], swapped[od] / o[od])
            o_ref[0, :, (2 * g + pr) * LANE:(2 * g + pr + 1) * LANE] = res.astype(BF16)


def attention(p_q, p_ctx, sink, has_window):
    B, T, _ = p_q.shape
    L = p_ctx.shape[1]
    tq = 128
    in_specs = [pl.BlockSpec(memory_space=pltpu.SMEM),
                pl.BlockSpec((1, tq, 512), lambda b, j: (b, j, C_QA // 512))]
    args = [sink, p_q]
    if has_window:
        in_specs += [pl.BlockSpec((1, T, 256), lambda b, j: (b, 0, C_KD // 256)),
                     pl.BlockSpec((1, T, 256), lambda b, j: (b, 0, C_VD // 256))]
        args += [p_q, p_q]
    in_specs += [pl.BlockSpec((1, L, 256), lambda b, j: (b, 0, C_KD // 256)),
                 pl.BlockSpec((1, L, 256), lambda b, j: (b, 0, C_VD // 256))]
    args += [p_ctx, p_ctx]
    return pl.pallas_call(
        functools.partial(_attn_kernel, tq=tq, has_window=has_window),
        out_shape=jax.ShapeDtypeStruct((B, T, 512), BF16),
        grid=(B, T // tq),
        in_specs=in_specs,
        out_specs=pl.BlockSpec((1, tq, 512), lambda b, j: (b, j, 0)),
        compiler_params=_cparams(("parallel", "arbitrary")),
        name="window_attention" if has_window else "context_attention",
    )(*args)


def _log_sigmoid(x):
    return jnp.minimum(x, 0.0) - jnp.log(1.0 + jnp.exp(-jnp.abs(x)))


def _gla_kernel(qf_ref, kf_ref, vf_ref, gf_ref, qb_ref, kb_ref, vb_ref, gb_ref, wa_ref, ba_ref, s0_ref,
                of_ref, ob_ref, sfin_ref, s_sc):
    j = pl.program_id(1)
    nblk = pl.num_programs(1)
    tb = qf_ref.shape[1]
    nc = tb // GLA_CHUNK

    @pl.when(j == 0)
    def _():
        s_sc[...] = s0_ref[0]

    C = GLA_CHUNK
    ri = lax.broadcasted_iota(jnp.int32, (2 * C, 2 * C), 0) % C
    ci = lax.broadcasted_iota(jnp.int32, (2 * C, 2 * C), 1) % C
    rb = lax.broadcasted_iota(jnp.int32, (tb, tb), 0)
    cb = lax.broadcasted_iota(jnp.int32, (tb, tb), 1)
    same_chunk = (rb // C) == (cb // C)
    lo = lax.broadcasted_iota(jnp.int32, (C, LANE), 1) < GLA_DK

    def per_head(x):
        zero = jnp.zeros_like(x)
        return jnp.concatenate([jnp.where(lo, x, zero), jnp.where(lo, zero, x)], axis=0)

    io = ((qf_ref, kf_ref, vf_ref, gf_ref, of_ref), (qb_ref, kb_ref, vb_ref, gb_ref, ob_ref))
    causal = ((ri >= ci), (ci >= ri))
    tri = (jnp.logical_and(same_chunk, rb >= cb).astype(BF16), jnp.logical_and(same_chunk, cb >= rb).astype(BF16))
    b_all = []
    for d in range(2):
        g = _dot(io[d][3][0, :, d * LANE:(d + 1) * LANE], wa_ref[d]) + ba_ref[d]
        log_a = _log_sigmoid(g) / GLA_TAU
        la1 = log_a.astype(BF16)
        rem = log_a - la1.astype(F32)
        la2 = rem.astype(BF16)
        la3 = (rem - la2.astype(F32)).astype(BF16)
        b_all.append(_dot(tri[d], la1) + _dot(tri[d], la2) + _dot(tri[d], la3))
    state = [[s_sc[d, pair] for pair in range(2)] for d in range(2)]
    for step in range(nc):
        for d in range(2):
            q_ref, k_ref, v_ref, _, o_ref = io[d]
            c = step if d == 0 else nc - 1 - step
            rows = slice(c * C, (c + 1) * C)
            b = b_all[d][rows]
            b_last = b[C - 1:C] if d == 0 else b[0:1]
            qc = q_ref[0, rows, :].astype(F32)
            kc = k_ref[0, rows, :].astype(F32)
            q_in = (qc * jnp.exp(b)).astype(BF16)
            k_in = (kc * jnp.exp(-b)).astype(BF16)
            k_st = (kc * jnp.exp(b_last - b)).astype(BF16)
            dl = jnp.exp(b_last)
            for pair in range(2):
                cols = slice(pair * LANE, (pair + 1) * LANE)
                st = state[d][pair]
                q2 = per_head(q_in[:, cols])
                res = _dot_nt(q2, jnp.concatenate([st.astype(BF16), per_head(k_in[:, cols])], axis=0))
                attn = jnp.where(causal[d], res[:, GLA_DV:], 0.0).astype(BF16)
                v2 = jnp.concatenate([v_ref[0, rows, (2 * pair + hh) * GLA_DV:(2 * pair + hh + 1) * GLA_DV]
                                      for hh in range(2)], axis=0)
                o2 = _dot(attn, v2) + res[:, 0:GLA_DV]
                for hh in range(2):
                    o_ref[0, rows, (2 * pair + hh) * GLA_DV:(2 * pair + hh + 1) * GLA_DV] = (
                        o2[hh * C:(hh + 1) * C].astype(o_ref.dtype))
                upd = _dot_tn(v2, per_head(k_st[:, cols]))
                state[d][pair] = st * dl[:, cols] + upd
    for d in range(2):
        for pair in range(2):
            s_sc[d, pair] = state[d][pair]

    @pl.when(j == nblk - 1)
    def _():
        sfin_ref[0] = s_sc[...]


def gla_scan(p, wa_p, ba, s0):
    B, T, _ = p.shape
    tb = min(512, T)
    nblk = T // tb
    fwd = lambda b, j: (b, j)
    bwd = lambda b, j: (b, nblk - 1 - j)

    def specs(im):
        return [pl.BlockSpec((1, tb, 256), lambda b, j: im(b, j) + (C_QG // 256,)),
                pl.BlockSpec((1, tb, 256), lambda b, j: im(b, j) + (C_KG // 256,)),
                pl.BlockSpec((1, tb, 512), lambda b, j: im(b, j) + (C_VG // 512,)),
                pl.BlockSpec((1, tb, 256), lambda b, j: im(b, j) + (C_GG // 256,))]

    return pl.pallas_call(
        _gla_kernel,
        out_shape=(jax.ShapeDtypeStruct((B, T, 512), BF16), jax.ShapeDtypeStruct((B, T, 512), BF16),
                   jax.ShapeDtypeStruct(s0.shape, F32)),
        grid=(B, nblk),
        in_specs=specs(fwd) + specs(bwd) + [
            pl.BlockSpec((2, LANE, 256), lambda b, j: (0, 0, 0)),
            pl.BlockSpec((2, 1, 256), lambda b, j: (0, 0, 0)),
            pl.BlockSpec((1, 2, 2, GLA_DV, LANE), lambda b, j: (b, 0, 0, 0, 0))],
        out_specs=(pl.BlockSpec((1, tb, 512), lambda b, j: (b, j, 0)),
                   pl.BlockSpec((1, tb, 512), lambda b, j: (b, nblk - 1 - j, 0)),
                   pl.BlockSpec((1, 2, 2, GLA_DV, LANE), lambda b, j: (b, 0, 0, 0, 0))),
        scratch_shapes=[pltpu.VMEM((2, 2, GLA_DV, LANE), F32)],
        compiler_params=_cparams(("parallel", "arbitrary")),
        name="gla_scan",
    )(p, p, p, p, p, p, p, p, wa_p, ba, s0)


def _layer_norm(r, w, b):
    mu = jnp.mean(r, axis=-1, keepdims=True)
    xc = r - mu
    var = jnp.mean(xc * xc, axis=-1, keepdims=True)
    return xc * lax.rsqrt(var + LN_EPS) * w + b


def _pack_bf16_pairs(x):
    half = x.shape[1] // 2
    lo = lax.bitcast_convert_type(x[:, :half].astype(BF16).astype(F32), jnp.uint32)
    hi = lax.bitcast_convert_type(x[:, half:].astype(BF16).astype(F32), jnp.uint32)
    return jnp.bitwise_or(hi, lax.shift_right_logical(lo, jnp.uint32(16)))


def _unpack_bf16_pairs(p):
    lo = lax.bitcast_convert_type(lax.shift_left(p, jnp.uint32(16)), F32)
    hi = lax.bitcast_convert_type(jnp.bitwise_and(p, jnp.uint32(0xFFFF0000)), F32)
    return lo, hi


def _post_norm_and_route(h, y, mod_ref, lnw_ref, lnb_ref, rwh_ref, rwl_ref, rb_ref, h1_ref, tok_ref, lg_ref):
    h1 = _layer_norm(ALPHA * h + mod_ref[0, 2:3, :] * y, lnw_ref[...], lnb_ref[...])
    h1_ref[0] = h1
    tok = h1 * (1.0 + mod_ref[0, 4:5, :]) + mod_ref[0, 3:4, :]
    tok_ref[0] = _pack_bf16_pairs(tok)
    hi = tok.astype(BF16)
    lo = (tok - hi.astype(F32)).astype(BF16)
    lg = _dot(hi, rwh_ref[...]) + _dot(lo, rwh_ref[...]) + _dot(hi, rwl_ref[...]) + rb_ref[...]
    lg_ref[...] = lg.T[0:ROUTE_ROWS, :]


def _epilogue_specs(tm, mb):
    mod_map = (lambda b, j: (b, 0, 0)) if mb > 1 else (lambda b, j: (0, 0, 0))
    const2 = lambda b, j: (0, 0)
    return [pl.BlockSpec((1, tm, D), lambda b, j: (b, j, 0)),
            pl.BlockSpec((1, 6, D), mod_map),
            pl.BlockSpec((1, D), const2), pl.BlockSpec((1, D), const2),
            pl.BlockSpec((D, ROUTE_W), const2), pl.BlockSpec((D, ROUTE_W), const2),
            pl.BlockSpec((1, ROUTE_W), const2)]


def _epilogue_outs(B, T, tm):
    nj = T // tm
    shapes = (jax.ShapeDtypeStruct((B, T, D), F32), jax.ShapeDtypeStruct((B, T, D // 2), jnp.uint32),
              jax.ShapeDtypeStruct((ROUTE_ROWS, B * T), F32))
    specs = (pl.BlockSpec((1, tm, D), lambda b, j: (b, j, 0)), pl.BlockSpec((1, tm, D // 2), lambda b, j: (b, j, 0)),
             pl.BlockSpec((ROUTE_ROWS, tm), lambda b, j: (0, b * nj + j)))
    return shapes, specs


def _outproj_even_kernel(a_ref, of_ref, ob_ref, rg_ref, nw_ref, wo_ref,
                         h_ref, mod_ref, lnw_ref, lnb_ref, rwh_ref, rwl_ref, rb_ref,
                         h1_ref, tok_ref, lg_ref):
    y = _dot(a_ref[0], wo_ref[0:512, :])
    for hd in range(GLA_HEADS):
        cols = slice(hd * GLA_DV, (hd + 1) * GLA_DV)
        o = of_ref[0, :, cols].astype(F32) + ob_ref[0, :, cols].astype(F32)
        o = o * lax.rsqrt(jnp.mean(o * o, axis=-1, keepdims=True) + RMS_EPS)
        gated = o * nw_ref[:, cols] * _silu(rg_ref[0, :, cols].astype(F32))
        y = y + _dot(gated.astype(BF16), wo_ref[512 + hd * GLA_DV:512 + (hd + 1) * GLA_DV, :])
    _post_norm_and_route(h_ref[0], y, mod_ref, lnw_ref, lnb_ref, rwh_ref, rwl_ref, rb_ref, h1_ref, tok_ref, lg_ref)


def outproj_even(a, o_f, o_b, p, norm_w, w_out, h, mod, lnw, lnb, rwh, rwl, rb):
    B, T, _ = h.shape
    tm = min(512, T)
    tile = lambda b, j: (b, j, 0)
    shapes, ospecs = _epilogue_outs(B, T, tm)
    return pl.pallas_call(
        _outproj_even_kernel,
        out_shape=shapes,
        grid=(B, T // tm),
        in_specs=[pl.BlockSpec((1, tm, 512), tile), pl.BlockSpec((1, tm, 512), tile), pl.BlockSpec((1, tm, 512), tile),
                  pl.BlockSpec((1, tm, 512), lambda b, j: (b, j, C_RG // 512)),
                  pl.BlockSpec((1, 512), lambda b, j: (0, 0)),
                  pl.BlockSpec((D, D), lambda b, j: (0, 0))] + _epilogue_specs(tm, mod.shape[0]),
        out_specs=ospecs,
        compiler_params=_cparams(("parallel", "arbitrary")),
        name="outproj_even",
    )(a, o_f, o_b, p, norm_w, w_out, h, mod, lnw, lnb, rwh, rwl, rb)


def _inproj_odd_kernel(h_ref, mod_ref, w_ref, o_ref):
    u = (h_ref[0] * (1.0 + mod_ref[0, 1:2, :]) + mod_ref[0, 0:1, :]).astype(BF16)
    o_ref[0, :, 0:D] = _dot(u, w_ref[:, 0:D]).astype(BF16)
    o_ref[0, :, D:2 * D] = (_dot(u, w_ref[:, D:2 * D]) * _dot(u, w_ref[:, 2 * D:3 * D])).astype(BF16)


def inproj_odd(h, mod, w):
    B, T, _ = h.shape
    tm = min(512, T)
    mb = mod.shape[0]
    return pl.pallas_call(
        _inproj_odd_kernel,
        out_shape=jax.ShapeDtypeStruct((B, T, 2 * D), BF16),
        grid=(B, T // tm),
        in_specs=[pl.BlockSpec((1, tm, D), lambda b, j: (b, j, 0)),
                  pl.BlockSpec((1, 6, D), (lambda b, j: (b, 0, 0)) if mb > 1 else (lambda b, j: (0, 0, 0))),
                  pl.BlockSpec((D, 3 * D), lambda b, j: (0, 0))],
        out_specs=pl.BlockSpec((1, tm, 2 * D), lambda b, j: (b, j, 0)),
        compiler_params=_cparams(("parallel", "arbitrary")),
        name="inproj_odd",
    )(h, mod, w)


HALO = 16


def _outproj_odd_kernel(gb_ref, z_ref, zp_ref, zn_ref, cw_ref, cb_ref, wo_ref,
                        h_ref, mod_ref, lnw_ref, lnb_ref, rwh_ref, rwl_ref, rb_ref,
                        h1_ref, tok_ref, lg_ref):
    j = pl.program_id(1)
    tm = z_ref.shape[1]
    z = z_ref[0].astype(F32)
    prev_row = jnp.where(j > 0, zp_ref[0, HALO - 1:HALO, :].astype(F32), 0.0)
    next_row = jnp.where(j < pl.num_programs(1) - 1, zn_ref[0, 0:1, :].astype(F32), 0.0)
    row = lax.broadcasted_iota(jnp.int32, (tm, D), 0)
    z_prev = jnp.where(row == 0, prev_row, pltpu.roll(z, 1, 0))
    z_next = jnp.where(row == tm - 1, next_row, pltpu.roll(z, tm - 1, 0))
    conv = z_prev * cw_ref[0:1, :] + z * cw_ref[1:2, :] + z_next * cw_ref[2:3, :] + cb_ref[...]
    y = _dot((gb_ref[0].astype(F32) * conv).astype(BF16), wo_ref[...])
    _post_norm_and_route(h_ref[0], y, mod_ref, lnw_ref, lnb_ref, rwh_ref, rwl_ref, rb_ref, h1_ref, tok_ref, lg_ref)


def outproj_odd(gz, conv_w, conv_b, w_out, h, mod, lnw, lnb, rwh, rwl, rb):
    B, T, _ = h.shape
    tm = min(512, T)
    r = tm // HALO
    nh = T // HALO
    shapes, ospecs = _epilogue_outs(B, T, tm)
    return pl.pallas_call(
        _outproj_odd_kernel,
        out_shape=shapes,
        grid=(B, T // tm),
        in_specs=[pl.BlockSpec((1, tm, D), lambda b, j: (b, j, 0)),
                  pl.BlockSpec((1, tm, D), lambda b, j: (b, j, 1)),
                  pl.BlockSpec((1, HALO, D), lambda b, j: (b, jnp.maximum(j * r - 1, 0), 1)),
                  pl.BlockSpec((1, HALO, D), lambda b, j: (b, jnp.minimum((j + 1) * r, nh - 1), 1)),
                  pl.BlockSpec((3, D), lambda b, j: (0, 0)),
                  pl.BlockSpec((1, D), lambda b, j: (0, 0)),
                  pl.BlockSpec((D, D), lambda b, j: (0, 0))] + _epilogue_specs(tm, mod.shape[0]),
        out_specs=ospecs,
        compiler_params=_cparams(("parallel", "arbitrary")),
        name="outproj_odd",
    )(gz, gz, gz, gz, conv_w, conv_b, w_out, h, mod, lnw, lnb, rwh, rwl, rb)


def _sc_mesh():
    return plsc.VectorSubcoreMesh(core_axis_name="c", subcore_axis_name="s")


def sc_gather_rows(table, idx):
    n = idx.shape[0]
    width = table.shape[1]
    per_w = n // SC_WORKERS
    n_chunks = per_w // SC_CHUNK
    assert n_chunks % 2 == 0

    @functools.partial(
        pl.kernel, mesh=_sc_mesh(),
        out_type=jax.ShapeDtypeStruct((n, width), table.dtype),
        scratch_types=[pltpu.VMEM((n_chunks, SC_CHUNK), jnp.int32),
                       pltpu.VMEM((SC_CHUNK, width), table.dtype), pltpu.VMEM((SC_CHUNK, width), table.dtype),
                       pltpu.SemaphoreType.DMA, pltpu.SemaphoreType.DMA],
    )
    def gather_kernel(table_hbm, idx_hbm, out_hbm, idx_v, buf0, buf1, sem0, sem1):
        wid = lax.axis_index("s") * SC_CORES + lax.axis_index("c")
        pltpu.sync_copy(idx_hbm.at[wid], idx_v)

        def fetch(j, buf, sem):
            return pltpu.make_async_copy(table_hbm.at[idx_v.at[j]], buf, sem)

        def flush(j, buf):
            pltpu.sync_copy(buf, out_hbm.at[pl.ds(wid * per_w + j * SC_CHUNK, SC_CHUNK)])

        fetch(0, buf0, sem0).start()

        @pl.loop(0, n_chunks, step=2)
        def _(j):
            fetch(j + 1, buf1, sem1).start()
            fetch(j, buf0, sem0).wait()
            flush(j, buf0)

            @pl.when(j + 2 < n_chunks)
            def _():
                fetch(j + 2, buf0, sem0).start()

            fetch(j + 1, buf1, sem1).wait()
            flush(j + 1, buf1)

    return gather_kernel(table, idx.reshape(SC_WORKERS, n_chunks, SC_CHUNK))


def sc_scatter_rows(srcs, idxs, n_rows):
    width, dt = srcs[0].shape[1], srcs[0].dtype
    plans, args = [], []
    for src, idx in zip(srcs, idxs):
        per_w = src.shape[0] // SC_WORKERS
        chunk = min(SC_CHUNK, per_w // 2)
        assert (per_w // chunk) % 2 == 0
        plans.append((per_w, chunk, per_w // chunk, idx.shape[0]))
        args += [src, idx.reshape(idx.shape[0], SC_WORKERS, per_w // chunk, chunk)]
    max_chunk = max(p[1] for p in plans)
    scratch = [pltpu.VMEM((max_chunk, width), dt), pltpu.VMEM((max_chunk, width), dt),
               pltpu.SemaphoreType.DMA, pltpu.SemaphoreType.DMA]
    scratch += [pltpu.VMEM((lists, n_chunks, chunk), jnp.int32) for _, chunk, n_chunks, lists in plans]

    @functools.partial(pl.kernel, mesh=_sc_mesh(), out_type=jax.ShapeDtypeStruct((n_rows, width), dt),
                       scratch_types=scratch)
    def scatter_kernel(*refs):
        ins, out_hbm = refs[:2 * len(plans)], refs[2 * len(plans)]
        rows0, rows1, sem0, sem1 = refs[2 * len(plans) + 1:2 * len(plans) + 5]
        idx_vs = refs[2 * len(plans) + 5:]
        wid = lax.axis_index("s") * SC_CORES + lax.axis_index("c")
        for s, (per_w, chunk, n_chunks, lists) in enumerate(plans):
            src_hbm, idx_hbm, idx_v = ins[2 * s], ins[2 * s + 1], idx_vs[s]
            for k in range(lists):
                pltpu.sync_copy(idx_hbm.at[k, wid], idx_v.at[k])
            buf0 = rows0 if chunk == max_chunk else rows0.at[pl.ds(0, chunk)]
            buf1 = rows1 if chunk == max_chunk else rows1.at[pl.ds(0, chunk)]

            def load(j, buf, sem, src_hbm=src_hbm, per_w=per_w, chunk=chunk):
                return pltpu.make_async_copy(src_hbm.at[pl.ds(wid * per_w + j * chunk, chunk)], buf, sem)

            def spread(j, buf, idx_v=idx_v, lists=lists):
                for k in range(lists):
                    pltpu.sync_copy(buf, out_hbm.at[idx_v.at[k, j]])

            load(0, buf0, sem0).start()

            @pl.loop(0, n_chunks, step=2)
            def _(j, load=load, spread=spread, buf0=buf0, buf1=buf1, n_chunks=n_chunks):
                load(j + 1, buf1, sem1).start()
                load(j, buf0, sem0).wait()
                spread(j, buf0)

                @pl.when(j + 2 < n_chunks)
                def _():
                    load(j + 2, buf0, sem0).start()

                load(j + 1, buf1, sem1).wait()
                spread(j + 1, buf1)

    return scatter_kernel(*args)


def _ffn_kernel(be_ref, nv_ref, x_ref, w1_ref, w3_ref, w2_ref, y_ref, w1b, w3b, w2b):
    i = pl.program_id(0)
    changed = jnp.logical_or(i == 0, be_ref[i] != be_ref[jnp.maximum(i - 1, 0)])

    @pl.when(changed)
    def _():
        w1b[...] = w1_ref[0, 0].astype(BF16)
        w3b[...] = w3_ref[0, 0].astype(BF16)
        w2b[...] = w2_ref[0, 0].astype(BF16)

    @pl.when(i < nv_ref[0])
    def _():
        half = D // 2
        x_lo, x_hi = (v.astype(BF16) for v in _unpack_bf16_pairs(x_ref[...]))
        gate = _dot(x_lo, w1b[0:half, :]) + _dot(x_hi, w1b[half:D, :])
        up = _dot(x_lo, w3b[0:half, :]) + _dot(x_hi, w3b[half:D, :])
        y_ref[...] = _pack_bf16_pairs(_dot((_silu(gate) * up).astype(BF16), w2b[...]))

    @pl.when(i >= nv_ref[0])
    def _():
        y_ref[...] = jnp.zeros_like(y_ref)


def moe_ffn(xs, blk_expert, n_valid, w1, w3, w2, layer):
    n_rows = xs.shape[0]
    nb = n_rows // MOE_TM
    return pl.pallas_call(
        _ffn_kernel,
        out_shape=jax.ShapeDtypeStruct((n_rows, D // 2), jnp.uint32),
        grid_spec=pltpu.PrefetchScalarGridSpec(
            num_scalar_prefetch=2,
            grid=(nb,),
            in_specs=[pl.BlockSpec((MOE_TM, D // 2), lambda i, be, nv: (i, 0)),
                      pl.BlockSpec((1, 1, D, D_EXPERT), lambda i, be, nv: (layer, be[i], 0, 0)),
                      pl.BlockSpec((1, 1, D, D_EXPERT), lambda i, be, nv: (layer, be[i], 0, 0)),
                      pl.BlockSpec((1, 1, D_EXPERT, D), lambda i, be, nv: (layer, be[i], 0, 0))],
            out_specs=pl.BlockSpec((MOE_TM, D // 2), lambda i, be, nv: (i, 0)),
            scratch_shapes=[pltpu.VMEM((D, D_EXPERT), BF16), pltpu.VMEM((D, D_EXPERT), BF16),
                            pltpu.VMEM((D_EXPERT, D), BF16)]),
        compiler_params=_cparams(("arbitrary",)),
        name="moe_ffn",
    )(blk_expert, n_valid, xs, w1, w3, w2)


def _combine_kernel(h_ref, y0_ref, y1_ref, wt_ref, mod_ref, lnw_ref, lnb_ref, o_ref):
    half = D // 2
    lo0, hi0 = _unpack_bf16_pairs(y0_ref[0])
    lo1, hi1 = _unpack_bf16_pairs(y1_ref[0])
    pick = (lax.broadcasted_iota(jnp.int32, (WT_ROWS, LANE), 0) == lax.broadcasted_iota(jnp.int32, (WT_ROWS, LANE), 1))
    wcols = lax.dot_general(wt_ref[...], pick.astype(F32), (((0,), (0,)), ((), ())), precision=HIGHEST,
                            preferred_element_type=F32)
    w0, w1 = wcols[:, 0:1], wcols[:, 1:2]
    r_lo = ALPHA * h_ref[:, 0:half] + mod_ref[0, 5:6, 0:half] * (w0 * lo0 + w1 * lo1)
    r_hi = ALPHA * h_ref[:, half:D] + mod_ref[0, 5:6, half:D] * (w0 * hi0 + w1 * hi1)
    mu = (jnp.sum(r_lo, axis=-1, keepdims=True) + jnp.sum(r_hi, axis=-1, keepdims=True)) * (1.0 / D)
    c_lo, c_hi = r_lo - mu, r_hi - mu
    var = (jnp.sum(c_lo * c_lo, axis=-1, keepdims=True) + jnp.sum(c_hi * c_hi, axis=-1, keepdims=True)) * (1.0 / D)
    inv = lax.rsqrt(var + LN_EPS)
    o_ref[:, 0:half] = c_lo * inv * lnw_ref[:, 0:half] + lnb_ref[:, 0:half]
    o_ref[:, half:D] = c_hi * inv * lnw_ref[:, half:D] + lnb_ref[:, half:D]


def moe_combine(h1, y_rows, wts, tile_off, mod, lnw, lnb):
    B, T, _ = h1.shape
    N = B * T
    tm = min(512, N)
    per_b = T // tm
    mod_map = (lambda i: (i // per_b, 0, 0)) if mod.shape[0] > 1 else (lambda i: (0, 0, 0))
    out = pl.pallas_call(
        _combine_kernel,
        out_shape=jax.ShapeDtypeStruct((N, D), F32),
        grid=(N // tm,),
        in_specs=[pl.BlockSpec((tm, D), lambda i: (i, 0)),
                  pl.BlockSpec((1, tm, D // 2), lambda i: (0, i + tile_off, 0)),
                  pl.BlockSpec((1, tm, D // 2), lambda i: (1, i + tile_off, 0)),
                  pl.BlockSpec((WT_ROWS, tm), lambda i: (0, i + tile_off)),
                  pl.BlockSpec((1, 6, D), mod_map),
                  pl.BlockSpec((1, D), lambda i: (0, 0)), pl.BlockSpec((1, D), lambda i: (0, 0))],
        out_specs=pl.BlockSpec((tm, D), lambda i: (i, 0)),
        compiler_params=_cparams(("parallel",)),
        name="moe_combine",
    )(h1.reshape(N, D), y_rows, y_rows, wts, mod, lnw, lnb)
    return out.reshape(B, T, D)


def _route_kernel(lg_ref, dest_ref, wt_ref, cnt_ref, tri_sc, start_sc, run_sc):
    ph, i = pl.program_id(0), pl.program_id(1)
    tr = lg_ref.shape[1]

    @pl.when(jnp.logical_and(ph == 0, i == 0))
    def _():
        r = lax.broadcasted_iota(jnp.int32, (tr, tr), 0)
        c = lax.broadcasted_iota(jnp.int32, (tr, tr), 1)
        tri_sc[...] = (r < c).astype(BF16)
        start_sc[...] = jnp.zeros_like(start_sc)
        run_sc[...] = jnp.zeros_like(run_sc)

    @pl.when(jnp.logical_and(ph == 1, i == 0))
    def _():
        cnt = run_sc[...].astype(jnp.int32)
        cnt_ref[...] = cnt
        padded = jnp.bitwise_and(cnt + (MOE_TM - 1), -MOE_TM)
        row = lax.broadcasted_iota(jnp.int32, padded.shape, 0)
        acc = padded
        for s in (1, 2, 4, 8, 16):
            acc = acc + jnp.where(row >= s, pltpu.roll(acc, s, 0), 0)
        start_sc[...] = (acc - padded).astype(F32)
        run_sc[...] = jnp.zeros_like(run_sc)

    lg = lg_ref[...]
    gl = lg[N_EXPERTS:N_EXPERTS + N_GROUPS]
    gmax = jnp.max(gl, axis=0, keepdims=True)
    sub4 = lax.broadcasted_iota(jnp.int32, gl.shape, 0)
    g_sel = jnp.min(jnp.where(gl == gmax, sub4, N_GROUPS), axis=0, keepdims=True)
    p_group = 1.0 / jnp.sum(jnp.exp(gl - gmax), axis=0, keepdims=True)
    el = lg[0:EXPERTS_PER_GROUP]
    for g in range(1, N_GROUPS):
        el = jnp.where(g_sel == g, lg[g * EXPERTS_PER_GROUP:(g + 1) * EXPERTS_PER_GROUP], el)
    sub8 = lax.broadcasted_iota(jnp.int32, el.shape, 0)
    e1 = jnp.max(el, axis=0, keepdims=True)
    i1 = jnp.min(jnp.where(el == e1, sub8, EXPERTS_PER_GROUP), axis=0, keepdims=True)
    rest = jnp.where(sub8 == i1, -jnp.inf, el)
    e2 = jnp.max(rest, axis=0, keepdims=True)
    i2 = jnp.min(jnp.where(rest == e2, sub8, EXPERTS_PER_GROUP), axis=0, keepdims=True)
    den = jnp.sum(jnp.exp(el - e1), axis=0, keepdims=True)
    p1 = 1.0 / den
    p2 = jnp.exp(e2 - e1) / den
    wt_ref[...] = jnp.zeros_like(wt_ref)
    wt_ref[0:1, :] = p_group * p1 / (p1 + p2)
    wt_ref[1:2, :] = p_group * p2 / (p1 + p2)

    sub32 = lax.broadcasted_iota(jnp.int32, (N_EXPERTS, tr), 0)
    oh = [(sub32 == g_sel * EXPERTS_PER_GROUP + ix).astype(F32) for ix in (i1, i2)]
    cnt = [jnp.sum(o, axis=1, keepdims=True) for o in oh]
    @pl.when(ph == 0)
    def _():
        dest_ref[...] = jnp.zeros_like(dest_ref)

    @pl.when(ph == 1)
    def _():
        before = start_sc[:, 0:1] + run_sc[:, 0:1]
        for k in range(TOP_K):
            prior = _dot(oh[k].astype(BF16), tri_sc[...]) + before + (cnt[0] if k == 1 else 0.0)
            dest_ref[k:k + 1, :] = jnp.sum(oh[k] * prior, axis=0, keepdims=True).astype(jnp.int32)

    run_sc[...] = run_sc[...] + (cnt[0] + cnt[1])


def moe_route(logits_t):
    N = logits_t.shape[1]
    tr = next(t for t in (1024, 512, 256) if N % t == 0)
    return pl.pallas_call(
        _route_kernel,
        out_shape=(jax.ShapeDtypeStruct((TOP_K, N), jnp.int32), jax.ShapeDtypeStruct((WT_ROWS, N), F32),
                   jax.ShapeDtypeStruct((N_EXPERTS, LANE), jnp.int32)),
        grid=(2, N // tr),
        in_specs=[pl.BlockSpec((ROUTE_ROWS, tr), lambda p, i: (0, i))],
        out_specs=(pl.BlockSpec((TOP_K, tr), lambda p, i: (0, i * p)), pl.BlockSpec((WT_ROWS, tr), lambda p, i: (0, i * p)),
                   pl.BlockSpec((N_EXPERTS, LANE), lambda p, i: (0, 0))),
        scratch_shapes=[pltpu.VMEM((tr, tr), BF16), pltpu.VMEM((N_EXPERTS, LANE), F32),
                        pltpu.VMEM((N_EXPERTS, LANE), F32)],
        compiler_params=_cparams(("arbitrary", "arbitrary")),
        name="moe_route",
    )(logits_t)


def _block_tables(counts, n_assign):
    padded = (counts + MOE_TM - 1) // MOE_TM * MOE_TM
    pad_end = jnp.cumsum(padded)
    pad_start = pad_end - padded
    nb = -(-n_assign // MOE_TM) + N_EXPERTS
    blk_start = jnp.arange(nb, dtype=jnp.int32) * MOE_TM
    blk_expert = jnp.minimum(jnp.sum((pad_end[None, :] <= blk_start[:, None]).astype(jnp.int32), axis=1), N_EXPERTS - 1)
    n_valid = (pad_end[-1] // MOE_TM).astype(jnp.int32).reshape(1)
    n_fill = nb * MOE_TM - n_assign
    gap = padded - counts
    gap_end = jnp.cumsum(gap)
    k = jnp.arange(n_fill, dtype=jnp.int32)[:, None]
    sel = jnp.logical_and(k >= (gap_end - gap)[None, :], k < gap_end[None, :])
    in_gap = jnp.sum(jnp.where(sel, (pad_start + counts - (gap_end - gap))[None, :] + k, 0), axis=1)
    fill = jnp.where(k[:, 0] < gap_end[-1], in_gap, pad_end[-1] + k[:, 0] - gap_end[-1])
    return blk_expert.astype(jnp.int32), n_valid, fill.astype(jnp.int32), nb * MOE_TM


def hier_moe_and_norm(streams, w1, w3, w2, layer, lnw, lnb):
    logits_t = streams[0][2] if len(streams) == 1 else jnp.concatenate([s[2] for s in streams], 1)
    dest, wts, counts = moe_route(logits_t)
    blk_expert, n_valid, fill_rows, n_rows = _block_tables(counts[:, 0], TOP_K * logits_t.shape[1])
    n_all = logits_t.shape[1]
    toks, idxs, offs, off = [], [], [], 0
    for h1, tok, _, _ in streams:
        n = h1.shape[0] * h1.shape[1]
        toks.append(tok.reshape(n, D // 2))
        idxs.append(dest[:, off:off + n])
        offs.append(off // min(512, n))
        off += n
    toks.append(jnp.zeros((fill_rows.shape[0], D // 2), jnp.uint32))
    idxs.append(fill_rows.reshape(1, -1))
    xs = sc_scatter_rows(toks, idxs, n_rows)
    y = moe_ffn(xs, blk_expert, n_valid, w1, w3, w2, layer)
    y_rows = sc_gather_rows(y, dest.reshape(-1)).reshape(TOP_K, n_all, D // 2)
    return [moe_combine(h1, y_rows, wts, o, mod, lnw, lnb) for (h1, _, _, mod), o in zip(streams, offs)]


def _even_weight_columns():
    n = np.arange(HEAD_DIM)
    perm = (n % 32) // 16 * 32 + n // 32 * 16 + n % 16
    idx = np.zeros(P_W, np.int32)
    scale = np.zeros(P_W, np.float32)
    for hd in range(A_Q_HEADS):
        idx[C_QA + hd * 64:C_QA + (hd + 1) * 64] = hd * 64 + perm
    scale[C_QA:C_QA + 512] = HEAD_DIM ** -0.5 * LOG2_E
    for g in range(A_KV_HEADS):
        for rep in range(2):
            o = g * 128 + rep * 64
            idx[C_KD + o:C_KD + o + 64] = 512 + g * 64 + perm
            idx[C_VD + o:C_VD + o + 64] = 640 + g * 64 + n
    scale[C_KD:C_VD + 256] = 1.0
    idx[C_QG:C_QG + 256] = 768 + np.arange(256)
    scale[C_QG:C_QG + 256] = GLA_DK ** -0.5
    idx[C_KG:C_KG + 256] = 1024 + np.arange(256)
    idx[C_VG:C_VG + 512] = 1280 + np.arange(512)
    idx[C_RG:C_RG + 512] = 1792 + np.arange(512)
    scale[C_KG:C_KG + 256] = 1.0
    scale[C_VG:C_RG + 512] = 1.0
    for d in range(2):
        idx[C_GG + d * 128:C_GG + d * 128 + GLA_RANK] = 2304 + d * GLA_RANK + np.arange(GLA_RANK)
        scale[C_GG + d * 128:C_GG + d * 128 + GLA_RANK] = 1.0
    return idx, scale


def _rope_tables(S):
    row = jnp.repeat(jnp.arange(S // GRID_W), GRID_W).astype(F32)
    col = jnp.tile(jnp.arange(GRID_W), S // GRID_W).astype(F32)
    axis_dim = HEAD_DIM // 2
    inv_freq = ROPE_BASE ** (-jnp.arange(0, axis_dim, 2, dtype=F32) / axis_dim)
    ang = jnp.concatenate([row[:, None] * inv_freq, col[:, None] * inv_freq], -1)
    cos, sin = jnp.cos(ang), jnp.sin(ang)
    cos_t = jnp.tile(cos, (1, 4))
    sin_t = jnp.tile(jnp.concatenate([-sin, sin], -1), (1, 2))
    return cos_t, sin_t


def _router_weights(wg, bg, we, be):
    w = jnp.zeros((D, ROUTE_W), F32).at[:, :N_EXPERTS].set(we).at[:, N_EXPERTS:N_EXPERTS + N_GROUPS].set(wg)
    b = jnp.zeros((1, ROUTE_W), F32).at[0, :N_EXPERTS].set(be).at[0, N_EXPERTS:N_EXPERTS + N_GROUPS].set(bg)
    hi = w.astype(BF16)
    return hi, (w - hi.astype(F32)).astype(BF16), b


def kernel(x, c, ctx, c_ctx, w_in_even, w_out_even, attn_sink, gla_wa2, gla_ba, gla_norm_w, w_in_odd, conv_w, conv_b, w_out_odd, ada_w, ada_b, ln_w, ln_b, router_wg, router_bg, router_we, router_be, moe_w1, moe_w3, moe_w2):
    B, S, _ = x.shape
    L = ctx.shape[1]
    cos_t, sin_t = _rope_tables(S)
    cos_c, sin_c = jnp.ones((L, LANE), F32), jnp.zeros((L, LANE), F32)
    col_idx, col_scale = _even_weight_columns()

    n_cond = -(-(B + 1) // 8) * 8
    cc = jnp.zeros((n_cond, D), F32).at[:B].set(c).at[B].set(c_ctx)
    mods = ada_modulation_all(cc, ada_w, ada_b).reshape(DEPTH, n_cond, 6, D)

    h_lat, h_ctx = x, ctx
    for l in range(DEPTH):
        i = l // 2
        need_ctx = any(j % 2 == 0 for j in range(l + 1, DEPTH))
        m_lat = mods[l, :B]
        m_ctx = mods[l, B:B + 1]
        lnw0, lnb0 = ln_w[l, 0:1], ln_b[l, 0:1]
        lnw1, lnb1 = ln_w[l, 1:2], ln_b[l, 1:2]
        rwh, rwl, rb = _router_weights(router_wg[l], router_bg[l], router_we[l], router_be[l])
        streams = []
        if l % 2 == 0:
            w_in = (w_in_even[i][:, col_idx] * col_scale[None, :]).astype(BF16)
            w_out = w_out_even[i].astype(BF16)
            wa_p = jnp.zeros((2, LANE, GLA_HEADS * GLA_DK), F32).at[:, :GLA_RANK].set(gla_wa2[i]).astype(BF16)
            ba = gla_ba[i].reshape(2, 1, -1)
            nw = gla_norm_w[i].reshape(1, -1)
            p_ctx = inproj_even(h_ctx, m_ctx, w_in, cos_c, sin_c)
            p_lat = inproj_even(h_lat, m_lat, w_in, cos_t, sin_t)
            a_lat = attention(p_lat, p_ctx, attn_sink[i], True)
            s0 = jnp.zeros((B, 2, 2, GLA_DV, LANE), F32)
            oc_f, oc_b, s_ctx = gla_scan(p_ctx, wa_p, ba, s0)
            ol_f, ol_b, _ = gla_scan(p_lat, wa_p, ba, s_ctx)
            streams.append(outproj_even(a_lat, ol_f, ol_b, p_lat, nw, w_out, h_lat, m_lat, lnw0, lnb0, rwh, rwl, rb)
                           + (m_lat,))
            if need_ctx:
                a_ctx = attention(p_ctx, p_ctx, attn_sink[i], False)
                streams.append(outproj_even(a_ctx, oc_f, oc_b, p_ctx, nw, w_out, h_ctx, m_ctx, lnw0, lnb0,
                                            rwh, rwl, rb) + (m_ctx,))
        else:
            w_in = w_in_odd[i].astype(BF16)
            w_out = w_out_odd[i].astype(BF16)
            cb = conv_b[i].reshape(1, D)
            pairs = [(h_lat, m_lat)] + ([(h_ctx, m_ctx)] if need_ctx else [])
            for h, m in pairs:
                gz = inproj_odd(h, m, w_in)
                streams.append(outproj_odd(gz, conv_w[i], cb, w_out, h, m, lnw0, lnb0, rwh, rwl, rb) + (m,))
        outs = hier_moe_and_norm(streams, moe_w1, moe_w3, moe_w2, l, lnw1, lnb1)
        h_lat = outs[0]
        if need_ctx:
            h_ctx = outs[1]
    return h_lat
```

```python
import functools

import numpy as np
import jax
import jax.numpy as jnp
from jax import lax
from jax.experimental import pallas as pl
from jax.experimental.pallas import tpu as pltpu
from jax.experimental.pallas import tpu_sc as plsc

F32 = jnp.float32
BF16 = jnp.bfloat16
HIGHEST = lax.Precision.HIGHEST

D = 1024
DEPTH = 4
GRID_W = 64
HEAD_DIM = 64
A_Q_HEADS = 8
A_KV_HEADS = 2
WINDOW = 128
ROPE_BASE = 10000.0
GLA_HEADS = 4
GLA_DK = 64
GLA_DV = 128
GLA_RANK = 16
GLA_TAU = 16.0
GLA_CHUNK = 64
N_GROUPS = 4
EXPERTS_PER_GROUP = 8
N_EXPERTS = 32
TOP_K = 2
D_EXPERT = 512
ALPHA = (2.0 * DEPTH) ** 0.25
LN_EPS = 1e-5
RMS_EPS = 1e-6

LANE = 128
VMEM_LIMIT = 48 * 1024 * 1024

C_QA, C_KD, C_VD, C_VG, C_RG, C_QG, C_KG, C_GG = 0, 512, 768, 1024, 1536, 2048, 2304, 2560
P_W = 2816
ROUTE_W = 128
WT_ROWS = 8
ROUTE_ROWS = 40
MOE_TM = 512
SC_CORES, SC_SUBCORES = 2, 16
SC_WORKERS = SC_CORES * SC_SUBCORES
SC_CHUNK = 64
NEG = -1e30
LOG2_E = 1.4426950408889634


def _cparams(sem):
    return pltpu.CompilerParams(dimension_semantics=sem, vmem_limit_bytes=VMEM_LIMIT)


def _dot(a, b):
    return jnp.dot(a, b, preferred_element_type=F32)


def _dot_nt(a, b):
    return lax.dot_general(a, b, (((1,), (1,)), ((), ())), preferred_element_type=F32)


def _dot_tn(a, b):
    return lax.dot_general(a, b, (((0,), (0,)), ((), ())), preferred_element_type=F32)


def _silu(x):
    return x * (1.0 / (1.0 + jnp.exp(-x)))


def _ada_kernel(c_ref, w_ref, b_ref, o_ref):
    s = _silu(c_ref[...])
    o_ref[0] = jnp.dot(s, w_ref[0], precision=HIGHEST, preferred_element_type=F32) + b_ref[0]


def ada_modulation_all(cc, ada_w, ada_b):
    R = cc.shape[0]
    tn = 1536
    return pl.pallas_call(
        _ada_kernel,
        out_shape=jax.ShapeDtypeStruct((DEPTH, R, 6 * D), F32),
        grid=(DEPTH, 6 * D // tn),
        in_specs=[pl.BlockSpec((R, D), lambda l, n: (0, 0)),
                  pl.BlockSpec((1, D, tn), lambda l, n: (l, 0, n)),
                  pl.BlockSpec((1, 1, tn), lambda l, n: (l, 0, n))],
        out_specs=pl.BlockSpec((1, R, tn), lambda l, n: (l, 0, n)),
        compiler_params=_cparams(("arbitrary", "arbitrary")),
        name="ada_modulation",
    )(cc, ada_w, ada_b.reshape(DEPTH, 1, 6 * D))


_EVEN_CHUNKS = ((0, 512, True), (512, 768, True), (768, 1024, False), (1024, 1536, False),
                (1536, 2048, False), (2048, 2560, False), (2560, 2816, False))


def _inproj_even_kernel(h_ref, mod_ref, w_ref, cos_ref, sin_ref, p_ref):
    tm = h_ref.shape[1]
    u = (h_ref[0] * (1.0 + mod_ref[0, 1:2, :]) + mod_ref[0, 0:1, :]).astype(BF16)
    cos = cos_ref[...]
    sin = sin_ref[...]
    lane = lax.broadcasted_iota(jnp.int32, (tm, LANE), 1)
    first_half = (lane % HEAD_DIM) < (HEAD_DIM // 2)
    for c0, c1, rope in _EVEN_CHUNKS:
        acc = _dot(u, w_ref[:, c0:c1])
        if rope:
            for i in range((c1 - c0) // LANE):
                x = acc[:, i * LANE:(i + 1) * LANE]
                partner = jnp.where(first_half, pltpu.roll(x, LANE - 32, 1), pltpu.roll(x, 32, 1))
                p_ref[0, :, c0 + i * LANE:c0 + (i + 1) * LANE] = (x * cos + partner * sin).astype(BF16)
        else:
            p_ref[0, :, c0:c1] = acc.astype(BF16)


def inproj_even(h, mod, w, cos_t, sin_t):
    B, T, _ = h.shape
    tm = min(512, T)
    mb = mod.shape[0]
    return pl.pallas_call(
        _inproj_even_kernel,
        out_shape=jax.ShapeDtypeStruct((B, T, P_W), BF16),
        grid=(B, T // tm),
        in_specs=[pl.BlockSpec((1, tm, D), lambda b, j: (b, j, 0)),
                  pl.BlockSpec((1, 6, D), (lambda b, j: (b, 0, 0)) if mb > 1 else (lambda b, j: (0, 0, 0))),
                  pl.BlockSpec((D, P_W), lambda b, j: (0, 0)),
                  pl.BlockSpec((tm, LANE), lambda b, j: (j, 0)),
                  pl.BlockSpec((tm, LANE), lambda b, j: (j, 0))],
        out_specs=pl.BlockSpec((1, tm, P_W), lambda b, j: (b, j, 0)),
        compiler_params=_cparams(("parallel", "arbitrary")),
        name="inproj_even",
    )(h, mod, w, cos_t, sin_t)


def _attn_kernel(*refs, tq, has_window):
    if has_window:
        sink_ref, q_ref, kw_ref, vw_ref, kc_ref, vc_ref, o_ref = refs
    else:
        sink_ref, q_ref, kc_ref, vc_ref, o_ref = refs
    group = A_Q_HEADS // A_KV_HEADS
    rows = group * tq
    lo = lax.broadcasted_iota(jnp.int32, (tq, LANE), 1) < HEAD_DIM
    den_lanes = lax.broadcasted_iota(jnp.int32, (rows, LANE), 1) >= HEAD_DIM

    def with_ones(v):
        return jnp.where(lax.broadcasted_iota(jnp.int32, v.shape, 1) < HEAD_DIM, v, jnp.ones_like(v))

    if has_window:
        S = kw_ref.shape[1]
        wk = tq + 2 * WINDOW
        q0 = pl.program_id(1) * tq
        wstart = pl.multiple_of(jnp.clip(q0 - WINDOW, 0, S - wk), LANE)
        qpos = q0 + lax.broadcasted_iota(jnp.int32, (tq, wk), 0)
        kpos = wstart + lax.broadcasted_iota(jnp.int32, (tq, wk), 1)
        band = jnp.tile(jnp.where(jnp.abs(qpos - kpos) <= WINDOW, 0.0, NEG), (group, 1))
    groups = range(A_KV_HEADS)
    cols = [slice(g * LANE, (g + 1) * LANE) for g in groups]
    q4, snk, sc, sw, m, outs = [], [], [], [], [], []
    for g in groups:
        qs = []
        for pr in range(group // 2):
            qblk = q_ref[0, :, (2 * g + pr) * LANE:(2 * g + pr + 1) * LANE]
            zero = jnp.zeros_like(qblk)
            qs += [jnp.where(lo, qblk, zero), jnp.where(lo, zero, qblk)]
        q4.append(jnp.concatenate(qs, axis=0))
        snk.append(jnp.concatenate([jnp.full((tq, 1), sink_ref[group * g + i] * LOG2_E, F32) for i in range(group)],
                                   axis=0))
    for g in groups:
        if has_window:
            keys = jnp.concatenate([kw_ref[0, pl.ds(wstart, wk), cols[g]], kc_ref[0, :, cols[g]]], axis=0)
            s = _dot_nt(q4[g], keys)
            sc.append(jnp.concatenate([s[:, 0:wk] + band, s[:, wk:]], axis=1))
        else:
            sc.append(_dot_nt(q4[g], kc_ref[0, :, cols[g]]))
    for g in groups:
        m.append(jnp.maximum(jnp.max(sc[g], axis=-1, keepdims=True), snk[g]))
    for g in groups:
        if has_window:
            vals = jnp.concatenate([vw_ref[0, pl.ds(wstart, wk), cols[g]], vc_ref[0, :, cols[g]]], axis=0)
        else:
            vals = vc_ref[0, :, cols[g]]
        o = _dot(jnp.exp2((sc[g] - m[g]).astype(BF16)), with_ones(vals))
        outs.append(o + jnp.where(den_lanes, jnp.exp2(snk[g] - m[g]), 0.0))
    for g in groups:
        o = outs[g]
        swapped = pltpu.roll(o, HEAD_DIM, 1)
        for pr in range(group // 2):
            ev = slice(2 * pr * tq, (2 * pr + 1) * tq)
            od = slice((2 * pr + 1) * tq, (2 * pr + 2) * tq)
            res = jnp.where(lo, o[ev] / swapped[ev], swapped[od] / o[od])
            o_ref[0, :, (2 * g + pr) * LANE:(2 * g + pr + 1) * LANE] = res.astype(BF16)


def attention(p_q, p_ctx, sink, has_window):
    B, T, _ = p_q.shape
    L = p_ctx.shape[1]
    tq = 128
    in_specs = [pl.BlockSpec(memory_space=pltpu.SMEM),
                pl.BlockSpec((1, tq, 512), lambda b, j: (b, j, C_QA // 512))]
    args = [sink, p_q]
    if has_window:
        in_specs += [pl.BlockSpec((1, T, 256), lambda b, j: (b, 0, C_KD // 256)),
                     pl.BlockSpec((1, T, 256), lambda b, j: (b, 0, C_VD // 256))]
        args += [p_q, p_q]
    in_specs += [pl.BlockSpec((1, L, 256), lambda b, j: (b, 0, C_KD // 256)),
                 pl.BlockSpec((1, L, 256), lambda b, j: (b, 0, C_VD // 256))]
    args += [p_ctx, p_ctx]
    return pl.pallas_call(
        functools.partial(_attn_kernel, tq=tq, has_window=has_window),
        out_shape=jax.ShapeDtypeStruct((B, T, 512), BF16),
        grid=(B, T // tq),
        in_specs=in_specs,
        out_specs=pl.BlockSpec((1, tq, 512), lambda b, j: (b, j, 0)),
        compiler_params=_cparams(("parallel", "arbitrary")),
        name="window_attention" if has_window else "context_attention",
    )(*args)


def _log_sigmoid(x):
    return jnp.minimum(x, 0.0) - jnp.log(1.0 + jnp.exp(-jnp.abs(x)))


def _gla_kernel(qf_ref, kf_ref, vf_ref, gf_ref, qb_ref, kb_ref, vb_ref, gb_ref, wa_ref, ba_ref, s0_ref,
                of_ref, ob_ref, sfin_ref, s_sc):
    j = pl.program_id(1)
    nblk = pl.num_programs(1)
    tb = qf_ref.shape[1]
    nc = tb // GLA_CHUNK

    @pl.when(j == 0)
    def _():
        s_sc[...] = s0_ref[0]

    C = GLA_CHUNK
    ri = lax.broadcasted_iota(jnp.int32, (2 * C, 2 * C), 0) % C
    ci = lax.broadcasted_iota(jnp.int32, (2 * C, 2 * C), 1) % C
    rb = lax.broadcasted_iota(jnp.int32, (tb, tb), 0)
    cb = lax.broadcasted_iota(jnp.int32, (tb, tb), 1)
    same_chunk = (rb // C) == (cb // C)
    lo = lax.broadcasted_iota(jnp.int32, (C, LANE), 1) < GLA_DK

    def per_head(x):
        zero = jnp.zeros_like(x)
        return jnp.concatenate([jnp.where(lo, x, zero), jnp.where(lo, zero, x)], axis=0)

    io = ((qf_ref, kf_ref, vf_ref, gf_ref, of_ref), (qb_ref, kb_ref, vb_ref, gb_ref, ob_ref))
    causal = ((ri >= ci), (ci >= ri))
    tri = (jnp.logical_and(same_chunk, rb >= cb).astype(BF16), jnp.logical_and(same_chunk, cb >= rb).astype(BF16))
    b_all = []
    for d in range(2):
        g = _dot(io[d][3][0, :, d * LANE:(d + 1) * LANE], wa_ref[d]) + ba_ref[d]
        log_a = _log_sigmoid(g) / GLA_TAU
        la1 = log_a.astype(BF16)
        rem = log_a - la1.astype(F32)
        la2 = rem.astype(BF16)
        la3 = (rem - la2.astype(F32)).astype(BF16)
        b_all.append(_dot(tri[d], la1) + _dot(tri[d], la2) + _dot(tri[d], la3))
    state = [[s_sc[d, pair] for pair in range(2)] for d in range(2)]
    for step in range(nc):
        for d in range(2):
            q_ref, k_ref, v_ref, _, o_ref = io[d]
            c = step if d == 0 else nc - 1 - step
            rows = slice(c * C, (c + 1) * C)
            b = b_all[d][rows]
            b_last = b[C - 1:C] if d == 0 else b[0:1]
            qc = q_ref[0, rows, :].astype(F32)
            kc = k_ref[0, rows, :].astype(F32)
            q_in = (qc * jnp.exp(b)).astype(BF16)
            k_in = (kc * jnp.exp(-b)).astype(BF16)
            k_st = (kc * jnp.exp(b_last - b)).astype(BF16)
            dl = jnp.exp(b_last)
            for pair in range(2):
                cols = slice(pair * LANE, (pair + 1) * LANE)
                st = state[d][pair]
                q2 = per_head(q_in[:, cols])
                res = _dot_nt(q2, jnp.concatenate([st.astype(BF16), per_head(k_in[:, cols])], axis=0))
                attn = jnp.where(causal[d], res[:, GLA_DV:], 0.0).astype(BF16)
                v2 = jnp.concatenate([v_ref[0, rows, (2 * pair + hh) * GLA_DV:(2 * pair + hh + 1) * GLA_DV]
                                      for hh in range(2)], axis=0)
                o2 = _dot(attn, v2) + res[:, 0:GLA_DV]
                for hh in range(2):
                    o_ref[0, rows, (2 * pair + hh) * GLA_DV:(2 * pair + hh + 1) * GLA_DV] = (
                        o2[hh * C:(hh + 1) * C].astype(o_ref.dtype))
                upd = _dot_tn(v2, per_head(k_st[:, cols]))
                state[d][pair] = st * dl[:, cols] + upd
    for d in range(2):
        for pair in range(2):
            s_sc[d, pair] = state[d][pair]

    @pl.when(j == nblk - 1)
    def _():
        sfin_ref[0] = s_sc[...]


def gla_scan(p, wa_p, ba, s0):
    B, T, _ = p.shape
    tb = min(512, T)
    nblk = T // tb
    fwd = lambda b, j: (b, j)
    bwd = lambda b, j: (b, nblk - 1 - j)

    def specs(im):
        return [pl.BlockSpec((1, tb, 256), lambda b, j: im(b, j) + (C_QG // 256,)),
                pl.BlockSpec((1, tb, 256), lambda b, j: im(b, j) + (C_KG // 256,)),
                pl.BlockSpec((1, tb, 512), lambda b, j: im(b, j) + (C_VG // 512,)),
                pl.BlockSpec((1, tb, 256), lambda b, j: im(b, j) + (C_GG // 256,))]

    return pl.pallas_call(
        _gla_kernel,
        out_shape=(jax.ShapeDtypeStruct((B, T, 512), BF16), jax.ShapeDtypeStruct((B, T, 512), BF16),
                   jax.ShapeDtypeStruct(s0.shape, F32)),
        grid=(B, nblk),
        in_specs=specs(fwd) + specs(bwd) + [
            pl.BlockSpec((2, LANE, 256), lambda b, j: (0, 0, 0)),
            pl.BlockSpec((2, 1, 256), lambda b, j: (0, 0, 0)),
            pl.BlockSpec((1, 2, 2, GLA_DV, LANE), lambda b, j: (b, 0, 0, 0, 0))],
        out_specs=(pl.BlockSpec((1, tb, 512), lambda b, j: (b, j, 0)),
                   pl.BlockSpec((1, tb, 512), lambda b, j: (b, nblk - 1 - j, 0)),
                   pl.BlockSpec((1, 2, 2, GLA_DV, LANE), lambda b, j: (b, 0, 0, 0, 0))),
        scratch_shapes=[pltpu.VMEM((2, 2, GLA_DV, LANE), F32)],
        compiler_params=_cparams(("parallel", "arbitrary")),
        name="gla_scan",
    )(p, p, p, p, p, p, p, p, wa_p, ba, s0)


def _deepnorm(h, gate, y, w, b):
    r = h + (gate * (1.0 / ALPHA)) * y
    mu = jnp.mean(r, axis=-1, keepdims=True)
    xc = r - mu
    var = jnp.mean(xc * xc, axis=-1, keepdims=True)
    return xc * lax.rsqrt(var + LN_EPS / (ALPHA * ALPHA)) * w + b


def _pack_rounded_pairs(xr):
    half = xr.shape[1] // 2
    lo = lax.bitcast_convert_type(xr[:, :half], jnp.uint32)
    hi = lax.bitcast_convert_type(xr[:, half:], jnp.uint32)
    return jnp.bitwise_or(hi, lax.shift_right_logical(lo, jnp.uint32(16)))


def _pack_bf16_pairs(x):
    return _pack_rounded_pairs(x.astype(BF16).astype(F32))


def _unpack_bf16_pairs(p):
    lo = lax.bitcast_convert_type(lax.shift_left(p, jnp.uint32(16)), F32)
    hi = lax.bitcast_convert_type(jnp.bitwise_and(p, jnp.uint32(0xFFFF0000)), F32)
    return lo, hi


def _post_norm_and_route(h, y, mod_ref, lnw_ref, lnb_ref, rwh_ref, rwl_ref, rb_ref, h1_ref, tok_ref, lg_ref):
    h1 = _deepnorm(h, mod_ref[0, 2:3, :], y, lnw_ref[...], lnb_ref[...])
    h1_ref[0] = h1
    tok = h1 * (1.0 + mod_ref[0, 4:5, :]) + mod_ref[0, 3:4, :]
    hi = tok.astype(BF16)
    hi_f = hi.astype(F32)
    tok_ref[0] = _pack_rounded_pairs(hi_f)
    lo = (tok - hi_f).astype(BF16)
    lg = _dot(hi, rwh_ref[...]) + _dot(lo, rwh_ref[...]) + _dot(hi, rwl_ref[...]) + rb_ref[...]
    lg_ref[...] = lg.T[0:ROUTE_ROWS, :]


def _epilogue_specs(tm, mb):
    mod_map = (lambda b, j: (b, 0, 0)) if mb > 1 else (lambda b, j: (0, 0, 0))
    const2 = lambda b, j: (0, 0)
    return [pl.BlockSpec((1, tm, D), lambda b, j: (b, j, 0)),
            pl.BlockSpec((1, 6, D), mod_map),
            pl.BlockSpec((1, D), const2), pl.BlockSpec((1, D), const2),
            pl.BlockSpec((D, ROUTE_W), const2), pl.BlockSpec((D, ROUTE_W), const2),
            pl.BlockSpec((1, ROUTE_W), const2)]


def _epilogue_outs(B, T, tm):
    nj = T // tm
    shapes = (jax.ShapeDtypeStruct((B, T, D), F32), jax.ShapeDtypeStruct((B, T, D // 2), jnp.uint32),
              jax.ShapeDtypeStruct((ROUTE_ROWS, B * T), F32))
    specs = (pl.BlockSpec((1, tm, D), lambda b, j: (b, j, 0)), pl.BlockSpec((1, tm, D // 2), lambda b, j: (b, j, 0)),
             pl.BlockSpec((ROUTE_ROWS, tm), lambda b, j: (0, b * nj + j)))
    return shapes, specs


def _outproj_even_kernel(a_ref, of_ref, ob_ref, rg_ref, nw_ref, wo_ref,
                         h_ref, mod_ref, lnw_ref, lnb_ref, rwh_ref, rwl_ref, rb_ref,
                         h1_ref, tok_ref, lg_ref):
    parts = [a_ref[0]]
    for hd in range(GLA_HEADS):
        cols = slice(hd * GLA_DV, (hd + 1) * GLA_DV)
        o = of_ref[0, :, cols].astype(F32) + ob_ref[0, :, cols].astype(F32)
        o = o * lax.rsqrt(jnp.mean(o * o, axis=-1, keepdims=True) + RMS_EPS)
        parts.append((o * nw_ref[:, cols] * _silu(rg_ref[0, :, cols].astype(F32))).astype(BF16))
    y = _dot(jnp.concatenate(parts, axis=1), wo_ref[...])
    _post_norm_and_route(h_ref[0], y, mod_ref, lnw_ref, lnb_ref, rwh_ref, rwl_ref, rb_ref, h1_ref, tok_ref, lg_ref)


def outproj_even(a, o_f, o_b, p, norm_w, w_out, h, mod, lnw, lnb, rwh, rwl, rb):
    B, T, _ = h.shape
    tm = min(512, T)
    tile = lambda b, j: (b, j, 0)
    shapes, ospecs = _epilogue_outs(B, T, tm)
    return pl.pallas_call(
        _outproj_even_kernel,
        out_shape=shapes,
        grid=(B, T // tm),
        in_specs=[pl.BlockSpec((1, tm, 512), tile), pl.BlockSpec((1, tm, 512), tile), pl.BlockSpec((1, tm, 512), tile),
                  pl.BlockSpec((1, tm, 512), lambda b, j: (b, j, C_RG // 512)),
                  pl.BlockSpec((1, 512), lambda b, j: (0, 0)),
                  pl.BlockSpec((D, D), lambda b, j: (0, 0))] + _epilogue_specs(tm, mod.shape[0]),
        out_specs=ospecs,
        compiler_params=_cparams(("parallel", "arbitrary")),
        name="outproj_even",
    )(a, o_f, o_b, p, norm_w, w_out, h, mod, lnw, lnb, rwh, rwl, rb)


def _inproj_odd_kernel(h_ref, mod_ref, w_ref, o_ref):
    u = (h_ref[0] * (1.0 + mod_ref[0, 1:2, :]) + mod_ref[0, 0:1, :]).astype(BF16)
    o_ref[0, :, 0:D] = _dot(u, w_ref[:, 0:D]).astype(BF16)
    o_ref[0, :, D:2 * D] = (_dot(u, w_ref[:, D:2 * D]) * _dot(u, w_ref[:, 2 * D:3 * D])).astype(BF16)


def inproj_odd(h, mod, w):
    B, T, _ = h.shape
    tm = min(512, T)
    mb = mod.shape[0]
    return pl.pallas_call(
        _inproj_odd_kernel,
        out_shape=jax.ShapeDtypeStruct((B, T, 2 * D), BF16),
        grid=(B, T // tm),
        in_specs=[pl.BlockSpec((1, tm, D), lambda b, j: (b, j, 0)),
                  pl.BlockSpec((1, 6, D), (lambda b, j: (b, 0, 0)) if mb > 1 else (lambda b, j: (0, 0, 0))),
                  pl.BlockSpec((D, 3 * D), lambda b, j: (0, 0))],
        out_specs=pl.BlockSpec((1, tm, 2 * D), lambda b, j: (b, j, 0)),
        compiler_params=_cparams(("parallel", "arbitrary")),
        name="inproj_odd",
    )(h, mod, w)


HALO = 16


def _outproj_odd_kernel(gb_ref, z_ref, zp_ref, zn_ref, cw_ref, cb_ref, wo_ref,
                        h_ref, mod_ref, lnw_ref, lnb_ref, rwh_ref, rwl_ref, rb_ref,
                        h1_ref, tok_ref, lg_ref):
    j = pl.program_id(1)
    tm = z_ref.shape[1]
    z = z_ref[0].astype(F32)
    prev_row = jnp.where(j > 0, zp_ref[0, HALO - 1:HALO, :].astype(F32), 0.0)
    next_row = jnp.where(j < pl.num_programs(1) - 1, zn_ref[0, 0:1, :].astype(F32), 0.0)
    row = lax.broadcasted_iota(jnp.int32, (tm, D), 0)
    z_prev = jnp.where(row == 0, prev_row, pltpu.roll(z, 1, 0))
    z_next = jnp.where(row == tm - 1, next_row, pltpu.roll(z, tm - 1, 0))
    conv = z_prev * cw_ref[0:1, :] + z * cw_ref[1:2, :] + z_next * cw_ref[2:3, :] + cb_ref[...]
    y = _dot((gb_ref[0].astype(F32) * conv).astype(BF16), wo_ref[...])
    _post_norm_and_route(h_ref[0], y, mod_ref, lnw_ref, lnb_ref, rwh_ref, rwl_ref, rb_ref, h1_ref, tok_ref, lg_ref)


def outproj_odd(gz, conv_w, conv_b, w_out, h, mod, lnw, lnb, rwh, rwl, rb):
    B, T, _ = h.shape
    tm = min(512, T)
    r = tm // HALO
    nh = T // HALO
    shapes, ospecs = _epilogue_outs(B, T, tm)
    return pl.pallas_call(
        _outproj_odd_kernel,
        out_shape=shapes,
        grid=(B, T // tm),
        in_specs=[pl.BlockSpec((1, tm, D), lambda b, j: (b, j, 0)),
                  pl.BlockSpec((1, tm, D), lambda b, j: (b, j, 1)),
                  pl.BlockSpec((1, HALO, D), lambda b, j: (b, jnp.maximum(j * r - 1, 0), 1)),
                  pl.BlockSpec((1, HALO, D), lambda b, j: (b, jnp.minimum((j + 1) * r, nh - 1), 1)),
                  pl.BlockSpec((3, D), lambda b, j: (0, 0)),
                  pl.BlockSpec((1, D), lambda b, j: (0, 0)),
                  pl.BlockSpec((D, D), lambda b, j: (0, 0))] + _epilogue_specs(tm, mod.shape[0]),
        out_specs=ospecs,
        compiler_params=_cparams(("parallel", "arbitrary")),
        name="outproj_odd",
    )(gz, gz, gz, gz, conv_w, conv_b, w_out, h, mod, lnw, lnb, rwh, rwl, rb)


def _sc_mesh():
    return plsc.VectorSubcoreMesh(core_axis_name="c", subcore_axis_name="s")


def sc_gather_rows(table, idx):
    n = idx.shape[0]
    width = table.shape[1]
    per_w = n // SC_WORKERS
    n_chunks = per_w // SC_CHUNK
    assert n_chunks % 2 == 0

    @functools.partial(
        pl.kernel, mesh=_sc_mesh(),
        out_type=jax.ShapeDtypeStruct((n, width), table.dtype),
        scratch_types=[pltpu.VMEM((n_chunks, SC_CHUNK), jnp.int32),
                       pltpu.VMEM((SC_CHUNK, width), table.dtype), pltpu.VMEM((SC_CHUNK, width), table.dtype),
                       pltpu.SemaphoreType.DMA, pltpu.SemaphoreType.DMA],
    )
    def gather_kernel(table_hbm, idx_hbm, out_hbm, idx_v, buf0, buf1, sem0, sem1):
        wid = lax.axis_index("s") * SC_CORES + lax.axis_index("c")
        pltpu.sync_copy(idx_hbm.at[wid], idx_v)

        def fetch(j, buf, sem):
            return pltpu.make_async_copy(table_hbm.at[idx_v.at[j]], buf, sem)

        def flush(j, buf):
            pltpu.sync_copy(buf, out_hbm.at[pl.ds(wid * per_w + j * SC_CHUNK, SC_CHUNK)])

        fetch(0, buf0, sem0).start()

        @pl.loop(0, n_chunks, step=2)
        def _(j):
            fetch(j + 1, buf1, sem1).start()
            fetch(j, buf0, sem0).wait()
            flush(j, buf0)

            @pl.when(j + 2 < n_chunks)
            def _():
                fetch(j + 2, buf0, sem0).start()

            fetch(j + 1, buf1, sem1).wait()
            flush(j + 1, buf1)

    return gather_kernel(table, idx.reshape(SC_WORKERS, n_chunks, SC_CHUNK))


def sc_scatter_rows(srcs, idxs, n_rows):
    width, dt = srcs[0][0].shape[1], srcs[0][0].dtype
    plans, args = [], []
    for (src, row0, n_src), idx in zip(srcs, idxs):
        per_w = n_src // SC_WORKERS
        chunk = min(SC_CHUNK, per_w // 2)
        assert (per_w // chunk) % 2 == 0
        plans.append((per_w, chunk, per_w // chunk, idx.shape[0], row0))
        args += [src, idx.reshape(idx.shape[0], SC_WORKERS, per_w // chunk, chunk)]
    max_chunk = max(p[1] for p in plans)
    scratch = [pltpu.VMEM((max_chunk, width), dt), pltpu.VMEM((max_chunk, width), dt),
               pltpu.SemaphoreType.DMA, pltpu.SemaphoreType.DMA]
    scratch += [pltpu.VMEM((lists, n_chunks, chunk), jnp.int32) for _, chunk, n_chunks, lists, _ in plans]

    @functools.partial(pl.kernel, mesh=_sc_mesh(), out_type=jax.ShapeDtypeStruct((n_rows, width), dt),
                       scratch_types=scratch)
    def scatter_kernel(*refs):
        ins, out_hbm = refs[:2 * len(plans)], refs[2 * len(plans)]
        rows0, rows1, sem0, sem1 = refs[2 * len(plans) + 1:2 * len(plans) + 5]
        idx_vs = refs[2 * len(plans) + 5:]
        wid = lax.axis_index("s") * SC_CORES + lax.axis_index("c")
        for s, (per_w, chunk, n_chunks, lists, row0) in enumerate(plans):
            src_hbm, idx_hbm, idx_v = ins[2 * s], ins[2 * s + 1], idx_vs[s]
            for k in range(lists):
                pltpu.sync_copy(idx_hbm.at[k, wid], idx_v.at[k])
            buf0 = rows0 if chunk == max_chunk else rows0.at[pl.ds(0, chunk)]
            buf1 = rows1 if chunk == max_chunk else rows1.at[pl.ds(0, chunk)]

            def load(j, buf, sem, src_hbm=src_hbm, per_w=per_w, chunk=chunk, row0=row0):
                return pltpu.make_async_copy(src_hbm.at[pl.ds(row0 + wid * per_w + j * chunk, chunk)], buf, sem)

            def spread(j, buf, idx_v=idx_v, lists=lists):
                for k in range(lists):
                    pltpu.sync_copy(buf, out_hbm.at[idx_v.at[k, j]])

            load(0, buf0, sem0).start()

            @pl.loop(0, n_chunks, step=2)
            def _(j, load=load, spread=spread, buf0=buf0, buf1=buf1, n_chunks=n_chunks):
                load(j + 1, buf1, sem1).start()
                load(j, buf0, sem0).wait()
                spread(j, buf0)

                @pl.when(j + 2 < n_chunks)
                def _():
                    load(j + 2, buf0, sem0).start()

                load(j + 1, buf1, sem1).wait()
                spread(j + 1, buf1)

    return scatter_kernel(*args)


def _ffn_kernel(be_ref, nv_ref, x_ref, w1_ref, w3_ref, w2_ref, y_ref, w1b, w3b, w2b):
    i = pl.program_id(0)
    changed = jnp.logical_or(i == 0, be_ref[i] != be_ref[jnp.maximum(i - 1, 0)])

    @pl.when(changed)
    def _():
        w1b[...] = w1_ref[0, 0].astype(BF16)
        w3b[...] = w3_ref[0, 0].astype(BF16)
        w2b[...] = w2_ref[0, 0].astype(BF16)

    @pl.when(i < nv_ref[0])
    def _():
        x = jnp.concatenate([v.astype(BF16) for v in _unpack_bf16_pairs(x_ref[...])], axis=1)
        y_ref[...] = _pack_bf16_pairs(_dot((_silu(_dot(x, w1b[...])) * _dot(x, w3b[...])).astype(BF16), w2b[...]))

    @pl.when(i >= nv_ref[0])
    def _():
        y_ref[...] = jnp.zeros_like(y_ref)


def moe_ffn(xs, blk_expert, n_valid, w1, w3, w2, layer):
    n_rows = xs.shape[0]
    nb = n_rows // MOE_TM
    return pl.pallas_call(
        _ffn_kernel,
        out_shape=jax.ShapeDtypeStruct((n_rows, D // 2), jnp.uint32),
        grid_spec=pltpu.PrefetchScalarGridSpec(
            num_scalar_prefetch=2,
            grid=(nb,),
            in_specs=[pl.BlockSpec((MOE_TM, D // 2), lambda i, be, nv: (i, 0)),
                      pl.BlockSpec((1, 1, D, D_EXPERT), lambda i, be, nv: (layer, be[i], 0, 0)),
                      pl.BlockSpec((1, 1, D, D_EXPERT), lambda i, be, nv: (layer, be[i], 0, 0)),
                      pl.BlockSpec((1, 1, D_EXPERT, D), lambda i, be, nv: (layer, be[i], 0, 0))],
            out_specs=pl.BlockSpec((MOE_TM, D // 2), lambda i, be, nv: (i, 0)),
            scratch_shapes=[pltpu.VMEM((D, D_EXPERT), BF16), pltpu.VMEM((D, D_EXPERT), BF16),
                            pltpu.VMEM((D_EXPERT, D), BF16)]),
        compiler_params=_cparams(("arbitrary",)),
        name="moe_ffn",
    )(blk_expert, n_valid, xs, w1, w3, w2)


def _combine_kernel(h_ref, y0_ref, y1_ref, wt_ref, mod_ref, lnw_ref, lnb_ref, o_ref):
    half = D // 2
    lo0, hi0 = _unpack_bf16_pairs(y0_ref[0])
    lo1, hi1 = _unpack_bf16_pairs(y1_ref[0])
    pick = (lax.broadcasted_iota(jnp.int32, (WT_ROWS, LANE), 0) == lax.broadcasted_iota(jnp.int32, (WT_ROWS, LANE), 1))
    wcols = lax.dot_general(wt_ref[...], pick.astype(F32), (((0,), (0,)), ((), ())), precision=HIGHEST,
                            preferred_element_type=F32)
    w0, w1 = wcols[:, 0:1], wcols[:, 1:2]
    gate = mod_ref[0, 5:6, :] * (1.0 / ALPHA)
    r_lo = h_ref[:, 0:half] + gate[:, 0:half] * (w0 * lo0 + w1 * lo1)
    r_hi = h_ref[:, half:D] + gate[:, half:D] * (w0 * hi0 + w1 * hi1)
    mu = (jnp.sum(r_lo, axis=-1, keepdims=True) + jnp.sum(r_hi, axis=-1, keepdims=True)) * (1.0 / D)
    c_lo, c_hi = r_lo - mu, r_hi - mu
    var = (jnp.sum(c_lo * c_lo, axis=-1, keepdims=True) + jnp.sum(c_hi * c_hi, axis=-1, keepdims=True)) * (1.0 / D)
    inv = lax.rsqrt(var + LN_EPS / (ALPHA * ALPHA))
    o_ref[:, 0:half] = c_lo * inv * lnw_ref[:, 0:half] + lnb_ref[:, 0:half]
    o_ref[:, half:D] = c_hi * inv * lnw_ref[:, half:D] + lnb_ref[:, half:D]


def _combine_into_kernel(*refs):
    _combine_kernel(*refs[:7], refs[8])


def moe_combine(h1, y_rows, wts, mod, lnw, lnb, tok0, ntok, y_tok0, prev=None):
    B, T, _ = h1.shape
    N = B * T
    tm = min(512, ntok)
    per_b = T // tm
    h_off, y_off = tok0 // tm, y_tok0 // tm
    mod_map = (lambda i: ((i + h_off) // per_b, 0, 0)) if mod.shape[0] > 1 else (lambda i: (0, 0, 0))
    in_specs = [pl.BlockSpec((tm, D), lambda i: (i + h_off, 0)),
                pl.BlockSpec((1, tm, D // 2), lambda i: (0, i + y_off, 0)),
                pl.BlockSpec((1, tm, D // 2), lambda i: (1, i + y_off, 0)),
                pl.BlockSpec((WT_ROWS, tm), lambda i: (0, i + y_off)),
                pl.BlockSpec((1, 6, D), mod_map),
                pl.BlockSpec((1, D), lambda i: (0, 0)), pl.BlockSpec((1, D), lambda i: (0, 0))]
    args = [h1.reshape(N, D), y_rows, y_rows, wts, mod, lnw, lnb]
    if prev is not None:
        in_specs.append(pl.BlockSpec(memory_space=pl.ANY))
        args.append(prev)
    return pl.pallas_call(
        _combine_kernel if prev is None else _combine_into_kernel,
        out_shape=jax.ShapeDtypeStruct((N, D), F32),
        grid=(ntok // tm,),
        in_specs=in_specs,
        out_specs=pl.BlockSpec((tm, D), lambda i: (i + h_off, 0)),
        input_output_aliases={} if prev is None else {7: 0},
        compiler_params=_cparams(("parallel",)),
        name="moe_combine",
    )(*args)


def _route_kernel(lg_ref, dest_ref, wt_ref, cnt_ref, tri_sc, start_sc, run_sc):
    ph, i = pl.program_id(0), pl.program_id(1)
    tr = lg_ref.shape[1]

    @pl.when(jnp.logical_and(ph == 0, i == 0))
    def _():
        r = lax.broadcasted_iota(jnp.int32, (tr, tr), 0)
        c = lax.broadcasted_iota(jnp.int32, (tr, tr), 1)
        tri_sc[...] = (r < c).astype(BF16)
        start_sc[...] = jnp.zeros_like(start_sc)
        run_sc[...] = jnp.zeros_like(run_sc)

    @pl.when(jnp.logical_and(ph == 1, i == 0))
    def _():
        cnt = run_sc[...].astype(jnp.int32)
        cnt_ref[...] = cnt
        padded = jnp.bitwise_and(cnt + (MOE_TM - 1), -MOE_TM)
        row = lax.broadcasted_iota(jnp.int32, padded.shape, 0)
        acc = padded
        for s in (1, 2, 4, 8, 16):
            acc = acc + jnp.where(row >= s, pltpu.roll(acc, s, 0), 0)
        start_sc[...] = (acc - padded).astype(F32)
        run_sc[...] = jnp.zeros_like(run_sc)

    lg = lg_ref[...]
    gl = lg[N_EXPERTS:N_EXPERTS + N_GROUPS]
    gmax = jnp.max(gl, axis=0, keepdims=True)
    sub4 = lax.broadcasted_iota(jnp.int32, gl.shape, 0)
    g_sel = jnp.min(jnp.where(gl == gmax, sub4, N_GROUPS), axis=0, keepdims=True)
    p_group = 1.0 / jnp.sum(jnp.exp(gl - gmax), axis=0, keepdims=True)
    el = lg[0:EXPERTS_PER_GROUP]
    for g in range(1, N_GROUPS):
        el = jnp.where(g_sel == g, lg[g * EXPERTS_PER_GROUP:(g + 1) * EXPERTS_PER_GROUP], el)
    sub8 = lax.broadcasted_iota(jnp.int32, el.shape, 0)
    e1 = jnp.max(el, axis=0, keepdims=True)
    i1 = jnp.min(jnp.where(el == e1, sub8, EXPERTS_PER_GROUP), axis=0, keepdims=True)
    rest = jnp.where(sub8 == i1, -jnp.inf, el)
    e2 = jnp.max(rest, axis=0, keepdims=True)
    i2 = jnp.min(jnp.where(rest == e2, sub8, EXPERTS_PER_GROUP), axis=0, keepdims=True)
    den = jnp.sum(jnp.exp(el - e1), axis=0, keepdims=True)
    p1 = 1.0 / den
    p2 = jnp.exp(e2 - e1) / den
    wt_ref[...] = jnp.zeros_like(wt_ref)
    wt_ref[0:1, :] = p_group * p1 / (p1 + p2)
    wt_ref[1:2, :] = p_group * p2 / (p1 + p2)

    sub32 = lax.broadcasted_iota(jnp.int32, (N_EXPERTS, tr), 0)
    oh = [(sub32 == g_sel * EXPERTS_PER_GROUP + ix).astype(F32) for ix in (i1, i2)]
    cnt = [jnp.sum(o, axis=1, keepdims=True) for o in oh]
    @pl.when(ph == 0)
    def _():
        dest_ref[...] = jnp.zeros_like(dest_ref)

    @pl.when(ph == 1)
    def _():
        before = start_sc[:, 0:1] + run_sc[:, 0:1]
        for k in range(TOP_K):
            prior = _dot(oh[k].astype(BF16), tri_sc[...]) + before + (cnt[0] if k == 1 else 0.0)
            dest_ref[k:k + 1, :] = jnp.sum(oh[k] * prior, axis=0, keepdims=True).astype(jnp.int32)

    run_sc[...] = run_sc[...] + (cnt[0] + cnt[1])


def moe_route(logits_t, col0, N):
    tr = next(t for t in (1024, 512, 256) if N % t == 0 and col0 % t == 0)
    t0 = col0 // tr
    return pl.pallas_call(
        _route_kernel,
        out_shape=(jax.ShapeDtypeStruct((TOP_K, N), jnp.int32), jax.ShapeDtypeStruct((WT_ROWS, N), F32),
                   jax.ShapeDtypeStruct((N_EXPERTS, LANE), jnp.int32)),
        grid=(2, N // tr),
        in_specs=[pl.BlockSpec((ROUTE_ROWS, tr), lambda p, i: (0, i + t0))],
        out_specs=(pl.BlockSpec((TOP_K, tr), lambda p, i: (0, i * p)), pl.BlockSpec((WT_ROWS, tr), lambda p, i: (0, i * p)),
                   pl.BlockSpec((N_EXPERTS, LANE), lambda p, i: (0, 0))),
        scratch_shapes=[pltpu.VMEM((tr, tr), BF16), pltpu.VMEM((N_EXPERTS, LANE), F32),
                        pltpu.VMEM((N_EXPERTS, LANE), F32)],
        compiler_params=_cparams(("arbitrary", "arbitrary")),
        name="moe_route",
    )(logits_t)


def _block_tables(counts, n_assign):
    padded = (counts + MOE_TM - 1) // MOE_TM * MOE_TM
    pad_end = jnp.cumsum(padded)
    pad_start = pad_end - padded
    nb = -(-n_assign // MOE_TM) + N_EXPERTS
    blk_start = jnp.arange(nb, dtype=jnp.int32) * MOE_TM
    blk_expert = jnp.minimum(jnp.sum((pad_end[None, :] <= blk_start[:, None]).astype(jnp.int32), axis=1), N_EXPERTS - 1)
    n_valid = (pad_end[-1] // MOE_TM).astype(jnp.int32).reshape(1)
    n_fill = nb * MOE_TM - n_assign
    gap = padded - counts
    gap_end = jnp.cumsum(gap)
    k = jnp.arange(n_fill, dtype=jnp.int32)[:, None]
    sel = jnp.logical_and(k >= (gap_end - gap)[None, :], k < gap_end[None, :])
    in_gap = jnp.sum(jnp.where(sel, (pad_start + counts - (gap_end - gap))[None, :] + k, 0), axis=1)
    fill = jnp.where(k[:, 0] < gap_end[-1], in_gap, pad_end[-1] + k[:, 0] - gap_end[-1])
    return blk_expert.astype(jnp.int32), n_valid, fill.astype(jnp.int32), nb * MOE_TM


def hier_moe_and_norm(streams, w1, w3, w2, layer, lnw, lnb):
    lat = streams[0]
    ctx = streams[1] if len(streams) > 1 else None
    n_lat = lat[0].shape[0] * lat[0].shape[1]
    n_ctx = ctx[0].shape[0] * ctx[0].shape[1] if ctx else 0
    logits_t = lat[2] if ctx is None else jnp.concatenate([lat[2], ctx[2]], 1)
    tok_lat = lat[1].reshape(n_lat, D // 2)
    half = n_lat // 2
    ranges = ((0, half), (half, n_lat - half + n_ctx))
    routed = [moe_route(logits_t, c0, n) for c0, n in ranges]
    sorted_in, tables = [], []
    for (c0, n), (dest, _, counts) in zip(ranges, routed):
        blk_expert, n_valid, fill_rows, n_rows = _block_tables(counts[:, 0], TOP_K * n)
        n_lat_here = min(c0 + n, n_lat) - c0
        srcs, idxs = [(tok_lat, c0, n_lat_here)], [dest[:, :n_lat_here]]
        if n > n_lat_here:
            srcs.append((ctx[1].reshape(n_ctx, D // 2), 0, n_ctx))
            idxs.append(dest[:, n_lat_here:])
        srcs.append((jnp.zeros((fill_rows.shape[0], D // 2), jnp.uint32), 0, fill_rows.shape[0]))
        idxs.append(fill_rows.reshape(1, -1))
        sorted_in.append(sc_scatter_rows(srcs, idxs, n_rows))
        tables.append((blk_expert, n_valid, n_lat_here))
    out_lat, out_ctx = None, None
    for (c0, n), (dest, wts, _), xs, (blk_expert, n_valid, n_lat_here) in zip(ranges, routed, sorted_in, tables):
        y = moe_ffn(xs, blk_expert, n_valid, w1, w3, w2, layer)
        y_rows = sc_gather_rows(y, dest.reshape(-1)).reshape(TOP_K, n, D // 2)
        out_lat = moe_combine(lat[0], y_rows, wts, lat[3], lnw, lnb, c0, n_lat_here, 0, out_lat)
        if n > n_lat_here:
            out_ctx = moe_combine(ctx[0], y_rows, wts, ctx[3], lnw, lnb, 0, n_ctx, n_lat_here)
    outs = [out_lat.reshape(lat[0].shape)]
    if ctx:
        outs.append(out_ctx.reshape(ctx[0].shape))
    return outs


def _even_weight_columns():
    n = np.arange(HEAD_DIM)
    perm = (n % 32) // 16 * 32 + n // 32 * 16 + n % 16
    idx = np.zeros(P_W, np.int32)
    scale = np.zeros(P_W, np.float32)
    for hd in range(A_Q_HEADS):
        idx[C_QA + hd * 64:C_QA + (hd + 1) * 64] = hd * 64 + perm
    scale[C_QA:C_QA + 512] = HEAD_DIM ** -0.5 * LOG2_E
    for g in range(A_KV_HEADS):
        for rep in range(2):
            o = g * 128 + rep * 64
            idx[C_KD + o:C_KD + o + 64] = 512 + g * 64 + perm
            idx[C_VD + o:C_VD + o + 64] = 640 + g * 64 + n
    scale[C_KD:C_VD + 256] = 1.0
    idx[C_QG:C_QG + 256] = 768 + np.arange(256)
    scale[C_QG:C_QG + 256] = GLA_DK ** -0.5
    idx[C_KG:C_KG + 256] = 1024 + np.arange(256)
    idx[C_VG:C_VG + 512] = 1280 + np.arange(512)
    idx[C_RG:C_RG + 512] = 1792 + np.arange(512)
    scale[C_KG:C_KG + 256] = 1.0
    scale[C_VG:C_RG + 512] = 1.0
    for d in range(2):
        idx[C_GG + d * 128:C_GG + d * 128 + GLA_RANK] = 2304 + d * GLA_RANK + np.arange(GLA_RANK)
        scale[C_GG + d * 128:C_GG + d * 128 + GLA_RANK] = 1.0
    return idx, scale


def _rope_tables(S):
    row = jnp.repeat(jnp.arange(S // GRID_W), GRID_W).astype(F32)
    col = jnp.tile(jnp.arange(GRID_W), S // GRID_W).astype(F32)
    axis_dim = HEAD_DIM // 2
    inv_freq = ROPE_BASE ** (-jnp.arange(0, axis_dim, 2, dtype=F32) / axis_dim)
    ang = jnp.concatenate([row[:, None] * inv_freq, col[:, None] * inv_freq], -1)
    cos, sin = jnp.cos(ang), jnp.sin(ang)
    cos_t = jnp.tile(cos, (1, 4))
    sin_t = jnp.tile(jnp.concatenate([-sin, sin], -1), (1, 2))
    return cos_t, sin_t


def _router_weights(wg, bg, we, be):
    w = jnp.zeros((D, ROUTE_W), F32).at[:, :N_EXPERTS].set(we).at[:, N_EXPERTS:N_EXPERTS + N_GROUPS].set(wg)
    b = jnp.zeros((1, ROUTE_W), F32).at[0, :N_EXPERTS].set(be).at[0, N_EXPERTS:N_EXPERTS + N_GROUPS].set(bg)
    hi = w.astype(BF16)
    return hi, (w - hi.astype(F32)).astype(BF16), b


def kernel(x, c, ctx, c_ctx, w_in_even, w_out_even, attn_sink, gla_wa2, gla_ba, gla_norm_w, w_in_odd, conv_w, conv_b, w_out_odd, ada_w, ada_b, ln_w, ln_b, router_wg, router_bg, router_we, router_be, moe_w1, moe_w3, moe_w2):
    B, S, _ = x.shape
    L = ctx.shape[1]
    cos_t, sin_t = _rope_tables(S)
    cos_c, sin_c = jnp.ones((L, LANE), F32), jnp.zeros((L, LANE), F32)
    col_idx, col_scale = _even_weight_columns()

    n_cond = -(-(B + 1) // 8) * 8
    cc = jnp.zeros((n_cond, D), F32).at[:B].set(c).at[B].set(c_ctx)
    mods = ada_modulation_all(cc, ada_w, ada_b).reshape(DEPTH, n_cond, 6, D)

    h_lat, h_ctx = x, ctx
    for l in range(DEPTH):
        i = l // 2
        need_ctx = any(j % 2 == 0 for j in range(l + 1, DEPTH))
        m_lat = mods[l, :B]
        m_ctx = mods[l, B:B + 1]
        lnw0, lnb0 = ln_w[l, 0:1], ln_b[l, 0:1]
        lnw1, lnb1 = ln_w[l, 1:2], ln_b[l, 1:2]
        rwh, rwl, rb = _router_weights(router_wg[l], router_bg[l], router_we[l], router_be[l])
        streams = []
        if l % 2 == 0:
            w_in = (w_in_even[i][:, col_idx] * col_scale[None, :]).astype(BF16)
            w_out = w_out_even[i].astype(BF16)
            wa_p = jnp.zeros((2, LANE, GLA_HEADS * GLA_DK), F32).at[:, :GLA_RANK].set(gla_wa2[i]).astype(BF16)
            ba = gla_ba[i].reshape(2, 1, -1)
            nw = gla_norm_w[i].reshape(1, -1)
            p_ctx = inproj_even(h_ctx, m_ctx, w_in, cos_c, sin_c)
            p_lat = inproj_even(h_lat, m_lat, w_in, cos_t, sin_t)
            a_lat = attention(p_lat, p_ctx, attn_sink[i], True)
            s0 = jnp.zeros((B, 2, 2, GLA_DV, LANE), F32)
            oc_f, oc_b, s_ctx = gla_scan(p_ctx, wa_p, ba, s0)
            ol_f, ol_b, _ = gla_scan(p_lat, wa_p, ba, s_ctx)
            streams.append(outproj_even(a_lat, ol_f, ol_b, p_lat, nw, w_out, h_lat, m_lat, lnw0, lnb0, rwh, rwl, rb)
                           + (m_lat,))
            if need_ctx:
                a_ctx = attention(p_ctx, p_ctx, attn_sink[i], False)
                streams.append(outproj_even(a_ctx, oc_f, oc_b, p_ctx, nw, w_out, h_ctx, m_ctx, lnw0, lnb0,
                                            rwh, rwl, rb) + (m_ctx,))
        else:
            w_in = w_in_odd[i].astype(BF16)
            w_out = w_out_odd[i].astype(BF16)
            cb = conv_b[i].reshape(1, D)
            pairs = [(h_lat, m_lat)] + ([(h_ctx, m_ctx)] if need_ctx else [])
            for h, m in pairs:
                gz = inproj_odd(h, m, w_in)
                streams.append(outproj_odd(gz, conv_w[i], cb, w_out, h, m, lnw0, lnb0, rwh, rwl, rb) + (m,))
        outs = hier_moe_and_norm(streams, moe_w1, moe_w3, moe_w2, l, lnw1, lnb1)
        h_lat = outs[0]
        if need_ctx:
            h_ctx = outs[1]
    return h_lat
```

```python
import functools

import jax
import jax.numpy as jnp
from jax import lax
from jax.experimental import pallas as pl
from jax.experimental.pallas import tpu as pltpu
from jax.experimental.pallas import tpu_sc as plsc

F32 = jnp.float32
BF16 = jnp.bfloat16
HIGHEST = lax.Precision.HIGHEST

D = 1024
DEPTH = 4
GRID_W = 64
HEAD_DIM = 64
A_Q_HEADS = 8
A_KV_HEADS = 2
WINDOW = 128
ROPE_BASE = 10000.0
GLA_HEADS = 4
GLA_DK = 64
GLA_DV = 128
GLA_RANK = 16
GLA_TAU = 16.0
GLA_CHUNK = 64
N_GROUPS = 4
EXPERTS_PER_GROUP = 8
N_EXPERTS = 32
TOP_K = 2
D_EXPERT = 512
ALPHA = (2.0 * DEPTH) ** 0.25
LN_EPS = 1e-5
RMS_EPS = 1e-6

LANE = 128
VMEM_LIMIT = 48 * 1024 * 1024

C_QA, C_KD, C_VD, C_VG, C_RG, C_QG, C_KG, C_GG = 0, 512, 768, 1024, 1536, 2048, 2304, 2560
P_W = 2816
ROUTE_W = 128
WT_ROWS = 8
ROUTE_ROWS = 40
MOE_TM = 512
SC_CORES, SC_SUBCORES = 2, 16
SC_WORKERS = SC_CORES * SC_SUBCORES
SC_CHUNK = 64
NEG = -1e30
LOG2_E = 1.4426950408889634


def _cparams(sem):
    return pltpu.CompilerParams(dimension_semantics=sem, vmem_limit_bytes=VMEM_LIMIT)


def _dot(a, b):
    return jnp.dot(a, b, preferred_element_type=F32)


def _dot_nt(a, b):
    return lax.dot_general(a, b, (((1,), (1,)), ((), ())), preferred_element_type=F32)


def _dot_tn(a, b):
    return lax.dot_general(a, b, (((0,), (0,)), ((), ())), preferred_element_type=F32)


def _silu(x):
    return x * (1.0 / (1.0 + jnp.exp(-x)))


def _ada_kernel(c_ref, w_ref, b_ref, o_ref):
    s = _silu(c_ref[...])
    o_ref[0] = jnp.dot(s, w_ref[0], precision=HIGHEST, preferred_element_type=F32) + b_ref[0]


def ada_modulation_all(cc, ada_w, ada_b):
    R = cc.shape[0]
    tn = 1536
    return pl.pallas_call(
        _ada_kernel,
        out_shape=jax.ShapeDtypeStruct((DEPTH, R, 6 * D), F32),
        grid=(DEPTH, 6 * D // tn),
        in_specs=[pl.BlockSpec((R, D), lambda l, n: (0, 0)),
                  pl.BlockSpec((1, D, tn), lambda l, n: (l, 0, n)),
                  pl.BlockSpec((1, 1, tn), lambda l, n: (l, 0, n))],
        out_specs=pl.BlockSpec((1, R, tn), lambda l, n: (l, 0, n)),
        compiler_params=_cparams(("arbitrary", "arbitrary")),
        name="ada_modulation",
    )(cc, ada_w, ada_b.reshape(DEPTH, 1, 6 * D))


_EVEN_CHUNKS = ((0, 512, True), (512, 768, True), (768, 1024, False), (1024, 1536, False),
                (1536, 2048, False), (2048, 2560, False), (2560, 2816, False))


def _inproj_even_kernel(h_ref, mod_ref, w_ref, cos_ref, sin_ref, p_ref):
    tm = h_ref.shape[1]
    u = (h_ref[0] * (1.0 + mod_ref[0, 1:2, :]) + mod_ref[0, 0:1, :]).astype(BF16)
    cos = cos_ref[...]
    sin = sin_ref[...]
    lane = lax.broadcasted_iota(jnp.int32, (tm, LANE), 1)
    first_half = (lane % HEAD_DIM) < (HEAD_DIM // 2)
    for c0, c1, rope in _EVEN_CHUNKS:
        acc = _dot(u, w_ref[:, c0:c1])
        if rope:
            for i in range((c1 - c0) // LANE):
                x = acc[:, i * LANE:(i + 1) * LANE]
                partner = jnp.where(first_half, pltpu.roll(x, LANE - 32, 1), pltpu.roll(x, 32, 1))
                p_ref[0, :, c0 + i * LANE:c0 + (i + 1) * LANE] = (x * cos + partner * sin).astype(BF16)
        else:
            p_ref[0, :, c0:c1] = acc.astype(BF16)


def inproj_even(h, mod, w, cos_t, sin_t):
    B, T, _ = h.shape
    tm = min(512, T)
    mb = mod.shape[0]
    return pl.pallas_call(
        _inproj_even_kernel,
        out_shape=jax.ShapeDtypeStruct((B, T, P_W), BF16),
        grid=(B, T // tm),
        in_specs=[pl.BlockSpec((1, tm, D), lambda b, j: (b, j, 0)),
                  pl.BlockSpec((1, 6, D), (lambda b, j: (b, 0, 0)) if mb > 1 else (lambda b, j: (0, 0, 0))),
                  pl.BlockSpec((D, P_W), lambda b, j: (0, 0)),
                  pl.BlockSpec((tm, LANE), lambda b, j: (j, 0)),
                  pl.BlockSpec((tm, LANE), lambda b, j: (j, 0))],
        out_specs=pl.BlockSpec((1, tm, P_W), lambda b, j: (b, j, 0)),
        compiler_params=_cparams(("parallel", "arbitrary")),
        name="inproj_even",
    )(h, mod, w, cos_t, sin_t)


def _attn_kernel(*refs, tq, has_window):
    if has_window:
        sink_ref, q_ref, kw_ref, vw_ref, kc_ref, vc_ref, o_ref = refs
    else:
        sink_ref, q_ref, kc_ref, vc_ref, o_ref = refs
    group = A_Q_HEADS // A_KV_HEADS
    rows = group * tq
    lo = lax.broadcasted_iota(jnp.int32, (tq, LANE), 1) < HEAD_DIM
    den_lanes = lax.broadcasted_iota(jnp.int32, (rows, LANE), 1) >= HEAD_DIM

    def with_ones(v):
        return jnp.where(lax.broadcasted_iota(jnp.int32, v.shape, 1) < HEAD_DIM, v, jnp.ones_like(v))

    if has_window:
        S = kw_ref.shape[1]
        wk = tq + 2 * WINDOW
        q0 = pl.program_id(1) * tq
        wstart = pl.multiple_of(jnp.clip(q0 - WINDOW, 0, S - wk), LANE)
        qpos = q0 + lax.broadcasted_iota(jnp.int32, (tq, wk), 0)
        kpos = wstart + lax.broadcasted_iota(jnp.int32, (tq, wk), 1)
        band = jnp.tile(jnp.where(jnp.abs(qpos - kpos) <= WINDOW, 0.0, NEG), (group, 1))
    groups = range(A_KV_HEADS)
    cols = [slice(g * LANE, (g + 1) * LANE) for g in groups]
    q4, snk, sc, sw, m, outs = [], [], [], [], [], []
    for g in groups:
        qs = []
        for pr in range(group // 2):
            qblk = q_ref[0, :, (2 * g + pr) * LANE:(2 * g + pr + 1) * LANE]
            zero = jnp.zeros_like(qblk)
            qs += [jnp.where(lo, qblk, zero), jnp.where(lo, zero, qblk)]
        q4.append(jnp.concatenate(qs, axis=0))
        snk.append(jnp.concatenate([jnp.full((tq, 1), sink_ref[group * g + i] * LOG2_E, F32) for i in range(group)],
                                   axis=0))
    for g in groups:
        if has_window:
            keys = jnp.concatenate([kw_ref[0, pl.ds(wstart, wk), cols[g]], kc_ref[0, :, cols[g]]], axis=0)
            s = _dot_nt(q4[g], keys)
            sc.append(jnp.concatenate([s[:, 0:wk] + band, s[:, wk:]], axis=1))
        else:
            sc.append(_dot_nt(q4[g], kc_ref[0, :, cols[g]]))
    for g in groups:
        m.append(jnp.maximum(jnp.max(sc[g], axis=-1, keepdims=True), snk[g]))
    for g in groups:
        if has_window:
            vals = jnp.concatenate([vw_ref[0, pl.ds(wstart, wk), cols[g]], vc_ref[0, :, cols[g]]], axis=0)
        else:
            vals = vc_ref[0, :, cols[g]]
        o = _dot(jnp.exp2((sc[g] - m[g]).astype(BF16)), with_ones(vals))
        outs.append(o + jnp.where(den_lanes, jnp.exp2(snk[g] - m[g]), 0.0))
    for g in groups:
        o = outs[g]
        swapped = pltpu.roll(o, HEAD_DIM, 1)
        for pr in range(group // 2):
            ev = slice(2 * pr * tq, (2 * pr + 1) * tq)
            od = slice((2 * pr + 1) * tq, (2 * pr + 2) * tq)
            res = jnp.where(lo, o[ev] / swapped[ev], swapped[od] / o[od])
            o_ref[0, :, (2 * g + pr) * LANE:(2 * g + pr + 1) * LANE] = res.astype(BF16)


def attention(p_q, p_ctx, sink, has_window):
    B, T, _ = p_q.shape
    L = p_ctx.shape[1]
    tq = 128
    in_specs = [pl.BlockSpec(memory_space=pltpu.SMEM),
                pl.BlockSpec((1, tq, 512), lambda b, j: (b, j, C_QA // 512))]
    args = [sink, p_q]
    if has_window:
        in_specs += [pl.BlockSpec((1, T, 256), lambda b, j: (b, 0, C_KD // 256)),
                     pl.BlockSpec((1, T, 256), lambda b, j: (b, 0, C_VD // 256))]
        args += [p_q, p_q]
    in_specs += [pl.BlockSpec((1, L, 256), lambda b, j: (b, 0, C_KD // 256)),
                 pl.BlockSpec((1, L, 256), lambda b, j: (b, 0, C_VD // 256))]
    args += [p_ctx, p_ctx]
    return pl.pallas_call(
        functools.partial(_attn_kernel, tq=tq, has_window=has_window),
        out_shape=jax.ShapeDtypeStruct((B, T, 512), BF16),
        grid=(B, T // tq),
        in_specs=in_specs,
        out_specs=pl.BlockSpec((1, tq, 512), lambda b, j: (b, j, 0)),
        compiler_params=_cparams(("parallel", "arbitrary")),
        name="window_attention" if has_window else "context_attention",
    )(*args)


def _log_sigmoid(x):
    return jnp.minimum(x, 0.0) - jnp.log(1.0 + jnp.exp(-jnp.abs(x)))


def _gla_kernel(qf_ref, kf_ref, vf_ref, gf_ref, qb_ref, kb_ref, vb_ref, gb_ref, wa_ref, ba_ref, s0_ref,
                of_ref, ob_ref, sfin_ref, s_sc):
    j = pl.program_id(1)
    nblk = pl.num_programs(1)
    tb = qf_ref.shape[1]
    nc = tb // GLA_CHUNK

    @pl.when(j == 0)
    def _():
        s_sc[...] = s0_ref[0]

    C = GLA_CHUNK
    hc = GLA_HEADS * C
    ri = lax.broadcasted_iota(jnp.int32, (hc, hc), 0) % C
    ci = lax.broadcasted_iota(jnp.int32, (hc, hc), 1) % C
    rb = lax.broadcasted_iota(jnp.int32, (tb, tb), 0)
    cb = lax.broadcasted_iota(jnp.int32, (tb, tb), 1)
    same_chunk = (rb // C) == (cb // C)
    lane_head = lax.broadcasted_iota(jnp.int32, (C, GLA_HEADS * GLA_DK), 1) // GLA_DK

    def per_head(x):
        zero = jnp.zeros_like(x)
        return jnp.concatenate([jnp.where(lane_head == h, x, zero) for h in range(GLA_HEADS)], axis=0)

    io = ((qf_ref, kf_ref, vf_ref, gf_ref, of_ref), (qb_ref, kb_ref, vb_ref, gb_ref, ob_ref))
    causal = ((ri >= ci), (ci >= ri))
    tri = (jnp.logical_and(same_chunk, rb >= cb).astype(BF16), jnp.logical_and(same_chunk, cb >= rb).astype(BF16))
    b_all = []
    for d in range(2):
        g = _dot(io[d][3][0, :, d * LANE:(d + 1) * LANE], wa_ref[d]) + ba_ref[d]
        log_a = _log_sigmoid(g) / GLA_TAU
        la1 = log_a.astype(BF16)
        rem = log_a - la1.astype(F32)
        la2 = rem.astype(BF16)
        la3 = (rem - la2.astype(F32)).astype(BF16)
        b_all.append(_dot(tri[d], la1) + _dot(tri[d], la2) + _dot(tri[d], la3))
    state = [jnp.concatenate([s_sc[d, 0], s_sc[d, 1]], axis=1) for d in range(2)]
    zero_blk = jnp.zeros((GLA_DV, LANE), BF16)
    for step in range(nc):
        for d in range(2):
            q_ref, k_ref, v_ref, _, o_ref = io[d]
            c = step if d == 0 else nc - 1 - step
            rows = slice(c * C, (c + 1) * C)
            b = b_all[d][rows]
            b_last = b[C - 1:C] if d == 0 else b[0:1]
            qc = q_ref[0, rows, :].astype(F32)
            kc = k_ref[0, rows, :].astype(F32)
            q4 = per_head((qc * jnp.exp(b)).astype(BF16))
            k4 = per_head((kc * jnp.exp(-b)).astype(BF16))
            ks4 = per_head((kc * jnp.exp(b_last - b)).astype(BF16))
            st = state[d]
            stb = st.astype(BF16)
            st_bd = jnp.concatenate([jnp.concatenate([stb[:, 0:LANE], zero_blk], axis=1),
                                     jnp.concatenate([zero_blk, stb[:, LANE:]], axis=1)], axis=0)
            res = _dot_nt(q4, jnp.concatenate([st_bd, k4], axis=0))
            attn = jnp.where(causal[d], res[:, 2 * GLA_DV:], 0.0).astype(BF16)
            v4 = jnp.concatenate([v_ref[0, rows, h * GLA_DV:(h + 1) * GLA_DV] for h in range(GLA_HEADS)], axis=0)
            o4 = _dot(attn, v4)
            for h in range(GLA_HEADS):
                hr = slice(h * C, (h + 1) * C)
                inter = res[hr, (h // 2) * GLA_DV:(h // 2 + 1) * GLA_DV]
                o_ref[0, rows, h * GLA_DV:(h + 1) * GLA_DV] = (o4[hr] + inter).astype(o_ref.dtype)
            state[d] = st * jnp.exp(b_last) + _dot_tn(v4, ks4)
    for d in range(2):
        for pair in range(2):
            s_sc[d, pair] = state[d][:, pair * LANE:(pair + 1) * LANE]

    @pl.when(j == nblk - 1)
    def _():
        sfin_ref[0] = s_sc[...]


def gla_scan(p, wa_p, ba, s0):
    B, T, _ = p.shape
    tb = min(512, T)
    nblk = T // tb
    fwd = lambda b, j: (b, j)
    bwd = lambda b, j: (b, nblk - 1 - j)

    def specs(im):
        return [pl.BlockSpec((1, tb, 256), lambda b, j: im(b, j) + (C_QG // 256,)),
                pl.BlockSpec((1, tb, 256), lambda b, j: im(b, j) + (C_KG // 256,)),
                pl.BlockSpec((1, tb, 512), lambda b, j: im(b, j) + (C_VG // 512,)),
                pl.BlockSpec((1, tb, 256), lambda b, j: im(b, j) + (C_GG // 256,))]

    return pl.pallas_call(
        _gla_kernel,
        out_shape=(jax.ShapeDtypeStruct((B, T, 512), BF16), jax.ShapeDtypeStruct((B, T, 512), BF16),
                   jax.ShapeDtypeStruct(s0.shape, F32)),
        grid=(B, nblk),
        in_specs=specs(fwd) + specs(bwd) + [
            pl.BlockSpec((2, LANE, 256), lambda b, j: (0, 0, 0)),
            pl.BlockSpec((2, 1, 256), lambda b, j: (0, 0, 0)),
            pl.BlockSpec((1, 2, 2, GLA_DV, LANE), lambda b, j: (b, 0, 0, 0, 0))],
        out_specs=(pl.BlockSpec((1, tb, 512), lambda b, j: (b, j, 0)),
                   pl.BlockSpec((1, tb, 512), lambda b, j: (b, nblk - 1 - j, 0)),
                   pl.BlockSpec((1, 2, 2, GLA_DV, LANE), lambda b, j: (b, 0, 0, 0, 0))),
        scratch_shapes=[pltpu.VMEM((2, 2, GLA_DV, LANE), F32)],
        compiler_params=_cparams(("parallel", "arbitrary")),
        name="gla_scan",
    )(p, p, p, p, p, p, p, p, wa_p, ba, s0)


def _deepnorm(h, gate, y, w, b):
    r = h + (gate * (1.0 / ALPHA)) * y
    mu = jnp.mean(r, axis=-1, keepdims=True)
    xc = r - mu
    var = jnp.mean(xc * xc, axis=-1, keepdims=True)
    return xc * lax.rsqrt(var + LN_EPS / (ALPHA * ALPHA)) * w + b


def _pack_rounded_pairs(xr):
    half = xr.shape[1] // 2
    lo = lax.bitcast_convert_type(xr[:, :half], jnp.uint32)
    hi = lax.bitcast_convert_type(xr[:, half:], jnp.uint32)
    return jnp.bitwise_or(hi, lax.shift_right_logical(lo, jnp.uint32(16)))


def _pack_bf16_pairs(x):
    return _pack_rounded_pairs(x.astype(BF16).astype(F32))


def _unpack_bf16_pairs(p):
    lo = lax.bitcast_convert_type(lax.shift_left(p, jnp.uint32(16)), F32)
    hi = lax.bitcast_convert_type(jnp.bitwise_and(p, jnp.uint32(0xFFFF0000)), F32)
    return lo, hi


def _post_norm_and_route(h, y, mod_ref, lnw_ref, lnb_ref, rwh_ref, rwl_ref, rb_ref, h1_ref, tok_ref, lg_ref):
    h1 = _deepnorm(h, mod_ref[0, 2:3, :], y, lnw_ref[...], lnb_ref[...])
    h1_ref[0] = h1
    tok = h1 * (1.0 + mod_ref[0, 4:5, :]) + mod_ref[0, 3:4, :]
    hi = tok.astype(BF16)
    hi_f = hi.astype(F32)
    tok_ref[0] = _pack_rounded_pairs(hi_f)
    lo = (tok - hi_f).astype(BF16)
    lg = _dot(hi, rwh_ref[...]) + _dot(lo, rwh_ref[...]) + _dot(hi, rwl_ref[...]) + rb_ref[...]
    lg_ref[...] = lg.T[0:ROUTE_ROWS, :]


def _epilogue_specs(tm, mb):
    mod_map = (lambda b, j: (b, 0, 0)) if mb > 1 else (lambda b, j: (0, 0, 0))
    const2 = lambda b, j: (0, 0)
    return [pl.BlockSpec((1, tm, D), lambda b, j: (b, j, 0)),
            pl.BlockSpec((1, 6, D), mod_map),
            pl.BlockSpec((1, D), const2), pl.BlockSpec((1, D), const2),
            pl.BlockSpec((D, ROUTE_W), const2), pl.BlockSpec((D, ROUTE_W), const2),
            pl.BlockSpec((1, ROUTE_W), const2)]


def _epilogue_outs(B, T, tm):
    nj = T // tm
    shapes = (jax.ShapeDtypeStruct((B, T, D), F32), jax.ShapeDtypeStruct((B, T, D // 2), jnp.uint32),
              jax.ShapeDtypeStruct((ROUTE_ROWS, B * T), F32))
    specs = (pl.BlockSpec((1, tm, D), lambda b, j: (b, j, 0)), pl.BlockSpec((1, tm, D // 2), lambda b, j: (b, j, 0)),
             pl.BlockSpec((ROUTE_ROWS, tm), lambda b, j: (0, b * nj + j)))
    return shapes, specs


def _outproj_even_kernel(a_ref, of_ref, ob_ref, rg_ref, nw_ref, wo_ref,
                         h_ref, mod_ref, lnw_ref, lnb_ref, rwh_ref, rwl_ref, rb_ref,
                         h1_ref, tok_ref, lg_ref):
    parts = [a_ref[0]]
    for hd in range(GLA_HEADS):
        cols = slice(hd * GLA_DV, (hd + 1) * GLA_DV)
        o = of_ref[0, :, cols].astype(F32) + ob_ref[0, :, cols].astype(F32)
        o = o * lax.rsqrt(jnp.mean(o * o, axis=-1, keepdims=True) + RMS_EPS)
        parts.append((o * nw_ref[:, cols] * _silu(rg_ref[0, :, cols].astype(F32))).astype(BF16))
    y = _dot(jnp.concatenate(parts, axis=1), wo_ref[...])
    _post_norm_and_route(h_ref[0], y, mod_ref, lnw_ref, lnb_ref, rwh_ref, rwl_ref, rb_ref, h1_ref, tok_ref, lg_ref)


def outproj_even(a, o_f, o_b, p, norm_w, w_out, h, mod, lnw, lnb, rwh, rwl, rb):
    B, T, _ = h.shape
    tm = min(512, T)
    tile = lambda b, j: (b, j, 0)
    shapes, ospecs = _epilogue_outs(B, T, tm)
    return pl.pallas_call(
        _outproj_even_kernel,
        out_shape=shapes,
        grid=(B, T // tm),
        in_specs=[pl.BlockSpec((1, tm, 512), tile), pl.BlockSpec((1, tm, 512), tile), pl.BlockSpec((1, tm, 512), tile),
                  pl.BlockSpec((1, tm, 512), lambda b, j: (b, j, C_RG // 512)),
                  pl.BlockSpec((1, 512), lambda b, j: (0, 0)),
                  pl.BlockSpec((D, D), lambda b, j: (0, 0))] + _epilogue_specs(tm, mod.shape[0]),
        out_specs=ospecs,
        compiler_params=_cparams(("parallel", "arbitrary")),
        name="outproj_even",
    )(a, o_f, o_b, p, norm_w, w_out, h, mod, lnw, lnb, rwh, rwl, rb)


def _inproj_odd_kernel(h_ref, mod_ref, w_ref, o_ref):
    u = (h_ref[0] * (1.0 + mod_ref[0, 1:2, :]) + mod_ref[0, 0:1, :]).astype(BF16)
    o_ref[0, :, 0:D] = _dot(u, w_ref[:, 0:D]).astype(BF16)
    o_ref[0, :, D:2 * D] = (_dot(u, w_ref[:, D:2 * D]) * _dot(u, w_ref[:, 2 * D:3 * D])).astype(BF16)


def inproj_odd(h, mod, w):
    B, T, _ = h.shape
    tm = min(512, T)
    mb = mod.shape[0]
    return pl.pallas_call(
        _inproj_odd_kernel,
        out_shape=jax.ShapeDtypeStruct((B, T, 2 * D), BF16),
        grid=(B, T // tm),
        in_specs=[pl.BlockSpec((1, tm, D), lambda b, j: (b, j, 0)),
                  pl.BlockSpec((1, 6, D), (lambda b, j: (b, 0, 0)) if mb > 1 else (lambda b, j: (0, 0, 0))),
                  pl.BlockSpec((D, 3 * D), lambda b, j: (0, 0))],
        out_specs=pl.BlockSpec((1, tm, 2 * D), lambda b, j: (b, j, 0)),
        compiler_params=_cparams(("parallel", "arbitrary")),
        name="inproj_odd",
    )(h, mod, w)


HALO = 16


def _outproj_odd_kernel(gb_ref, z_ref, zp_ref, zn_ref, cw_ref, cb_ref, wo_ref,
                        h_ref, mod_ref, lnw_ref, lnb_ref, rwh_ref, rwl_ref, rb_ref,
                        h1_ref, tok_ref, lg_ref):
    j = pl.program_id(1)
    tm = z_ref.shape[1]
    z = z_ref[0].astype(F32)
    prev_row = jnp.where(j > 0, zp_ref[0, HALO - 1:HALO, :].astype(F32), 0.0)
    next_row = jnp.where(j < pl.num_programs(1) - 1, zn_ref[0, 0:1, :].astype(F32), 0.0)
    row = lax.broadcasted_iota(jnp.int32, (tm, D), 0)
    z_prev = jnp.where(row == 0, prev_row, pltpu.roll(z, 1, 0))
    z_next = jnp.where(row == tm - 1, next_row, pltpu.roll(z, tm - 1, 0))
    conv = z_prev * cw_ref[0:1, :] + z * cw_ref[1:2, :] + z_next * cw_ref[2:3, :] + cb_ref[...]
    y = _dot((gb_ref[0].astype(F32) * conv).astype(BF16), wo_ref[...])
    _post_norm_and_route(h_ref[0], y, mod_ref, lnw_ref, lnb_ref, rwh_ref, rwl_ref, rb_ref, h1_ref, tok_ref, lg_ref)


def outproj_odd(gz, conv_w, conv_b, w_out, h, mod, lnw, lnb, rwh, rwl, rb):
    B, T, _ = h.shape
    tm = min(512, T)
    r = tm // HALO
    nh = T // HALO
    shapes, ospecs = _epilogue_outs(B, T, tm)
    return pl.pallas_call(
        _outproj_odd_kernel,
        out_shape=shapes,
        grid=(B, T // tm),
        in_specs=[pl.BlockSpec((1, tm, D), lambda b, j: (b, j, 0)),
                  pl.BlockSpec((1, tm, D), lambda b, j: (b, j, 1)),
                  pl.BlockSpec((1, HALO, D), lambda b, j: (b, jnp.maximum(j * r - 1, 0), 1)),
                  pl.BlockSpec((1, HALO, D), lambda b, j: (b, jnp.minimum((j + 1) * r, nh - 1), 1)),
                  pl.BlockSpec((3, D), lambda b, j: (0, 0)),
                  pl.BlockSpec((1, D), lambda b, j: (0, 0)),
                  pl.BlockSpec((D, D), lambda b, j: (0, 0))] + _epilogue_specs(tm, mod.shape[0]),
        out_specs=ospecs,
        compiler_params=_cparams(("parallel", "arbitrary")),
        name="outproj_odd",
    )(gz, gz, gz, gz, conv_w, conv_b, w_out, h, mod, lnw, lnb, rwh, rwl, rb)


def _sc_mesh():
    return plsc.VectorSubcoreMesh(core_axis_name="c", subcore_axis_name="s")


def sc_gather_rows(table, idx):
    n = idx.shape[0]
    width = table.shape[1]
    per_w = n // SC_WORKERS
    n_chunks = per_w // SC_CHUNK
    assert n_chunks % 2 == 0

    @functools.partial(
        pl.kernel, mesh=_sc_mesh(),
        out_type=jax.ShapeDtypeStruct((n, width), table.dtype),
        scratch_types=[pltpu.VMEM((n_chunks, SC_CHUNK), jnp.int32),
                       pltpu.VMEM((SC_CHUNK, width), table.dtype), pltpu.VMEM((SC_CHUNK, width), table.dtype),
                       pltpu.SemaphoreType.DMA, pltpu.SemaphoreType.DMA],
    )
    def gather_kernel(table_hbm, idx_hbm, out_hbm, idx_v, buf0, buf1, sem0, sem1):
        wid = lax.axis_index("s") * SC_CORES + lax.axis_index("c")
        pltpu.sync_copy(idx_hbm.at[wid], idx_v)

        def fetch(j, buf, sem):
            return pltpu.make_async_copy(table_hbm.at[idx_v.at[j]], buf, sem)

        def flush(j, buf):
            pltpu.sync_copy(buf, out_hbm.at[pl.ds(wid * per_w + j * SC_CHUNK, SC_CHUNK)])

        fetch(0, buf0, sem0).start()

        @pl.loop(0, n_chunks, step=2)
        def _(j):
            fetch(j + 1, buf1, sem1).start()
            fetch(j, buf0, sem0).wait()
            flush(j, buf0)

            @pl.when(j + 2 < n_chunks)
            def _():
                fetch(j + 2, buf0, sem0).start()

            fetch(j + 1, buf1, sem1).wait()
            flush(j + 1, buf1)

    return gather_kernel(table, idx.reshape(SC_WORKERS, n_chunks, SC_CHUNK))


def sc_scatter_rows(srcs, idxs, n_rows):
    width, dt = srcs[0][0].shape[1], srcs[0][0].dtype
    plans, args = [], []
    for (src, row0, n_src), idx in zip(srcs, idxs):
        per_w = n_src // SC_WORKERS
        chunk = min(SC_CHUNK, per_w // 2)
        assert (per_w // chunk) % 2 == 0
        plans.append((per_w, chunk, per_w // chunk, idx.shape[0], row0))
        args += [src, idx.reshape(idx.shape[0], SC_WORKERS, per_w // chunk, chunk)]
    max_chunk = max(p[1] for p in plans)
    scratch = [pltpu.VMEM((max_chunk, width), dt), pltpu.VMEM((max_chunk, width), dt),
               pltpu.SemaphoreType.DMA, pltpu.SemaphoreType.DMA]
    scratch += [pltpu.VMEM((lists, n_chunks, chunk), jnp.int32) for _, chunk, n_chunks, lists, _ in plans]

    @functools.partial(pl.kernel, mesh=_sc_mesh(), out_type=jax.ShapeDtypeStruct((n_rows, width), dt),
                       scratch_types=scratch)
    def scatter_kernel(*refs):
        ins, out_hbm = refs[:2 * len(plans)], refs[2 * len(plans)]
        rows0, rows1, sem0, sem1 = refs[2 * len(plans) + 1:2 * len(plans) + 5]
        idx_vs = refs[2 * len(plans) + 5:]
        wid = lax.axis_index("s") * SC_CORES + lax.axis_index("c")
        for s, (per_w, chunk, n_chunks, lists, row0) in enumerate(plans):
            src_hbm, idx_hbm, idx_v = ins[2 * s], ins[2 * s + 1], idx_vs[s]
            for k in range(lists):
                pltpu.sync_copy(idx_hbm.at[k, wid], idx_v.at[k])
            buf0 = rows0 if chunk == max_chunk else rows0.at[pl.ds(0, chunk)]
            buf1 = rows1 if chunk == max_chunk else rows1.at[pl.ds(0, chunk)]

            def load(j, buf, sem, src_hbm=src_hbm, per_w=per_w, chunk=chunk, row0=row0):
                return pltpu.make_async_copy(src_hbm.at[pl.ds(row0 + wid * per_w + j * chunk, chunk)], buf, sem)

            def spread(j, buf, idx_v=idx_v, lists=lists):
                for k in range(lists):
                    pltpu.sync_copy(buf, out_hbm.at[idx_v.at[k, j]])

            load(0, buf0, sem0).start()

            @pl.loop(0, n_chunks, step=2)
            def _(j, load=load, spread=spread, buf0=buf0, buf1=buf1, n_chunks=n_chunks):
                load(j + 1, buf1, sem1).start()
                load(j, buf0, sem0).wait()
                spread(j, buf0)

                @pl.when(j + 2 < n_chunks)
                def _():
                    load(j + 2, buf0, sem0).start()

                load(j + 1, buf1, sem1).wait()
                spread(j + 1, buf1)

    return scatter_kernel(*args)


def _ffn_kernel(be_ref, nv_ref, x_ref, w1_ref, w3_ref, w2_ref, y_ref, w1b, w3b, w2b):
    i = pl.program_id(0)
    changed = jnp.logical_or(i == 0, be_ref[i] != be_ref[jnp.maximum(i - 1, 0)])

    @pl.when(changed)
    def _():
        w1b[...] = w1_ref[0, 0].astype(BF16)
        w3b[...] = w3_ref[0, 0].astype(BF16)
        w2b[...] = w2_ref[0, 0].astype(BF16)

    @pl.when(i < nv_ref[0])
    def _():
        x = jnp.concatenate([v.astype(BF16) for v in _unpack_bf16_pairs(x_ref[...])], axis=1)
        y_ref[...] = _pack_bf16_pairs(_dot((_silu(_dot(x, w1b[...])) * _dot(x, w3b[...])).astype(BF16), w2b[...]))

    @pl.when(i >= nv_ref[0])
    def _():
        y_ref[...] = jnp.zeros_like(y_ref)


def moe_ffn(xs, blk_expert, n_valid, w1, w3, w2, layer):
    n_rows = xs.shape[0]
    nb = n_rows // MOE_TM
    return pl.pallas_call(
        _ffn_kernel,
        out_shape=jax.ShapeDtypeStruct((n_rows, D // 2), jnp.uint32),
        grid_spec=pltpu.PrefetchScalarGridSpec(
            num_scalar_prefetch=2,
            grid=(nb,),
            in_specs=[pl.BlockSpec((MOE_TM, D // 2), lambda i, be, nv: (i, 0)),
                      pl.BlockSpec((1, 1, D, D_EXPERT), lambda i, be, nv: (layer, be[i], 0, 0)),
                      pl.BlockSpec((1, 1, D, D_EXPERT), lambda i, be, nv: (layer, be[i], 0, 0)),
                      pl.BlockSpec((1, 1, D_EXPERT, D), lambda i, be, nv: (layer, be[i], 0, 0))],
            out_specs=pl.BlockSpec((MOE_TM, D // 2), lambda i, be, nv: (i, 0)),
            scratch_shapes=[pltpu.VMEM((D, D_EXPERT), BF16), pltpu.VMEM((D, D_EXPERT), BF16),
                            pltpu.VMEM((D_EXPERT, D), BF16)]),
        compiler_params=_cparams(("arbitrary",)),
        name="moe_ffn",
    )(blk_expert, n_valid, xs, w1, w3, w2)


def _combine_kernel(h_ref, y0_ref, y1_ref, wt_ref, mod_ref, lnw_ref, lnb_ref, o_ref):
    half = D // 2
    lo0, hi0 = _unpack_bf16_pairs(y0_ref[0])
    lo1, hi1 = _unpack_bf16_pairs(y1_ref[0])
    pick = (lax.broadcasted_iota(jnp.int32, (WT_ROWS, LANE), 0) == lax.broadcasted_iota(jnp.int32, (WT_ROWS, LANE), 1))
    wcols = lax.dot_general(wt_ref[...], pick.astype(F32), (((0,), (0,)), ((), ())), precision=HIGHEST,
                            preferred_element_type=F32)
    w0, w1 = wcols[:, 0:1], wcols[:, 1:2]
    gate = mod_ref[0, 5:6, :] * (1.0 / ALPHA)
    r_lo = h_ref[:, 0:half] + gate[:, 0:half] * (w0 * lo0 + w1 * lo1)
    r_hi = h_ref[:, half:D] + gate[:, half:D] * (w0 * hi0 + w1 * hi1)
    mu = (jnp.sum(r_lo, axis=-1, keepdims=True) + jnp.sum(r_hi, axis=-1, keepdims=True)) * (1.0 / D)
    c_lo, c_hi = r_lo - mu, r_hi - mu
    var = (jnp.sum(c_lo * c_lo, axis=-1, keepdims=True) + jnp.sum(c_hi * c_hi, axis=-1, keepdims=True)) * (1.0 / D)
    inv = lax.rsqrt(var + LN_EPS / (ALPHA * ALPHA))
    o_ref[:, 0:half] = c_lo * inv * lnw_ref[:, 0:half] + lnb_ref[:, 0:half]
    o_ref[:, half:D] = c_hi * inv * lnw_ref[:, half:D] + lnb_ref[:, half:D]


def _combine_into_kernel(*refs):
    _combine_kernel(*refs[:7], refs[8])


def moe_combine(h1, y_rows, wts, mod, lnw, lnb, tok0, ntok, y_tok0, prev=None):
    B, T, _ = h1.shape
    N = B * T
    tm = min(512, ntok)
    per_b = T // tm
    h_off, y_off = tok0 // tm, y_tok0 // tm
    mod_map = (lambda i: ((i + h_off) // per_b, 0, 0)) if mod.shape[0] > 1 else (lambda i: (0, 0, 0))
    in_specs = [pl.BlockSpec((tm, D), lambda i: (i + h_off, 0)),
                pl.BlockSpec((1, tm, D // 2), lambda i: (0, i + y_off, 0)),
                pl.BlockSpec((1, tm, D // 2), lambda i: (1, i + y_off, 0)),
                pl.BlockSpec((WT_ROWS, tm), lambda i: (0, i + y_off)),
                pl.BlockSpec((1, 6, D), mod_map),
                pl.BlockSpec((1, D), lambda i: (0, 0)), pl.BlockSpec((1, D), lambda i: (0, 0))]
    args = [h1.reshape(N, D), y_rows, y_rows, wts, mod, lnw, lnb]
    if prev is not None:
        in_specs.append(pl.BlockSpec(memory_space=pl.ANY))
        args.append(prev)
    return pl.pallas_call(
        _combine_kernel if prev is None else _combine_into_kernel,
        out_shape=jax.ShapeDtypeStruct((N, D), F32),
        grid=(ntok // tm,),
        in_specs=in_specs,
        out_specs=pl.BlockSpec((tm, D), lambda i: (i + h_off, 0)),
        input_output_aliases={} if prev is None else {7: 0},
        compiler_params=_cparams(("parallel",)),
        name="moe_combine",
    )(*args)


def _route_kernel(lg_ref, dest_ref, wt_ref, cnt_ref, tri_sc, start_sc, run_sc):
    ph, i = pl.program_id(0), pl.program_id(1)
    tr = lg_ref.shape[1]

    @pl.when(jnp.logical_and(ph == 0, i == 0))
    def _():
        r = lax.broadcasted_iota(jnp.int32, (tr, tr), 0)
        c = lax.broadcasted_iota(jnp.int32, (tr, tr), 1)
        tri_sc[...] = (r < c).astype(BF16)
        start_sc[...] = jnp.zeros_like(start_sc)
        run_sc[...] = jnp.zeros_like(run_sc)

    @pl.when(jnp.logical_and(ph == 1, i == 0))
    def _():
        cnt = run_sc[...].astype(jnp.int32)
        cnt_ref[...] = cnt
        padded = jnp.bitwise_and(cnt + (MOE_TM - 1), -MOE_TM)
        row = lax.broadcasted_iota(jnp.int32, padded.shape, 0)
        acc = padded
        for s in (1, 2, 4, 8, 16):
            acc = acc + jnp.where(row >= s, pltpu.roll(acc, s, 0), 0)
        start_sc[...] = (acc - padded).astype(F32)
        run_sc[...] = jnp.zeros_like(run_sc)

    lg = lg_ref[...]
    gl = lg[N_EXPERTS:N_EXPERTS + N_GROUPS]
    gmax = jnp.max(gl, axis=0, keepdims=True)
    sub4 = lax.broadcasted_iota(jnp.int32, gl.shape, 0)
    g_sel = jnp.min(jnp.where(gl == gmax, sub4, N_GROUPS), axis=0, keepdims=True)
    p_group = 1.0 / jnp.sum(jnp.exp(gl - gmax), axis=0, keepdims=True)
    el = lg[0:EXPERTS_PER_GROUP]
    for g in range(1, N_GROUPS):
        el = jnp.where(g_sel == g, lg[g * EXPERTS_PER_GROUP:(g + 1) * EXPERTS_PER_GROUP], el)
    sub8 = lax.broadcasted_iota(jnp.int32, el.shape, 0)
    e1 = jnp.max(el, axis=0, keepdims=True)
    i1 = jnp.min(jnp.where(el == e1, sub8, EXPERTS_PER_GROUP), axis=0, keepdims=True)
    rest = jnp.where(sub8 == i1, -jnp.inf, el)
    e2 = jnp.max(rest, axis=0, keepdims=True)
    i2 = jnp.min(jnp.where(rest == e2, sub8, EXPERTS_PER_GROUP), axis=0, keepdims=True)
    den = jnp.sum(jnp.exp(el - e1), axis=0, keepdims=True)
    p1 = 1.0 / den
    p2 = jnp.exp(e2 - e1) / den
    wt_ref[...] = jnp.zeros_like(wt_ref)
    wt_ref[0:1, :] = p_group * p1 / (p1 + p2)
    wt_ref[1:2, :] = p_group * p2 / (p1 + p2)

    sub32 = lax.broadcasted_iota(jnp.int32, (N_EXPERTS, tr), 0)
    oh = [(sub32 == g_sel * EXPERTS_PER_GROUP + ix).astype(F32) for ix in (i1, i2)]
    cnt = [jnp.sum(o, axis=1, keepdims=True) for o in oh]
    @pl.when(ph == 0)
    def _():
        dest_ref[...] = jnp.zeros_like(dest_ref)

    @pl.when(ph == 1)
    def _():
        before = start_sc[:, 0:1] + run_sc[:, 0:1]
        for k in range(TOP_K):
            prior = _dot(oh[k].astype(BF16), tri_sc[...]) + before + (cnt[0] if k == 1 else 0.0)
            dest_ref[k:k + 1, :] = jnp.sum(oh[k] * prior, axis=0, keepdims=True).astype(jnp.int32)

    run_sc[...] = run_sc[...] + (cnt[0] + cnt[1])


def moe_route(logits_t, col0, N):
    tr = next(t for t in (1024, 512, 256) if N % t == 0 and col0 % t == 0)
    t0 = col0 // tr
    return pl.pallas_call(
        _route_kernel,
        out_shape=(jax.ShapeDtypeStruct((TOP_K, N), jnp.int32), jax.ShapeDtypeStruct((WT_ROWS, N), F32),
                   jax.ShapeDtypeStruct((N_EXPERTS, LANE), jnp.int32)),
        grid=(2, N // tr),
        in_specs=[pl.BlockSpec((ROUTE_ROWS, tr), lambda p, i: (0, i + t0))],
        out_specs=(pl.BlockSpec((TOP_K, tr), lambda p, i: (0, i * p)), pl.BlockSpec((WT_ROWS, tr), lambda p, i: (0, i * p)),
                   pl.BlockSpec((N_EXPERTS, LANE), lambda p, i: (0, 0))),
        scratch_shapes=[pltpu.VMEM((tr, tr), BF16), pltpu.VMEM((N_EXPERTS, LANE), F32),
                        pltpu.VMEM((N_EXPERTS, LANE), F32)],
        compiler_params=_cparams(("arbitrary", "arbitrary")),
        name="moe_route",
    )(logits_t)


def _block_tables(counts, n_assign):
    padded = (counts + MOE_TM - 1) // MOE_TM * MOE_TM
    pad_end = jnp.cumsum(padded)
    pad_start = pad_end - padded
    nb = -(-n_assign // MOE_TM) + N_EXPERTS
    blk_start = jnp.arange(nb, dtype=jnp.int32) * MOE_TM
    blk_expert = jnp.minimum(jnp.sum((pad_end[None, :] <= blk_start[:, None]).astype(jnp.int32), axis=1), N_EXPERTS - 1)
    n_valid = (pad_end[-1] // MOE_TM).astype(jnp.int32).reshape(1)
    n_fill = nb * MOE_TM - n_assign
    gap = padded - counts
    gap_end = jnp.cumsum(gap)
    k = jnp.arange(n_fill, dtype=jnp.int32)[:, None]
    sel = jnp.logical_and(k >= (gap_end - gap)[None, :], k < gap_end[None, :])
    in_gap = jnp.sum(jnp.where(sel, (pad_start + counts - (gap_end - gap))[None, :] + k, 0), axis=1)
    fill = jnp.where(k[:, 0] < gap_end[-1], in_gap, pad_end[-1] + k[:, 0] - gap_end[-1])
    return blk_expert.astype(jnp.int32), n_valid, fill.astype(jnp.int32), nb * MOE_TM


def hier_moe_and_norm(streams, w1, w3, w2, layer, lnw, lnb):
    lat = streams[0]
    ctx = streams[1] if len(streams) > 1 else None
    n_lat = lat[0].shape[0] * lat[0].shape[1]
    n_ctx = ctx[0].shape[0] * ctx[0].shape[1] if ctx else 0
    logits_t = lat[2] if ctx is None else jnp.concatenate([lat[2], ctx[2]], 1)
    tok_lat = lat[1].reshape(n_lat, D // 2)
    half = n_lat // 2
    ranges = ((0, half), (half, n_lat - half + n_ctx))
    routed = [moe_route(logits_t, c0, n) for c0, n in ranges]
    sorted_in, tables = [], []
    for (c0, n), (dest, _, counts) in zip(ranges, routed):
        blk_expert, n_valid, fill_rows, n_rows = _block_tables(counts[:, 0], TOP_K * n)
        n_lat_here = min(c0 + n, n_lat) - c0
        srcs, idxs = [(tok_lat, c0, n_lat_here)], [dest[:, :n_lat_here]]
        if n > n_lat_here:
            srcs.append((ctx[1].reshape(n_ctx, D // 2), 0, n_ctx))
            idxs.append(dest[:, n_lat_here:])
        srcs.append((jnp.zeros((fill_rows.shape[0], D // 2), jnp.uint32), 0, fill_rows.shape[0]))
        idxs.append(fill_rows.reshape(1, -1))
        sorted_in.append(sc_scatter_rows(srcs, idxs, n_rows))
        tables.append((blk_expert, n_valid, n_lat_here))
    out_lat, out_ctx = None, None
    for (c0, n), (dest, wts, _), xs, (blk_expert, n_valid, n_lat_here) in zip(ranges, routed, sorted_in, tables):
        y = moe_ffn(xs, blk_expert, n_valid, w1, w3, w2, layer)
        y_rows = sc_gather_rows(y, dest.reshape(-1)).reshape(TOP_K, n, D // 2)
        out_lat = moe_combine(lat[0], y_rows, wts, lat[3], lnw, lnb, c0, n_lat_here, 0, out_lat)
        if n > n_lat_here:
            out_ctx = moe_combine(ctx[0], y_rows, wts, ctx[3], lnw, lnb, 0, n_ctx, n_lat_here)
    outs = [out_lat.reshape(lat[0].shape)]
    if ctx:
        outs.append(out_ctx.reshape(ctx[0].shape))
    return outs


def _even_projection_weights(w):
    def rope_layout(x, heads):
        return x.reshape(D, heads, 2, 2, HEAD_DIM // 4).transpose(0, 1, 3, 2, 4).reshape(D, heads * HEAD_DIM)

    def twice(x):
        return jnp.concatenate([x[:, :HEAD_DIM], x[:, :HEAD_DIM], x[:, HEAD_DIM:], x[:, HEAD_DIM:]], axis=1)

    def lane_pad(x):
        return jnp.pad(x, ((0, 0), (0, LANE - x.shape[1])))

    qa, ka, va, qg, kg, vg, rg, gg = jnp.split(w, [512, 640, 768, 1024, 1280, 1792, 2304], axis=1)
    cols = [rope_layout(qa, A_Q_HEADS) * (HEAD_DIM ** -0.5 * LOG2_E),
            twice(rope_layout(ka, A_KV_HEADS)), twice(va), vg, rg, qg * GLA_DK ** -0.5, kg,
            lane_pad(gg[:, :GLA_RANK]), lane_pad(gg[:, GLA_RANK:])]
    return jnp.concatenate(cols, axis=1).astype(BF16)


def _rope_tables(S):
    row = jnp.repeat(jnp.arange(S // GRID_W), GRID_W).astype(F32)
    col = jnp.tile(jnp.arange(GRID_W), S // GRID_W).astype(F32)
    axis_dim = HEAD_DIM // 2
    inv_freq = ROPE_BASE ** (-jnp.arange(0, axis_dim, 2, dtype=F32) / axis_dim)
    ang = jnp.concatenate([row[:, None] * inv_freq, col[:, None] * inv_freq], -1)
    cos, sin = jnp.cos(ang), jnp.sin(ang)
    cos_t = jnp.tile(cos, (1, 4))
    sin_t = jnp.tile(jnp.concatenate([-sin, sin], -1), (1, 2))
    return cos_t, sin_t


def _router_weights(wg, bg, we, be):
    w = jnp.zeros((D, ROUTE_W), F32).at[:, :N_EXPERTS].set(we).at[:, N_EXPERTS:N_EXPERTS + N_GROUPS].set(wg)
    b = jnp.zeros((1, ROUTE_W), F32).at[0, :N_EXPERTS].set(be).at[0, N_EXPERTS:N_EXPERTS + N_GROUPS].set(bg)
    hi = w.astype(BF16)
    return hi, (w - hi.astype(F32)).astype(BF16), b


def kernel(x, c, ctx, c_ctx, w_in_even, w_out_even, attn_sink, gla_wa2, gla_ba, gla_norm_w, w_in_odd, conv_w, conv_b, w_out_odd, ada_w, ada_b, ln_w, ln_b, router_wg, router_bg, router_we, router_be, moe_w1, moe_w3, moe_w2):
    B, S, _ = x.shape
    L = ctx.shape[1]
    cos_t, sin_t = _rope_tables(S)
    cos_c, sin_c = jnp.ones((L, LANE), F32), jnp.zeros((L, LANE), F32)

    n_cond = -(-(B + 1) // 8) * 8
    cc = jnp.zeros((n_cond, D), F32).at[:B].set(c).at[B].set(c_ctx)
    mods = ada_modulation_all(cc, ada_w, ada_b).reshape(DEPTH, n_cond, 6, D)

    h_lat, h_ctx = x, ctx
    for l in range(DEPTH):
        i = l // 2
        need_ctx = any(j % 2 == 0 for j in range(l + 1, DEPTH))
        m_lat = mods[l, :B]
        m_ctx = mods[l, B:B + 1]
        lnw0, lnb0 = ln_w[l, 0:1], ln_b[l, 0:1]
        lnw1, lnb1 = ln_w[l, 1:2], ln_b[l, 1:2]
        rwh, rwl, rb = _router_weights(router_wg[l], router_bg[l], router_we[l], router_be[l])
        streams = []
        if l % 2 == 0:
            w_in = _even_projection_weights(w_in_even[i])
            w_out = w_out_even[i].astype(BF16)
            wa_p = jnp.zeros((2, LANE, GLA_HEADS * GLA_DK), F32).at[:, :GLA_RANK].set(gla_wa2[i]).astype(BF16)
            ba = gla_ba[i].reshape(2, 1, -1)
            nw = gla_norm_w[i].reshape(1, -1)
            p_ctx = inproj_even(h_ctx, m_ctx, w_in, cos_c, sin_c)
            p_lat = inproj_even(h_lat, m_lat, w_in, cos_t, sin_t)
            a_lat = attention(p_lat, p_ctx, attn_sink[i], True)
            s0 = jnp.zeros((B, 2, 2, GLA_DV, LANE), F32)
            oc_f, oc_b, s_ctx = gla_scan(p_ctx, wa_p, ba, s0)
            ol_f, ol_b, _ = gla_scan(p_lat, wa_p, ba, s_ctx)
            streams.append(outproj_even(a_lat, ol_f, ol_b, p_lat, nw, w_out, h_lat, m_lat, lnw0, lnb0, rwh, rwl, rb)
                           + (m_lat,))
            if need_ctx:
                a_ctx = attention(p_ctx, p_ctx, attn_sink[i], False)
                streams.append(outproj_even(a_ctx, oc_f, oc_b, p_ctx, nw, w_out, h_ctx, m_ctx, lnw0, lnb0,
                                            rwh, rwl, rb) + (m_ctx,))
        else:
            w_in = w_in_odd[i].astype(BF16)
            w_out = w_out_odd[i].astype(BF16)
            cb = conv_b[i].reshape(1, D)
            pairs = [(h_lat, m_lat)] + ([(h_ctx, m_ctx)] if need_ctx else [])
            for h, m in pairs:
                gz = inproj_odd(h, m, w_in)
                streams.append(outproj_odd(gz, conv_w[i], cb, w_out, h, m, lnw0, lnb0, rwh, rwl, rb) + (m,))
        outs = hier_moe_and_norm(streams, moe_w1, moe_w3, moe_w2, l, lnw1, lnb1)
        h_lat = outs[0]
        if need_ctx:
            h_ctx = outs[1]
    return h_lat
```

```python
import functools

import jax
import jax.numpy as jnp
from jax import lax
from jax.experimental import pallas as pl
from jax.experimental.pallas import tpu as pltpu
from jax.experimental.pallas import tpu_sc as plsc

F32 = jnp.float32
BF16 = jnp.bfloat16
HIGHEST = lax.Precision.HIGHEST

D = 1024
DEPTH = 4
GRID_W = 64
HEAD_DIM = 64
A_Q_HEADS = 8
A_KV_HEADS = 2
WINDOW = 128
ROPE_BASE = 10000.0
GLA_HEADS = 4
GLA_DK = 64
GLA_DV = 128
GLA_RANK = 16
GLA_TAU = 16.0
GLA_CHUNK = 64
N_GROUPS = 4
EXPERTS_PER_GROUP = 8
N_EXPERTS = 32
TOP_K = 2
D_EXPERT = 512
ALPHA = (2.0 * DEPTH) ** 0.25
LN_EPS = 1e-5
RMS_EPS = 1e-6

LANE = 128
VMEM_LIMIT = 48 * 1024 * 1024

ROW_TILE = 512
W_AQ = A_Q_HEADS * HEAD_DIM
W_KV2 = 2 * A_KV_HEADS * HEAD_DIM
W_GK = GLA_HEADS * GLA_DK
W_GV = GLA_HEADS * GLA_DV
C_QA = 0
C_KD = C_QA + W_AQ
C_VD = C_KD + W_KV2
C_VG = C_VD + W_KV2
C_RG = C_VG + W_GV
C_QG = C_RG + W_GV
C_KG = C_QG + W_GK
C_GG = C_KG + W_GK
P_W = C_GG + 2 * LANE
ROUTE_W = 128
WT_ROWS = 8
ROUTE_ROWS = 40
MOE_TM = 512
SC_CORES, SC_SUBCORES = 2, 16
SC_WORKERS = SC_CORES * SC_SUBCORES
SC_CHUNK = 64
NEG = -1e30
LOG2_E = 1.4426950408889634


def _cparams(sem):
    return pltpu.CompilerParams(dimension_semantics=sem, vmem_limit_bytes=VMEM_LIMIT)


def _dot(a, b):
    return jnp.dot(a, b, preferred_element_type=F32)


def _dot_nt(a, b):
    return lax.dot_general(a, b, (((1,), (1,)), ((), ())), preferred_element_type=F32)


def _dot_tn(a, b):
    return lax.dot_general(a, b, (((0,), (0,)), ((), ())), preferred_element_type=F32)


def _silu(x):
    return x * (1.0 / (1.0 + jnp.exp(-x)))


def _ada_kernel(c_ref, w_ref, b_ref, o_ref):
    s = _silu(c_ref[...])
    o_ref[0] = jnp.dot(s, w_ref[0], precision=HIGHEST, preferred_element_type=F32) + b_ref[0]


def ada_modulation_all(cc, ada_w, ada_b):
    R = cc.shape[0]
    tn = 1536
    return pl.pallas_call(
        _ada_kernel,
        out_shape=jax.ShapeDtypeStruct((DEPTH, R, 6 * D), F32),
        grid=(DEPTH, 6 * D // tn),
        in_specs=[pl.BlockSpec((R, D), lambda l, n: (0, 0)),
                  pl.BlockSpec((1, D, tn), lambda l, n: (l, 0, n)),
                  pl.BlockSpec((1, 1, tn), lambda l, n: (l, 0, n))],
        out_specs=pl.BlockSpec((1, R, tn), lambda l, n: (l, 0, n)),
        compiler_params=_cparams(("arbitrary", "arbitrary")),
        name="ada_modulation",
    )(cc, ada_w, ada_b.reshape(DEPTH, 1, 6 * D))


_EVEN_CHUNKS = ((C_QA, C_KD, True), (C_KD, C_VD, True), (C_VD, C_VG, False), (C_VG, C_RG, False),
                (C_RG, C_QG, False), (C_QG, C_GG, False), (C_GG, P_W, False))


def _inproj_even_kernel(h_ref, mod_ref, w_ref, cos_ref, sin_ref, p_ref):
    tm = h_ref.shape[1]
    u = (h_ref[0] * (1.0 + mod_ref[0, 1:2, :]) + mod_ref[0, 0:1, :]).astype(BF16)
    cos = cos_ref[...]
    sin = sin_ref[...]
    lane = lax.broadcasted_iota(jnp.int32, (tm, LANE), 1)
    first_half = (lane % HEAD_DIM) < (HEAD_DIM // 2)
    for c0, c1, rope in _EVEN_CHUNKS:
        acc = _dot(u, w_ref[:, c0:c1])
        if rope:
            for i in range((c1 - c0) // LANE):
                x = acc[:, i * LANE:(i + 1) * LANE]
                partner = jnp.where(first_half, pltpu.roll(x, LANE - 32, 1), pltpu.roll(x, 32, 1))
                p_ref[0, :, c0 + i * LANE:c0 + (i + 1) * LANE] = (x * cos + partner * sin).astype(BF16)
        else:
            p_ref[0, :, c0:c1] = acc.astype(BF16)


def inproj_even(h, mod, w, cos_t, sin_t):
    B, T, _ = h.shape
    tm = min(ROW_TILE, T)
    mb = mod.shape[0]
    return pl.pallas_call(
        _inproj_even_kernel,
        out_shape=jax.ShapeDtypeStruct((B, T, P_W), BF16),
        grid=(B, T // tm),
        in_specs=[pl.BlockSpec((1, tm, D), lambda b, j: (b, j, 0)),
                  pl.BlockSpec((1, 6, D), (lambda b, j: (b, 0, 0)) if mb > 1 else (lambda b, j: (0, 0, 0))),
                  pl.BlockSpec((D, P_W), lambda b, j: (0, 0)),
                  pl.BlockSpec((tm, LANE), lambda b, j: (j, 0)),
                  pl.BlockSpec((tm, LANE), lambda b, j: (j, 0))],
        out_specs=pl.BlockSpec((1, tm, P_W), lambda b, j: (b, j, 0)),
        compiler_params=_cparams(("parallel", "arbitrary")),
        name="inproj_even",
    )(h, mod, w, cos_t, sin_t)


def _attn_kernel(*refs, tq, tiles, has_window):
    if has_window:
        sink_ref, q_ref, kw_ref, vw_ref, kc_ref, vc_ref, o_ref = refs
    else:
        sink_ref, q_ref, kc_ref, vc_ref, o_ref = refs
    group = A_Q_HEADS // A_KV_HEADS
    rows = group * tq
    lo = lax.broadcasted_iota(jnp.int32, (tq, LANE), 1) < HEAD_DIM
    den_lanes = lax.broadcasted_iota(jnp.int32, (rows, LANE), 1) >= HEAD_DIM

    def with_ones(v):
        return jnp.where(lax.broadcasted_iota(jnp.int32, v.shape, 1) < HEAD_DIM, v, jnp.ones_like(v))

    wstart, band = [], []
    if has_window:
        S = kw_ref.shape[1]
        wk = tq + 2 * WINDOW
        for t in range(tiles):
            q0 = (pl.program_id(1) * tiles + t) * tq
            wstart.append(pl.multiple_of(jnp.clip(q0 - WINDOW, 0, S - wk), LANE))
            qpos = q0 + lax.broadcasted_iota(jnp.int32, (tq, wk), 0)
            kpos = wstart[t] + lax.broadcasted_iota(jnp.int32, (tq, wk), 1)
            band.append(jnp.tile(jnp.where(jnp.abs(qpos - kpos) <= WINDOW, 0.0, NEG), (group, 1)))
    chains = [(t, g) for t in range(tiles) for g in range(A_KV_HEADS)]
    rt = [slice(t * tq, (t + 1) * tq) for t in range(tiles)]
    cols = [slice(g * LANE, (g + 1) * LANE) for g in range(A_KV_HEADS)]
    q4, snk, sc, m, outs = [], [], [], [], []
    for t, g in chains:
        qs = []
        for pr in range(group // 2):
            qblk = q_ref[0, rt[t], (2 * g + pr) * LANE:(2 * g + pr + 1) * LANE]
            zero = jnp.zeros_like(qblk)
            qs += [jnp.where(lo, qblk, zero), jnp.where(lo, zero, qblk)]
        q4.append(jnp.concatenate(qs, axis=0))
        snk.append(jnp.concatenate([jnp.full((tq, 1), sink_ref[group * g + i] * LOG2_E, F32) for i in range(group)],
                                   axis=0))
    for c, (t, g) in enumerate(chains):
        if has_window:
            keys = jnp.concatenate([kw_ref[0, pl.ds(wstart[t], wk), cols[g]], kc_ref[0, :, cols[g]]], axis=0)
            s = _dot_nt(q4[c], keys)
            sc.append(jnp.concatenate([s[:, 0:wk] + band[t], s[:, wk:]], axis=1))
        else:
            sc.append(_dot_nt(q4[c], kc_ref[0, :, cols[g]]))
    for c in range(len(chains)):
        m.append(jnp.maximum(jnp.max(sc[c], axis=-1, keepdims=True), snk[c]))
    for c, (t, g) in enumerate(chains):
        if has_window:
            vals = jnp.concatenate([vw_ref[0, pl.ds(wstart[t], wk), cols[g]], vc_ref[0, :, cols[g]]], axis=0)
        else:
            vals = vc_ref[0, :, cols[g]]
        o = _dot(jnp.exp2((sc[c] - m[c]).astype(BF16)), with_ones(vals))
        outs.append(o + jnp.where(den_lanes, jnp.exp2(snk[c] - m[c]), 0.0))
    for c, (t, g) in enumerate(chains):
        o = outs[c]
        swapped = pltpu.roll(o, HEAD_DIM, 1)
        for pr in range(group // 2):
            ev = slice(2 * pr * tq, (2 * pr + 1) * tq)
            od = slice((2 * pr + 1) * tq, (2 * pr + 2) * tq)
            res = jnp.where(lo, o[ev] / swapped[ev], swapped[od] / o[od])
            o_ref[0, rt[t], (2 * g + pr) * LANE:(2 * g + pr + 1) * LANE] = res.astype(BF16)


def attention(p_q, p_ctx, sink, has_window):
    B, T, _ = p_q.shape
    L = p_ctx.shape[1]
    tq = 128
    tiles = next(n for n in (4, 2, 1) if T % (n * tq) == 0)
    in_specs = [pl.BlockSpec(memory_space=pltpu.SMEM),
                pl.BlockSpec((1, tiles * tq, W_AQ), lambda b, j: (b, j, C_QA // W_AQ))]
    args = [sink, p_q]
    if has_window:
        in_specs += [pl.BlockSpec((1, T, W_KV2), lambda b, j: (b, 0, C_KD // W_KV2)),
                     pl.BlockSpec((1, T, W_KV2), lambda b, j: (b, 0, C_VD // W_KV2))]
        args += [p_q, p_q]
    in_specs += [pl.BlockSpec((1, L, W_KV2), lambda b, j: (b, 0, C_KD // W_KV2)),
                 pl.BlockSpec((1, L, W_KV2), lambda b, j: (b, 0, C_VD // W_KV2))]
    args += [p_ctx, p_ctx]
    return pl.pallas_call(
        functools.partial(_attn_kernel, tq=tq, tiles=tiles, has_window=has_window),
        out_shape=jax.ShapeDtypeStruct((B, T, W_AQ), BF16),
        grid=(B, T // (tiles * tq)),
        in_specs=in_specs,
        out_specs=pl.BlockSpec((1, tiles * tq, W_AQ), lambda b, j: (b, j, 0)),
        compiler_params=_cparams(("parallel", "arbitrary")),
        name="window_attention" if has_window else "context_attention",
    )(*args)


def _log_sigmoid(x):
    return jnp.minimum(x, 0.0) - jnp.log(1.0 + jnp.exp(-jnp.abs(x)))


def _gla_kernel(qf_ref, kf_ref, vf_ref, gf_ref, qb_ref, kb_ref, vb_ref, gb_ref, wa_ref, ba_ref, s0_ref,
                of_ref, ob_ref, sfin_ref, s_sc):
    j = pl.program_id(1)
    nblk = pl.num_programs(1)
    tb = qf_ref.shape[1]
    nc = tb // GLA_CHUNK

    @pl.when(j == 0)
    def _():
        s_sc[...] = s0_ref[0]

    C = GLA_CHUNK
    hc = GLA_HEADS * C
    ri = lax.broadcasted_iota(jnp.int32, (hc, hc), 0) % C
    ci = lax.broadcasted_iota(jnp.int32, (hc, hc), 1) % C
    rb = lax.broadcasted_iota(jnp.int32, (tb, tb), 0)
    cb = lax.broadcasted_iota(jnp.int32, (tb, tb), 1)
    same_chunk = (rb // C) == (cb // C)
    lane_head = lax.broadcasted_iota(jnp.int32, (C, GLA_HEADS * GLA_DK), 1) // GLA_DK

    def per_head(x):
        zero = jnp.zeros_like(x)
        return jnp.concatenate([jnp.where(lane_head == h, x, zero) for h in range(GLA_HEADS)], axis=0)

    io = ((qf_ref, kf_ref, vf_ref, gf_ref, of_ref), (qb_ref, kb_ref, vb_ref, gb_ref, ob_ref))
    causal = ((ri >= ci), (ci >= ri))
    tri = (jnp.logical_and(same_chunk, rb >= cb).astype(BF16), jnp.logical_and(same_chunk, cb >= rb).astype(BF16))
    b_all = []
    for d in range(2):
        g = _dot(io[d][3][0, :, d * LANE:(d + 1) * LANE], wa_ref[d]) + ba_ref[d]
        log_a = _log_sigmoid(g) / GLA_TAU
        la1 = log_a.astype(BF16)
        rem = log_a - la1.astype(F32)
        la2 = rem.astype(BF16)
        la3 = (rem - la2.astype(F32)).astype(BF16)
        b_all.append(_dot(tri[d], la1) + _dot(tri[d], la2) + _dot(tri[d], la3))
    state = [jnp.concatenate([s_sc[d, 0], s_sc[d, 1]], axis=1) for d in range(2)]
    zero_blk = jnp.zeros((GLA_DV, LANE), BF16)
    for step in range(nc):
        for d in range(2):
            q_ref, k_ref, v_ref, _, o_ref = io[d]
            c = step if d == 0 else nc - 1 - step
            rows = slice(c * C, (c + 1) * C)
            b = b_all[d][rows]
            b_last = b[C - 1:C] if d == 0 else b[0:1]
            qc = q_ref[0, rows, :].astype(F32)
            kc = k_ref[0, rows, :].astype(F32)
            q4 = per_head((qc * jnp.exp(b)).astype(BF16))
            k4 = per_head((kc * jnp.exp(-b)).astype(BF16))
            ks4 = per_head((kc * jnp.exp(b_last - b)).astype(BF16))
            st = state[d]
            stb = st.astype(BF16)
            st_bd = jnp.concatenate([jnp.concatenate([stb[:, 0:LANE], zero_blk], axis=1),
                                     jnp.concatenate([zero_blk, stb[:, LANE:]], axis=1)], axis=0)
            res = _dot_nt(q4, jnp.concatenate([st_bd, k4], axis=0))
            attn = jnp.where(causal[d], res[:, 2 * GLA_DV:], 0.0).astype(BF16)
            v4 = jnp.concatenate([v_ref[0, rows, h * GLA_DV:(h + 1) * GLA_DV] for h in range(GLA_HEADS)], axis=0)
            o4 = _dot(attn, v4)
            for h in range(GLA_HEADS):
                hr = slice(h * C, (h + 1) * C)
                inter = res[hr, (h // 2) * GLA_DV:(h // 2 + 1) * GLA_DV]
                o_ref[0, rows, h * GLA_DV:(h + 1) * GLA_DV] = (o4[hr] + inter).astype(o_ref.dtype)
            state[d] = st * jnp.exp(b_last) + _dot_tn(v4, ks4)
    for d in range(2):
        for pair in range(2):
            s_sc[d, pair] = state[d][:, pair * LANE:(pair + 1) * LANE]

    @pl.when(j == nblk - 1)
    def _():
        sfin_ref[0] = s_sc[...]


def gla_scan(p, wa_p, ba, s0):
    B, T, _ = p.shape
    tb = min(ROW_TILE, T)
    nblk = T // tb
    fwd = lambda b, j: (b, j)
    bwd = lambda b, j: (b, nblk - 1 - j)

    def specs(im):
        return [pl.BlockSpec((1, tb, W_GK), lambda b, j: im(b, j) + (C_QG // W_GK,)),
                pl.BlockSpec((1, tb, W_GK), lambda b, j: im(b, j) + (C_KG // W_GK,)),
                pl.BlockSpec((1, tb, W_GV), lambda b, j: im(b, j) + (C_VG // W_GV,)),
                pl.BlockSpec((1, tb, 2 * LANE), lambda b, j: im(b, j) + (C_GG // (2 * LANE),))]

    return pl.pallas_call(
        _gla_kernel,
        out_shape=(jax.ShapeDtypeStruct((B, T, W_GV), BF16), jax.ShapeDtypeStruct((B, T, W_GV), BF16),
                   jax.ShapeDtypeStruct(s0.shape, F32)),
        grid=(B, nblk),
        in_specs=specs(fwd) + specs(bwd) + [
            pl.BlockSpec((2, LANE, W_GK), lambda b, j: (0, 0, 0)),
            pl.BlockSpec((2, 1, W_GK), lambda b, j: (0, 0, 0)),
            pl.BlockSpec((1, 2, 2, GLA_DV, LANE), lambda b, j: (b, 0, 0, 0, 0))],
        out_specs=(pl.BlockSpec((1, tb, W_GV), lambda b, j: (b, j, 0)),
                   pl.BlockSpec((1, tb, W_GV), lambda b, j: (b, nblk - 1 - j, 0)),
                   pl.BlockSpec((1, 2, 2, GLA_DV, LANE), lambda b, j: (b, 0, 0, 0, 0))),
        scratch_shapes=[pltpu.VMEM((2, 2, GLA_DV, LANE), F32)],
        compiler_params=_cparams(("parallel", "arbitrary")),
        name="gla_scan",
    )(p, p, p, p, p, p, p, p, wa_p, ba, s0)


def _deepnorm(h, gate, y, w, b):
    r = h + (gate * (1.0 / ALPHA)) * y
    mu = jnp.mean(r, axis=-1, keepdims=True)
    xc = r - mu
    var = jnp.mean(xc * xc, axis=-1, keepdims=True)
    return xc * lax.rsqrt(var + LN_EPS / (ALPHA * ALPHA)) * w + b


def _pack_rounded_pairs(xr):
    half = xr.shape[1] // 2
    lo = lax.bitcast_convert_type(xr[:, :half], jnp.uint32)
    hi = lax.bitcast_convert_type(xr[:, half:], jnp.uint32)
    return jnp.bitwise_or(hi, lax.shift_right_logical(lo, jnp.uint32(16)))


def _pack_bf16_pairs(x):
    return _pack_rounded_pairs(x.astype(BF16).astype(F32))


def _unpack_bf16_pairs(p):
    lo = lax.bitcast_convert_type(lax.shift_left(p, jnp.uint32(16)), F32)
    hi = lax.bitcast_convert_type(jnp.bitwise_and(p, jnp.uint32(0xFFFF0000)), F32)
    return lo, hi


def _post_norm_and_route(h, y, mod_ref, lnw_ref, lnb_ref, rwh_ref, rwl_ref, rb_ref, h1_ref, tok_ref, lg_ref):
    h1 = _deepnorm(h, mod_ref[0, 2:3, :], y, lnw_ref[...], lnb_ref[...])
    h1_ref[0] = h1
    tok = h1 * (1.0 + mod_ref[0, 4:5, :]) + mod_ref[0, 3:4, :]
    hi = tok.astype(BF16)
    hi_f = hi.astype(F32)
    tok_ref[0] = _pack_rounded_pairs(hi_f)
    lo = (tok - hi_f).astype(BF16)
    lg = _dot(hi, rwh_ref[...]) + _dot(lo, rwh_ref[...]) + _dot(hi, rwl_ref[...]) + rb_ref[...]
    lg_ref[...] = lg.T[0:ROUTE_ROWS, :]


def _epilogue_specs(tm, mb):
    mod_map = (lambda b, j: (b, 0, 0)) if mb > 1 else (lambda b, j: (0, 0, 0))
    const2 = lambda b, j: (0, 0)
    return [pl.BlockSpec((1, tm, D), lambda b, j: (b, j, 0)),
            pl.BlockSpec((1, 6, D), mod_map),
            pl.BlockSpec((1, D), const2), pl.BlockSpec((1, D), const2),
            pl.BlockSpec((D, ROUTE_W), const2), pl.BlockSpec((D, ROUTE_W), const2),
            pl.BlockSpec((1, ROUTE_W), const2)]


def _epilogue_outs(B, T, tm):
    nj = T // tm
    shapes = (jax.ShapeDtypeStruct((B, T, D), F32), jax.ShapeDtypeStruct((B, T, D // 2), jnp.uint32),
              jax.ShapeDtypeStruct((ROUTE_ROWS, B * T), F32))
    specs = (pl.BlockSpec((1, tm, D), lambda b, j: (b, j, 0)), pl.BlockSpec((1, tm, D // 2), lambda b, j: (b, j, 0)),
             pl.BlockSpec((ROUTE_ROWS, tm), lambda b, j: (0, b * nj + j)))
    return shapes, specs


def _outproj_even_kernel(a_ref, of_ref, ob_ref, rg_ref, nw_ref, wo_ref,
                         h_ref, mod_ref, lnw_ref, lnb_ref, rwh_ref, rwl_ref, rb_ref,
                         h1_ref, tok_ref, lg_ref):
    parts = [a_ref[0]]
    for hd in range(GLA_HEADS):
        cols = slice(hd * GLA_DV, (hd + 1) * GLA_DV)
        o = of_ref[0, :, cols].astype(F32) + ob_ref[0, :, cols].astype(F32)
        o = o * lax.rsqrt(jnp.mean(o * o, axis=-1, keepdims=True) + RMS_EPS)
        parts.append((o * nw_ref[:, cols] * _silu(rg_ref[0, :, cols].astype(F32))).astype(BF16))
    y = _dot(jnp.concatenate(parts, axis=1), wo_ref[...])
    _post_norm_and_route(h_ref[0], y, mod_ref, lnw_ref, lnb_ref, rwh_ref, rwl_ref, rb_ref, h1_ref, tok_ref, lg_ref)


def outproj_even(a, o_f, o_b, p, norm_w, w_out, h, mod, lnw, lnb, rwh, rwl, rb):
    B, T, _ = h.shape
    tm = min(ROW_TILE, T)
    tile = lambda b, j: (b, j, 0)
    shapes, ospecs = _epilogue_outs(B, T, tm)
    return pl.pallas_call(
        _outproj_even_kernel,
        out_shape=shapes,
        grid=(B, T // tm),
        in_specs=[pl.BlockSpec((1, tm, W_AQ), tile), pl.BlockSpec((1, tm, W_GV), tile), pl.BlockSpec((1, tm, W_GV), tile),
                  pl.BlockSpec((1, tm, W_GV), lambda b, j: (b, j, C_RG // W_GV)),
                  pl.BlockSpec((1, W_GV), lambda b, j: (0, 0)),
                  pl.BlockSpec((D, D), lambda b, j: (0, 0))] + _epilogue_specs(tm, mod.shape[0]),
        out_specs=ospecs,
        compiler_params=_cparams(("parallel", "arbitrary")),
        name="outproj_even",
    )(a, o_f, o_b, p, norm_w, w_out, h, mod, lnw, lnb, rwh, rwl, rb)


def _inproj_odd_kernel(h_ref, mod_ref, w_ref, o_ref):
    u = (h_ref[0] * (1.0 + mod_ref[0, 1:2, :]) + mod_ref[0, 0:1, :]).astype(BF16)
    o_ref[0, :, 0:D] = _dot(u, w_ref[:, 0:D]).astype(BF16)
    o_ref[0, :, D:2 * D] = (_dot(u, w_ref[:, D:2 * D]) * _dot(u, w_ref[:, 2 * D:3 * D])).astype(BF16)


def inproj_odd(h, mod, w):
    B, T, _ = h.shape
    tm = min(ROW_TILE, T)
    mb = mod.shape[0]
    return pl.pallas_call(
        _inproj_odd_kernel,
        out_shape=jax.ShapeDtypeStruct((B, T, 2 * D), BF16),
        grid=(B, T // tm),
        in_specs=[pl.BlockSpec((1, tm, D), lambda b, j: (b, j, 0)),
                  pl.BlockSpec((1, 6, D), (lambda b, j: (b, 0, 0)) if mb > 1 else (lambda b, j: (0, 0, 0))),
                  pl.BlockSpec((D, 3 * D), lambda b, j: (0, 0))],
        out_specs=pl.BlockSpec((1, tm, 2 * D), lambda b, j: (b, j, 0)),
        compiler_params=_cparams(("parallel", "arbitrary")),
        name="inproj_odd",
    )(h, mod, w)


HALO = 16


def _outproj_odd_kernel(gb_ref, z_ref, zp_ref, zn_ref, cw_ref, cb_ref, wo_ref,
                        h_ref, mod_ref, lnw_ref, lnb_ref, rwh_ref, rwl_ref, rb_ref,
                        h1_ref, tok_ref, lg_ref):
    j = pl.program_id(1)
    tm = z_ref.shape[1]
    z = z_ref[0].astype(F32)
    prev_row = jnp.where(j > 0, zp_ref[0, HALO - 1:HALO, :].astype(F32), 0.0)
    next_row = jnp.where(j < pl.num_programs(1) - 1, zn_ref[0, 0:1, :].astype(F32), 0.0)
    row = lax.broadcasted_iota(jnp.int32, (tm, D), 0)
    z_prev = jnp.where(row == 0, prev_row, pltpu.roll(z, 1, 0))
    z_next = jnp.where(row == tm - 1, next_row, pltpu.roll(z, tm - 1, 0))
    conv = z_prev * cw_ref[0:1, :] + z * cw_ref[1:2, :] + z_next * cw_ref[2:3, :] + cb_ref[...]
    y = _dot((gb_ref[0].astype(F32) * conv).astype(BF16), wo_ref[...])
    _post_norm_and_route(h_ref[0], y, mod_ref, lnw_ref, lnb_ref, rwh_ref, rwl_ref, rb_ref, h1_ref, tok_ref, lg_ref)


def outproj_odd(gz, conv_w, conv_b, w_out, h, mod, lnw, lnb, rwh, rwl, rb):
    B, T, _ = h.shape
    tm = min(ROW_TILE, T)
    r = tm // HALO
    nh = T // HALO
    shapes, ospecs = _epilogue_outs(B, T, tm)
    return pl.pallas_call(
        _outproj_odd_kernel,
        out_shape=shapes,
        grid=(B, T // tm),
        in_specs=[pl.BlockSpec((1, tm, D), lambda b, j: (b, j, 0)),
                  pl.BlockSpec((1, tm, D), lambda b, j: (b, j, 1)),
                  pl.BlockSpec((1, HALO, D), lambda b, j: (b, jnp.maximum(j * r - 1, 0), 1)),
                  pl.BlockSpec((1, HALO, D), lambda b, j: (b, jnp.minimum((j + 1) * r, nh - 1), 1)),
                  pl.BlockSpec((3, D), lambda b, j: (0, 0)),
                  pl.BlockSpec((1, D), lambda b, j: (0, 0)),
                  pl.BlockSpec((D, D), lambda b, j: (0, 0))] + _epilogue_specs(tm, mod.shape[0]),
        out_specs=ospecs,
        compiler_params=_cparams(("parallel", "arbitrary")),
        name="outproj_odd",
    )(gz, gz, gz, gz, conv_w, conv_b, w_out, h, mod, lnw, lnb, rwh, rwl, rb)


def _sc_mesh():
    return plsc.VectorSubcoreMesh(core_axis_name="c", subcore_axis_name="s")


def sc_gather_rows(table, idx):
    n = idx.shape[0]
    width = table.shape[1]
    per_w = n // SC_WORKERS
    n_chunks = per_w // SC_CHUNK
    assert n_chunks % 2 == 0

    @functools.partial(
        pl.kernel, mesh=_sc_mesh(),
        out_type=jax.ShapeDtypeStruct((n, width), table.dtype),
        scratch_types=[pltpu.VMEM((n_chunks, SC_CHUNK), jnp.int32),
                       pltpu.VMEM((SC_CHUNK, width), table.dtype), pltpu.VMEM((SC_CHUNK, width), table.dtype),
                       pltpu.SemaphoreType.DMA, pltpu.SemaphoreType.DMA],
    )
    def gather_kernel(table_hbm, idx_hbm, out_hbm, idx_v, buf0, buf1, sem0, sem1):
        wid = lax.axis_index("s") * SC_CORES + lax.axis_index("c")
        pltpu.sync_copy(idx_hbm.at[wid], idx_v)

        def fetch(j, buf, sem):
            return pltpu.make_async_copy(table_hbm.at[idx_v.at[j]], buf, sem)

        def flush(j, buf):
            pltpu.sync_copy(buf, out_hbm.at[pl.ds(wid * per_w + j * SC_CHUNK, SC_CHUNK)])

        fetch(0, buf0, sem0).start()

        @pl.loop(0, n_chunks, step=2)
        def _(j):
            fetch(j + 1, buf1, sem1).start()
            fetch(j, buf0, sem0).wait()
            flush(j, buf0)

            @pl.when(j + 2 < n_chunks)
            def _():
                fetch(j + 2, buf0, sem0).start()

            fetch(j + 1, buf1, sem1).wait()
            flush(j + 1, buf1)

    return gather_kernel(table, idx.reshape(SC_WORKERS, n_chunks, SC_CHUNK))


def sc_scatter_rows(srcs, idxs, n_rows):
    width, dt = srcs[0][0].shape[1], srcs[0][0].dtype
    plans, args = [], []
    for (src, row0, n_src), idx in zip(srcs, idxs):
        per_w = n_src // SC_WORKERS
        chunk = min(SC_CHUNK, per_w // 2)
        assert (per_w // chunk) % 2 == 0
        plans.append((per_w, chunk, per_w // chunk, idx.shape[0], row0))
        args += [src, idx.reshape(idx.shape[0], SC_WORKERS, per_w // chunk, chunk)]
    max_chunk = max(p[1] for p in plans)
    scratch = [pltpu.VMEM((max_chunk, width), dt), pltpu.VMEM((max_chunk, width), dt),
               pltpu.SemaphoreType.DMA, pltpu.SemaphoreType.DMA]
    scratch += [pltpu.VMEM((lists, n_chunks, chunk), jnp.int32) for _, chunk, n_chunks, lists, _ in plans]

    @functools.partial(pl.kernel, mesh=_sc_mesh(), out_type=jax.ShapeDtypeStruct((n_rows, width), dt),
                       scratch_types=scratch)
    def scatter_kernel(*refs):
        ins, out_hbm = refs[:2 * len(plans)], refs[2 * len(plans)]
        rows0, rows1, sem0, sem1 = refs[2 * len(plans) + 1:2 * len(plans) + 5]
        idx_vs = refs[2 * len(plans) + 5:]
        wid = lax.axis_index("s") * SC_CORES + lax.axis_index("c")
        for s, (per_w, chunk, n_chunks, lists, row0) in enumerate(plans):
            src_hbm, idx_hbm, idx_v = ins[2 * s], ins[2 * s + 1], idx_vs[s]
            for k in range(lists):
                pltpu.sync_copy(idx_hbm.at[k, wid], idx_v.at[k])
            buf0 = rows0 if chunk == max_chunk else rows0.at[pl.ds(0, chunk)]
            buf1 = rows1 if chunk == max_chunk else rows1.at[pl.ds(0, chunk)]

            def load(j, buf, sem, src_hbm=src_hbm, per_w=per_w, chunk=chunk, row0=row0):
                return pltpu.make_async_copy(src_hbm.at[pl.ds(row0 + wid * per_w + j * chunk, chunk)], buf, sem)

            def spread(j, buf, idx_v=idx_v, lists=lists):
                for k in range(lists):
                    pltpu.sync_copy(buf, out_hbm.at[idx_v.at[k, j]])

            load(0, buf0, sem0).start()

            @pl.loop(0, n_chunks, step=2)
            def _(j, load=load, spread=spread, buf0=buf0, buf1=buf1, n_chunks=n_chunks):
                load(j + 1, buf1, sem1).start()
                load(j, buf0, sem0).wait()
                spread(j, buf0)

                @pl.when(j + 2 < n_chunks)
                def _():
                    load(j + 2, buf0, sem0).start()

                load(j + 1, buf1, sem1).wait()
                spread(j + 1, buf1)

    return scatter_kernel(*args)


def _ffn_kernel(be_ref, nv_ref, x_ref, w1_ref, w3_ref, w2_ref, y_ref, w1b, w3b, w2b):
    i = pl.program_id(0)
    changed = jnp.logical_or(i == 0, be_ref[i] != be_ref[jnp.maximum(i - 1, 0)])

    @pl.when(changed)
    def _():
        w1b[...] = w1_ref[0, 0].astype(BF16)
        w3b[...] = w3_ref[0, 0].astype(BF16)
        w2b[...] = w2_ref[0, 0].astype(BF16)

    @pl.when(i < nv_ref[0])
    def _():
        x = jnp.concatenate([v.astype(BF16) for v in _unpack_bf16_pairs(x_ref[...])], axis=1)
        y_ref[...] = _pack_bf16_pairs(_dot((_silu(_dot(x, w1b[...])) * _dot(x, w3b[...])).astype(BF16), w2b[...]))

    @pl.when(i >= nv_ref[0])
    def _():
        y_ref[...] = jnp.zeros_like(y_ref)


def moe_ffn(xs, blk_expert, n_valid, w1, w3, w2, layer):
    n_rows = xs.shape[0]
    nb = n_rows // MOE_TM
    return pl.pallas_call(
        _ffn_kernel,
        out_shape=jax.ShapeDtypeStruct((n_rows, D // 2), jnp.uint32),
        grid_spec=pltpu.PrefetchScalarGridSpec(
            num_scalar_prefetch=2,
            grid=(nb,),
            in_specs=[pl.BlockSpec((MOE_TM, D // 2), lambda i, be, nv: (i, 0)),
                      pl.BlockSpec((1, 1, D, D_EXPERT), lambda i, be, nv: (layer, be[i], 0, 0)),
                      pl.BlockSpec((1, 1, D, D_EXPERT), lambda i, be, nv: (layer, be[i], 0, 0)),
                      pl.BlockSpec((1, 1, D_EXPERT, D), lambda i, be, nv: (layer, be[i], 0, 0))],
            out_specs=pl.BlockSpec((MOE_TM, D // 2), lambda i, be, nv: (i, 0)),
            scratch_shapes=[pltpu.VMEM((D, D_EXPERT), BF16), pltpu.VMEM((D, D_EXPERT), BF16),
                            pltpu.VMEM((D_EXPERT, D), BF16)]),
        compiler_params=_cparams(("arbitrary",)),
        name="moe_ffn",
    )(blk_expert, n_valid, xs, w1, w3, w2)


def _combine_kernel(h_ref, y0_ref, y1_ref, wt_ref, mod_ref, lnw_ref, lnb_ref, o_ref):
    half = D // 2
    lo0, hi0 = _unpack_bf16_pairs(y0_ref[0])
    lo1, hi1 = _unpack_bf16_pairs(y1_ref[0])
    pick = (lax.broadcasted_iota(jnp.int32, (WT_ROWS, LANE), 0) == lax.broadcasted_iota(jnp.int32, (WT_ROWS, LANE), 1))
    wcols = lax.dot_general(wt_ref[...], pick.astype(F32), (((0,), (0,)), ((), ())), precision=HIGHEST,
                            preferred_element_type=F32)
    w0, w1 = wcols[:, 0:1], wcols[:, 1:2]
    gate = mod_ref[0, 5:6, :] * (1.0 / ALPHA)
    r_lo = h_ref[:, 0:half] + gate[:, 0:half] * (w0 * lo0 + w1 * lo1)
    r_hi = h_ref[:, half:D] + gate[:, half:D] * (w0 * hi0 + w1 * hi1)
    mu = (jnp.sum(r_lo, axis=-1, keepdims=True) + jnp.sum(r_hi, axis=-1, keepdims=True)) * (1.0 / D)
    c_lo, c_hi = r_lo - mu, r_hi - mu
    var = (jnp.sum(c_lo * c_lo, axis=-1, keepdims=True) + jnp.sum(c_hi * c_hi, axis=-1, keepdims=True)) * (1.0 / D)
    inv = lax.rsqrt(var + LN_EPS / (ALPHA * ALPHA))
    o_ref[:, 0:half] = c_lo * inv * lnw_ref[:, 0:half] + lnb_ref[:, 0:half]
    o_ref[:, half:D] = c_hi * inv * lnw_ref[:, half:D] + lnb_ref[:, half:D]


def _combine_into_kernel(*refs):
    _combine_kernel(*refs[:7], refs[8])


def moe_combine(h1, y_rows, wts, mod, lnw, lnb, tok0, ntok, y_tok0, prev=None):
    B, T, _ = h1.shape
    N = B * T
    tm = min(ROW_TILE, ntok)
    per_b = T // tm
    h_off, y_off = tok0 // tm, y_tok0 // tm
    mod_map = (lambda i: ((i + h_off) // per_b, 0, 0)) if mod.shape[0] > 1 else (lambda i: (0, 0, 0))
    in_specs = [pl.BlockSpec((tm, D), lambda i: (i + h_off, 0)),
                pl.BlockSpec((1, tm, D // 2), lambda i: (0, i + y_off, 0)),
                pl.BlockSpec((1, tm, D // 2), lambda i: (1, i + y_off, 0)),
                pl.BlockSpec((WT_ROWS, tm), lambda i: (0, i + y_off)),
                pl.BlockSpec((1, 6, D), mod_map),
                pl.BlockSpec((1, D), lambda i: (0, 0)), pl.BlockSpec((1, D), lambda i: (0, 0))]
    args = [h1.reshape(N, D), y_rows, y_rows, wts, mod, lnw, lnb]
    if prev is not None:
        in_specs.append(pl.BlockSpec(memory_space=pl.ANY))
        args.append(prev)
    return pl.pallas_call(
        _combine_kernel if prev is None else _combine_into_kernel,
        out_shape=jax.ShapeDtypeStruct((N, D), F32),
        grid=(ntok // tm,),
        in_specs=in_specs,
        out_specs=pl.BlockSpec((tm, D), lambda i: (i + h_off, 0)),
        input_output_aliases={} if prev is None else {7: 0},
        compiler_params=_cparams(("parallel",)),
        name="moe_combine",
    )(*args)


def _route_kernel(lg_ref, dest_ref, wt_ref, cnt_ref, tri_sc, start_sc, run_sc):
    ph, i = pl.program_id(0), pl.program_id(1)
    tr = lg_ref.shape[1]

    @pl.when(jnp.logical_and(ph == 0, i == 0))
    def _():
        r = lax.broadcasted_iota(jnp.int32, (tr, tr), 0)
        c = lax.broadcasted_iota(jnp.int32, (tr, tr), 1)
        tri_sc[...] = (r < c).astype(BF16)
        start_sc[...] = jnp.zeros_like(start_sc)
        run_sc[...] = jnp.zeros_like(run_sc)

    @pl.when(jnp.logical_and(ph == 1, i == 0))
    def _():
        cnt = run_sc[...].astype(jnp.int32)
        cnt_ref[...] = cnt
        padded = jnp.bitwise_and(cnt + (MOE_TM - 1), -MOE_TM)
        row = lax.broadcasted_iota(jnp.int32, padded.shape, 0)
        acc = padded
        for s in (1, 2, 4, 8, 16):
            acc = acc + jnp.where(row >= s, pltpu.roll(acc, s, 0), 0)
        start_sc[...] = (acc - padded).astype(F32)
        run_sc[...] = jnp.zeros_like(run_sc)

    lg = lg_ref[...]
    gl = lg[N_EXPERTS:N_EXPERTS + N_GROUPS]
    gmax = jnp.max(gl, axis=0, keepdims=True)
    sub4 = lax.broadcasted_iota(jnp.int32, gl.shape, 0)
    g_sel = jnp.min(jnp.where(gl == gmax, sub4, N_GROUPS), axis=0, keepdims=True)
    p_group = 1.0 / jnp.sum(jnp.exp(gl - gmax), axis=0, keepdims=True)
    el = lg[0:EXPERTS_PER_GROUP]
    for g in range(1, N_GROUPS):
        el = jnp.where(g_sel == g, lg[g * EXPERTS_PER_GROUP:(g + 1) * EXPERTS_PER_GROUP], el)
    sub8 = lax.broadcasted_iota(jnp.int32, el.shape, 0)
    e1 = jnp.max(el, axis=0, keepdims=True)
    i1 = jnp.min(jnp.where(el == e1, sub8, EXPERTS_PER_GROUP), axis=0, keepdims=True)
    rest = jnp.where(sub8 == i1, -jnp.inf, el)
    e2 = jnp.max(rest, axis=0, keepdims=True)
    i2 = jnp.min(jnp.where(rest == e2, sub8, EXPERTS_PER_GROUP), axis=0, keepdims=True)
    den = jnp.sum(jnp.exp(el - e1), axis=0, keepdims=True)
    p1 = 1.0 / den
    p2 = jnp.exp(e2 - e1) / den
    wt_ref[...] = jnp.zeros_like(wt_ref)
    wt_ref[0:1, :] = p_group * p1 / (p1 + p2)
    wt_ref[1:2, :] = p_group * p2 / (p1 + p2)

    sub32 = lax.broadcasted_iota(jnp.int32, (N_EXPERTS, tr), 0)
    oh = [(sub32 == g_sel * EXPERTS_PER_GROUP + ix).astype(F32) for ix in (i1, i2)]
    cnt = [jnp.sum(o, axis=1, keepdims=True) for o in oh]
    @pl.when(ph == 0)
    def _():
        dest_ref[...] = jnp.zeros_like(dest_ref)

    @pl.when(ph == 1)
    def _():
        before = start_sc[:, 0:1] + run_sc[:, 0:1]
        for k in range(TOP_K):
            prior = _dot(oh[k].astype(BF16), tri_sc[...]) + before + (cnt[0] if k == 1 else 0.0)
            dest_ref[k:k + 1, :] = jnp.sum(oh[k] * prior, axis=0, keepdims=True).astype(jnp.int32)

    run_sc[...] = run_sc[...] + (cnt[0] + cnt[1])


def moe_route(logits_t, col0, N):
    tr = next(t for t in (1024, 512, 256) if N % t == 0 and col0 % t == 0)
    t0 = col0 // tr
    return pl.pallas_call(
        _route_kernel,
        out_shape=(jax.ShapeDtypeStruct((TOP_K, N), jnp.int32), jax.ShapeDtypeStruct((WT_ROWS, N), F32),
                   jax.ShapeDtypeStruct((N_EXPERTS, LANE), jnp.int32)),
        grid=(2, N // tr),
        in_specs=[pl.BlockSpec((ROUTE_ROWS, tr), lambda p, i: (0, i + t0))],
        out_specs=(pl.BlockSpec((TOP_K, tr), lambda p, i: (0, i * p)), pl.BlockSpec((WT_ROWS, tr), lambda p, i: (0, i * p)),
                   pl.BlockSpec((N_EXPERTS, LANE), lambda p, i: (0, 0))),
        scratch_shapes=[pltpu.VMEM((tr, tr), BF16), pltpu.VMEM((N_EXPERTS, LANE), F32),
                        pltpu.VMEM((N_EXPERTS, LANE), F32)],
        compiler_params=_cparams(("arbitrary", "arbitrary")),
        name="moe_route",
    )(logits_t)


def _block_tables(counts, n_assign):
    padded = (counts + MOE_TM - 1) // MOE_TM * MOE_TM
    pad_end = jnp.cumsum(padded)
    pad_start = pad_end - padded
    nb = -(-n_assign // MOE_TM) + N_EXPERTS
    blk_start = jnp.arange(nb, dtype=jnp.int32) * MOE_TM
    blk_expert = jnp.minimum(jnp.sum((pad_end[None, :] <= blk_start[:, None]).astype(jnp.int32), axis=1), N_EXPERTS - 1)
    n_valid = (pad_end[-1] // MOE_TM).astype(jnp.int32).reshape(1)
    n_fill = nb * MOE_TM - n_assign
    gap = padded - counts
    gap_end = jnp.cumsum(gap)
    k = jnp.arange(n_fill, dtype=jnp.int32)[:, None]
    sel = jnp.logical_and(k >= (gap_end - gap)[None, :], k < gap_end[None, :])
    in_gap = jnp.sum(jnp.where(sel, (pad_start + counts - (gap_end - gap))[None, :] + k, 0), axis=1)
    fill = jnp.where(k[:, 0] < gap_end[-1], in_gap, pad_end[-1] + k[:, 0] - gap_end[-1])
    return blk_expert.astype(jnp.int32), n_valid, fill.astype(jnp.int32), nb * MOE_TM


def hier_moe_and_norm(streams, w1, w3, w2, layer, lnw, lnb):
    lat = streams[0]
    ctx = streams[1] if len(streams) > 1 else None
    n_lat = lat[0].shape[0] * lat[0].shape[1]
    n_ctx = ctx[0].shape[0] * ctx[0].shape[1] if ctx else 0
    logits_t = lat[2] if ctx is None else jnp.concatenate([lat[2], ctx[2]], 1)
    tok_lat = lat[1].reshape(n_lat, D // 2)
    half = n_lat // 2
    ranges = ((0, half), (half, n_lat - half + n_ctx))
    routed = [moe_route(logits_t, c0, n) for c0, n in ranges]
    sorted_in, tables = [], []
    for (c0, n), (dest, _, counts) in zip(ranges, routed):
        blk_expert, n_valid, fill_rows, n_rows = _block_tables(counts[:, 0], TOP_K * n)
        n_lat_here = min(c0 + n, n_lat) - c0
        srcs, idxs = [(tok_lat, c0, n_lat_here)], [dest[:, :n_lat_here]]
        if n > n_lat_here:
            srcs.append((ctx[1].reshape(n_ctx, D // 2), 0, n_ctx))
            idxs.append(dest[:, n_lat_here:])
        srcs.append((jnp.zeros((fill_rows.shape[0], D // 2), jnp.uint32), 0, fill_rows.shape[0]))
        idxs.append(fill_rows.reshape(1, -1))
        sorted_in.append(sc_scatter_rows(srcs, idxs, n_rows))
        tables.append((blk_expert, n_valid, n_lat_here))
    out_lat, out_ctx = None, None
    for (c0, n), (dest, wts, _), xs, (blk_expert, n_valid, n_lat_here) in zip(ranges, routed, sorted_in, tables):
        y = moe_ffn(xs, blk_expert, n_valid, w1, w3, w2, layer)
        y_rows = sc_gather_rows(y, dest.reshape(-1)).reshape(TOP_K, n, D // 2)
        out_lat = moe_combine(lat[0], y_rows, wts, lat[3], lnw, lnb, c0, n_lat_here, 0, out_lat)
        if n > n_lat_here:
            out_ctx = moe_combine(ctx[0], y_rows, wts, ctx[3], lnw, lnb, 0, n_ctx, n_lat_here)
    outs = [out_lat.reshape(lat[0].shape)]
    if ctx:
        outs.append(out_ctx.reshape(ctx[0].shape))
    return outs


def _even_projection_weights(w):
    def rope_layout(x, heads):
        return x.reshape(D, heads, 2, 2, HEAD_DIM // 4).transpose(0, 1, 3, 2, 4).reshape(D, heads * HEAD_DIM)

    def twice(x):
        return jnp.concatenate([x[:, :HEAD_DIM], x[:, :HEAD_DIM], x[:, HEAD_DIM:], x[:, HEAD_DIM:]], axis=1)

    def lane_pad(x):
        return jnp.pad(x, ((0, 0), (0, LANE - x.shape[1])))

    qa, ka, va, qg, kg, vg, rg, gg = jnp.split(w, [512, 640, 768, 1024, 1280, 1792, 2304], axis=1)
    cols = [rope_layout(qa, A_Q_HEADS) * (HEAD_DIM ** -0.5 * LOG2_E),
            twice(rope_layout(ka, A_KV_HEADS)), twice(va), vg, rg, qg * GLA_DK ** -0.5, kg,
            lane_pad(gg[:, :GLA_RANK]), lane_pad(gg[:, GLA_RANK:])]
    return jnp.concatenate(cols, axis=1).astype(BF16)


def _rope_tables(S):
    row = jnp.repeat(jnp.arange(S // GRID_W), GRID_W).astype(F32)
    col = jnp.tile(jnp.arange(GRID_W), S // GRID_W).astype(F32)
    axis_dim = HEAD_DIM // 2
    inv_freq = ROPE_BASE ** (-jnp.arange(0, axis_dim, 2, dtype=F32) / axis_dim)
    ang = jnp.concatenate([row[:, None] * inv_freq, col[:, None] * inv_freq], -1)
    cos, sin = jnp.cos(ang), jnp.sin(ang)
    cos_t = jnp.tile(cos, (1, 4))
    sin_t = jnp.tile(jnp.concatenate([-sin, sin], -1), (1, 2))
    return cos_t, sin_t


def _router_weights(wg, bg, we, be):
    w = jnp.zeros((D, ROUTE_W), F32).at[:, :N_EXPERTS].set(we).at[:, N_EXPERTS:N_EXPERTS + N_GROUPS].set(wg)
    b = jnp.zeros((1, ROUTE_W), F32).at[0, :N_EXPERTS].set(be).at[0, N_EXPERTS:N_EXPERTS + N_GROUPS].set(bg)
    hi = w.astype(BF16)
    return hi, (w - hi.astype(F32)).astype(BF16), b


def kernel(x, c, ctx, c_ctx, w_in_even, w_out_even, attn_sink, gla_wa2, gla_ba, gla_norm_w, w_in_odd, conv_w, conv_b, w_out_odd, ada_w, ada_b, ln_w, ln_b, router_wg, router_bg, router_we, router_be, moe_w1, moe_w3, moe_w2):
    B, S, _ = x.shape
    L = ctx.shape[1]
    cos_t, sin_t = _rope_tables(S)
    cos_c, sin_c = jnp.ones((L, LANE), F32), jnp.zeros((L, LANE), F32)

    n_cond = -(-(B + 1) // 8) * 8
    cc = jnp.zeros((n_cond, D), F32).at[:B].set(c).at[B].set(c_ctx)
    mods = ada_modulation_all(cc, ada_w, ada_b).reshape(DEPTH, n_cond, 6, D)

    h_lat, h_ctx = x, ctx
    for l in range(DEPTH):
        i = l // 2
        need_ctx = any(j % 2 == 0 for j in range(l + 1, DEPTH))
        m_lat = mods[l, :B]
        m_ctx = mods[l, B:B + 1]
        lnw0, lnb0 = ln_w[l, 0:1], ln_b[l, 0:1]
        lnw1, lnb1 = ln_w[l, 1:2], ln_b[l, 1:2]
        rwh, rwl, rb = _router_weights(router_wg[l], router_bg[l], router_we[l], router_be[l])
        streams = []
        if l % 2 == 0:
            w_in = _even_projection_weights(w_in_even[i])
            w_out = w_out_even[i].astype(BF16)
            wa_p = jnp.zeros((2, LANE, GLA_HEADS * GLA_DK), F32).at[:, :GLA_RANK].set(gla_wa2[i]).astype(BF16)
            ba = gla_ba[i].reshape(2, 1, -1)
            nw = gla_norm_w[i].reshape(1, -1)
            p_ctx = inproj_even(h_ctx, m_ctx, w_in, cos_c, sin_c)
            p_lat = inproj_even(h_lat, m_lat, w_in, cos_t, sin_t)
            a_lat = attention(p_lat, p_ctx, attn_sink[i], True)
            s0 = jnp.zeros((B, 2, 2, GLA_DV, LANE), F32)
            oc_f, oc_b, s_ctx = gla_scan(p_ctx, wa_p, ba, s0)
            ol_f, ol_b, _ = gla_scan(p_lat, wa_p, ba, s_ctx)
            streams.append(outproj_even(a_lat, ol_f, ol_b, p_lat, nw, w_out, h_lat, m_lat, lnw0, lnb0, rwh, rwl, rb)
                           + (m_lat,))
            if need_ctx:
                a_ctx = attention(p_ctx, p_ctx, attn_sink[i], False)
                streams.append(outproj_even(a_ctx, oc_f, oc_b, p_ctx, nw, w_out, h_ctx, m_ctx, lnw0, lnb0,
                                            rwh, rwl, rb) + (m_ctx,))
        else:
            w_in = w_in_odd[i].astype(BF16)
            w_out = w_out_odd[i].astype(BF16)
            cb = conv_b[i].reshape(1, D)
            pairs = [(h_lat, m_lat)] + ([(h_ctx, m_ctx)] if need_ctx else [])
            for h, m in pairs:
                gz = inproj_odd(h, m, w_in)
                streams.append(outproj_odd(gz, conv_w[i], cb, w_out, h, m, lnw0, lnb0, rwh, rwl, rb) + (m,))
        outs = hier_moe_and_norm(streams, moe_w1, moe_w3, moe_w2, l, lnw1, lnb1)
        h_lat = outs[0]
        if need_ctx:
            h_ctx = outs[1]
    return h_lat
```

```python
import functools

import jax
import jax.numpy as jnp
from jax import lax
from jax.experimental import pallas as pl
from jax.experimental.pallas import tpu as pltpu
from jax.experimental.pallas import tpu_sc as plsc

F32 = jnp.float32
BF16 = jnp.bfloat16
HIGHEST = lax.Precision.HIGHEST

D = 1024
DEPTH = 4
GRID_W = 64
HEAD_DIM = 64
A_Q_HEADS = 8
A_KV_HEADS = 2
WINDOW = 128
ROPE_BASE = 10000.0
GLA_HEADS = 4
GLA_DK = 64
GLA_DV = 128
GLA_RANK = 16
GLA_TAU = 16.0
GLA_CHUNK = 64
N_GROUPS = 4
EXPERTS_PER_GROUP = 8
N_EXPERTS = 32
TOP_K = 2
D_EXPERT = 512
ALPHA = (2.0 * DEPTH) ** 0.25
LN_EPS = 1e-5
RMS_EPS = 1e-6

LANE = 128
VMEM_LIMIT = 48 * 1024 * 1024

ROW_TILE = 512
IN_ROW_TILE = 1024
W_AQ = A_Q_HEADS * HEAD_DIM
W_KV2 = 2 * A_KV_HEADS * HEAD_DIM
W_GK = GLA_HEADS * GLA_DK
W_GV = GLA_HEADS * GLA_DV
C_QA = 0
C_KD = C_QA + W_AQ
C_VD = C_KD + W_KV2
C_VG = C_VD + W_KV2
C_RG = C_VG + W_GV
C_QG = C_RG + W_GV
C_KG = C_QG + W_GK
C_GG = C_KG + W_GK
P_W = C_GG + 2 * LANE
ROUTE_W = 128
WT_ROWS = 8
ROUTE_ROWS = 40
MOE_TM = 512
SC_CORES, SC_SUBCORES = 2, 16
SC_WORKERS = SC_CORES * SC_SUBCORES
SC_CHUNK = 64
NEG = -1e30
LOG2_E = 1.4426950408889634


def _cparams(sem):
    return pltpu.CompilerParams(dimension_semantics=sem, vmem_limit_bytes=VMEM_LIMIT)


def _dot(a, b):
    return jnp.dot(a, b, preferred_element_type=F32)


def _dot_nt(a, b):
    return lax.dot_general(a, b, (((1,), (1,)), ((), ())), preferred_element_type=F32)


def _dot_tn(a, b):
    return lax.dot_general(a, b, (((0,), (0,)), ((), ())), preferred_element_type=F32)


def _silu(x):
    return x * (1.0 / (1.0 + jnp.exp(-x)))


def _ada_kernel(c_ref, w_ref, b_ref, o_ref):
    s = _silu(c_ref[...])
    o_ref[0] = jnp.dot(s, w_ref[0], precision=HIGHEST, preferred_element_type=F32) + b_ref[0]


def ada_modulation_all(cc, ada_w, ada_b):
    R = cc.shape[0]
    tn = 1536
    return pl.pallas_call(
        _ada_kernel,
        out_shape=jax.ShapeDtypeStruct((DEPTH, R, 6 * D), F32),
        grid=(DEPTH, 6 * D // tn),
        in_specs=[pl.BlockSpec((R, D), lambda l, n: (0, 0)),
                  pl.BlockSpec((1, D, tn), lambda l, n: (l, 0, n)),
                  pl.BlockSpec((1, 1, tn), lambda l, n: (l, 0, n))],
        out_specs=pl.BlockSpec((1, R, tn), lambda l, n: (l, 0, n)),
        compiler_params=_cparams(("arbitrary", "arbitrary")),
        name="ada_modulation",
    )(cc, ada_w, ada_b.reshape(DEPTH, 1, 6 * D))


_EVEN_CHUNKS = ((C_QA, C_KD, True), (C_KD, C_VD, True), (C_VD, C_VG, False), (C_VG, C_RG, False),
                (C_RG, C_QG, False), (C_QG, C_GG, False), (C_GG, P_W, False))


def _inproj_even_kernel(h_ref, mod_ref, w_ref, cos_ref, sin_ref, p_ref):
    tm = h_ref.shape[1]
    u = (h_ref[0] * (1.0 + mod_ref[0, 1:2, :]) + mod_ref[0, 0:1, :]).astype(BF16)
    cos = cos_ref[...]
    sin = sin_ref[...]
    lane = lax.broadcasted_iota(jnp.int32, (tm, LANE), 1)
    first_half = (lane % HEAD_DIM) < (HEAD_DIM // 2)
    for c0, c1, rope in _EVEN_CHUNKS:
        acc = _dot(u, w_ref[:, c0:c1])
        if rope:
            for i in range((c1 - c0) // LANE):
                x = acc[:, i * LANE:(i + 1) * LANE]
                partner = jnp.where(first_half, pltpu.roll(x, LANE - 32, 1), pltpu.roll(x, 32, 1))
                p_ref[0, :, c0 + i * LANE:c0 + (i + 1) * LANE] = (x * cos + partner * sin).astype(BF16)
        else:
            p_ref[0, :, c0:c1] = acc.astype(BF16)


def inproj_even(h, mod, w, cos_t, sin_t):
    B, T, _ = h.shape
    tm = min(IN_ROW_TILE, T)
    mb = mod.shape[0]
    return pl.pallas_call(
        _inproj_even_kernel,
        out_shape=jax.ShapeDtypeStruct((B, T, P_W), BF16),
        grid=(B, T // tm),
        in_specs=[pl.BlockSpec((1, tm, D), lambda b, j: (b, j, 0)),
                  pl.BlockSpec((1, 6, D), (lambda b, j: (b, 0, 0)) if mb > 1 else (lambda b, j: (0, 0, 0))),
                  pl.BlockSpec((D, P_W), lambda b, j: (0, 0)),
                  pl.BlockSpec((tm, LANE), lambda b, j: (j, 0)),
                  pl.BlockSpec((tm, LANE), lambda b, j: (j, 0))],
        out_specs=pl.BlockSpec((1, tm, P_W), lambda b, j: (b, j, 0)),
        compiler_params=_cparams(("parallel", "arbitrary")),
        name="inproj_even",
    )(h, mod, w, cos_t, sin_t)


def _attn_kernel(*refs, tq, tiles, has_window):
    if has_window:
        sink_ref, q_ref, kw_ref, vw_ref, kc_ref, vc_ref, o_ref = refs
    else:
        sink_ref, q_ref, kc_ref, vc_ref, o_ref = refs
    group = A_Q_HEADS // A_KV_HEADS
    rows = group * tq
    lo = lax.broadcasted_iota(jnp.int32, (tq, LANE), 1) < HEAD_DIM
    den_lanes = lax.broadcasted_iota(jnp.int32, (rows, LANE), 1) >= HEAD_DIM

    def with_ones(v):
        return jnp.where(lax.broadcasted_iota(jnp.int32, v.shape, 1) < HEAD_DIM, v, jnp.ones_like(v))

    wstart, band = [], []
    if has_window:
        S = kw_ref.shape[1]
        wk = tq + 2 * WINDOW
        for t in range(tiles):
            q0 = (pl.program_id(1) * tiles + t) * tq
            wstart.append(pl.multiple_of(jnp.clip(q0 - WINDOW, 0, S - wk), LANE))
            qpos = q0 + lax.broadcasted_iota(jnp.int32, (tq, wk), 0)
            kpos = wstart[t] + lax.broadcasted_iota(jnp.int32, (tq, wk), 1)
            band.append(jnp.tile(jnp.where(jnp.abs(qpos - kpos) <= WINDOW, 0.0, NEG), (group, 1)))
    chains = [(t, g) for t in range(tiles) for g in range(A_KV_HEADS)]
    rt = [slice(t * tq, (t + 1) * tq) for t in range(tiles)]
    cols = [slice(g * LANE, (g + 1) * LANE) for g in range(A_KV_HEADS)]
    q4, snk, sc, m, outs = [], [], [], [], []
    for t, g in chains:
        qs = []
        for pr in range(group // 2):
            qblk = q_ref[0, rt[t], (2 * g + pr) * LANE:(2 * g + pr + 1) * LANE]
            zero = jnp.zeros_like(qblk)
            qs += [jnp.where(lo, qblk, zero), jnp.where(lo, zero, qblk)]
        q4.append(jnp.concatenate(qs, axis=0))
        snk.append(jnp.concatenate([jnp.full((tq, 1), sink_ref[group * g + i] * LOG2_E, F32) for i in range(group)],
                                   axis=0))
    for c, (t, g) in enumerate(chains):
        if has_window:
            keys = jnp.concatenate([kw_ref[0, pl.ds(wstart[t], wk), cols[g]], kc_ref[0, :, cols[g]]], axis=0)
            s = _dot_nt(q4[c], keys)
            sc.append(jnp.concatenate([s[:, 0:wk] + band[t], s[:, wk:]], axis=1))
        else:
            sc.append(_dot_nt(q4[c], kc_ref[0, :, cols[g]]))
    for c in range(len(chains)):
        m.append(jnp.maximum(jnp.max(sc[c], axis=-1, keepdims=True), snk[c]))
    for c, (t, g) in enumerate(chains):
        if has_window:
            vals = jnp.concatenate([vw_ref[0, pl.ds(wstart[t], wk), cols[g]], vc_ref[0, :, cols[g]]], axis=0)
        else:
            vals = vc_ref[0, :, cols[g]]
        o = _dot(jnp.exp2((sc[c] - m[c]).astype(BF16)), with_ones(vals))
        outs.append(o + jnp.where(den_lanes, jnp.exp2(snk[c] - m[c]), 0.0))
    for c, (t, g) in enumerate(chains):
        o = outs[c]
        swapped = pltpu.roll(o, HEAD_DIM, 1)
        for pr in range(group // 2):
            ev = slice(2 * pr * tq, (2 * pr + 1) * tq)
            od = slice((2 * pr + 1) * tq, (2 * pr + 2) * tq)
            res = jnp.where(lo, o[ev] / swapped[ev], swapped[od] / o[od])
            o_ref[0, rt[t], (2 * g + pr) * LANE:(2 * g + pr + 1) * LANE] = res.astype(BF16)


def attention(p_q, p_ctx, sink, has_window):
    B, T, _ = p_q.shape
    L = p_ctx.shape[1]
    tq = 128
    tiles = next(n for n in (4, 2, 1) if T % (n * tq) == 0)
    in_specs = [pl.BlockSpec(memory_space=pltpu.SMEM),
                pl.BlockSpec((1, tiles * tq, W_AQ), lambda b, j: (b, j, C_QA // W_AQ))]
    args = [sink, p_q]
    if has_window:
        in_specs += [pl.BlockSpec((1, T, W_KV2), lambda b, j: (b, 0, C_KD // W_KV2)),
                     pl.BlockSpec((1, T, W_KV2), lambda b, j: (b, 0, C_VD // W_KV2))]
        args += [p_q, p_q]
    in_specs += [pl.BlockSpec((1, L, W_KV2), lambda b, j: (b, 0, C_KD // W_KV2)),
                 pl.BlockSpec((1, L, W_KV2), lambda b, j: (b, 0, C_VD // W_KV2))]
    args += [p_ctx, p_ctx]
    return pl.pallas_call(
        functools.partial(_attn_kernel, tq=tq, tiles=tiles, has_window=has_window),
        out_shape=jax.ShapeDtypeStruct((B, T, W_AQ), BF16),
        grid=(B, T // (tiles * tq)),
        in_specs=in_specs,
        out_specs=pl.BlockSpec((1, tiles * tq, W_AQ), lambda b, j: (b, j, 0)),
        compiler_params=_cparams(("parallel", "arbitrary")),
        name="window_attention" if has_window else "context_attention",
    )(*args)


def _log_sigmoid(x):
    return jnp.minimum(x, 0.0) - jnp.log(1.0 + jnp.exp(-jnp.abs(x)))


def _gla_kernel(qf_ref, kf_ref, vf_ref, gf_ref, qb_ref, kb_ref, vb_ref, gb_ref, wa_ref, ba_ref, s0_ref,
                of_ref, ob_ref, sfin_ref, s_sc):
    j = pl.program_id(1)
    nblk = pl.num_programs(1)
    tb = qf_ref.shape[1]
    nc = tb // GLA_CHUNK

    @pl.when(j == 0)
    def _():
        s_sc[...] = s0_ref[0]

    C = GLA_CHUNK
    hc = GLA_HEADS * C
    ri = lax.broadcasted_iota(jnp.int32, (hc, hc), 0) % C
    ci = lax.broadcasted_iota(jnp.int32, (hc, hc), 1) % C
    rb = lax.broadcasted_iota(jnp.int32, (tb, tb), 0)
    cb = lax.broadcasted_iota(jnp.int32, (tb, tb), 1)
    same_chunk = (rb // C) == (cb // C)
    lane_head = lax.broadcasted_iota(jnp.int32, (C, GLA_HEADS * GLA_DK), 1) // GLA_DK

    def per_head(x):
        zero = jnp.zeros_like(x)
        return jnp.concatenate([jnp.where(lane_head == h, x, zero) for h in range(GLA_HEADS)], axis=0)

    io = ((qf_ref, kf_ref, vf_ref, gf_ref, of_ref), (qb_ref, kb_ref, vb_ref, gb_ref, ob_ref))
    causal = ((ri >= ci), (ci >= ri))
    tri = (jnp.logical_and(same_chunk, rb >= cb).astype(BF16), jnp.logical_and(same_chunk, cb >= rb).astype(BF16))
    b_all = []
    for d in range(2):
        g = _dot(io[d][3][0, :, d * LANE:(d + 1) * LANE], wa_ref[d]) + ba_ref[d]
        log_a = _log_sigmoid(g) / GLA_TAU
        la1 = log_a.astype(BF16)
        rem = log_a - la1.astype(F32)
        la2 = rem.astype(BF16)
        la3 = (rem - la2.astype(F32)).astype(BF16)
        b_all.append(_dot(tri[d], la1) + _dot(tri[d], la2) + _dot(tri[d], la3))
    state = [jnp.concatenate([s_sc[d, 0], s_sc[d, 1]], axis=1) for d in range(2)]
    zero_blk = jnp.zeros((GLA_DV, LANE), BF16)
    for step in range(nc):
        for d in range(2):
            q_ref, k_ref, v_ref, _, o_ref = io[d]
            c = step if d == 0 else nc - 1 - step
            rows = slice(c * C, (c + 1) * C)
            b = b_all[d][rows]
            b_last = b[C - 1:C] if d == 0 else b[0:1]
            qc = q_ref[0, rows, :].astype(F32)
            kc = k_ref[0, rows, :].astype(F32)
            q4 = per_head((qc * jnp.exp(b)).astype(BF16))
            k4 = per_head((kc * jnp.exp(-b)).astype(BF16))
            ks4 = per_head((kc * jnp.exp(b_last - b)).astype(BF16))
            st = state[d]
            stb = st.astype(BF16)
            st_bd = jnp.concatenate([jnp.concatenate([stb[:, 0:LANE], zero_blk], axis=1),
                                     jnp.concatenate([zero_blk, stb[:, LANE:]], axis=1)], axis=0)
            res = _dot_nt(q4, jnp.concatenate([st_bd, k4], axis=0))
            attn = jnp.where(causal[d], res[:, 2 * GLA_DV:], 0.0).astype(BF16)
            v4 = jnp.concatenate([v_ref[0, rows, h * GLA_DV:(h + 1) * GLA_DV] for h in range(GLA_HEADS)], axis=0)
            o4 = _dot(attn, v4)
            for h in range(GLA_HEADS):
                hr = slice(h * C, (h + 1) * C)
                inter = res[hr, (h // 2) * GLA_DV:(h // 2 + 1) * GLA_DV]
                o_ref[0, rows, h * GLA_DV:(h + 1) * GLA_DV] = (o4[hr] + inter).astype(o_ref.dtype)
            state[d] = st * jnp.exp(b_last) + _dot_tn(v4, ks4)
    for d in range(2):
        for pair in range(2):
            s_sc[d, pair] = state[d][:, pair * LANE:(pair + 1) * LANE]

    @pl.when(j == nblk - 1)
    def _():
        sfin_ref[0] = s_sc[...]


def gla_scan(p, wa_p, ba, s0):
    B, T, _ = p.shape
    tb = min(ROW_TILE, T)
    nblk = T // tb
    fwd = lambda b, j: (b, j)
    bwd = lambda b, j: (b, nblk - 1 - j)

    def specs(im):
        return [pl.BlockSpec((1, tb, W_GK), lambda b, j: im(b, j) + (C_QG // W_GK,)),
                pl.BlockSpec((1, tb, W_GK), lambda b, j: im(b, j) + (C_KG // W_GK,)),
                pl.BlockSpec((1, tb, W_GV), lambda b, j: im(b, j) + (C_VG // W_GV,)),
                pl.BlockSpec((1, tb, 2 * LANE), lambda b, j: im(b, j) + (C_GG // (2 * LANE),))]

    return pl.pallas_call(
        _gla_kernel,
        out_shape=(jax.ShapeDtypeStruct((B, T, W_GV), BF16), jax.ShapeDtypeStruct((B, T, W_GV), BF16),
                   jax.ShapeDtypeStruct(s0.shape, F32)),
        grid=(B, nblk),
        in_specs=specs(fwd) + specs(bwd) + [
            pl.BlockSpec((2, LANE, W_GK), lambda b, j: (0, 0, 0)),
            pl.BlockSpec((2, 1, W_GK), lambda b, j: (0, 0, 0)),
            pl.BlockSpec((1, 2, 2, GLA_DV, LANE), lambda b, j: (b, 0, 0, 0, 0))],
        out_specs=(pl.BlockSpec((1, tb, W_GV), lambda b, j: (b, j, 0)),
                   pl.BlockSpec((1, tb, W_GV), lambda b, j: (b, nblk - 1 - j, 0)),
                   pl.BlockSpec((1, 2, 2, GLA_DV, LANE), lambda b, j: (b, 0, 0, 0, 0))),
        scratch_shapes=[pltpu.VMEM((2, 2, GLA_DV, LANE), F32)],
        compiler_params=_cparams(("parallel", "arbitrary")),
        name="gla_scan",
    )(p, p, p, p, p, p, p, p, wa_p, ba, s0)


def _deepnorm(h, gate, y, w, b):
    r = h + (gate * (1.0 / ALPHA)) * y
    mu = jnp.mean(r, axis=-1, keepdims=True)
    xc = r - mu
    var = jnp.mean(xc * xc, axis=-1, keepdims=True)
    return xc * lax.rsqrt(var + LN_EPS / (ALPHA * ALPHA)) * w + b


def _pack_rounded_pairs(xr):
    half = xr.shape[1] // 2
    lo = lax.bitcast_convert_type(xr[:, :half], jnp.uint32)
    hi = lax.bitcast_convert_type(xr[:, half:], jnp.uint32)
    return jnp.bitwise_or(hi, lax.shift_right_logical(lo, jnp.uint32(16)))


def _pack_bf16_pairs(x):
    return _pack_rounded_pairs(x.astype(BF16).astype(F32))


def _unpack_bf16_pairs(p):
    lo = lax.bitcast_convert_type(lax.shift_left(p, jnp.uint32(16)), F32)
    hi = lax.bitcast_convert_type(jnp.bitwise_and(p, jnp.uint32(0xFFFF0000)), F32)
    return lo, hi


def _post_norm_and_route(h, y, mod_ref, lnw_ref, lnb_ref, rwh_ref, rwl_ref, rb_ref, h1_ref, tok_ref, lg_ref):
    h1 = _deepnorm(h, mod_ref[0, 2:3, :], y, lnw_ref[...], lnb_ref[...])
    h1_ref[0] = h1
    tok = h1 * (1.0 + mod_ref[0, 4:5, :]) + mod_ref[0, 3:4, :]
    hi = tok.astype(BF16)
    hi_f = hi.astype(F32)
    tok_ref[0] = _pack_rounded_pairs(hi_f)
    lo = (tok - hi_f).astype(BF16)
    lg = _dot(hi, rwh_ref[...]) + _dot(lo, rwh_ref[...]) + _dot(hi, rwl_ref[...]) + rb_ref[...]
    lg_ref[...] = lg.T[0:ROUTE_ROWS, :]


def _epilogue_specs(tm, mb):
    mod_map = (lambda b, j: (b, 0, 0)) if mb > 1 else (lambda b, j: (0, 0, 0))
    const2 = lambda b, j: (0, 0)
    return [pl.BlockSpec((1, tm, D), lambda b, j: (b, j, 0)),
            pl.BlockSpec((1, 6, D), mod_map),
            pl.BlockSpec((1, D), const2), pl.BlockSpec((1, D), const2),
            pl.BlockSpec((D, ROUTE_W), const2), pl.BlockSpec((D, ROUTE_W), const2),
            pl.BlockSpec((1, ROUTE_W), const2)]


def _epilogue_outs(B, T, tm):
    nj = T // tm
    shapes = (jax.ShapeDtypeStruct((B, T, D), F32), jax.ShapeDtypeStruct((B, T, D // 2), jnp.uint32),
              jax.ShapeDtypeStruct((ROUTE_ROWS, B * T), F32))
    specs = (pl.BlockSpec((1, tm, D), lambda b, j: (b, j, 0)), pl.BlockSpec((1, tm, D // 2), lambda b, j: (b, j, 0)),
             pl.BlockSpec((ROUTE_ROWS, tm), lambda b, j: (0, b * nj + j)))
    return shapes, specs


def _outproj_even_kernel(a_ref, of_ref, ob_ref, rg_ref, nw_ref, wo_ref,
                         h_ref, mod_ref, lnw_ref, lnb_ref, rwh_ref, rwl_ref, rb_ref,
                         h1_ref, tok_ref, lg_ref):
    parts = [a_ref[0]]
    for hd in range(GLA_HEADS):
        cols = slice(hd * GLA_DV, (hd + 1) * GLA_DV)
        o = of_ref[0, :, cols].astype(F32) + ob_ref[0, :, cols].astype(F32)
        o = o * lax.rsqrt(jnp.mean(o * o, axis=-1, keepdims=True) + RMS_EPS)
        parts.append((o * nw_ref[:, cols] * _silu(rg_ref[0, :, cols].astype(F32))).astype(BF16))
    y = _dot(jnp.concatenate(parts, axis=1), wo_ref[...])
    _post_norm_and_route(h_ref[0], y, mod_ref, lnw_ref, lnb_ref, rwh_ref, rwl_ref, rb_ref, h1_ref, tok_ref, lg_ref)


def outproj_even(a, o_f, o_b, p, norm_w, w_out, h, mod, lnw, lnb, rwh, rwl, rb):
    B, T, _ = h.shape
    tm = min(ROW_TILE, T)
    tile = lambda b, j: (b, j, 0)
    shapes, ospecs = _epilogue_outs(B, T, tm)
    return pl.pallas_call(
        _outproj_even_kernel,
        out_shape=shapes,
        grid=(B, T // tm),
        in_specs=[pl.BlockSpec((1, tm, W_AQ), tile), pl.BlockSpec((1, tm, W_GV), tile), pl.BlockSpec((1, tm, W_GV), tile),
                  pl.BlockSpec((1, tm, W_GV), lambda b, j: (b, j, C_RG // W_GV)),
                  pl.BlockSpec((1, W_GV), lambda b, j: (0, 0)),
                  pl.BlockSpec((D, D), lambda b, j: (0, 0))] + _epilogue_specs(tm, mod.shape[0]),
        out_specs=ospecs,
        compiler_params=_cparams(("parallel", "arbitrary")),
        name="outproj_even",
    )(a, o_f, o_b, p, norm_w, w_out, h, mod, lnw, lnb, rwh, rwl, rb)


def _inproj_odd_kernel(h_ref, mod_ref, w_ref, o_ref):
    u = (h_ref[0] * (1.0 + mod_ref[0, 1:2, :]) + mod_ref[0, 0:1, :]).astype(BF16)
    o_ref[0, :, 0:D] = _dot(u, w_ref[:, 0:D]).astype(BF16)
    o_ref[0, :, D:2 * D] = (_dot(u, w_ref[:, D:2 * D]) * _dot(u, w_ref[:, 2 * D:3 * D])).astype(BF16)


def inproj_odd(h, mod, w):
    B, T, _ = h.shape
    tm = min(IN_ROW_TILE, T)
    mb = mod.shape[0]
    return pl.pallas_call(
        _inproj_odd_kernel,
        out_shape=jax.ShapeDtypeStruct((B, T, 2 * D), BF16),
        grid=(B, T // tm),
        in_specs=[pl.BlockSpec((1, tm, D), lambda b, j: (b, j, 0)),
                  pl.BlockSpec((1, 6, D), (lambda b, j: (b, 0, 0)) if mb > 1 else (lambda b, j: (0, 0, 0))),
                  pl.BlockSpec((D, 3 * D), lambda b, j: (0, 0))],
        out_specs=pl.BlockSpec((1, tm, 2 * D), lambda b, j: (b, j, 0)),
        compiler_params=_cparams(("parallel", "arbitrary")),
        name="inproj_odd",
    )(h, mod, w)


HALO = 16


def _outproj_odd_kernel(gb_ref, z_ref, zp_ref, zn_ref, cw_ref, cb_ref, wo_ref,
                        h_ref, mod_ref, lnw_ref, lnb_ref, rwh_ref, rwl_ref, rb_ref,
                        h1_ref, tok_ref, lg_ref):
    j = pl.program_id(1)
    tm = z_ref.shape[1]
    z = z_ref[0].astype(F32)
    prev_row = jnp.where(j > 0, zp_ref[0, HALO - 1:HALO, :].astype(F32), 0.0)
    next_row = jnp.where(j < pl.num_programs(1) - 1, zn_ref[0, 0:1, :].astype(F32), 0.0)
    row = lax.broadcasted_iota(jnp.int32, (tm, D), 0)
    z_prev = jnp.where(row == 0, prev_row, pltpu.roll(z, 1, 0))
    z_next = jnp.where(row == tm - 1, next_row, pltpu.roll(z, tm - 1, 0))
    conv = z_prev * cw_ref[0:1, :] + z * cw_ref[1:2, :] + z_next * cw_ref[2:3, :] + cb_ref[...]
    y = _dot((gb_ref[0].astype(F32) * conv).astype(BF16), wo_ref[...])
    _post_norm_and_route(h_ref[0], y, mod_ref, lnw_ref, lnb_ref, rwh_ref, rwl_ref, rb_ref, h1_ref, tok_ref, lg_ref)


def outproj_odd(gz, conv_w, conv_b, w_out, h, mod, lnw, lnb, rwh, rwl, rb):
    B, T, _ = h.shape
    tm = min(ROW_TILE, T)
    r = tm // HALO
    nh = T // HALO
    shapes, ospecs = _epilogue_outs(B, T, tm)
    return pl.pallas_call(
        _outproj_odd_kernel,
        out_shape=shapes,
        grid=(B, T // tm),
        in_specs=[pl.BlockSpec((1, tm, D), lambda b, j: (b, j, 0)),
                  pl.BlockSpec((1, tm, D), lambda b, j: (b, j, 1)),
                  pl.BlockSpec((1, HALO, D), lambda b, j: (b, jnp.maximum(j * r - 1, 0), 1)),
                  pl.BlockSpec((1, HALO, D), lambda b, j: (b, jnp.minimum((j + 1) * r, nh - 1), 1)),
                  pl.BlockSpec((3, D), lambda b, j: (0, 0)),
                  pl.BlockSpec((1, D), lambda b, j: (0, 0)),
                  pl.BlockSpec((D, D), lambda b, j: (0, 0))] + _epilogue_specs(tm, mod.shape[0]),
        out_specs=ospecs,
        compiler_params=_cparams(("parallel", "arbitrary")),
        name="outproj_odd",
    )(gz, gz, gz, gz, conv_w, conv_b, w_out, h, mod, lnw, lnb, rwh, rwl, rb)


def _sc_mesh():
    return plsc.VectorSubcoreMesh(core_axis_name="c", subcore_axis_name="s")


def sc_gather_rows(table, idx):
    n = idx.shape[0]
    width = table.shape[1]
    per_w = n // SC_WORKERS
    n_chunks = per_w // SC_CHUNK
    assert n_chunks % 2 == 0

    @functools.partial(
        pl.kernel, mesh=_sc_mesh(),
        out_type=jax.ShapeDtypeStruct((n, width), table.dtype),
        scratch_types=[pltpu.VMEM((n_chunks, SC_CHUNK), jnp.int32),
                       pltpu.VMEM((SC_CHUNK, width), table.dtype), pltpu.VMEM((SC_CHUNK, width), table.dtype),
                       pltpu.SemaphoreType.DMA, pltpu.SemaphoreType.DMA],
    )
    def gather_kernel(table_hbm, idx_hbm, out_hbm, idx_v, buf0, buf1, sem0, sem1):
        wid = lax.axis_index("s") * SC_CORES + lax.axis_index("c")
        pltpu.sync_copy(idx_hbm.at[wid], idx_v)

        def fetch(j, buf, sem):
            return pltpu.make_async_copy(table_hbm.at[idx_v.at[j]], buf, sem)

        def flush(j, buf):
            pltpu.sync_copy(buf, out_hbm.at[pl.ds(wid * per_w + j * SC_CHUNK, SC_CHUNK)])

        fetch(0, buf0, sem0).start()

        @pl.loop(0, n_chunks, step=2)
        def _(j):
            fetch(j + 1, buf1, sem1).start()
            fetch(j, buf0, sem0).wait()
            flush(j, buf0)

            @pl.when(j + 2 < n_chunks)
            def _():
                fetch(j + 2, buf0, sem0).start()

            fetch(j + 1, buf1, sem1).wait()
            flush(j + 1, buf1)

    return gather_kernel(table, idx.reshape(SC_WORKERS, n_chunks, SC_CHUNK))


def sc_scatter_rows(srcs, idxs, n_rows):
    width, dt = srcs[0][0].shape[1], srcs[0][0].dtype
    plans, args = [], []
    for (src, row0, n_src), idx in zip(srcs, idxs):
        per_w = n_src // SC_WORKERS
        chunk = min(SC_CHUNK, per_w // 2)
        assert (per_w // chunk) % 2 == 0
        plans.append((per_w, chunk, per_w // chunk, idx.shape[0], row0))
        args += [src, idx.reshape(idx.shape[0], SC_WORKERS, per_w // chunk, chunk)]
    max_chunk = max(p[1] for p in plans)
    scratch = [pltpu.VMEM((max_chunk, width), dt), pltpu.VMEM((max_chunk, width), dt),
               pltpu.SemaphoreType.DMA, pltpu.SemaphoreType.DMA]
    scratch += [pltpu.VMEM((lists, n_chunks, chunk), jnp.int32) for _, chunk, n_chunks, lists, _ in plans]

    @functools.partial(pl.kernel, mesh=_sc_mesh(), out_type=jax.ShapeDtypeStruct((n_rows, width), dt),
                       scratch_types=scratch)
    def scatter_kernel(*refs):
        ins, out_hbm = refs[:2 * len(plans)], refs[2 * len(plans)]
        rows0, rows1, sem0, sem1 = refs[2 * len(plans) + 1:2 * len(plans) + 5]
        idx_vs = refs[2 * len(plans) + 5:]
        wid = lax.axis_index("s") * SC_CORES + lax.axis_index("c")
        for s, (per_w, chunk, n_chunks, lists, row0) in enumerate(plans):
            src_hbm, idx_hbm, idx_v = ins[2 * s], ins[2 * s + 1], idx_vs[s]
            for k in range(lists):
                pltpu.sync_copy(idx_hbm.at[k, wid], idx_v.at[k])
            buf0 = rows0 if chunk == max_chunk else rows0.at[pl.ds(0, chunk)]
            buf1 = rows1 if chunk == max_chunk else rows1.at[pl.ds(0, chunk)]

            def load(j, buf, sem, src_hbm=src_hbm, per_w=per_w, chunk=chunk, row0=row0):
                return pltpu.make_async_copy(src_hbm.at[pl.ds(row0 + wid * per_w + j * chunk, chunk)], buf, sem)

            def spread(j, buf, idx_v=idx_v, lists=lists):
                for k in range(lists):
                    pltpu.sync_copy(buf, out_hbm.at[idx_v.at[k, j]])

            load(0, buf0, sem0).start()

            @pl.loop(0, n_chunks, step=2)
            def _(j, load=load, spread=spread, buf0=buf0, buf1=buf1, n_chunks=n_chunks):
                load(j + 1, buf1, sem1).start()
                load(j, buf0, sem0).wait()
                spread(j, buf0)

                @pl.when(j + 2 < n_chunks)
                def _():
                    load(j + 2, buf0, sem0).start()

                load(j + 1, buf1, sem1).wait()
                spread(j + 1, buf1)

    return scatter_kernel(*args)


def _ffn_kernel(be_ref, nv_ref, x_ref, w1_ref, w3_ref, w2_ref, y_ref, w1b, w3b, w2b):
    i = pl.program_id(0)
    changed = jnp.logical_or(i == 0, be_ref[i] != be_ref[jnp.maximum(i - 1, 0)])

    @pl.when(changed)
    def _():
        w1b[...] = w1_ref[0, 0].astype(BF16)
        w3b[...] = w3_ref[0, 0].astype(BF16)
        w2b[...] = w2_ref[0, 0].astype(BF16)

    @pl.when(i < nv_ref[0])
    def _():
        x = jnp.concatenate([v.astype(BF16) for v in _unpack_bf16_pairs(x_ref[...])], axis=1)
        y_ref[...] = _pack_bf16_pairs(_dot((_silu(_dot(x, w1b[...])) * _dot(x, w3b[...])).astype(BF16), w2b[...]))

    @pl.when(i >= nv_ref[0])
    def _():
        y_ref[...] = jnp.zeros_like(y_ref)


def moe_ffn(xs, blk_expert, n_valid, w1, w3, w2, layer):
    n_rows = xs.shape[0]
    nb = n_rows // MOE_TM
    return pl.pallas_call(
        _ffn_kernel,
        out_shape=jax.ShapeDtypeStruct((n_rows, D // 2), jnp.uint32),
        grid_spec=pltpu.PrefetchScalarGridSpec(
            num_scalar_prefetch=2,
            grid=(nb,),
            in_specs=[pl.BlockSpec((MOE_TM, D // 2), lambda i, be, nv: (i, 0)),
                      pl.BlockSpec((1, 1, D, D_EXPERT), lambda i, be, nv: (layer, be[i], 0, 0)),
                      pl.BlockSpec((1, 1, D, D_EXPERT), lambda i, be, nv: (layer, be[i], 0, 0)),
                      pl.BlockSpec((1, 1, D_EXPERT, D), lambda i, be, nv: (layer, be[i], 0, 0))],
            out_specs=pl.BlockSpec((MOE_TM, D // 2), lambda i, be, nv: (i, 0)),
            scratch_shapes=[pltpu.VMEM((D, D_EXPERT), BF16), pltpu.VMEM((D, D_EXPERT), BF16),
                            pltpu.VMEM((D_EXPERT, D), BF16)]),
        compiler_params=_cparams(("arbitrary",)),
        name="moe_ffn",
    )(blk_expert, n_valid, xs, w1, w3, w2)


def _combine_kernel(h_ref, y0_ref, y1_ref, wt_ref, mod_ref, lnw_ref, lnb_ref, o_ref):
    half = D // 2
    lo0, hi0 = _unpack_bf16_pairs(y0_ref[0])
    lo1, hi1 = _unpack_bf16_pairs(y1_ref[0])
    pick = (lax.broadcasted_iota(jnp.int32, (WT_ROWS, LANE), 0) == lax.broadcasted_iota(jnp.int32, (WT_ROWS, LANE), 1))
    wcols = lax.dot_general(wt_ref[...], pick.astype(F32), (((0,), (0,)), ((), ())), precision=HIGHEST,
                            preferred_element_type=F32)
    w0, w1 = wcols[:, 0:1], wcols[:, 1:2]
    gate = mod_ref[0, 5:6, :] * (1.0 / ALPHA)
    r_lo = h_ref[:, 0:half] + gate[:, 0:half] * (w0 * lo0 + w1 * lo1)
    r_hi = h_ref[:, half:D] + gate[:, half:D] * (w0 * hi0 + w1 * hi1)
    mu = (jnp.sum(r_lo, axis=-1, keepdims=True) + jnp.sum(r_hi, axis=-1, keepdims=True)) * (1.0 / D)
    c_lo, c_hi = r_lo - mu, r_hi - mu
    var = (jnp.sum(c_lo * c_lo, axis=-1, keepdims=True) + jnp.sum(c_hi * c_hi, axis=-1, keepdims=True)) * (1.0 / D)
    inv = lax.rsqrt(var + LN_EPS / (ALPHA * ALPHA))
    o_ref[:, 0:half] = c_lo * inv * lnw_ref[:, 0:half] + lnb_ref[:, 0:half]
    o_ref[:, half:D] = c_hi * inv * lnw_ref[:, half:D] + lnb_ref[:, half:D]


def _combine_into_kernel(*refs):
    _combine_kernel(*refs[:7], refs[8])


def moe_combine(h1, y_rows, wts, mod, lnw, lnb, tok0, ntok, y_tok0, prev=None):
    B, T, _ = h1.shape
    N = B * T
    tm = min(ROW_TILE, ntok)
    per_b = T // tm
    h_off, y_off = tok0 // tm, y_tok0 // tm
    mod_map = (lambda i: ((i + h_off) // per_b, 0, 0)) if mod.shape[0] > 1 else (lambda i: (0, 0, 0))
    in_specs = [pl.BlockSpec((tm, D), lambda i: (i + h_off, 0)),
                pl.BlockSpec((1, tm, D // 2), lambda i: (0, i + y_off, 0)),
                pl.BlockSpec((1, tm, D // 2), lambda i: (1, i + y_off, 0)),
                pl.BlockSpec((WT_ROWS, tm), lambda i: (0, i + y_off)),
                pl.BlockSpec((1, 6, D), mod_map),
                pl.BlockSpec((1, D), lambda i: (0, 0)), pl.BlockSpec((1, D), lambda i: (0, 0))]
    args = [h1.reshape(N, D), y_rows, y_rows, wts, mod, lnw, lnb]
    if prev is not None:
        in_specs.append(pl.BlockSpec(memory_space=pl.ANY))
        args.append(prev)
    return pl.pallas_call(
        _combine_kernel if prev is None else _combine_into_kernel,
        out_shape=jax.ShapeDtypeStruct((N, D), F32),
        grid=(ntok // tm,),
        in_specs=in_specs,
        out_specs=pl.BlockSpec((tm, D), lambda i: (i + h_off, 0)),
        input_output_aliases={} if prev is None else {7: 0},
        compiler_params=_cparams(("parallel",)),
        name="moe_combine",
    )(*args)


def _route_kernel(lg_ref, dest_ref, wt_ref, cnt_ref, tri_sc, start_sc, run_sc):
    ph, i = pl.program_id(0), pl.program_id(1)
    tr = lg_ref.shape[1]

    @pl.when(jnp.logical_and(ph == 0, i == 0))
    def _():
        r = lax.broadcasted_iota(jnp.int32, (tr, tr), 0)
        c = lax.broadcasted_iota(jnp.int32, (tr, tr), 1)
        tri_sc[...] = (r < c).astype(BF16)
        start_sc[...] = jnp.zeros_like(start_sc)
        run_sc[...] = jnp.zeros_like(run_sc)

    @pl.when(jnp.logical_and(ph == 1, i == 0))
    def _():
        cnt = run_sc[...].astype(jnp.int32)
        cnt_ref[...] = cnt
        padded = jnp.bitwise_and(cnt + (MOE_TM - 1), -MOE_TM)
        row = lax.broadcasted_iota(jnp.int32, padded.shape, 0)
        acc = padded
        for s in (1, 2, 4, 8, 16):
            acc = acc + jnp.where(row >= s, pltpu.roll(acc, s, 0), 0)
        start_sc[...] = (acc - padded).astype(F32)
        run_sc[...] = jnp.zeros_like(run_sc)

    lg = lg_ref[...]
    gl = lg[N_EXPERTS:N_EXPERTS + N_GROUPS]
    gmax = jnp.max(gl, axis=0, keepdims=True)
    sub4 = lax.broadcasted_iota(jnp.int32, gl.shape, 0)
    g_sel = jnp.min(jnp.where(gl == gmax, sub4, N_GROUPS), axis=0, keepdims=True)
    p_group = 1.0 / jnp.sum(jnp.exp(gl - gmax), axis=0, keepdims=True)
    el = lg[0:EXPERTS_PER_GROUP]
    for g in range(1, N_GROUPS):
        el = jnp.where(g_sel == g, lg[g * EXPERTS_PER_GROUP:(g + 1) * EXPERTS_PER_GROUP], el)
    sub8 = lax.broadcasted_iota(jnp.int32, el.shape, 0)
    e1 = jnp.max(el, axis=0, keepdims=True)
    i1 = jnp.min(jnp.where(el == e1, sub8, EXPERTS_PER_GROUP), axis=0, keepdims=True)
    rest = jnp.where(sub8 == i1, -jnp.inf, el)
    e2 = jnp.max(rest, axis=0, keepdims=True)
    i2 = jnp.min(jnp.where(rest == e2, sub8, EXPERTS_PER_GROUP), axis=0, keepdims=True)
    den = jnp.sum(jnp.exp(el - e1), axis=0, keepdims=True)
    p1 = 1.0 / den
    p2 = jnp.exp(e2 - e1) / den
    wt_ref[...] = jnp.zeros_like(wt_ref)
    wt_ref[0:1, :] = p_group * p1 / (p1 + p2)
    wt_ref[1:2, :] = p_group * p2 / (p1 + p2)

    sub32 = lax.broadcasted_iota(jnp.int32, (N_EXPERTS, tr), 0)
    oh = [(sub32 == g_sel * EXPERTS_PER_GROUP + ix).astype(F32) for ix in (i1, i2)]
    cnt = [jnp.sum(o, axis=1, keepdims=True) for o in oh]
    @pl.when(ph == 0)
    def _():
        dest_ref[...] = jnp.zeros_like(dest_ref)

    @pl.when(ph == 1)
    def _():
        before = start_sc[:, 0:1] + run_sc[:, 0:1]
        for k in range(TOP_K):
            prior = _dot(oh[k].astype(BF16), tri_sc[...]) + before + (cnt[0] if k == 1 else 0.0)
            dest_ref[k:k + 1, :] = jnp.sum(oh[k] * prior, axis=0, keepdims=True).astype(jnp.int32)

    run_sc[...] = run_sc[...] + (cnt[0] + cnt[1])


def moe_route(logits_t, col0, N):
    tr = next(t for t in (2048, 1024, 512, 256) if N % t == 0 and col0 % t == 0)
    t0 = col0 // tr
    return pl.pallas_call(
        _route_kernel,
        out_shape=(jax.ShapeDtypeStruct((TOP_K, N), jnp.int32), jax.ShapeDtypeStruct((WT_ROWS, N), F32),
                   jax.ShapeDtypeStruct((N_EXPERTS, LANE), jnp.int32)),
        grid=(2, N // tr),
        in_specs=[pl.BlockSpec((ROUTE_ROWS, tr), lambda p, i: (0, i + t0))],
        out_specs=(pl.BlockSpec((TOP_K, tr), lambda p, i: (0, i * p)), pl.BlockSpec((WT_ROWS, tr), lambda p, i: (0, i * p)),
                   pl.BlockSpec((N_EXPERTS, LANE), lambda p, i: (0, 0))),
        scratch_shapes=[pltpu.VMEM((tr, tr), BF16), pltpu.VMEM((N_EXPERTS, LANE), F32),
                        pltpu.VMEM((N_EXPERTS, LANE), F32)],
        compiler_params=_cparams(("arbitrary", "arbitrary")),
        name="moe_route",
    )(logits_t)


def _block_tables(counts, n_assign):
    padded = (counts + MOE_TM - 1) // MOE_TM * MOE_TM
    pad_end = jnp.cumsum(padded)
    pad_start = pad_end - padded
    nb = -(-n_assign // MOE_TM) + N_EXPERTS
    blk_start = jnp.arange(nb, dtype=jnp.int32) * MOE_TM
    blk_expert = jnp.minimum(jnp.sum((pad_end[None, :] <= blk_start[:, None]).astype(jnp.int32), axis=1), N_EXPERTS - 1)
    n_valid = (pad_end[-1] // MOE_TM).astype(jnp.int32).reshape(1)
    n_fill = nb * MOE_TM - n_assign
    gap = padded - counts
    gap_end = jnp.cumsum(gap)
    k = jnp.arange(n_fill, dtype=jnp.int32)[:, None]
    sel = jnp.logical_and(k >= (gap_end - gap)[None, :], k < gap_end[None, :])
    in_gap = jnp.sum(jnp.where(sel, (pad_start + counts - (gap_end - gap))[None, :] + k, 0), axis=1)
    fill = jnp.where(k[:, 0] < gap_end[-1], in_gap, pad_end[-1] + k[:, 0] - gap_end[-1])
    return blk_expert.astype(jnp.int32), n_valid, fill.astype(jnp.int32), nb * MOE_TM


def hier_moe_and_norm(streams, w1, w3, w2, layer, lnw, lnb):
    lat = streams[0]
    ctx = streams[1] if len(streams) > 1 else None
    n_lat = lat[0].shape[0] * lat[0].shape[1]
    n_ctx = ctx[0].shape[0] * ctx[0].shape[1] if ctx else 0
    logits_t = lat[2] if ctx is None else jnp.concatenate([lat[2], ctx[2]], 1)
    tok_lat = lat[1].reshape(n_lat, D // 2)
    half = n_lat // 2
    ranges = ((0, half), (half, n_lat - half + n_ctx))
    routed = [moe_route(logits_t, c0, n) for c0, n in ranges]
    sorted_in, tables = [], []
    for (c0, n), (dest, _, counts) in zip(ranges, routed):
        blk_expert, n_valid, fill_rows, n_rows = _block_tables(counts[:, 0], TOP_K * n)
        n_lat_here = min(c0 + n, n_lat) - c0
        srcs, idxs = [(tok_lat, c0, n_lat_here)], [dest[:, :n_lat_here]]
        if n > n_lat_here:
            srcs.append((ctx[1].reshape(n_ctx, D // 2), 0, n_ctx))
            idxs.append(dest[:, n_lat_here:])
        srcs.append((jnp.zeros((fill_rows.shape[0], D // 2), jnp.uint32), 0, fill_rows.shape[0]))
        idxs.append(fill_rows.reshape(1, -1))
        sorted_in.append(sc_scatter_rows(srcs, idxs, n_rows))
        tables.append((blk_expert, n_valid, n_lat_here))
    out_lat, out_ctx = None, None
    for (c0, n), (dest, wts, _), xs, (blk_expert, n_valid, n_lat_here) in zip(ranges, routed, sorted_in, tables):
        y = moe_ffn(xs, blk_expert, n_valid, w1, w3, w2, layer)
        y_rows = sc_gather_rows(y, dest.reshape(-1)).reshape(TOP_K, n, D // 2)
        out_lat = moe_combine(lat[0], y_rows, wts, lat[3], lnw, lnb, c0, n_lat_here, 0, out_lat)
        if n > n_lat_here:
            out_ctx = moe_combine(ctx[0], y_rows, wts, ctx[3], lnw, lnb, 0, n_ctx, n_lat_here)
    outs = [out_lat.reshape(lat[0].shape)]
    if ctx:
        outs.append(out_ctx.reshape(ctx[0].shape))
    return outs


def _even_projection_weights(w):
    def rope_layout(x, heads):
        return x.reshape(D, heads, 2, 2, HEAD_DIM // 4).transpose(0, 1, 3, 2, 4).reshape(D, heads * HEAD_DIM)

    def twice(x):
        return jnp.concatenate([x[:, :HEAD_DIM], x[:, :HEAD_DIM], x[:, HEAD_DIM:], x[:, HEAD_DIM:]], axis=1)

    def lane_pad(x):
        return jnp.pad(x, ((0, 0), (0, LANE - x.shape[1])))

    qa, ka, va, qg, kg, vg, rg, gg = jnp.split(w, [512, 640, 768, 1024, 1280, 1792, 2304], axis=1)
    cols = [rope_layout(qa, A_Q_HEADS) * (HEAD_DIM ** -0.5 * LOG2_E),
            twice(rope_layout(ka, A_KV_HEADS)), twice(va), vg, rg, qg * GLA_DK ** -0.5, kg,
            lane_pad(gg[:, :GLA_RANK]), lane_pad(gg[:, GLA_RANK:])]
    return jnp.concatenate(cols, axis=1).astype(BF16)


def _rope_tables(S):
    row = jnp.repeat(jnp.arange(S // GRID_W), GRID_W).astype(F32)
    col = jnp.tile(jnp.arange(GRID_W), S // GRID_W).astype(F32)
    axis_dim = HEAD_DIM // 2
    inv_freq = ROPE_BASE ** (-jnp.arange(0, axis_dim, 2, dtype=F32) / axis_dim)
    ang = jnp.concatenate([row[:, None] * inv_freq, col[:, None] * inv_freq], -1)
    cos, sin = jnp.cos(ang), jnp.sin(ang)
    cos_t = jnp.tile(cos, (1, 4))
    sin_t = jnp.tile(jnp.concatenate([-sin, sin], -1), (1, 2))
    return cos_t, sin_t


def _router_weights(wg, bg, we, be):
    w = jnp.zeros((D, ROUTE_W), F32).at[:, :N_EXPERTS].set(we).at[:, N_EXPERTS:N_EXPERTS + N_GROUPS].set(wg)
    b = jnp.zeros((1, ROUTE_W), F32).at[0, :N_EXPERTS].set(be).at[0, N_EXPERTS:N_EXPERTS + N_GROUPS].set(bg)
    hi = w.astype(BF16)
    return hi, (w - hi.astype(F32)).astype(BF16), b


def kernel(x, c, ctx, c_ctx, w_in_even, w_out_even, attn_sink, gla_wa2, gla_ba, gla_norm_w, w_in_odd, conv_w, conv_b, w_out_odd, ada_w, ada_b, ln_w, ln_b, router_wg, router_bg, router_we, router_be, moe_w1, moe_w3, moe_w2):
    B, S, _ = x.shape
    L = ctx.shape[1]
    cos_t, sin_t = _rope_tables(S)
    cos_c, sin_c = jnp.ones((L, LANE), F32), jnp.zeros((L, LANE), F32)

    n_cond = -(-(B + 1) // 8) * 8
    cc = jnp.zeros((n_cond, D), F32).at[:B].set(c).at[B].set(c_ctx)
    mods = ada_modulation_all(cc, ada_w, ada_b).reshape(DEPTH, n_cond, 6, D)

    h_lat, h_ctx = x, ctx
    for l in range(DEPTH):
        i = l // 2
        need_ctx = any(j % 2 == 0 for j in range(l + 1, DEPTH))
        m_lat = mods[l, :B]
        m_ctx = mods[l, B:B + 1]
        lnw0, lnb0 = ln_w[l, 0:1], ln_b[l, 0:1]
        lnw1, lnb1 = ln_w[l, 1:2], ln_b[l, 1:2]
        rwh, rwl, rb = _router_weights(router_wg[l], router_bg[l], router_we[l], router_be[l])
        streams = []
        if l % 2 == 0:
            w_in = _even_projection_weights(w_in_even[i])
            w_out = w_out_even[i].astype(BF16)
            wa_p = jnp.zeros((2, LANE, GLA_HEADS * GLA_DK), F32).at[:, :GLA_RANK].set(gla_wa2[i]).astype(BF16)
            ba = gla_ba[i].reshape(2, 1, -1)
            nw = gla_norm_w[i].reshape(1, -1)
            p_ctx = inproj_even(h_ctx, m_ctx, w_in, cos_c, sin_c)
            p_lat = inproj_even(h_lat, m_lat, w_in, cos_t, sin_t)
            a_lat = attention(p_lat, p_ctx, attn_sink[i], True)
            s0 = jnp.zeros((B, 2, 2, GLA_DV, LANE), F32)
            oc_f, oc_b, s_ctx = gla_scan(p_ctx, wa_p, ba, s0)
            ol_f, ol_b, _ = gla_scan(p_lat, wa_p, ba, s_ctx)
            streams.append(outproj_even(a_lat, ol_f, ol_b, p_lat, nw, w_out, h_lat, m_lat, lnw0, lnb0, rwh, rwl, rb)
                           + (m_lat,))
            if need_ctx:
                a_ctx = attention(p_ctx, p_ctx, attn_sink[i], False)
                streams.append(outproj_even(a_ctx, oc_f, oc_b, p_ctx, nw, w_out, h_ctx, m_ctx, lnw0, lnb0,
                                            rwh, rwl, rb) + (m_ctx,))
        else:
            w_in = w_in_odd[i].astype(BF16)
            w_out = w_out_odd[i].astype(BF16)
            cb = conv_b[i].reshape(1, D)
            pairs = [(h_lat, m_lat)] + ([(h_ctx, m_ctx)] if need_ctx else [])
            for h, m in pairs:
                gz = inproj_odd(h, m, w_in)
                streams.append(outproj_odd(gz, conv_w[i], cb, w_out, h, m, lnw0, lnb0, rwh, rwl, rb) + (m,))
        outs = hier_moe_and_norm(streams, moe_w1, moe_w3, moe_w2, l, lnw1, lnb1)
        h_lat = outs[0]
        if need_ctx:
            h_ctx = outs[1]
    return h_lat
```

```python
import functools

import jax
import jax.numpy as jnp
from jax import lax
from jax.experimental import pallas as pl
from jax.experimental.pallas import tpu as pltpu
from jax.experimental.pallas import tpu_sc as plsc

F32 = jnp.float32
BF16 = jnp.bfloat16
HIGHEST = lax.Precision.HIGHEST

D = 1024
DEPTH = 4
GRID_W = 64
HEAD_DIM = 64
A_Q_HEADS = 8
A_KV_HEADS = 2
WINDOW = 128
ROPE_BASE = 10000.0
GLA_HEADS = 4
GLA_DK = 64
GLA_DV = 128
GLA_RANK = 16
GLA_TAU = 16.0
GLA_CHUNK = 64
N_GROUPS = 4
EXPERTS_PER_GROUP = 8
N_EXPERTS = 32
TOP_K = 2
D_EXPERT = 512
ALPHA = (2.0 * DEPTH) ** 0.25
LN_EPS = 1e-5
RMS_EPS = 1e-6

LANE = 128
VMEM_LIMIT = 48 * 1024 * 1024

ROW_TILE = 512
IN_ROW_TILE = 1024
W_AQ = A_Q_HEADS * HEAD_DIM
W_KV2 = 2 * A_KV_HEADS * HEAD_DIM
W_GK = GLA_HEADS * GLA_DK
W_GV = GLA_HEADS * GLA_DV
C_QA = 0
C_KD = C_QA + W_AQ
C_VD = C_KD + W_KV2
C_VG = C_VD + W_KV2
C_RG = C_VG + W_GV
C_QG = C_RG + W_GV
C_KG = C_QG + W_GK
C_GG = C_KG + W_GK
P_W = C_GG + 2 * LANE
ROUTE_W = 128
WT_ROWS = 8
ROUTE_ROWS = 40
MOE_TM = 512
SC_CORES, SC_SUBCORES = 2, 16
SC_WORKERS = SC_CORES * SC_SUBCORES
SC_CHUNK = 64
NEG = -1e30
LOG2_E = 1.4426950408889634


def _cparams(sem):
    return pltpu.CompilerParams(dimension_semantics=sem, vmem_limit_bytes=VMEM_LIMIT)


def _dot(a, b):
    return jnp.dot(a, b, preferred_element_type=F32)


def _dot_nt(a, b):
    return lax.dot_general(a, b, (((1,), (1,)), ((), ())), preferred_element_type=F32)


def _dot_tn(a, b):
    return lax.dot_general(a, b, (((0,), (0,)), ((), ())), preferred_element_type=F32)


def _silu(x):
    return x * (1.0 / (1.0 + jnp.exp(-x)))


def _ada_kernel(c_ref, w_ref, b_ref, o_ref):
    s = _silu(c_ref[...])
    o_ref[0] = jnp.dot(s, w_ref[0], precision=HIGHEST, preferred_element_type=F32) + b_ref[0]


def ada_modulation_all(cc, ada_w, ada_b):
    R = cc.shape[0]
    tn = 1536
    return pl.pallas_call(
        _ada_kernel,
        out_shape=jax.ShapeDtypeStruct((DEPTH, R, 6 * D), F32),
        grid=(DEPTH, 6 * D // tn),
        in_specs=[pl.BlockSpec((R, D), lambda l, n: (0, 0)),
                  pl.BlockSpec((1, D, tn), lambda l, n: (l, 0, n)),
                  pl.BlockSpec((1, 1, tn), lambda l, n: (l, 0, n))],
        out_specs=pl.BlockSpec((1, R, tn), lambda l, n: (l, 0, n)),
        compiler_params=_cparams(("arbitrary", "arbitrary")),
        name="ada_modulation",
    )(cc, ada_w, ada_b.reshape(DEPTH, 1, 6 * D))


_EVEN_CHUNKS = ((C_QA, C_KD, True), (C_KD, C_VD, True), (C_VD, C_VG, False), (C_VG, C_RG, False),
                (C_RG, C_QG, False), (C_QG, C_GG, False), (C_GG, P_W, False))


def _inproj_even_kernel(h_ref, mod_ref, w_ref, cos_ref, sin_ref, p_ref):
    tm = h_ref.shape[1]
    u = (h_ref[0] * (1.0 + mod_ref[0, 1:2, :]) + mod_ref[0, 0:1, :]).astype(BF16)
    cos = cos_ref[...]
    sin = sin_ref[...]
    lane = lax.broadcasted_iota(jnp.int32, (tm, LANE), 1)
    first_half = (lane % HEAD_DIM) < (HEAD_DIM // 2)
    for c0, c1, rope in _EVEN_CHUNKS:
        acc = _dot(u, w_ref[:, c0:c1])
        if rope:
            for i in range((c1 - c0) // LANE):
                x = acc[:, i * LANE:(i + 1) * LANE]
                partner = jnp.where(first_half, pltpu.roll(x, LANE - 32, 1), pltpu.roll(x, 32, 1))
                p_ref[0, :, c0 + i * LANE:c0 + (i + 1) * LANE] = (x * cos + partner * sin).astype(BF16)
        else:
            p_ref[0, :, c0:c1] = acc.astype(BF16)


def inproj_even(h, mod, w, cos_t, sin_t):
    B, T, _ = h.shape
    tm = min(IN_ROW_TILE, T)
    mb = mod.shape[0]
    return pl.pallas_call(
        _inproj_even_kernel,
        out_shape=jax.ShapeDtypeStruct((B, T, P_W), BF16),
        grid=(B, T // tm),
        in_specs=[pl.BlockSpec((1, tm, D), lambda b, j: (b, j, 0)),
                  pl.BlockSpec((1, 6, D), (lambda b, j: (b, 0, 0)) if mb > 1 else (lambda b, j: (0, 0, 0))),
                  pl.BlockSpec((D, P_W), lambda b, j: (0, 0)),
                  pl.BlockSpec((tm, LANE), lambda b, j: (j, 0)),
                  pl.BlockSpec((tm, LANE), lambda b, j: (j, 0))],
        out_specs=pl.BlockSpec((1, tm, P_W), lambda b, j: (b, j, 0)),
        compiler_params=_cparams(("parallel", "arbitrary")),
        name="inproj_even",
    )(h, mod, w, cos_t, sin_t)


def _attn_kernel(*refs, tq, tiles, has_window):
    if has_window:
        sink_ref, q_ref, kw_ref, vw_ref, kc_ref, vc_ref, o_ref = refs
    else:
        sink_ref, q_ref, kc_ref, vc_ref, o_ref = refs
    group = A_Q_HEADS // A_KV_HEADS
    rows = group * tq
    lo = lax.broadcasted_iota(jnp.int32, (tq, LANE), 1) < HEAD_DIM
    den_lanes = lax.broadcasted_iota(jnp.int32, (rows, LANE), 1) >= HEAD_DIM

    def with_ones(v):
        return jnp.where(lax.broadcasted_iota(jnp.int32, v.shape, 1) < HEAD_DIM, v, jnp.ones_like(v))

    wstart, band = [], []
    if has_window:
        S = kw_ref.shape[1]
        wk = tq + 2 * WINDOW
        for t in range(tiles):
            q0 = (pl.program_id(1) * tiles + t) * tq
            wstart.append(pl.multiple_of(jnp.clip(q0 - WINDOW, 0, S - wk), LANE))
            qpos = q0 + lax.broadcasted_iota(jnp.int32, (tq, wk), 0)
            kpos = wstart[t] + lax.broadcasted_iota(jnp.int32, (tq, wk), 1)
            band.append(jnp.tile(jnp.where(jnp.abs(qpos - kpos) <= WINDOW, 0.0, NEG), (group, 1)))
    chains = [(t, g) for t in range(tiles) for g in range(A_KV_HEADS)]
    rt = [slice(t * tq, (t + 1) * tq) for t in range(tiles)]
    cols = [slice(g * LANE, (g + 1) * LANE) for g in range(A_KV_HEADS)]
    q4, snk, sc, m, outs = [], [], [], [], []
    for t, g in chains:
        qs = []
        for pr in range(group // 2):
            qblk = q_ref[0, rt[t], (2 * g + pr) * LANE:(2 * g + pr + 1) * LANE]
            zero = jnp.zeros_like(qblk)
            qs += [jnp.where(lo, qblk, zero), jnp.where(lo, zero, qblk)]
        q4.append(jnp.concatenate(qs, axis=0))
        snk.append(jnp.concatenate([jnp.full((tq, 1), sink_ref[group * g + i] * LOG2_E, F32) for i in range(group)],
                                   axis=0))
    for c, (t, g) in enumerate(chains):
        if has_window:
            keys = jnp.concatenate([kw_ref[0, pl.ds(wstart[t], wk), cols[g]], kc_ref[0, :, cols[g]]], axis=0)
            s = _dot_nt(q4[c], keys)
            sc.append(jnp.concatenate([s[:, 0:wk] + band[t], s[:, wk:]], axis=1))
        else:
            sc.append(_dot_nt(q4[c], kc_ref[0, :, cols[g]]))
    for c in range(len(chains)):
        m.append(jnp.maximum(jnp.max(sc[c], axis=-1, keepdims=True), snk[c]))
    for c, (t, g) in enumerate(chains):
        if has_window:
            vals = jnp.concatenate([vw_ref[0, pl.ds(wstart[t], wk), cols[g]], vc_ref[0, :, cols[g]]], axis=0)
        else:
            vals = vc_ref[0, :, cols[g]]
        o = _dot(jnp.exp2((sc[c] - m[c]).astype(BF16)), with_ones(vals))
        outs.append(o + jnp.where(den_lanes, jnp.exp2(snk[c] - m[c]), 0.0))
    for c, (t, g) in enumerate(chains):
        o = outs[c]
        swapped = pltpu.roll(o, HEAD_DIM, 1)
        for pr in range(group // 2):
            ev = slice(2 * pr * tq, (2 * pr + 1) * tq)
            od = slice((2 * pr + 1) * tq, (2 * pr + 2) * tq)
            res = jnp.where(lo, o[ev] / swapped[ev], swapped[od] / o[od])
            o_ref[0, rt[t], (2 * g + pr) * LANE:(2 * g + pr + 1) * LANE] = res.astype(BF16)


def attention(p_q, p_ctx, sink, has_window):
    B, T, _ = p_q.shape
    L = p_ctx.shape[1]
    tq = 128
    tiles = next(n for n in (4, 2, 1) if T % (n * tq) == 0)
    in_specs = [pl.BlockSpec(memory_space=pltpu.SMEM),
                pl.BlockSpec((1, tiles * tq, W_AQ), lambda b, j: (b, j, C_QA // W_AQ))]
    args = [sink, p_q]
    if has_window:
        in_specs += [pl.BlockSpec((1, T, W_KV2), lambda b, j: (b, 0, C_KD // W_KV2)),
                     pl.BlockSpec((1, T, W_KV2), lambda b, j: (b, 0, C_VD // W_KV2))]
        args += [p_q, p_q]
    in_specs += [pl.BlockSpec((1, L, W_KV2), lambda b, j: (b, 0, C_KD // W_KV2)),
                 pl.BlockSpec((1, L, W_KV2), lambda b, j: (b, 0, C_VD // W_KV2))]
    args += [p_ctx, p_ctx]
    return pl.pallas_call(
        functools.partial(_attn_kernel, tq=tq, tiles=tiles, has_window=has_window),
        out_shape=jax.ShapeDtypeStruct((B, T, W_AQ), BF16),
        grid=(B, T // (tiles * tq)),
        in_specs=in_specs,
        out_specs=pl.BlockSpec((1, tiles * tq, W_AQ), lambda b, j: (b, j, 0)),
        compiler_params=_cparams(("parallel", "arbitrary")),
        name="window_attention" if has_window else "context_attention",
    )(*args)


def _log_sigmoid(x):
    return jnp.minimum(x, 0.0) - jnp.log(1.0 + jnp.exp(-jnp.abs(x)))


def _gla_kernel(qf_ref, kf_ref, vf_ref, gf_ref, qb_ref, kb_ref, vb_ref, gb_ref, wa_ref, ba_ref, s0_ref,
                of_ref, ob_ref, sfin_ref, s_sc):
    j = pl.program_id(1)
    nblk = pl.num_programs(1)
    tb = qf_ref.shape[1]
    nc = tb // GLA_CHUNK

    @pl.when(j == 0)
    def _():
        s_sc[...] = s0_ref[0]

    C = GLA_CHUNK
    hc = GLA_HEADS * C
    ri = lax.broadcasted_iota(jnp.int32, (hc, hc), 0) % C
    ci = lax.broadcasted_iota(jnp.int32, (hc, hc), 1) % C
    rb = lax.broadcasted_iota(jnp.int32, (tb, tb), 0)
    cb = lax.broadcasted_iota(jnp.int32, (tb, tb), 1)
    same_chunk = (rb // C) == (cb // C)
    lane_head = lax.broadcasted_iota(jnp.int32, (C, GLA_HEADS * GLA_DK), 1) // GLA_DK

    def per_head(x):
        zero = jnp.zeros_like(x)
        return jnp.concatenate([jnp.where(lane_head == h, x, zero) for h in range(GLA_HEADS)], axis=0)

    io = ((qf_ref, kf_ref, vf_ref, gf_ref, of_ref), (qb_ref, kb_ref, vb_ref, gb_ref, ob_ref))
    causal = ((ri >= ci), (ci >= ri))
    tri = (jnp.logical_and(same_chunk, rb >= cb).astype(BF16), jnp.logical_and(same_chunk, cb >= rb).astype(BF16))
    b_all = []
    for d in range(2):
        g = _dot(io[d][3][0, :, d * LANE:(d + 1) * LANE], wa_ref[d]) + ba_ref[d]
        log_a = _log_sigmoid(g) / GLA_TAU
        la1 = log_a.astype(BF16)
        rem = log_a - la1.astype(F32)
        la2 = rem.astype(BF16)
        la3 = (rem - la2.astype(F32)).astype(BF16)
        b_all.append(_dot(tri[d], la1) + _dot(tri[d], la2) + _dot(tri[d], la3))
    state = [jnp.concatenate([s_sc[d, 0], s_sc[d, 1]], axis=1) for d in range(2)]
    zero_blk = jnp.zeros((GLA_DV, LANE), BF16)
    for step in range(nc):
        for d in range(2):
            q_ref, k_ref, v_ref, _, o_ref = io[d]
            c = step if d == 0 else nc - 1 - step
            rows = slice(c * C, (c + 1) * C)
            b = b_all[d][rows]
            b_last = b[C - 1:C] if d == 0 else b[0:1]
            qc = q_ref[0, rows, :].astype(F32)
            kc = k_ref[0, rows, :].astype(F32)
            q4 = per_head((qc * jnp.exp(b)).astype(BF16))
            k4 = per_head((kc * jnp.exp(-b)).astype(BF16))
            ks4 = per_head((kc * jnp.exp(b_last - b)).astype(BF16))
            st = state[d]
            stb = st.astype(BF16)
            st_bd = jnp.concatenate([jnp.concatenate([stb[:, 0:LANE], zero_blk], axis=1),
                                     jnp.concatenate([zero_blk, stb[:, LANE:]], axis=1)], axis=0)
            res = _dot_nt(q4, jnp.concatenate([st_bd, k4], axis=0))
            attn = jnp.where(causal[d], res[:, 2 * GLA_DV:], 0.0).astype(BF16)
            v4 = jnp.concatenate([v_ref[0, rows, h * GLA_DV:(h + 1) * GLA_DV] for h in range(GLA_HEADS)], axis=0)
            o4 = _dot(attn, v4)
            for h in range(GLA_HEADS):
                hr = slice(h * C, (h + 1) * C)
                inter = res[hr, (h // 2) * GLA_DV:(h // 2 + 1) * GLA_DV]
                o_ref[0, rows, h * GLA_DV:(h + 1) * GLA_DV] = (o4[hr] + inter).astype(o_ref.dtype)
            state[d] = st * jnp.exp(b_last) + _dot_tn(v4, ks4)
    for d in range(2):
        for pair in range(2):
            s_sc[d, pair] = state[d][:, pair * LANE:(pair + 1) * LANE]

    @pl.when(j == nblk - 1)
    def _():
        sfin_ref[0] = s_sc[...]


def gla_scan(p, wa_p, ba, s0):
    B, T, _ = p.shape
    tb = min(ROW_TILE, T)
    nblk = T // tb
    fwd = lambda b, j: (b, j)
    bwd = lambda b, j: (b, nblk - 1 - j)

    def specs(im):
        return [pl.BlockSpec((1, tb, W_GK), lambda b, j: im(b, j) + (C_QG // W_GK,)),
                pl.BlockSpec((1, tb, W_GK), lambda b, j: im(b, j) + (C_KG // W_GK,)),
                pl.BlockSpec((1, tb, W_GV), lambda b, j: im(b, j) + (C_VG // W_GV,)),
                pl.BlockSpec((1, tb, 2 * LANE), lambda b, j: im(b, j) + (C_GG // (2 * LANE),))]

    return pl.pallas_call(
        _gla_kernel,
        out_shape=(jax.ShapeDtypeStruct((B, T, W_GV), BF16), jax.ShapeDtypeStruct((B, T, W_GV), BF16),
                   jax.ShapeDtypeStruct(s0.shape, F32)),
        grid=(B, nblk),
        in_specs=specs(fwd) + specs(bwd) + [
            pl.BlockSpec((2, LANE, W_GK), lambda b, j: (0, 0, 0)),
            pl.BlockSpec((2, 1, W_GK), lambda b, j: (0, 0, 0)),
            pl.BlockSpec((1, 2, 2, GLA_DV, LANE), lambda b, j: (b, 0, 0, 0, 0))],
        out_specs=(pl.BlockSpec((1, tb, W_GV), lambda b, j: (b, j, 0)),
                   pl.BlockSpec((1, tb, W_GV), lambda b, j: (b, nblk - 1 - j, 0)),
                   pl.BlockSpec((1, 2, 2, GLA_DV, LANE), lambda b, j: (b, 0, 0, 0, 0))),
        scratch_shapes=[pltpu.VMEM((2, 2, GLA_DV, LANE), F32)],
        compiler_params=_cparams(("parallel", "arbitrary")),
        name="gla_scan",
    )(p, p, p, p, p, p, p, p, wa_p, ba, s0)


def _deepnorm(h, gate, y, w, b):
    r = h + (gate * (1.0 / ALPHA)) * y
    mu = jnp.mean(r, axis=-1, keepdims=True)
    xc = r - mu
    var = jnp.mean(xc * xc, axis=-1, keepdims=True)
    return xc * lax.rsqrt(var + LN_EPS / (ALPHA * ALPHA)) * w + b


def _pack_rounded_pairs(xr):
    half = xr.shape[1] // 2
    lo = lax.bitcast_convert_type(xr[:, :half], jnp.uint32)
    hi = lax.bitcast_convert_type(xr[:, half:], jnp.uint32)
    return jnp.bitwise_or(hi, lax.shift_right_logical(lo, jnp.uint32(16)))


def _pack_bf16_pairs(x):
    return _pack_rounded_pairs(x.astype(BF16).astype(F32))


def _unpack_bf16_pairs(p):
    lo = lax.bitcast_convert_type(lax.shift_left(p, jnp.uint32(16)), F32)
    hi = lax.bitcast_convert_type(jnp.bitwise_and(p, jnp.uint32(0xFFFF0000)), F32)
    return lo, hi


def _post_norm_and_route(h, y, mod_ref, lnw_ref, lnb_ref, rwh_ref, rwl_ref, rb_ref, h1_ref, tok_ref, lg_ref):
    h1 = _deepnorm(h, mod_ref[0, 2:3, :], y, lnw_ref[...], lnb_ref[...])
    h1_ref[0] = h1
    tok = h1 * (1.0 + mod_ref[0, 4:5, :]) + mod_ref[0, 3:4, :]
    hi = tok.astype(BF16)
    hi_f = hi.astype(F32)
    tok_ref[0] = _pack_rounded_pairs(hi_f)
    lo = (tok - hi_f).astype(BF16)
    lg = _dot(hi, rwh_ref[...]) + _dot(lo, rwh_ref[...]) + _dot(hi, rwl_ref[...]) + rb_ref[...]
    lg_ref[...] = lg.T[0:ROUTE_ROWS, :]


def _epilogue_specs(tm, mb):
    mod_map = (lambda b, j: (b, 0, 0)) if mb > 1 else (lambda b, j: (0, 0, 0))
    const2 = lambda b, j: (0, 0)
    return [pl.BlockSpec((1, tm, D), lambda b, j: (b, j, 0)),
            pl.BlockSpec((1, 6, D), mod_map),
            pl.BlockSpec((1, D), const2), pl.BlockSpec((1, D), const2),
            pl.BlockSpec((D, ROUTE_W), const2), pl.BlockSpec((D, ROUTE_W), const2),
            pl.BlockSpec((1, ROUTE_W), const2)]


def _epilogue_outs(B, T, tm):
    nj = T // tm
    shapes = (jax.ShapeDtypeStruct((B, T, D), F32), jax.ShapeDtypeStruct((B, T, D // 2), jnp.uint32),
              jax.ShapeDtypeStruct((ROUTE_ROWS, B * T), F32))
    specs = (pl.BlockSpec((1, tm, D), lambda b, j: (b, j, 0)), pl.BlockSpec((1, tm, D // 2), lambda b, j: (b, j, 0)),
             pl.BlockSpec((ROUTE_ROWS, tm), lambda b, j: (0, b * nj + j)))
    return shapes, specs


def _outproj_even_kernel(a_ref, of_ref, ob_ref, rg_ref, nw_ref, wo_ref,
                         h_ref, mod_ref, lnw_ref, lnb_ref, rwh_ref, rwl_ref, rb_ref,
                         h1_ref, tok_ref, lg_ref):
    parts = [a_ref[0]]
    for hd in range(GLA_HEADS):
        cols = slice(hd * GLA_DV, (hd + 1) * GLA_DV)
        o = of_ref[0, :, cols].astype(F32) + ob_ref[0, :, cols].astype(F32)
        o = o * lax.rsqrt(jnp.mean(o * o, axis=-1, keepdims=True) + RMS_EPS)
        parts.append((o * nw_ref[:, cols] * _silu(rg_ref[0, :, cols].astype(F32))).astype(BF16))
    y = _dot(jnp.concatenate(parts, axis=1), wo_ref[...])
    _post_norm_and_route(h_ref[0], y, mod_ref, lnw_ref, lnb_ref, rwh_ref, rwl_ref, rb_ref, h1_ref, tok_ref, lg_ref)


def outproj_even(a, o_f, o_b, p, norm_w, w_out, h, mod, lnw, lnb, rwh, rwl, rb):
    B, T, _ = h.shape
    tm = min(ROW_TILE, T)
    tile = lambda b, j: (b, j, 0)
    shapes, ospecs = _epilogue_outs(B, T, tm)
    return pl.pallas_call(
        _outproj_even_kernel,
        out_shape=shapes,
        grid=(B, T // tm),
        in_specs=[pl.BlockSpec((1, tm, W_AQ), tile), pl.BlockSpec((1, tm, W_GV), tile), pl.BlockSpec((1, tm, W_GV), tile),
                  pl.BlockSpec((1, tm, W_GV), lambda b, j: (b, j, C_RG // W_GV)),
                  pl.BlockSpec((1, W_GV), lambda b, j: (0, 0)),
                  pl.BlockSpec((D, D), lambda b, j: (0, 0))] + _epilogue_specs(tm, mod.shape[0]),
        out_specs=ospecs,
        compiler_params=_cparams(("parallel", "arbitrary")),
        name="outproj_even",
    )(a, o_f, o_b, p, norm_w, w_out, h, mod, lnw, lnb, rwh, rwl, rb)


def _inproj_odd_kernel(h_ref, mod_ref, w_ref, o_ref):
    u = (h_ref[0] * (1.0 + mod_ref[0, 1:2, :]) + mod_ref[0, 0:1, :]).astype(BF16)
    o_ref[0, :, 0:D] = _dot(u, w_ref[:, 0:D]).astype(BF16)
    o_ref[0, :, D:2 * D] = (_dot(u, w_ref[:, D:2 * D]) * _dot(u, w_ref[:, 2 * D:3 * D])).astype(BF16)


def inproj_odd(h, mod, w):
    B, T, _ = h.shape
    tm = min(IN_ROW_TILE, T)
    mb = mod.shape[0]
    return pl.pallas_call(
        _inproj_odd_kernel,
        out_shape=jax.ShapeDtypeStruct((B, T, 2 * D), BF16),
        grid=(B, T // tm),
        in_specs=[pl.BlockSpec((1, tm, D), lambda b, j: (b, j, 0)),
                  pl.BlockSpec((1, 6, D), (lambda b, j: (b, 0, 0)) if mb > 1 else (lambda b, j: (0, 0, 0))),
                  pl.BlockSpec((D, 3 * D), lambda b, j: (0, 0))],
        out_specs=pl.BlockSpec((1, tm, 2 * D), lambda b, j: (b, j, 0)),
        compiler_params=_cparams(("parallel", "arbitrary")),
        name="inproj_odd",
    )(h, mod, w)


HALO = 16


def _outproj_odd_kernel(gb_ref, z_ref, zp_ref, zn_ref, cw_ref, cb_ref, wo_ref,
                        h_ref, mod_ref, lnw_ref, lnb_ref, rwh_ref, rwl_ref, rb_ref,
                        h1_ref, tok_ref, lg_ref):
    j = pl.program_id(1)
    tm = z_ref.shape[1]
    z = z_ref[0].astype(F32)
    prev_row = jnp.where(j > 0, zp_ref[0, HALO - 1:HALO, :].astype(F32), 0.0)
    next_row = jnp.where(j < pl.num_programs(1) - 1, zn_ref[0, 0:1, :].astype(F32), 0.0)
    row = lax.broadcasted_iota(jnp.int32, (tm, D), 0)
    z_prev = jnp.where(row == 0, prev_row, pltpu.roll(z, 1, 0))
    z_next = jnp.where(row == tm - 1, next_row, pltpu.roll(z, tm - 1, 0))
    conv = z_prev * cw_ref[0:1, :] + z * cw_ref[1:2, :] + z_next * cw_ref[2:3, :] + cb_ref[...]
    y = _dot((gb_ref[0].astype(F32) * conv).astype(BF16), wo_ref[...])
    _post_norm_and_route(h_ref[0], y, mod_ref, lnw_ref, lnb_ref, rwh_ref, rwl_ref, rb_ref, h1_ref, tok_ref, lg_ref)


def outproj_odd(gz, conv_w, conv_b, w_out, h, mod, lnw, lnb, rwh, rwl, rb):
    B, T, _ = h.shape
    tm = min(ROW_TILE, T)
    r = tm // HALO
    nh = T // HALO
    shapes, ospecs = _epilogue_outs(B, T, tm)
    return pl.pallas_call(
        _outproj_odd_kernel,
        out_shape=shapes,
        grid=(B, T // tm),
        in_specs=[pl.BlockSpec((1, tm, D), lambda b, j: (b, j, 0)),
                  pl.BlockSpec((1, tm, D), lambda b, j: (b, j, 1)),
                  pl.BlockSpec((1, HALO, D), lambda b, j: (b, jnp.maximum(j * r - 1, 0), 1)),
                  pl.BlockSpec((1, HALO, D), lambda b, j: (b, jnp.minimum((j + 1) * r, nh - 1), 1)),
                  pl.BlockSpec((3, D), lambda b, j: (0, 0)),
                  pl.BlockSpec((1, D), lambda b, j: (0, 0)),
                  pl.BlockSpec((D, D), lambda b, j: (0, 0))] + _epilogue_specs(tm, mod.shape[0]),
        out_specs=ospecs,
        compiler_params=_cparams(("parallel", "arbitrary")),
        name="outproj_odd",
    )(gz, gz, gz, gz, conv_w, conv_b, w_out, h, mod, lnw, lnb, rwh, rwl, rb)


def _sc_mesh():
    return plsc.VectorSubcoreMesh(core_axis_name="c", subcore_axis_name="s")


def sc_gather_rows(table, idx):
    n = idx.shape[0]
    width = table.shape[1]
    per_w = n // SC_WORKERS
    n_chunks = per_w // SC_CHUNK
    assert n_chunks % 2 == 0

    @functools.partial(
        pl.kernel, mesh=_sc_mesh(),
        out_type=jax.ShapeDtypeStruct((n, width), table.dtype),
        scratch_types=[pltpu.VMEM((n_chunks, SC_CHUNK), jnp.int32),
                       pltpu.VMEM((SC_CHUNK, width), table.dtype), pltpu.VMEM((SC_CHUNK, width), table.dtype),
                       pltpu.SemaphoreType.DMA, pltpu.SemaphoreType.DMA],
    )
    def gather_kernel(table_hbm, idx_hbm, out_hbm, idx_v, buf0, buf1, sem0, sem1):
        wid = lax.axis_index("s") * SC_CORES + lax.axis_index("c")
        pltpu.sync_copy(idx_hbm.at[wid], idx_v)

        def fetch(j, buf, sem):
            return pltpu.make_async_copy(table_hbm.at[idx_v.at[j]], buf, sem)

        def flush(j, buf):
            pltpu.sync_copy(buf, out_hbm.at[pl.ds(wid * per_w + j * SC_CHUNK, SC_CHUNK)])

        fetch(0, buf0, sem0).start()

        @pl.loop(0, n_chunks, step=2)
        def _(j):
            fetch(j + 1, buf1, sem1).start()
            fetch(j, buf0, sem0).wait()
            flush(j, buf0)

            @pl.when(j + 2 < n_chunks)
            def _():
                fetch(j + 2, buf0, sem0).start()

            fetch(j + 1, buf1, sem1).wait()
            flush(j + 1, buf1)

    return gather_kernel(table, idx.reshape(SC_WORKERS, n_chunks, SC_CHUNK))


def sc_scatter_rows(srcs, idxs, n_rows):
    width, dt = srcs[0][0].shape[1], srcs[0][0].dtype
    plans, args = [], []
    for (src, row0, n_src), idx in zip(srcs, idxs):
        per_w = n_src // SC_WORKERS
        chunk = min(SC_CHUNK, per_w // 2)
        assert (per_w // chunk) % 2 == 0
        plans.append((per_w, chunk, per_w // chunk, idx.shape[0], row0))
        args += [src, idx.reshape(idx.shape[0], SC_WORKERS, per_w // chunk, chunk)]
    max_chunk = max(p[1] for p in plans)
    scratch = [pltpu.VMEM((max_chunk, width), dt), pltpu.VMEM((max_chunk, width), dt),
               pltpu.SemaphoreType.DMA, pltpu.SemaphoreType.DMA]
    scratch += [pltpu.VMEM((lists, n_chunks, chunk), jnp.int32) for _, chunk, n_chunks, lists, _ in plans]

    @functools.partial(pl.kernel, mesh=_sc_mesh(), out_type=jax.ShapeDtypeStruct((n_rows, width), dt),
                       scratch_types=scratch)
    def scatter_kernel(*refs):
        ins, out_hbm = refs[:2 * len(plans)], refs[2 * len(plans)]
        rows0, rows1, sem0, sem1 = refs[2 * len(plans) + 1:2 * len(plans) + 5]
        idx_vs = refs[2 * len(plans) + 5:]
        wid = lax.axis_index("s") * SC_CORES + lax.axis_index("c")
        for s, (per_w, chunk, n_chunks, lists, row0) in enumerate(plans):
            src_hbm, idx_hbm, idx_v = ins[2 * s], ins[2 * s + 1], idx_vs[s]
            for k in range(lists):
                pltpu.sync_copy(idx_hbm.at[k, wid], idx_v.at[k])
            buf0 = rows0 if chunk == max_chunk else rows0.at[pl.ds(0, chunk)]
            buf1 = rows1 if chunk == max_chunk else rows1.at[pl.ds(0, chunk)]

            def load(j, buf, sem, src_hbm=src_hbm, per_w=per_w, chunk=chunk, row0=row0):
                return pltpu.make_async_copy(src_hbm.at[pl.ds(row0 + wid * per_w + j * chunk, chunk)], buf, sem)

            def spread(j, buf, idx_v=idx_v, lists=lists):
                for k in range(lists):
                    pltpu.sync_copy(buf, out_hbm.at[idx_v.at[k, j]])

            load(0, buf0, sem0).start()

            @pl.loop(0, n_chunks, step=2)
            def _(j, load=load, spread=spread, buf0=buf0, buf1=buf1, n_chunks=n_chunks):
                load(j + 1, buf1, sem1).start()
                load(j, buf0, sem0).wait()
                spread(j, buf0)

                @pl.when(j + 2 < n_chunks)
                def _():
                    load(j + 2, buf0, sem0).start()

                load(j + 1, buf1, sem1).wait()
                spread(j + 1, buf1)

    return scatter_kernel(*args)


def _ffn_kernel(be_ref, nv_ref, x_ref, w1_ref, w3_ref, w2_ref, y_ref, w1b, w3b, w2b):
    i = pl.program_id(0)
    changed = jnp.logical_or(i == 0, be_ref[i] != be_ref[jnp.maximum(i - 1, 0)])

    @pl.when(changed)
    def _():
        w1b[...] = w1_ref[0, 0].astype(BF16)
        w3b[...] = w3_ref[0, 0].astype(BF16)
        w2b[...] = w2_ref[0, 0].astype(BF16)

    @pl.when(i < nv_ref[0])
    def _():
        x = jnp.concatenate([v.astype(BF16) for v in _unpack_bf16_pairs(x_ref[...])], axis=1)
        y_ref[...] = _pack_bf16_pairs(_dot((_silu(_dot(x, w1b[...])) * _dot(x, w3b[...])).astype(BF16), w2b[...]))

    @pl.when(i >= nv_ref[0])
    def _():
        y_ref[...] = jnp.zeros_like(y_ref)


def moe_ffn(xs, blk_expert, n_valid, w1, w3, w2, layer):
    n_rows = xs.shape[0]
    nb = n_rows // MOE_TM
    return pl.pallas_call(
        _ffn_kernel,
        out_shape=jax.ShapeDtypeStruct((n_rows, D // 2), jnp.uint32),
        grid_spec=pltpu.PrefetchScalarGridSpec(
            num_scalar_prefetch=2,
            grid=(nb,),
            in_specs=[pl.BlockSpec((MOE_TM, D // 2), lambda i, be, nv: (i, 0)),
                      pl.BlockSpec((1, 1, D, D_EXPERT), lambda i, be, nv: (layer, be[i], 0, 0)),
                      pl.BlockSpec((1, 1, D, D_EXPERT), lambda i, be, nv: (layer, be[i], 0, 0)),
                      pl.BlockSpec((1, 1, D_EXPERT, D), lambda i, be, nv: (layer, be[i], 0, 0))],
            out_specs=pl.BlockSpec((MOE_TM, D // 2), lambda i, be, nv: (i, 0)),
            scratch_shapes=[pltpu.VMEM((D, D_EXPERT), BF16), pltpu.VMEM((D, D_EXPERT), BF16),
                            pltpu.VMEM((D_EXPERT, D), BF16)]),
        compiler_params=_cparams(("arbitrary",)),
        name="moe_ffn",
    )(blk_expert, n_valid, xs, w1, w3, w2)


def _combine_kernel(h_ref, y0_ref, y1_ref, wt_ref, mod_ref, lnw_ref, lnb_ref, o_ref):
    half = D // 2
    lo0, hi0 = _unpack_bf16_pairs(y0_ref[0])
    lo1, hi1 = _unpack_bf16_pairs(y1_ref[0])
    pick = (lax.broadcasted_iota(jnp.int32, (WT_ROWS, LANE), 0) == lax.broadcasted_iota(jnp.int32, (WT_ROWS, LANE), 1))
    wcols = lax.dot_general(wt_ref[...], pick.astype(F32), (((0,), (0,)), ((), ())), precision=HIGHEST,
                            preferred_element_type=F32)
    w0, w1 = wcols[:, 0:1], wcols[:, 1:2]
    gate = mod_ref[0, 5:6, :] * (1.0 / ALPHA)
    r_lo = h_ref[:, 0:half] + gate[:, 0:half] * (w0 * lo0 + w1 * lo1)
    r_hi = h_ref[:, half:D] + gate[:, half:D] * (w0 * hi0 + w1 * hi1)
    mu = (jnp.sum(r_lo, axis=-1, keepdims=True) + jnp.sum(r_hi, axis=-1, keepdims=True)) * (1.0 / D)
    c_lo, c_hi = r_lo - mu, r_hi - mu
    var = (jnp.sum(c_lo * c_lo, axis=-1, keepdims=True) + jnp.sum(c_hi * c_hi, axis=-1, keepdims=True)) * (1.0 / D)
    inv = lax.rsqrt(var + LN_EPS / (ALPHA * ALPHA))
    o_ref[:, 0:half] = c_lo * inv * lnw_ref[:, 0:half] + lnb_ref[:, 0:half]
    o_ref[:, half:D] = c_hi * inv * lnw_ref[:, half:D] + lnb_ref[:, half:D]


def _combine_into_kernel(*refs):
    _combine_kernel(*refs[:7], refs[8])


def moe_combine(h1, y_rows, wts, mod, lnw, lnb, tok0, ntok, y_tok0, prev=None):
    B, T, _ = h1.shape
    N = B * T
    tm = min(ROW_TILE, ntok)
    per_b = T // tm
    h_off, y_off = tok0 // tm, y_tok0 // tm
    mod_map = (lambda i: ((i + h_off) // per_b, 0, 0)) if mod.shape[0] > 1 else (lambda i: (0, 0, 0))
    in_specs = [pl.BlockSpec((tm, D), lambda i: (i + h_off, 0)),
                pl.BlockSpec((1, tm, D // 2), lambda i: (0, i + y_off, 0)),
                pl.BlockSpec((1, tm, D // 2), lambda i: (1, i + y_off, 0)),
                pl.BlockSpec((WT_ROWS, tm), lambda i: (0, i + y_off)),
                pl.BlockSpec((1, 6, D), mod_map),
                pl.BlockSpec((1, D), lambda i: (0, 0)), pl.BlockSpec((1, D), lambda i: (0, 0))]
    args = [h1.reshape(N, D), y_rows, y_rows, wts, mod, lnw, lnb]
    if prev is not None:
        in_specs.append(pl.BlockSpec(memory_space=pl.ANY))
        args.append(prev)
    return pl.pallas_call(
        _combine_kernel if prev is None else _combine_into_kernel,
        out_shape=jax.ShapeDtypeStruct((N, D), F32),
        grid=(ntok // tm,),
        in_specs=in_specs,
        out_specs=pl.BlockSpec((tm, D), lambda i: (i + h_off, 0)),
        input_output_aliases={} if prev is None else {7: 0},
        compiler_params=_cparams(("parallel",)),
        name="moe_combine",
    )(*args)


def _route_kernel(lg_ref, dest_ref, wt_ref, cnt_ref, tri_sc, start_sc, run_sc):
    ph, i = pl.program_id(0), pl.program_id(1)
    tr = lg_ref.shape[1]

    @pl.when(jnp.logical_and(ph == 0, i == 0))
    def _():
        r = lax.broadcasted_iota(jnp.int32, (LANE, LANE), 0)
        c = lax.broadcasted_iota(jnp.int32, (LANE, LANE), 1)
        tri_sc[...] = (r < c).astype(BF16)
        start_sc[...] = jnp.zeros_like(start_sc)
        run_sc[...] = jnp.zeros_like(run_sc)

    @pl.when(jnp.logical_and(ph == 1, i == 0))
    def _():
        cnt = run_sc[...].astype(jnp.int32)
        cnt_ref[...] = cnt
        padded = jnp.bitwise_and(cnt + (MOE_TM - 1), -MOE_TM)
        row = lax.broadcasted_iota(jnp.int32, padded.shape, 0)
        acc = padded
        for s in (1, 2, 4, 8, 16):
            acc = acc + jnp.where(row >= s, pltpu.roll(acc, s, 0), 0)
        start_sc[...] = (acc - padded).astype(F32)
        run_sc[...] = jnp.zeros_like(run_sc)

    lg = lg_ref[...]
    gl = lg[N_EXPERTS:N_EXPERTS + N_GROUPS]
    gmax = jnp.max(gl, axis=0, keepdims=True)
    sub4 = lax.broadcasted_iota(jnp.int32, gl.shape, 0)
    g_sel = jnp.min(jnp.where(gl == gmax, sub4, N_GROUPS), axis=0, keepdims=True)
    p_group = 1.0 / jnp.sum(jnp.exp(gl - gmax), axis=0, keepdims=True)
    el = lg[0:EXPERTS_PER_GROUP]
    for g in range(1, N_GROUPS):
        el = jnp.where(g_sel == g, lg[g * EXPERTS_PER_GROUP:(g + 1) * EXPERTS_PER_GROUP], el)
    sub8 = lax.broadcasted_iota(jnp.int32, el.shape, 0)
    e1 = jnp.max(el, axis=0, keepdims=True)
    i1 = jnp.min(jnp.where(el == e1, sub8, EXPERTS_PER_GROUP), axis=0, keepdims=True)
    rest = jnp.where(sub8 == i1, -jnp.inf, el)
    e2 = jnp.max(rest, axis=0, keepdims=True)
    i2 = jnp.min(jnp.where(rest == e2, sub8, EXPERTS_PER_GROUP), axis=0, keepdims=True)
    den = jnp.sum(jnp.exp(el - e1), axis=0, keepdims=True)
    p1 = 1.0 / den
    p2 = jnp.exp(e2 - e1) / den
    wt_ref[...] = jnp.zeros_like(wt_ref)
    wt_ref[0:1, :] = p_group * p1 / (p1 + p2)
    wt_ref[1:2, :] = p_group * p2 / (p1 + p2)

    sub32 = lax.broadcasted_iota(jnp.int32, (N_EXPERTS, tr), 0)
    oh = [(sub32 == g_sel * EXPERTS_PER_GROUP + ix).astype(F32) for ix in (i1, i2)]
    cnt = [jnp.sum(o, axis=1, keepdims=True) for o in oh]

    @pl.when(ph == 0)
    def _():
        dest_ref[...] = jnp.zeros_like(dest_ref)

    @pl.when(ph == 1)
    def _():
        before = start_sc[:, 0:1] + run_sc[:, 0:1]
        for k in range(TOP_K):
            subs = [oh[k][:, s * LANE:(s + 1) * LANE] for s in range(tr // LANE)]
            local = _dot(jnp.concatenate(subs, axis=0).astype(BF16), tri_sc[...])
            seen = before + (cnt[0] if k == 1 else 0.0)
            for s, sub in enumerate(subs):
                prior = local[s * N_EXPERTS:(s + 1) * N_EXPERTS] + seen
                dest_ref[k:k + 1, s * LANE:(s + 1) * LANE] = (
                    jnp.sum(sub * prior, axis=0, keepdims=True).astype(jnp.int32))
                seen = seen + jnp.sum(sub, axis=1, keepdims=True)

    run_sc[...] = run_sc[...] + (cnt[0] + cnt[1])


def moe_route(logits_t, col0, N):
    tr = next(t for t in (4096, 2048, 1024, 512, 256) if N % t == 0 and col0 % t == 0)
    t0 = col0 // tr
    return pl.pallas_call(
        _route_kernel,
        out_shape=(jax.ShapeDtypeStruct((TOP_K, N), jnp.int32), jax.ShapeDtypeStruct((WT_ROWS, N), F32),
                   jax.ShapeDtypeStruct((N_EXPERTS, LANE), jnp.int32)),
        grid=(2, N // tr),
        in_specs=[pl.BlockSpec((ROUTE_ROWS, tr), lambda p, i: (0, i + t0))],
        out_specs=(pl.BlockSpec((TOP_K, tr), lambda p, i: (0, i * p)), pl.BlockSpec((WT_ROWS, tr), lambda p, i: (0, i * p)),
                   pl.BlockSpec((N_EXPERTS, LANE), lambda p, i: (0, 0))),
        scratch_shapes=[pltpu.VMEM((LANE, LANE), BF16), pltpu.VMEM((N_EXPERTS, LANE), F32),
                        pltpu.VMEM((N_EXPERTS, LANE), F32)],
        compiler_params=_cparams(("arbitrary", "arbitrary")),
        name="moe_route",
    )(logits_t)


def _block_tables(counts, n_assign):
    padded = (counts + MOE_TM - 1) // MOE_TM * MOE_TM
    pad_end = jnp.cumsum(padded)
    pad_start = pad_end - padded
    nb = -(-n_assign // MOE_TM) + N_EXPERTS
    blk_start = jnp.arange(nb, dtype=jnp.int32) * MOE_TM
    blk_expert = jnp.minimum(jnp.sum((pad_end[None, :] <= blk_start[:, None]).astype(jnp.int32), axis=1), N_EXPERTS - 1)
    n_valid = (pad_end[-1] // MOE_TM).astype(jnp.int32).reshape(1)
    n_fill = nb * MOE_TM - n_assign
    gap = padded - counts
    gap_end = jnp.cumsum(gap)
    k = jnp.arange(n_fill, dtype=jnp.int32)[:, None]
    sel = jnp.logical_and(k >= (gap_end - gap)[None, :], k < gap_end[None, :])
    in_gap = jnp.sum(jnp.where(sel, (pad_start + counts - (gap_end - gap))[None, :] + k, 0), axis=1)
    fill = jnp.where(k[:, 0] < gap_end[-1], in_gap, pad_end[-1] + k[:, 0] - gap_end[-1])
    return blk_expert.astype(jnp.int32), n_valid, fill.astype(jnp.int32), nb * MOE_TM


def hier_moe_and_norm(streams, w1, w3, w2, layer, lnw, lnb):
    lat = streams[0]
    ctx = streams[1] if len(streams) > 1 else None
    n_lat = lat[0].shape[0] * lat[0].shape[1]
    n_ctx = ctx[0].shape[0] * ctx[0].shape[1] if ctx else 0
    logits_t = lat[2] if ctx is None else jnp.concatenate([lat[2], ctx[2]], 1)
    tok_lat = lat[1].reshape(n_lat, D // 2)
    half = n_lat // 2
    ranges = ((0, half), (half, n_lat - half + n_ctx))
    routed = [moe_route(logits_t, c0, n) for c0, n in ranges]
    sorted_in, tables = [], []
    for (c0, n), (dest, _, counts) in zip(ranges, routed):
        blk_expert, n_valid, fill_rows, n_rows = _block_tables(counts[:, 0], TOP_K * n)
        n_lat_here = min(c0 + n, n_lat) - c0
        srcs, idxs = [(tok_lat, c0, n_lat_here)], [dest[:, :n_lat_here]]
        if n > n_lat_here:
            srcs.append((ctx[1].reshape(n_ctx, D // 2), 0, n_ctx))
            idxs.append(dest[:, n_lat_here:])
        srcs.append((jnp.zeros((fill_rows.shape[0], D // 2), jnp.uint32), 0, fill_rows.shape[0]))
        idxs.append(fill_rows.reshape(1, -1))
        sorted_in.append(sc_scatter_rows(srcs, idxs, n_rows))
        tables.append((blk_expert, n_valid, n_lat_here))
    out_lat, out_ctx = None, None
    for (c0, n), (dest, wts, _), xs, (blk_expert, n_valid, n_lat_here) in zip(ranges, routed, sorted_in, tables):
        y = moe_ffn(xs, blk_expert, n_valid, w1, w3, w2, layer)
        y_rows = sc_gather_rows(y, dest.reshape(-1)).reshape(TOP_K, n, D // 2)
        out_lat = moe_combine(lat[0], y_rows, wts, lat[3], lnw, lnb, c0, n_lat_here, 0, out_lat)
        if n > n_lat_here:
            out_ctx = moe_combine(ctx[0], y_rows, wts, ctx[3], lnw, lnb, 0, n_ctx, n_lat_here)
    outs = [out_lat.reshape(lat[0].shape)]
    if ctx:
        outs.append(out_ctx.reshape(ctx[0].shape))
    return outs


def _even_projection_weights(w):
    def rope_layout(x, heads):
        return x.reshape(D, heads, 2, 2, HEAD_DIM // 4).transpose(0, 1, 3, 2, 4).reshape(D, heads * HEAD_DIM)

    def twice(x):
        return jnp.concatenate([x[:, :HEAD_DIM], x[:, :HEAD_DIM], x[:, HEAD_DIM:], x[:, HEAD_DIM:]], axis=1)

    def lane_pad(x):
        return jnp.pad(x, ((0, 0), (0, LANE - x.shape[1])))

    qa, ka, va, qg, kg, vg, rg, gg = jnp.split(w, [512, 640, 768, 1024, 1280, 1792, 2304], axis=1)
    cols = [rope_layout(qa, A_Q_HEADS) * (HEAD_DIM ** -0.5 * LOG2_E),
            twice(rope_layout(ka, A_KV_HEADS)), twice(va), vg, rg, qg * GLA_DK ** -0.5, kg,
            lane_pad(gg[:, :GLA_RANK]), lane_pad(gg[:, GLA_RANK:])]
    return jnp.concatenate(cols, axis=1).astype(BF16)


def _rope_tables(S):
    row = jnp.repeat(jnp.arange(S // GRID_W), GRID_W).astype(F32)
    col = jnp.tile(jnp.arange(GRID_W), S // GRID_W).astype(F32)
    axis_dim = HEAD_DIM // 2
    inv_freq = ROPE_BASE ** (-jnp.arange(0, axis_dim, 2, dtype=F32) / axis_dim)
    ang = jnp.concatenate([row[:, None] * inv_freq, col[:, None] * inv_freq], -1)
    cos, sin = jnp.cos(ang), jnp.sin(ang)
    cos_t = jnp.tile(cos, (1, 4))
    sin_t = jnp.tile(jnp.concatenate([-sin, sin], -1), (1, 2))
    return cos_t, sin_t


def _router_weights(wg, bg, we, be):
    w = jnp.zeros((D, ROUTE_W), F32).at[:, :N_EXPERTS].set(we).at[:, N_EXPERTS:N_EXPERTS + N_GROUPS].set(wg)
    b = jnp.zeros((1, ROUTE_W), F32).at[0, :N_EXPERTS].set(be).at[0, N_EXPERTS:N_EXPERTS + N_GROUPS].set(bg)
    hi = w.astype(BF16)
    return hi, (w - hi.astype(F32)).astype(BF16), b


def kernel(x, c, ctx, c_ctx, w_in_even, w_out_even, attn_sink, gla_wa2, gla_ba, gla_norm_w, w_in_odd, conv_w, conv_b, w_out_odd, ada_w, ada_b, ln_w, ln_b, router_wg, router_bg, router_we, router_be, moe_w1, moe_w3, moe_w2):
    B, S, _ = x.shape
    L = ctx.shape[1]
    cos_t, sin_t = _rope_tables(S)
    cos_c, sin_c = jnp.ones((L, LANE), F32), jnp.zeros((L, LANE), F32)

    n_cond = -(-(B + 1) // 8) * 8
    cc = jnp.zeros((n_cond, D), F32).at[:B].set(c).at[B].set(c_ctx)
    mods = ada_modulation_all(cc, ada_w, ada_b).reshape(DEPTH, n_cond, 6, D)

    h_lat, h_ctx = x, ctx
    for l in range(DEPTH):
        i = l // 2
        need_ctx = any(j % 2 == 0 for j in range(l + 1, DEPTH))
        m_lat = mods[l, :B]
        m_ctx = mods[l, B:B + 1]
        lnw0, lnb0 = ln_w[l, 0:1], ln_b[l, 0:1]
        lnw1, lnb1 = ln_w[l, 1:2], ln_b[l, 1:2]
        rwh, rwl, rb = _router_weights(router_wg[l], router_bg[l], router_we[l], router_be[l])
        streams = []
        if l % 2 == 0:
            w_in = _even_projection_weights(w_in_even[i])
            w_out = w_out_even[i].astype(BF16)
            wa_p = jnp.zeros((2, LANE, GLA_HEADS * GLA_DK), F32).at[:, :GLA_RANK].set(gla_wa2[i]).astype(BF16)
            ba = gla_ba[i].reshape(2, 1, -1)
            nw = gla_norm_w[i].reshape(1, -1)
            p_ctx = inproj_even(h_ctx, m_ctx, w_in, cos_c, sin_c)
            p_lat = inproj_even(h_lat, m_lat, w_in, cos_t, sin_t)
            a_lat = attention(p_lat, p_ctx, attn_sink[i], True)
            s0 = jnp.zeros((B, 2, 2, GLA_DV, LANE), F32)
            oc_f, oc_b, s_ctx = gla_scan(p_ctx, wa_p, ba, s0)
            ol_f, ol_b, _ = gla_scan(p_lat, wa_p, ba, s_ctx)
            streams.append(outproj_even(a_lat, ol_f, ol_b, p_lat, nw, w_out, h_lat, m_lat, lnw0, lnb0, rwh, rwl, rb)
                           + (m_lat,))
            if need_ctx:
                a_ctx = attention(p_ctx, p_ctx, attn_sink[i], False)
                streams.append(outproj_even(a_ctx, oc_f, oc_b, p_ctx, nw, w_out, h_ctx, m_ctx, lnw0, lnb0,
                                            rwh, rwl, rb) + (m_ctx,))
        else:
            w_in = w_in_odd[i].astype(BF16)
            w_out = w_out_odd[i].astype(BF16)
            cb = conv_b[i].reshape(1, D)
            pairs = [(h_lat, m_lat)] + ([(h_ctx, m_ctx)] if need_ctx else [])
            for h, m in pairs:
                gz = inproj_odd(h, m, w_in)
                streams.append(outproj_odd(gz, conv_w[i], cb, w_out, h, m, lnw0, lnb0, rwh, rwl, rb) + (m,))
        outs = hier_moe_and_norm(streams, moe_w1, moe_w3, moe_w2, l, lnw1, lnb1)
        h_lat = outs[0]
        if need_ctx:
            h_ctx = outs[1]
    return h_lat
```

```python
import functools

import jax
import jax.numpy as jnp
from jax import lax
from jax.experimental import pallas as pl
from jax.experimental.pallas import tpu as pltpu
from jax.experimental.pallas import tpu_sc as plsc

F32 = jnp.float32
BF16 = jnp.bfloat16
HIGHEST = lax.Precision.HIGHEST

D = 1024
DEPTH = 4
GRID_W = 64
HEAD_DIM = 64
A_Q_HEADS = 8
A_KV_HEADS = 2
WINDOW = 128
ROPE_BASE = 10000.0
GLA_HEADS = 4
GLA_DK = 64
GLA_DV = 128
GLA_RANK = 16
GLA_TAU = 16.0
GLA_CHUNK = 64
N_GROUPS = 4
EXPERTS_PER_GROUP = 8
N_EXPERTS = 32
TOP_K = 2
D_EXPERT = 512
ALPHA = (2.0 * DEPTH) ** 0.25
LN_EPS = 1e-5
RMS_EPS = 1e-6

LANE = 128
VMEM_LIMIT = 48 * 1024 * 1024

ROW_TILE = 512
IN_ROW_TILE = 1024
W_AQ = A_Q_HEADS * HEAD_DIM
W_KV2 = 2 * A_KV_HEADS * HEAD_DIM
W_GK = GLA_HEADS * GLA_DK
W_GV = GLA_HEADS * GLA_DV
C_QA = 0
C_KD = C_QA + W_AQ
C_VD = C_KD + W_KV2
C_VG = C_VD + W_KV2
C_RG = C_VG + W_GV
C_QG = C_RG + W_GV
C_KG = C_QG + W_GK
C_GG = C_KG + W_GK
P_W = C_GG + 2 * LANE
ROUTE_W = 128
WT_ROWS = 8
ROUTE_ROWS = 40
MOE_TM = 512
SC_CORES, SC_SUBCORES = 2, 16
SC_WORKERS = SC_CORES * SC_SUBCORES
SC_CHUNK = 64
NEG = -1e30
LOG2_E = 1.4426950408889634


def _cparams(sem):
    return pltpu.CompilerParams(dimension_semantics=sem, vmem_limit_bytes=VMEM_LIMIT)


def _dot(a, b):
    return jnp.dot(a, b, preferred_element_type=F32)


def _dot_nt(a, b):
    return lax.dot_general(a, b, (((1,), (1,)), ((), ())), preferred_element_type=F32)


def _dot_tn(a, b):
    return lax.dot_general(a, b, (((0,), (0,)), ((), ())), preferred_element_type=F32)


def _silu(x):
    return x * (1.0 / (1.0 + jnp.exp(-x)))


def _ada_kernel(c_ref, w_ref, b_ref, o_ref):
    s = _silu(c_ref[...])
    o_ref[0] = jnp.dot(s, w_ref[0], precision=HIGHEST, preferred_element_type=F32) + b_ref[0]


def ada_modulation_all(cc, ada_w, ada_b):
    R = cc.shape[0]
    tn = 1536
    return pl.pallas_call(
        _ada_kernel,
        out_shape=jax.ShapeDtypeStruct((DEPTH, R, 6 * D), F32),
        grid=(DEPTH, 6 * D // tn),
        in_specs=[pl.BlockSpec((R, D), lambda l, n: (0, 0)),
                  pl.BlockSpec((1, D, tn), lambda l, n: (l, 0, n)),
                  pl.BlockSpec((1, 1, tn), lambda l, n: (l, 0, n))],
        out_specs=pl.BlockSpec((1, R, tn), lambda l, n: (l, 0, n)),
        compiler_params=_cparams(("arbitrary", "arbitrary")),
        name="ada_modulation",
    )(cc, ada_w, ada_b.reshape(DEPTH, 1, 6 * D))


_EVEN_CHUNKS = ((C_QA, C_KD, True), (C_KD, C_VD, True), (C_VD, C_VG, False), (C_VG, C_RG, False),
                (C_RG, C_QG, False), (C_QG, C_GG, False), (C_GG, P_W, False))


def _inproj_even_kernel(h_ref, mod_ref, w_ref, cos_ref, sin_ref, p_ref):
    tm = h_ref.shape[1]
    u = (h_ref[0] * (1.0 + mod_ref[0, 1:2, :]) + mod_ref[0, 0:1, :]).astype(BF16)
    cos = cos_ref[...]
    sin = sin_ref[...]
    lane = lax.broadcasted_iota(jnp.int32, (tm, LANE), 1)
    first_half = (lane % HEAD_DIM) < (HEAD_DIM // 2)
    for c0, c1, rope in _EVEN_CHUNKS:
        acc = _dot(u, w_ref[:, c0:c1])
        if rope:
            for i in range((c1 - c0) // LANE):
                x = acc[:, i * LANE:(i + 1) * LANE]
                partner = jnp.where(first_half, pltpu.roll(x, LANE - 32, 1), pltpu.roll(x, 32, 1))
                p_ref[0, :, c0 + i * LANE:c0 + (i + 1) * LANE] = (x * cos + partner * sin).astype(BF16)
        else:
            p_ref[0, :, c0:c1] = acc.astype(BF16)


def inproj_even(h, mod, w, cos_t, sin_t):
    B, T, _ = h.shape
    tm = min(IN_ROW_TILE, T)
    mb = mod.shape[0]
    return pl.pallas_call(
        _inproj_even_kernel,
        out_shape=jax.ShapeDtypeStruct((B, T, P_W), BF16),
        grid=(B, T // tm),
        in_specs=[pl.BlockSpec((1, tm, D), lambda b, j: (b, j, 0)),
                  pl.BlockSpec((1, 6, D), (lambda b, j: (b, 0, 0)) if mb > 1 else (lambda b, j: (0, 0, 0))),
                  pl.BlockSpec((D, P_W), lambda b, j: (0, 0)),
                  pl.BlockSpec((tm, LANE), lambda b, j: (j, 0)),
                  pl.BlockSpec((tm, LANE), lambda b, j: (j, 0))],
        out_specs=pl.BlockSpec((1, tm, P_W), lambda b, j: (b, j, 0)),
        compiler_params=_cparams(("parallel", "arbitrary")),
        name="inproj_even",
    )(h, mod, w, cos_t, sin_t)


def _attn_kernel(*refs, tq, tiles, has_window):
    if has_window:
        sink_ref, q_ref, kw_ref, vw_ref, kc_ref, vc_ref, o_ref = refs
    else:
        sink_ref, q_ref, kc_ref, vc_ref, o_ref = refs
    group = A_Q_HEADS // A_KV_HEADS
    rows = group * tq
    lo = lax.broadcasted_iota(jnp.int32, (tq, LANE), 1) < HEAD_DIM
    den_lanes = lax.broadcasted_iota(jnp.int32, (rows, LANE), 1) >= HEAD_DIM

    def with_ones(v):
        return jnp.where(lax.broadcasted_iota(jnp.int32, v.shape, 1) < HEAD_DIM, v, jnp.ones_like(v))

    wstart, band = [], []
    if has_window:
        S = kw_ref.shape[1]
        wk = tq + 2 * WINDOW
        for t in range(tiles):
            q0 = (pl.program_id(1) * tiles + t) * tq
            wstart.append(pl.multiple_of(jnp.clip(q0 - WINDOW, 0, S - wk), LANE))
            qpos = q0 + lax.broadcasted_iota(jnp.int32, (tq, wk), 0)
            kpos = wstart[t] + lax.broadcasted_iota(jnp.int32, (tq, wk), 1)
            band.append(jnp.tile(jnp.where(jnp.abs(qpos - kpos) <= WINDOW, 0.0, NEG), (group, 1)))
    chains = [(t, g) for t in range(tiles) for g in range(A_KV_HEADS)]
    rt = [slice(t * tq, (t + 1) * tq) for t in range(tiles)]
    cols = [slice(g * LANE, (g + 1) * LANE) for g in range(A_KV_HEADS)]
    q4, snk, sc, m, outs = [], [], [], [], []
    for t, g in chains:
        qs = []
        for pr in range(group // 2):
            qblk = q_ref[0, rt[t], (2 * g + pr) * LANE:(2 * g + pr + 1) * LANE]
            zero = jnp.zeros_like(qblk)
            qs += [jnp.where(lo, qblk, zero), jnp.where(lo, zero, qblk)]
        q4.append(jnp.concatenate(qs, axis=0))
        snk.append(jnp.concatenate([jnp.full((tq, 1), sink_ref[group * g + i] * LOG2_E, F32) for i in range(group)],
                                   axis=0))
    for c, (t, g) in enumerate(chains):
        if has_window:
            keys = jnp.concatenate([kw_ref[0, pl.ds(wstart[t], wk), cols[g]], kc_ref[0, :, cols[g]]], axis=0)
            s = _dot_nt(q4[c], keys)
            sc.append(jnp.concatenate([s[:, 0:wk] + band[t], s[:, wk:]], axis=1))
        else:
            sc.append(_dot_nt(q4[c], kc_ref[0, :, cols[g]]))
    for c in range(len(chains)):
        m.append(jnp.maximum(jnp.max(sc[c], axis=-1, keepdims=True), snk[c]))
    for c, (t, g) in enumerate(chains):
        if has_window:
            vals = jnp.concatenate([vw_ref[0, pl.ds(wstart[t], wk), cols[g]], vc_ref[0, :, cols[g]]], axis=0)
        else:
            vals = vc_ref[0, :, cols[g]]
        o = _dot(jnp.exp2((sc[c] - m[c]).astype(BF16)), with_ones(vals))
        outs.append(o + jnp.where(den_lanes, jnp.exp2(snk[c] - m[c]), 0.0))
    for c, (t, g) in enumerate(chains):
        o = outs[c]
        swapped = pltpu.roll(o, HEAD_DIM, 1)
        for pr in range(group // 2):
            ev = slice(2 * pr * tq, (2 * pr + 1) * tq)
            od = slice((2 * pr + 1) * tq, (2 * pr + 2) * tq)
            res = jnp.where(lo, o[ev] / swapped[ev], swapped[od] / o[od])
            o_ref[0, rt[t], (2 * g + pr) * LANE:(2 * g + pr + 1) * LANE] = res.astype(BF16)


def attention(p_q, p_ctx, sink, has_window):
    B, T, _ = p_q.shape
    L = p_ctx.shape[1]
    tq = 128
    tiles = next(n for n in (4, 2, 1) if T % (n * tq) == 0)
    in_specs = [pl.BlockSpec(memory_space=pltpu.SMEM),
                pl.BlockSpec((1, tiles * tq, W_AQ), lambda b, j: (b, j, C_QA // W_AQ))]
    args = [sink, p_q]
    if has_window:
        in_specs += [pl.BlockSpec((1, T, W_KV2), lambda b, j: (b, 0, C_KD // W_KV2)),
                     pl.BlockSpec((1, T, W_KV2), lambda b, j: (b, 0, C_VD // W_KV2))]
        args += [p_q, p_q]
    in_specs += [pl.BlockSpec((1, L, W_KV2), lambda b, j: (b, 0, C_KD // W_KV2)),
                 pl.BlockSpec((1, L, W_KV2), lambda b, j: (b, 0, C_VD // W_KV2))]
    args += [p_ctx, p_ctx]
    return pl.pallas_call(
        functools.partial(_attn_kernel, tq=tq, tiles=tiles, has_window=has_window),
        out_shape=jax.ShapeDtypeStruct((B, T, W_AQ), BF16),
        grid=(B, T // (tiles * tq)),
        in_specs=in_specs,
        out_specs=pl.BlockSpec((1, tiles * tq, W_AQ), lambda b, j: (b, j, 0)),
        compiler_params=_cparams(("parallel", "arbitrary")),
        name="window_attention" if has_window else "context_attention",
    )(*args)


def _log_sigmoid(x):
    return jnp.minimum(x, 0.0) - jnp.log(1.0 + jnp.exp(-jnp.abs(x)))


def _gla_kernel(qf_ref, kf_ref, vf_ref, gf_ref, qb_ref, kb_ref, vb_ref, gb_ref, wa_ref, ba_ref, s0_ref,
                of_ref, ob_ref, sfin_ref, s_sc):
    j = pl.program_id(1)
    nblk = pl.num_programs(1)
    tb = qf_ref.shape[1]
    nc = tb // GLA_CHUNK

    @pl.when(j == 0)
    def _():
        s_sc[...] = s0_ref[0]

    C = GLA_CHUNK
    hc = GLA_HEADS * C
    ri = lax.broadcasted_iota(jnp.int32, (hc, hc), 0) % C
    ci = lax.broadcasted_iota(jnp.int32, (hc, hc), 1) % C
    rb = lax.broadcasted_iota(jnp.int32, (tb, tb), 0)
    cb = lax.broadcasted_iota(jnp.int32, (tb, tb), 1)
    same_chunk = (rb // C) == (cb // C)
    lane_head = lax.broadcasted_iota(jnp.int32, (C, GLA_HEADS * GLA_DK), 1) // GLA_DK

    def per_head(x):
        zero = jnp.zeros_like(x)
        return jnp.concatenate([jnp.where(lane_head == h, x, zero) for h in range(GLA_HEADS)], axis=0)

    io = ((qf_ref, kf_ref, vf_ref, gf_ref, of_ref), (qb_ref, kb_ref, vb_ref, gb_ref, ob_ref))
    causal = ((ri >= ci), (ci >= ri))
    tri = (jnp.logical_and(same_chunk, rb >= cb).astype(BF16), jnp.logical_and(same_chunk, cb >= rb).astype(BF16))
    b_all = []
    for d in range(2):
        g = _dot(io[d][3][0, :, d * LANE:(d + 1) * LANE], wa_ref[d]) + ba_ref[d]
        log_a = _log_sigmoid(g) / GLA_TAU
        la1 = log_a.astype(BF16)
        rem = log_a - la1.astype(F32)
        la2 = rem.astype(BF16)
        la3 = (rem - la2.astype(F32)).astype(BF16)
        b_all.append(_dot(tri[d], la1) + _dot(tri[d], la2) + _dot(tri[d], la3))
    state = [jnp.concatenate([s_sc[d, 0], s_sc[d, 1]], axis=1) for d in range(2)]
    zero_blk = jnp.zeros((GLA_DV, LANE), BF16)
    for step in range(nc):
        for d in range(2):
            q_ref, k_ref, v_ref, _, o_ref = io[d]
            c = step if d == 0 else nc - 1 - step
            rows = slice(c * C, (c + 1) * C)
            b = b_all[d][rows]
            b_last = b[C - 1:C] if d == 0 else b[0:1]
            qc = q_ref[0, rows, :].astype(F32)
            kc = k_ref[0, rows, :].astype(F32)
            q4 = per_head((qc * jnp.exp(b)).astype(BF16))
            k4 = per_head((kc * jnp.exp(-b)).astype(BF16))
            ks4 = per_head((kc * jnp.exp(b_last - b)).astype(BF16))
            st = state[d]
            stb = st.astype(BF16)
            st_bd = jnp.concatenate([jnp.concatenate([stb[:, 0:LANE], zero_blk], axis=1),
                                     jnp.concatenate([zero_blk, stb[:, LANE:]], axis=1)], axis=0)
            res = _dot_nt(q4, jnp.concatenate([st_bd, k4], axis=0))
            attn = jnp.where(causal[d], res[:, 2 * GLA_DV:], 0.0).astype(BF16)
            v4 = jnp.concatenate([v_ref[0, rows, h * GLA_DV:(h + 1) * GLA_DV] for h in range(GLA_HEADS)], axis=0)
            o4 = _dot(attn, v4)
            for h in range(GLA_HEADS):
                hr = slice(h * C, (h + 1) * C)
                inter = res[hr, (h // 2) * GLA_DV:(h // 2 + 1) * GLA_DV]
                o_ref[0, rows, h * GLA_DV:(h + 1) * GLA_DV] = (o4[hr] + inter).astype(o_ref.dtype)
            state[d] = st * jnp.exp(b_last) + _dot_tn(v4, ks4)
    for d in range(2):
        for pair in range(2):
            s_sc[d, pair] = state[d][:, pair * LANE:(pair + 1) * LANE]

    @pl.when(j == nblk - 1)
    def _():
        sfin_ref[0] = s_sc[...]


def gla_scan(p, wa_p, ba, s0):
    B, T, _ = p.shape
    tb = min(ROW_TILE, T)
    nblk = T // tb
    fwd = lambda b, j: (b, j)
    bwd = lambda b, j: (b, nblk - 1 - j)

    def specs(im):
        return [pl.BlockSpec((1, tb, W_GK), lambda b, j: im(b, j) + (C_QG // W_GK,)),
                pl.BlockSpec((1, tb, W_GK), lambda b, j: im(b, j) + (C_KG // W_GK,)),
                pl.BlockSpec((1, tb, W_GV), lambda b, j: im(b, j) + (C_VG // W_GV,)),
                pl.BlockSpec((1, tb, 2 * LANE), lambda b, j: im(b, j) + (C_GG // (2 * LANE),))]

    return pl.pallas_call(
        _gla_kernel,
        out_shape=(jax.ShapeDtypeStruct((B, T, W_GV), BF16), jax.ShapeDtypeStruct((B, T, W_GV), BF16),
                   jax.ShapeDtypeStruct(s0.shape, F32)),
        grid=(B, nblk),
        in_specs=specs(fwd) + specs(bwd) + [
            pl.BlockSpec((2, LANE, W_GK), lambda b, j: (0, 0, 0)),
            pl.BlockSpec((2, 1, W_GK), lambda b, j: (0, 0, 0)),
            pl.BlockSpec((1, 2, 2, GLA_DV, LANE), lambda b, j: (b, 0, 0, 0, 0))],
        out_specs=(pl.BlockSpec((1, tb, W_GV), lambda b, j: (b, j, 0)),
                   pl.BlockSpec((1, tb, W_GV), lambda b, j: (b, nblk - 1 - j, 0)),
                   pl.BlockSpec((1, 2, 2, GLA_DV, LANE), lambda b, j: (b, 0, 0, 0, 0))),
        scratch_shapes=[pltpu.VMEM((2, 2, GLA_DV, LANE), F32)],
        compiler_params=_cparams(("parallel", "arbitrary")),
        name="gla_scan",
    )(p, p, p, p, p, p, p, p, wa_p, ba, s0)


def _deepnorm(h, gate, y, w, b):
    r = h + (gate * (1.0 / ALPHA)) * y
    mu = jnp.mean(r, axis=-1, keepdims=True)
    xc = r - mu
    var = jnp.mean(xc * xc, axis=-1, keepdims=True)
    return xc * lax.rsqrt(var + LN_EPS / (ALPHA * ALPHA)) * w + b


def _pack_rounded_pairs(xr):
    half = xr.shape[1] // 2
    lo = lax.bitcast_convert_type(xr[:, :half], jnp.uint32)
    hi = lax.bitcast_convert_type(xr[:, half:], jnp.uint32)
    return jnp.bitwise_or(hi, lax.shift_right_logical(lo, jnp.uint32(16)))


def _pack_bf16_pairs(x):
    return _pack_rounded_pairs(x.astype(BF16).astype(F32))


def _unpack_bf16_pairs(p):
    lo = lax.bitcast_convert_type(lax.shift_left(p, jnp.uint32(16)), F32)
    hi = lax.bitcast_convert_type(jnp.bitwise_and(p, jnp.uint32(0xFFFF0000)), F32)
    return lo, hi


def _post_norm_and_route(h, y, mod_ref, lnw_ref, lnb_ref, rwh_ref, rwl_ref, rb_ref, h1_ref, tok_ref, lg_ref):
    h1 = _deepnorm(h, mod_ref[0, 2:3, :], y, lnw_ref[...], lnb_ref[...])
    h1_ref[0] = h1
    tok = h1 * (1.0 + mod_ref[0, 4:5, :]) + mod_ref[0, 3:4, :]
    hi = tok.astype(BF16)
    hi_f = hi.astype(F32)
    tok_ref[0] = _pack_rounded_pairs(hi_f)
    lo = (tok - hi_f).astype(BF16)
    lg = _dot(hi, rwh_ref[...]) + _dot(lo, rwh_ref[...]) + _dot(hi, rwl_ref[...]) + rb_ref[...]
    lg_ref[...] = lg.T[0:ROUTE_ROWS, :]


def _epilogue_specs(tm, mb, b0):
    mod_map = (lambda b, j: (b + b0, 0, 0)) if mb > 1 else (lambda b, j: (0, 0, 0))
    const2 = lambda b, j: (0, 0)
    return [pl.BlockSpec((1, tm, D), lambda b, j: (b + b0, j, 0)),
            pl.BlockSpec((1, 6, D), mod_map),
            pl.BlockSpec((1, D), const2), pl.BlockSpec((1, D), const2),
            pl.BlockSpec((D, ROUTE_W), const2), pl.BlockSpec((D, ROUTE_W), const2),
            pl.BlockSpec((1, ROUTE_W), const2)]


def _epilogue_outs(B, T, tm):
    nj = T // tm
    shapes = (jax.ShapeDtypeStruct((B, T, D), F32), jax.ShapeDtypeStruct((B, T, D // 2), jnp.uint32),
              jax.ShapeDtypeStruct((ROUTE_ROWS, B * T), F32))
    specs = (pl.BlockSpec((1, tm, D), lambda b, j: (b, j, 0)), pl.BlockSpec((1, tm, D // 2), lambda b, j: (b, j, 0)),
             pl.BlockSpec((ROUTE_ROWS, tm), lambda b, j: (0, b * nj + j)))
    return shapes, specs


def _outproj_even_kernel(a_ref, of_ref, ob_ref, rg_ref, nw_ref, wo_ref,
                         h_ref, mod_ref, lnw_ref, lnb_ref, rwh_ref, rwl_ref, rb_ref,
                         h1_ref, tok_ref, lg_ref):
    parts = [a_ref[0]]
    for hd in range(GLA_HEADS):
        cols = slice(hd * GLA_DV, (hd + 1) * GLA_DV)
        o = of_ref[0, :, cols].astype(F32) + ob_ref[0, :, cols].astype(F32)
        o = o * lax.rsqrt(jnp.mean(o * o, axis=-1, keepdims=True) + RMS_EPS)
        parts.append((o * nw_ref[:, cols] * _silu(rg_ref[0, :, cols].astype(F32))).astype(BF16))
    y = _dot(jnp.concatenate(parts, axis=1), wo_ref[...])
    _post_norm_and_route(h_ref[0], y, mod_ref, lnw_ref, lnb_ref, rwh_ref, rwl_ref, rb_ref, h1_ref, tok_ref, lg_ref)


def outproj_even(a, o_f, o_b, p, norm_w, w_out, h, mod, lnw, lnb, rwh, rwl, rb, b0, nb):
    T = h.shape[1]
    tm = min(ROW_TILE, T)
    tile = lambda b, j: (b + b0, j, 0)
    shapes, ospecs = _epilogue_outs(nb, T, tm)
    return pl.pallas_call(
        _outproj_even_kernel,
        out_shape=shapes,
        grid=(nb, T // tm),
        in_specs=[pl.BlockSpec((1, tm, W_AQ), tile), pl.BlockSpec((1, tm, W_GV), tile), pl.BlockSpec((1, tm, W_GV), tile),
                  pl.BlockSpec((1, tm, W_GV), lambda b, j: (b + b0, j, C_RG // W_GV)),
                  pl.BlockSpec((1, W_GV), lambda b, j: (0, 0)),
                  pl.BlockSpec((D, D), lambda b, j: (0, 0))] + _epilogue_specs(tm, mod.shape[0], b0),
        out_specs=ospecs,
        compiler_params=_cparams(("parallel", "arbitrary")),
        name="outproj_even",
    )(a, o_f, o_b, p, norm_w, w_out, h, mod, lnw, lnb, rwh, rwl, rb)


def _inproj_odd_kernel(h_ref, mod_ref, w_ref, o_ref):
    u = (h_ref[0] * (1.0 + mod_ref[0, 1:2, :]) + mod_ref[0, 0:1, :]).astype(BF16)
    o_ref[0, :, 0:D] = _dot(u, w_ref[:, 0:D]).astype(BF16)
    o_ref[0, :, D:2 * D] = (_dot(u, w_ref[:, D:2 * D]) * _dot(u, w_ref[:, 2 * D:3 * D])).astype(BF16)


def inproj_odd(h, mod, w):
    B, T, _ = h.shape
    tm = min(IN_ROW_TILE, T)
    mb = mod.shape[0]
    return pl.pallas_call(
        _inproj_odd_kernel,
        out_shape=jax.ShapeDtypeStruct((B, T, 2 * D), BF16),
        grid=(B, T // tm),
        in_specs=[pl.BlockSpec((1, tm, D), lambda b, j: (b, j, 0)),
                  pl.BlockSpec((1, 6, D), (lambda b, j: (b, 0, 0)) if mb > 1 else (lambda b, j: (0, 0, 0))),
                  pl.BlockSpec((D, 3 * D), lambda b, j: (0, 0))],
        out_specs=pl.BlockSpec((1, tm, 2 * D), lambda b, j: (b, j, 0)),
        compiler_params=_cparams(("parallel", "arbitrary")),
        name="inproj_odd",
    )(h, mod, w)


HALO = 16


def _outproj_odd_kernel(gb_ref, z_ref, zp_ref, zn_ref, cw_ref, cb_ref, wo_ref,
                        h_ref, mod_ref, lnw_ref, lnb_ref, rwh_ref, rwl_ref, rb_ref,
                        h1_ref, tok_ref, lg_ref):
    j = pl.program_id(1)
    tm = z_ref.shape[1]
    z = z_ref[0].astype(F32)
    prev_row = jnp.where(j > 0, zp_ref[0, HALO - 1:HALO, :].astype(F32), 0.0)
    next_row = jnp.where(j < pl.num_programs(1) - 1, zn_ref[0, 0:1, :].astype(F32), 0.0)
    row = lax.broadcasted_iota(jnp.int32, (tm, D), 0)
    z_prev = jnp.where(row == 0, prev_row, pltpu.roll(z, 1, 0))
    z_next = jnp.where(row == tm - 1, next_row, pltpu.roll(z, tm - 1, 0))
    conv = z_prev * cw_ref[0:1, :] + z * cw_ref[1:2, :] + z_next * cw_ref[2:3, :] + cb_ref[...]
    y = _dot((gb_ref[0].astype(F32) * conv).astype(BF16), wo_ref[...])
    _post_norm_and_route(h_ref[0], y, mod_ref, lnw_ref, lnb_ref, rwh_ref, rwl_ref, rb_ref, h1_ref, tok_ref, lg_ref)


def outproj_odd(gz, conv_w, conv_b, w_out, h, mod, lnw, lnb, rwh, rwl, rb, b0, nb):
    T = h.shape[1]
    tm = min(ROW_TILE, T)
    r = tm // HALO
    nh = T // HALO
    shapes, ospecs = _epilogue_outs(nb, T, tm)
    return pl.pallas_call(
        _outproj_odd_kernel,
        out_shape=shapes,
        grid=(nb, T // tm),
        in_specs=[pl.BlockSpec((1, tm, D), lambda b, j: (b + b0, j, 0)),
                  pl.BlockSpec((1, tm, D), lambda b, j: (b + b0, j, 1)),
                  pl.BlockSpec((1, HALO, D), lambda b, j: (b + b0, jnp.maximum(j * r - 1, 0), 1)),
                  pl.BlockSpec((1, HALO, D), lambda b, j: (b + b0, jnp.minimum((j + 1) * r, nh - 1), 1)),
                  pl.BlockSpec((3, D), lambda b, j: (0, 0)),
                  pl.BlockSpec((1, D), lambda b, j: (0, 0)),
                  pl.BlockSpec((D, D), lambda b, j: (0, 0))] + _epilogue_specs(tm, mod.shape[0], b0),
        out_specs=ospecs,
        compiler_params=_cparams(("parallel", "arbitrary")),
        name="outproj_odd",
    )(gz, gz, gz, gz, conv_w, conv_b, w_out, h, mod, lnw, lnb, rwh, rwl, rb)


def _sc_mesh():
    return plsc.VectorSubcoreMesh(core_axis_name="c", subcore_axis_name="s")


def sc_gather_rows(table, idx):
    n = idx.shape[0]
    width = table.shape[1]
    per_w = n // SC_WORKERS
    n_chunks = per_w // SC_CHUNK
    assert n_chunks % 2 == 0

    @functools.partial(
        pl.kernel, mesh=_sc_mesh(),
        out_type=jax.ShapeDtypeStruct((n, width), table.dtype),
        scratch_types=[pltpu.VMEM((n_chunks, SC_CHUNK), jnp.int32),
                       pltpu.VMEM((SC_CHUNK, width), table.dtype), pltpu.VMEM((SC_CHUNK, width), table.dtype),
                       pltpu.SemaphoreType.DMA, pltpu.SemaphoreType.DMA],
    )
    def gather_kernel(table_hbm, idx_hbm, out_hbm, idx_v, buf0, buf1, sem0, sem1):
        wid = lax.axis_index("s") * SC_CORES + lax.axis_index("c")
        pltpu.sync_copy(idx_hbm.at[wid], idx_v)

        def fetch(j, buf, sem):
            return pltpu.make_async_copy(table_hbm.at[idx_v.at[j]], buf, sem)

        def flush(j, buf):
            pltpu.sync_copy(buf, out_hbm.at[pl.ds(wid * per_w + j * SC_CHUNK, SC_CHUNK)])

        fetch(0, buf0, sem0).start()

        @pl.loop(0, n_chunks, step=2)
        def _(j):
            fetch(j + 1, buf1, sem1).start()
            fetch(j, buf0, sem0).wait()
            flush(j, buf0)

            @pl.when(j + 2 < n_chunks)
            def _():
                fetch(j + 2, buf0, sem0).start()

            fetch(j + 1, buf1, sem1).wait()
            flush(j + 1, buf1)

    return gather_kernel(table, idx.reshape(SC_WORKERS, n_chunks, SC_CHUNK))


def sc_scatter_rows(srcs, idxs, n_rows):
    width, dt = srcs[0][0].shape[1], srcs[0][0].dtype
    plans, args = [], []
    for (src, row0, n_src), idx in zip(srcs, idxs):
        per_w = n_src // SC_WORKERS
        chunk = min(SC_CHUNK, per_w // 2)
        assert (per_w // chunk) % 2 == 0
        plans.append((per_w, chunk, per_w // chunk, idx.shape[0], row0))
        args += [src, idx.reshape(idx.shape[0], SC_WORKERS, per_w // chunk, chunk)]
    max_chunk = max(p[1] for p in plans)
    scratch = [pltpu.VMEM((max_chunk, width), dt), pltpu.VMEM((max_chunk, width), dt),
               pltpu.SemaphoreType.DMA, pltpu.SemaphoreType.DMA]
    scratch += [pltpu.VMEM((lists, n_chunks, chunk), jnp.int32) for _, chunk, n_chunks, lists, _ in plans]

    @functools.partial(pl.kernel, mesh=_sc_mesh(), out_type=jax.ShapeDtypeStruct((n_rows, width), dt),
                       scratch_types=scratch)
    def scatter_kernel(*refs):
        ins, out_hbm = refs[:2 * len(plans)], refs[2 * len(plans)]
        rows0, rows1, sem0, sem1 = refs[2 * len(plans) + 1:2 * len(plans) + 5]
        idx_vs = refs[2 * len(plans) + 5:]
        wid = lax.axis_index("s") * SC_CORES + lax.axis_index("c")
        for s, (per_w, chunk, n_chunks, lists, row0) in enumerate(plans):
            src_hbm, idx_hbm, idx_v = ins[2 * s], ins[2 * s + 1], idx_vs[s]
            for k in range(lists):
                pltpu.sync_copy(idx_hbm.at[k, wid], idx_v.at[k])
            buf0 = rows0 if chunk == max_chunk else rows0.at[pl.ds(0, chunk)]
            buf1 = rows1 if chunk == max_chunk else rows1.at[pl.ds(0, chunk)]

            def load(j, buf, sem, src_hbm=src_hbm, per_w=per_w, chunk=chunk, row0=row0):
                return pltpu.make_async_copy(src_hbm.at[pl.ds(row0 + wid * per_w + j * chunk, chunk)], buf, sem)

            def spread(j, buf, idx_v=idx_v, lists=lists):
                for k in range(lists):
                    pltpu.sync_copy(buf, out_hbm.at[idx_v.at[k, j]])

            load(0, buf0, sem0).start()

            @pl.loop(0, n_chunks, step=2)
            def _(j, load=load, spread=spread, buf0=buf0, buf1=buf1, n_chunks=n_chunks):
                load(j + 1, buf1, sem1).start()
                load(j, buf0, sem0).wait()
                spread(j, buf0)

                @pl.when(j + 2 < n_chunks)
                def _():
                    load(j + 2, buf0, sem0).start()

                load(j + 1, buf1, sem1).wait()
                spread(j + 1, buf1)

    return scatter_kernel(*args)


def _ffn_kernel(be_ref, nv_ref, x_ref, w1_ref, w3_ref, w2_ref, y_ref, w1b, w3b, w2b):
    i = pl.program_id(0)
    changed = jnp.logical_or(i == 0, be_ref[i] != be_ref[jnp.maximum(i - 1, 0)])

    @pl.when(changed)
    def _():
        w1b[...] = w1_ref[0, 0].astype(BF16)
        w3b[...] = w3_ref[0, 0].astype(BF16)
        w2b[...] = w2_ref[0, 0].astype(BF16)

    @pl.when(i < nv_ref[0])
    def _():
        x = jnp.concatenate([v.astype(BF16) for v in _unpack_bf16_pairs(x_ref[...])], axis=1)
        y_ref[...] = _pack_bf16_pairs(_dot((_silu(_dot(x, w1b[...])) * _dot(x, w3b[...])).astype(BF16), w2b[...]))

    @pl.when(i >= nv_ref[0])
    def _():
        y_ref[...] = jnp.zeros_like(y_ref)


def moe_ffn(xs, blk_expert, n_valid, w1, w3, w2, layer):
    n_rows = xs.shape[0]
    nb = n_rows // MOE_TM
    return pl.pallas_call(
        _ffn_kernel,
        out_shape=jax.ShapeDtypeStruct((n_rows, D // 2), jnp.uint32),
        grid_spec=pltpu.PrefetchScalarGridSpec(
            num_scalar_prefetch=2,
            grid=(nb,),
            in_specs=[pl.BlockSpec((MOE_TM, D // 2), lambda i, be, nv: (i, 0)),
                      pl.BlockSpec((1, 1, D, D_EXPERT), lambda i, be, nv: (layer, be[i], 0, 0)),
                      pl.BlockSpec((1, 1, D, D_EXPERT), lambda i, be, nv: (layer, be[i], 0, 0)),
                      pl.BlockSpec((1, 1, D_EXPERT, D), lambda i, be, nv: (layer, be[i], 0, 0))],
            out_specs=pl.BlockSpec((MOE_TM, D // 2), lambda i, be, nv: (i, 0)),
            scratch_shapes=[pltpu.VMEM((D, D_EXPERT), BF16), pltpu.VMEM((D, D_EXPERT), BF16),
                            pltpu.VMEM((D_EXPERT, D), BF16)]),
        compiler_params=_cparams(("arbitrary",)),
        name="moe_ffn",
    )(blk_expert, n_valid, xs, w1, w3, w2)


def _combine_kernel(h_ref, y0_ref, y1_ref, wt_ref, mod_ref, lnw_ref, lnb_ref, o_ref):
    half = D // 2
    lo0, hi0 = _unpack_bf16_pairs(y0_ref[0])
    lo1, hi1 = _unpack_bf16_pairs(y1_ref[0])
    pick = (lax.broadcasted_iota(jnp.int32, (WT_ROWS, LANE), 0) == lax.broadcasted_iota(jnp.int32, (WT_ROWS, LANE), 1))
    wcols = lax.dot_general(wt_ref[...], pick.astype(F32), (((0,), (0,)), ((), ())), precision=HIGHEST,
                            preferred_element_type=F32)
    w0, w1 = wcols[:, 0:1], wcols[:, 1:2]
    gate = mod_ref[0, 5:6, :] * (1.0 / ALPHA)
    r_lo = h_ref[:, 0:half] + gate[:, 0:half] * (w0 * lo0 + w1 * lo1)
    r_hi = h_ref[:, half:D] + gate[:, half:D] * (w0 * hi0 + w1 * hi1)
    mu = (jnp.sum(r_lo, axis=-1, keepdims=True) + jnp.sum(r_hi, axis=-1, keepdims=True)) * (1.0 / D)
    c_lo, c_hi = r_lo - mu, r_hi - mu
    var = (jnp.sum(c_lo * c_lo, axis=-1, keepdims=True) + jnp.sum(c_hi * c_hi, axis=-1, keepdims=True)) * (1.0 / D)
    inv = lax.rsqrt(var + LN_EPS / (ALPHA * ALPHA))
    o_ref[:, 0:half] = c_lo * inv * lnw_ref[:, 0:half] + lnb_ref[:, 0:half]
    o_ref[:, half:D] = c_hi * inv * lnw_ref[:, half:D] + lnb_ref[:, half:D]


def _combine_into_kernel(*refs):
    _combine_kernel(*refs[:7], refs[8])


def moe_combine(h1, y_rows, wts, mod, lnw, lnb, y_tok0, out_tok0, n_out, prev=None):
    nb, T, _ = h1.shape
    ntok, N = nb * T, n_out
    tm = min(ROW_TILE, ntok)
    per_b = T // tm
    h_off, y_off = out_tok0 // tm, y_tok0 // tm
    mod_map = (lambda i: ((i + h_off) // per_b, 0, 0)) if mod.shape[0] > 1 else (lambda i: (0, 0, 0))
    in_specs = [pl.BlockSpec((tm, D), lambda i: (i, 0)),
                pl.BlockSpec((1, tm, D // 2), lambda i: (0, i + y_off, 0)),
                pl.BlockSpec((1, tm, D // 2), lambda i: (1, i + y_off, 0)),
                pl.BlockSpec((WT_ROWS, tm), lambda i: (0, i + y_off)),
                pl.BlockSpec((1, 6, D), mod_map),
                pl.BlockSpec((1, D), lambda i: (0, 0)), pl.BlockSpec((1, D), lambda i: (0, 0))]
    args = [h1.reshape(ntok, D), y_rows, y_rows, wts, mod, lnw, lnb]
    if prev is not None:
        in_specs.append(pl.BlockSpec(memory_space=pl.ANY))
        args.append(prev)
    return pl.pallas_call(
        _combine_kernel if prev is None else _combine_into_kernel,
        out_shape=jax.ShapeDtypeStruct((N, D), F32),
        grid=(ntok // tm,),
        in_specs=in_specs,
        out_specs=pl.BlockSpec((tm, D), lambda i: (i + h_off, 0)),
        input_output_aliases={} if prev is None else {7: 0},
        compiler_params=_cparams(("parallel",)),
        name="moe_combine",
    )(*args)


def _route_kernel(lg_ref, dest_ref, wt_ref, cnt_ref, tri_sc, start_sc, run_sc):
    ph, i = pl.program_id(0), pl.program_id(1)
    tr = lg_ref.shape[1]

    @pl.when(jnp.logical_and(ph == 0, i == 0))
    def _():
        r = lax.broadcasted_iota(jnp.int32, (LANE, LANE), 0)
        c = lax.broadcasted_iota(jnp.int32, (LANE, LANE), 1)
        tri_sc[...] = (r < c).astype(BF16)
        start_sc[...] = jnp.zeros_like(start_sc)
        run_sc[...] = jnp.zeros_like(run_sc)

    @pl.when(jnp.logical_and(ph == 1, i == 0))
    def _():
        cnt = run_sc[...].astype(jnp.int32)
        cnt_ref[...] = cnt
        padded = jnp.bitwise_and(cnt + (MOE_TM - 1), -MOE_TM)
        row = lax.broadcasted_iota(jnp.int32, padded.shape, 0)
        acc = padded
        for s in (1, 2, 4, 8, 16):
            acc = acc + jnp.where(row >= s, pltpu.roll(acc, s, 0), 0)
        start_sc[...] = (acc - padded).astype(F32)
        run_sc[...] = jnp.zeros_like(run_sc)

    lg = lg_ref[...]
    gl = lg[N_EXPERTS:N_EXPERTS + N_GROUPS]
    gmax = jnp.max(gl, axis=0, keepdims=True)
    sub4 = lax.broadcasted_iota(jnp.int32, gl.shape, 0)
    g_sel = jnp.min(jnp.where(gl == gmax, sub4, N_GROUPS), axis=0, keepdims=True)
    p_group = 1.0 / jnp.sum(jnp.exp(gl - gmax), axis=0, keepdims=True)
    el = lg[0:EXPERTS_PER_GROUP]
    for g in range(1, N_GROUPS):
        el = jnp.where(g_sel == g, lg[g * EXPERTS_PER_GROUP:(g + 1) * EXPERTS_PER_GROUP], el)
    sub8 = lax.broadcasted_iota(jnp.int32, el.shape, 0)
    e1 = jnp.max(el, axis=0, keepdims=True)
    i1 = jnp.min(jnp.where(el == e1, sub8, EXPERTS_PER_GROUP), axis=0, keepdims=True)
    rest = jnp.where(sub8 == i1, -jnp.inf, el)
    e2 = jnp.max(rest, axis=0, keepdims=True)
    i2 = jnp.min(jnp.where(rest == e2, sub8, EXPERTS_PER_GROUP), axis=0, keepdims=True)
    den = jnp.sum(jnp.exp(el - e1), axis=0, keepdims=True)
    p1 = 1.0 / den
    p2 = jnp.exp(e2 - e1) / den
    wt_ref[...] = jnp.zeros_like(wt_ref)
    wt_ref[0:1, :] = p_group * p1 / (p1 + p2)
    wt_ref[1:2, :] = p_group * p2 / (p1 + p2)

    sub32 = lax.broadcasted_iota(jnp.int32, (N_EXPERTS, tr), 0)
    oh = [(sub32 == g_sel * EXPERTS_PER_GROUP + ix).astype(F32) for ix in (i1, i2)]
    cnt = [jnp.sum(o, axis=1, keepdims=True) for o in oh]

    @pl.when(ph == 0)
    def _():
        dest_ref[...] = jnp.zeros_like(dest_ref)

    @pl.when(ph == 1)
    def _():
        before = start_sc[:, 0:1] + run_sc[:, 0:1]
        for k in range(TOP_K):
            subs = [oh[k][:, s * LANE:(s + 1) * LANE] for s in range(tr // LANE)]
            local = _dot(jnp.concatenate(subs, axis=0).astype(BF16), tri_sc[...])
            seen = before + (cnt[0] if k == 1 else 0.0)
            for s, sub in enumerate(subs):
                prior = local[s * N_EXPERTS:(s + 1) * N_EXPERTS] + seen
                dest_ref[k:k + 1, s * LANE:(s + 1) * LANE] = (
                    jnp.sum(sub * prior, axis=0, keepdims=True).astype(jnp.int32))
                seen = seen + jnp.sum(sub, axis=1, keepdims=True)

    run_sc[...] = run_sc[...] + (cnt[0] + cnt[1])


def moe_route(logits_t, col0, N):
    tr = next(t for t in (4096, 2048, 1024, 512, 256) if N % t == 0 and col0 % t == 0)
    t0 = col0 // tr
    return pl.pallas_call(
        _route_kernel,
        out_shape=(jax.ShapeDtypeStruct((TOP_K, N), jnp.int32), jax.ShapeDtypeStruct((WT_ROWS, N), F32),
                   jax.ShapeDtypeStruct((N_EXPERTS, LANE), jnp.int32)),
        grid=(2, N // tr),
        in_specs=[pl.BlockSpec((ROUTE_ROWS, tr), lambda p, i: (0, i + t0))],
        out_specs=(pl.BlockSpec((TOP_K, tr), lambda p, i: (0, i * p)), pl.BlockSpec((WT_ROWS, tr), lambda p, i: (0, i * p)),
                   pl.BlockSpec((N_EXPERTS, LANE), lambda p, i: (0, 0))),
        scratch_shapes=[pltpu.VMEM((LANE, LANE), BF16), pltpu.VMEM((N_EXPERTS, LANE), F32),
                        pltpu.VMEM((N_EXPERTS, LANE), F32)],
        compiler_params=_cparams(("arbitrary", "arbitrary")),
        name="moe_route",
    )(logits_t)


def _block_tables(counts, n_assign):
    padded = (counts + MOE_TM - 1) // MOE_TM * MOE_TM
    pad_end = jnp.cumsum(padded)
    pad_start = pad_end - padded
    nb = -(-n_assign // MOE_TM) + N_EXPERTS
    blk_start = jnp.arange(nb, dtype=jnp.int32) * MOE_TM
    blk_expert = jnp.minimum(jnp.sum((pad_end[None, :] <= blk_start[:, None]).astype(jnp.int32), axis=1), N_EXPERTS - 1)
    n_valid = (pad_end[-1] // MOE_TM).astype(jnp.int32).reshape(1)
    n_fill = nb * MOE_TM - n_assign
    gap = padded - counts
    gap_end = jnp.cumsum(gap)
    k = jnp.arange(n_fill, dtype=jnp.int32)[:, None]
    sel = jnp.logical_and(k >= (gap_end - gap)[None, :], k < gap_end[None, :])
    in_gap = jnp.sum(jnp.where(sel, (pad_start + counts - (gap_end - gap))[None, :] + k, 0), axis=1)
    fill = jnp.where(k[:, 0] < gap_end[-1], in_gap, pad_end[-1] + k[:, 0] - gap_end[-1])
    return blk_expert.astype(jnp.int32), n_valid, fill.astype(jnp.int32), nb * MOE_TM


def hier_moe_and_norm(streams, w1, w3, w2, layer, lnw, lnb):
    lat_parts, ctx = [s for s in streams if s[4]], next((s for s in streams if not s[4]), None)
    n_ctx = ctx[0].shape[0] * ctx[0].shape[1] if ctx else 0
    n_lat_all = sum(s[0].shape[0] * s[0].shape[1] for s in lat_parts)
    routed, sorted_in, tables = [], [], []
    for r, part in enumerate(lat_parts):
        n_lat = part[0].shape[0] * part[0].shape[1]
        with_ctx = ctx is not None and r == len(lat_parts) - 1
        n = n_lat + (n_ctx if with_ctx else 0)
        logits_t = jnp.concatenate([part[2], ctx[2]], 1) if with_ctx else part[2]
        dest, wts, counts = moe_route(logits_t, 0, n)
        blk_expert, n_valid, fill_rows, n_rows = _block_tables(counts[:, 0], TOP_K * n)
        srcs, idxs = [(part[1].reshape(n_lat, D // 2), 0, n_lat)], [dest[:, :n_lat]]
        if with_ctx:
            srcs.append((ctx[1].reshape(n_ctx, D // 2), 0, n_ctx))
            idxs.append(dest[:, n_lat:])
        srcs.append((jnp.zeros((fill_rows.shape[0], D // 2), jnp.uint32), 0, fill_rows.shape[0]))
        idxs.append(fill_rows.reshape(1, -1))
        routed.append((dest, wts))
        sorted_in.append(sc_scatter_rows(srcs, idxs, n_rows))
        tables.append((blk_expert, n_valid, n_lat, n, with_ctx))
    out_lat, out_ctx, tok0 = None, None, 0
    for part, (dest, wts), xs, (blk_expert, n_valid, n_lat, n, with_ctx) in zip(lat_parts, routed, sorted_in, tables):
        y = moe_ffn(xs, blk_expert, n_valid, w1, w3, w2, layer)
        y_rows = sc_gather_rows(y, dest.reshape(-1)).reshape(TOP_K, n, D // 2)
        out_lat = moe_combine(part[0], y_rows, wts, part[3], lnw, lnb, 0, tok0, n_lat_all, out_lat)
        if with_ctx:
            out_ctx = moe_combine(ctx[0], y_rows, wts, ctx[3], lnw, lnb, n_lat, 0, n_ctx)
        tok0 += n_lat
    return out_lat, out_ctx


def _even_projection_weights(w):
    def rope_layout(x, heads):
        return x.reshape(D, heads, 2, 2, HEAD_DIM // 4).transpose(0, 1, 3, 2, 4).reshape(D, heads * HEAD_DIM)

    def twice(x):
        return jnp.concatenate([x[:, :HEAD_DIM], x[:, :HEAD_DIM], x[:, HEAD_DIM:], x[:, HEAD_DIM:]], axis=1)

    def lane_pad(x):
        return jnp.pad(x, ((0, 0), (0, LANE - x.shape[1])))

    qa, ka, va, qg, kg, vg, rg, gg = jnp.split(w, [512, 640, 768, 1024, 1280, 1792, 2304], axis=1)
    cols = [rope_layout(qa, A_Q_HEADS) * (HEAD_DIM ** -0.5 * LOG2_E),
            twice(rope_layout(ka, A_KV_HEADS)), twice(va), vg, rg, qg * GLA_DK ** -0.5, kg,
            lane_pad(gg[:, :GLA_RANK]), lane_pad(gg[:, GLA_RANK:])]
    return jnp.concatenate(cols, axis=1).astype(BF16)


def _rope_tables(S):
    row = jnp.repeat(jnp.arange(S // GRID_W), GRID_W).astype(F32)
    col = jnp.tile(jnp.arange(GRID_W), S // GRID_W).astype(F32)
    axis_dim = HEAD_DIM // 2
    inv_freq = ROPE_BASE ** (-jnp.arange(0, axis_dim, 2, dtype=F32) / axis_dim)
    ang = jnp.concatenate([row[:, None] * inv_freq, col[:, None] * inv_freq], -1)
    cos, sin = jnp.cos(ang), jnp.sin(ang)
    cos_t = jnp.tile(cos, (1, 4))
    sin_t = jnp.tile(jnp.concatenate([-sin, sin], -1), (1, 2))
    return cos_t, sin_t


def _router_weights(wg, bg, we, be):
    w = jnp.zeros((D, ROUTE_W), F32).at[:, :N_EXPERTS].set(we).at[:, N_EXPERTS:N_EXPERTS + N_GROUPS].set(wg)
    b = jnp.zeros((1, ROUTE_W), F32).at[0, :N_EXPERTS].set(be).at[0, N_EXPERTS:N_EXPERTS + N_GROUPS].set(bg)
    hi = w.astype(BF16)
    return hi, (w - hi.astype(F32)).astype(BF16), b


def kernel(x, c, ctx, c_ctx, w_in_even, w_out_even, attn_sink, gla_wa2, gla_ba, gla_norm_w, w_in_odd, conv_w, conv_b, w_out_odd, ada_w, ada_b, ln_w, ln_b, router_wg, router_bg, router_we, router_be, moe_w1, moe_w3, moe_w2):
    B, S, _ = x.shape
    L = ctx.shape[1]
    cos_t, sin_t = _rope_tables(S)
    cos_c, sin_c = jnp.ones((L, LANE), F32), jnp.zeros((L, LANE), F32)

    n_cond = -(-(B + 1) // 8) * 8
    cc = jnp.zeros((n_cond, D), F32).at[:B].set(c).at[B].set(c_ctx)
    mods = ada_modulation_all(cc, ada_w, ada_b).reshape(DEPTH, n_cond, 6, D)

    h_lat, h_ctx = x, ctx
    for l in range(DEPTH):
        i = l // 2
        need_ctx = any(j % 2 == 0 for j in range(l + 1, DEPTH))
        m_lat = mods[l, :B]
        m_ctx = mods[l, B:B + 1]
        lnw0, lnb0 = ln_w[l, 0:1], ln_b[l, 0:1]
        lnw1, lnb1 = ln_w[l, 1:2], ln_b[l, 1:2]
        rwh, rwl, rb = _router_weights(router_wg[l], router_bg[l], router_we[l], router_be[l])
        streams = []
        if l % 2 == 0:
            w_in = _even_projection_weights(w_in_even[i])
            w_out = w_out_even[i].astype(BF16)
            wa_p = jnp.zeros((2, LANE, GLA_HEADS * GLA_DK), F32).at[:, :GLA_RANK].set(gla_wa2[i]).astype(BF16)
            ba = gla_ba[i].reshape(2, 1, -1)
            nw = gla_norm_w[i].reshape(1, -1)
            p_ctx = inproj_even(h_ctx, m_ctx, w_in, cos_c, sin_c)
            p_lat = inproj_even(h_lat, m_lat, w_in, cos_t, sin_t)
            a_lat = attention(p_lat, p_ctx, attn_sink[i], True)
            s0 = jnp.zeros((B, 2, 2, GLA_DV, LANE), F32)
            oc_f, oc_b, s_ctx = gla_scan(p_ctx, wa_p, ba, s0)
            ol_f, ol_b, _ = gla_scan(p_lat, wa_p, ba, s_ctx)
            for b0 in range(0, B, B // 2):
                streams.append(outproj_even(a_lat, ol_f, ol_b, p_lat, nw, w_out, h_lat, m_lat, lnw0, lnb0,
                                            rwh, rwl, rb, b0, B // 2) + (m_lat, True))
            if need_ctx:
                a_ctx = attention(p_ctx, p_ctx, attn_sink[i], False)
                streams.append(outproj_even(a_ctx, oc_f, oc_b, p_ctx, nw, w_out, h_ctx, m_ctx, lnw0, lnb0,
                                            rwh, rwl, rb, 0, B) + (m_ctx, False))
        else:
            w_in = w_in_odd[i].astype(BF16)
            w_out = w_out_odd[i].astype(BF16)
            cb = conv_b[i].reshape(1, D)
            gz = inproj_odd(h_lat, m_lat, w_in)
            for b0 in range(0, B, B // 2):
                streams.append(outproj_odd(gz, conv_w[i], cb, w_out, h_lat, m_lat, lnw0, lnb0, rwh, rwl, rb,
                                           b0, B // 2) + (m_lat, True))
            if need_ctx:
                gz_ctx = inproj_odd(h_ctx, m_ctx, w_in)
                streams.append(outproj_odd(gz_ctx, conv_w[i], cb, w_out, h_ctx, m_ctx, lnw0, lnb0, rwh, rwl, rb,
                                           0, B) + (m_ctx, False))
        out_lat, out_ctx = hier_moe_and_norm(streams, moe_w1, moe_w3, moe_w2, l, lnw1, lnb1)
        h_lat = out_lat.reshape(B, S, D)
        if need_ctx:
            h_ctx = out_ctx.reshape(B, L, D)
    return h_lat
```

```python
import functools

import jax
import jax.numpy as jnp
from jax import lax
from jax.experimental import pallas as pl
from jax.experimental.pallas import tpu as pltpu
from jax.experimental.pallas import tpu_sc as plsc

F32 = jnp.float32
BF16 = jnp.bfloat16
HIGHEST = lax.Precision.HIGHEST

D = 1024
DEPTH = 4
GRID_W = 64
HEAD_DIM = 64
A_Q_HEADS = 8
A_KV_HEADS = 2
WINDOW = 128
ROPE_BASE = 10000.0
GLA_HEADS = 4
GLA_DK = 64
GLA_DV = 128
GLA_RANK = 16
GLA_TAU = 16.0
GLA_CHUNK = 64
N_GROUPS = 4
EXPERTS_PER_GROUP = 8
N_EXPERTS = 32
TOP_K = 2
D_EXPERT = 512
ALPHA = (2.0 * DEPTH) ** 0.25
LN_EPS = 1e-5
RMS_EPS = 1e-6

LANE = 128
VMEM_LIMIT = 48 * 1024 * 1024

ROW_TILE = 512
IN_ROW_TILE = 1024
W_AQ = A_Q_HEADS * HEAD_DIM
W_KV2 = 2 * A_KV_HEADS * HEAD_DIM
W_GK = GLA_HEADS * GLA_DK
W_GV = GLA_HEADS * GLA_DV
C_QA = 0
C_KD = C_QA + W_AQ
C_VD = C_KD + W_KV2
C_VG = C_VD + W_KV2
C_RG = C_VG + W_GV
C_QG = C_RG + W_GV
C_KG = C_QG + W_GK
C_GG = C_KG + W_GK
P_W = C_GG + 2 * LANE
ROUTE_W = 128
WT_ROWS = 8
ROUTE_ROWS = 40
MOE_TM = 512
SC_CORES, SC_SUBCORES = 2, 16
SC_WORKERS = SC_CORES * SC_SUBCORES
SC_CHUNK = 64
NEG = -1e30
LOG2_E = 1.4426950408889634


def _cparams(sem):
    return pltpu.CompilerParams(dimension_semantics=sem, vmem_limit_bytes=VMEM_LIMIT)


def _dot(a, b):
    return jnp.dot(a, b, preferred_element_type=F32)


def _dot_nt(a, b):
    return lax.dot_general(a, b, (((1,), (1,)), ((), ())), preferred_element_type=F32)


def _dot_tn(a, b):
    return lax.dot_general(a, b, (((0,), (0,)), ((), ())), preferred_element_type=F32)


def _silu(x):
    return x * (1.0 / (1.0 + jnp.exp(-x)))


def _ada_kernel(c_ref, w_ref, b_ref, o_ref):
    s = _silu(c_ref[...])
    o_ref[0] = jnp.dot(s, w_ref[0], precision=HIGHEST, preferred_element_type=F32) + b_ref[0]


def ada_modulation_all(cc, ada_w, ada_b):
    R = cc.shape[0]
    tn = 1536
    return pl.pallas_call(
        _ada_kernel,
        out_shape=jax.ShapeDtypeStruct((DEPTH, R, 6 * D), F32),
        grid=(DEPTH, 6 * D // tn),
        in_specs=[pl.BlockSpec((R, D), lambda l, n: (0, 0)),
                  pl.BlockSpec((1, D, tn), lambda l, n: (l, 0, n)),
                  pl.BlockSpec((1, 1, tn), lambda l, n: (l, 0, n))],
        out_specs=pl.BlockSpec((1, R, tn), lambda l, n: (l, 0, n)),
        compiler_params=_cparams(("arbitrary", "arbitrary")),
        name="ada_modulation",
    )(cc, ada_w, ada_b.reshape(DEPTH, 1, 6 * D))


_EVEN_CHUNKS = ((C_QA, C_KD, True), (C_KD, C_VD, True), (C_VD, C_VG, False), (C_VG, C_RG, False),
                (C_RG, C_QG, False), (C_QG, C_GG, False), (C_GG, P_W, False))


def _inproj_even_kernel(h_ref, mod_ref, w_ref, cos_ref, sin_ref, p_ref):
    tm = h_ref.shape[1]
    u = (h_ref[0] * (1.0 + mod_ref[0, 1:2, :]) + mod_ref[0, 0:1, :]).astype(BF16)
    cos = cos_ref[...]
    sin = sin_ref[...]
    lane = lax.broadcasted_iota(jnp.int32, (tm, LANE), 1)
    first_half = (lane % HEAD_DIM) < (HEAD_DIM // 2)
    for c0, c1, rope in _EVEN_CHUNKS:
        acc = _dot(u, w_ref[:, c0:c1])
        if rope:
            for i in range((c1 - c0) // LANE):
                x = acc[:, i * LANE:(i + 1) * LANE]
                partner = jnp.where(first_half, pltpu.roll(x, LANE - 32, 1), pltpu.roll(x, 32, 1))
                p_ref[0, :, c0 + i * LANE:c0 + (i + 1) * LANE] = (x * cos + partner * sin).astype(BF16)
        else:
            p_ref[0, :, c0:c1] = acc.astype(BF16)


def inproj_even(h, mod, w, cos_t, sin_t):
    B, T, _ = h.shape
    tm = min(IN_ROW_TILE, T)
    mb = mod.shape[0]
    return pl.pallas_call(
        _inproj_even_kernel,
        out_shape=jax.ShapeDtypeStruct((B, T, P_W), BF16),
        grid=(B, T // tm),
        in_specs=[pl.BlockSpec((1, tm, D), lambda b, j: (b, j, 0)),
                  pl.BlockSpec((1, 6, D), (lambda b, j: (b, 0, 0)) if mb > 1 else (lambda b, j: (0, 0, 0))),
                  pl.BlockSpec((D, P_W), lambda b, j: (0, 0)),
                  pl.BlockSpec((tm, LANE), lambda b, j: (j, 0)),
                  pl.BlockSpec((tm, LANE), lambda b, j: (j, 0))],
        out_specs=pl.BlockSpec((1, tm, P_W), lambda b, j: (b, j, 0)),
        compiler_params=_cparams(("parallel", "arbitrary")),
        name="inproj_even",
    )(h, mod, w, cos_t, sin_t)


def _attn_kernel(*refs, tq, tiles, has_window):
    if has_window:
        sink_ref, q_ref, kw_ref, vw_ref, kc_ref, vc_ref, o_ref = refs
    else:
        sink_ref, q_ref, kc_ref, vc_ref, o_ref = refs
    group = A_Q_HEADS // A_KV_HEADS
    rows = group * tq
    lo = lax.broadcasted_iota(jnp.int32, (tq, LANE), 1) < HEAD_DIM
    den_lanes = lax.broadcasted_iota(jnp.int32, (rows, LANE), 1) >= HEAD_DIM

    def with_ones(v):
        return jnp.where(lax.broadcasted_iota(jnp.int32, v.shape, 1) < HEAD_DIM, v, jnp.ones_like(v))

    wstart, band = [], []
    if has_window:
        S = kw_ref.shape[1]
        wk = tq + 2 * WINDOW
        for t in range(tiles):
            q0 = (pl.program_id(1) * tiles + t) * tq
            wstart.append(pl.multiple_of(jnp.clip(q0 - WINDOW, 0, S - wk), LANE))
            qpos = q0 + lax.broadcasted_iota(jnp.int32, (tq, wk), 0)
            kpos = wstart[t] + lax.broadcasted_iota(jnp.int32, (tq, wk), 1)
            band.append(jnp.tile(jnp.where(jnp.abs(qpos - kpos) <= WINDOW, 0.0, NEG), (group, 1)))
    chains = [(t, g) for t in range(tiles) for g in range(A_KV_HEADS)]
    rt = [slice(t * tq, (t + 1) * tq) for t in range(tiles)]
    cols = [slice(g * LANE, (g + 1) * LANE) for g in range(A_KV_HEADS)]
    q4, snk, sc, m, outs = [], [], [], [], []
    for t, g in chains:
        qs = []
        for pr in range(group // 2):
            qblk = q_ref[0, rt[t], (2 * g + pr) * LANE:(2 * g + pr + 1) * LANE]
            zero = jnp.zeros_like(qblk)
            qs += [jnp.where(lo, qblk, zero), jnp.where(lo, zero, qblk)]
        q4.append(jnp.concatenate(qs, axis=0))
        snk.append(jnp.concatenate([jnp.full((tq, 1), sink_ref[group * g + i] * LOG2_E, F32) for i in range(group)],
                                   axis=0))
    for c, (t, g) in enumerate(chains):
        if has_window:
            keys = jnp.concatenate([kw_ref[0, pl.ds(wstart[t], wk), cols[g]], kc_ref[0, :, cols[g]]], axis=0)
            s = _dot_nt(q4[c], keys)
            sc.append(jnp.concatenate([s[:, 0:wk] + band[t], s[:, wk:]], axis=1))
        else:
            sc.append(_dot_nt(q4[c], kc_ref[0, :, cols[g]]))
    for c in range(len(chains)):
        m.append(jnp.maximum(jnp.max(sc[c], axis=-1, keepdims=True), snk[c]))
    for c, (t, g) in enumerate(chains):
        if has_window:
            vals = jnp.concatenate([vw_ref[0, pl.ds(wstart[t], wk), cols[g]], vc_ref[0, :, cols[g]]], axis=0)
        else:
            vals = vc_ref[0, :, cols[g]]
        o = _dot(jnp.exp2((sc[c] - m[c]).astype(BF16)), with_ones(vals))
        outs.append(o + jnp.where(den_lanes, jnp.exp2(snk[c] - m[c]), 0.0))
    for c, (t, g) in enumerate(chains):
        o = outs[c]
        swapped = pltpu.roll(o, HEAD_DIM, 1)
        for pr in range(group // 2):
            ev = slice(2 * pr * tq, (2 * pr + 1) * tq)
            od = slice((2 * pr + 1) * tq, (2 * pr + 2) * tq)
            res = jnp.where(lo, o[ev] / swapped[ev], swapped[od] / o[od])
            o_ref[0, rt[t], (2 * g + pr) * LANE:(2 * g + pr + 1) * LANE] = res.astype(BF16)


def attention(p_q, p_ctx, sink, has_window):
    B, T, _ = p_q.shape
    L = p_ctx.shape[1]
    tq = 128
    tiles = next(n for n in (4, 2, 1) if T % (n * tq) == 0)
    in_specs = [pl.BlockSpec(memory_space=pltpu.SMEM),
                pl.BlockSpec((1, tiles * tq, W_AQ), lambda b, j: (b, j, C_QA // W_AQ))]
    args = [sink, p_q]
    if has_window:
        in_specs += [pl.BlockSpec((1, T, W_KV2), lambda b, j: (b, 0, C_KD // W_KV2)),
                     pl.BlockSpec((1, T, W_KV2), lambda b, j: (b, 0, C_VD // W_KV2))]
        args += [p_q, p_q]
    in_specs += [pl.BlockSpec((1, L, W_KV2), lambda b, j: (b, 0, C_KD // W_KV2)),
                 pl.BlockSpec((1, L, W_KV2), lambda b, j: (b, 0, C_VD // W_KV2))]
    args += [p_ctx, p_ctx]
    return pl.pallas_call(
        functools.partial(_attn_kernel, tq=tq, tiles=tiles, has_window=has_window),
        out_shape=jax.ShapeDtypeStruct((B, T, W_AQ), BF16),
        grid=(B, T // (tiles * tq)),
        in_specs=in_specs,
        out_specs=pl.BlockSpec((1, tiles * tq, W_AQ), lambda b, j: (b, j, 0)),
        compiler_params=_cparams(("parallel", "arbitrary")),
        name="window_attention" if has_window else "context_attention",
    )(*args)


def _log_sigmoid(x):
    return jnp.minimum(x, 0.0) - jnp.log(1.0 + jnp.exp(-jnp.abs(x)))


def _gla_kernel(qf_ref, kf_ref, vf_ref, gf_ref, qb_ref, kb_ref, vb_ref, gb_ref, wa_ref, ba_ref, s0_ref,
                of_ref, ob_ref, sfin_ref, s_sc):
    j = pl.program_id(1)
    nblk = pl.num_programs(1)
    tb = qf_ref.shape[1]
    nc = tb // GLA_CHUNK

    @pl.when(j == 0)
    def _():
        s_sc[...] = s0_ref[0]

    C = GLA_CHUNK
    hc = GLA_HEADS * C
    ri = lax.broadcasted_iota(jnp.int32, (hc, hc), 0) % C
    ci = lax.broadcasted_iota(jnp.int32, (hc, hc), 1) % C
    rb = lax.broadcasted_iota(jnp.int32, (tb, tb), 0)
    cb = lax.broadcasted_iota(jnp.int32, (tb, tb), 1)
    same_chunk = (rb // C) == (cb // C)
    lane_head = lax.broadcasted_iota(jnp.int32, (C, GLA_HEADS * GLA_DK), 1) // GLA_DK

    def per_head(x):
        zero = jnp.zeros_like(x)
        return jnp.concatenate([jnp.where(lane_head == h, x, zero) for h in range(GLA_HEADS)], axis=0)

    io = ((qf_ref, kf_ref, vf_ref, gf_ref, of_ref), (qb_ref, kb_ref, vb_ref, gb_ref, ob_ref))
    causal = ((ri >= ci), (ci >= ri))
    tri = (jnp.logical_and(same_chunk, rb >= cb).astype(BF16), jnp.logical_and(same_chunk, cb >= rb).astype(BF16))
    b_all = []
    for d in range(2):
        g = _dot(io[d][3][0, :, d * LANE:(d + 1) * LANE], wa_ref[d]) + ba_ref[d]
        log_a = _log_sigmoid(g) / GLA_TAU
        la1 = log_a.astype(BF16)
        rem = log_a - la1.astype(F32)
        la2 = rem.astype(BF16)
        la3 = (rem - la2.astype(F32)).astype(BF16)
        b_all.append(_dot(tri[d], la1) + _dot(tri[d], la2) + _dot(tri[d], la3))
    state = [jnp.concatenate([s_sc[d, 0], s_sc[d, 1]], axis=1) for d in range(2)]
    zero_blk = jnp.zeros((GLA_DV, LANE), BF16)
    for step in range(nc):
        for d in range(2):
            q_ref, k_ref, v_ref, _, o_ref = io[d]
            c = step if d == 0 else nc - 1 - step
            rows = slice(c * C, (c + 1) * C)
            b = b_all[d][rows]
            b_last = b[C - 1:C] if d == 0 else b[0:1]
            qc = q_ref[0, rows, :].astype(F32)
            kc = k_ref[0, rows, :].astype(F32)
            q4 = per_head((qc * jnp.exp(b)).astype(BF16))
            k4 = per_head((kc * jnp.exp(-b)).astype(BF16))
            ks4 = per_head((kc * jnp.exp(b_last - b)).astype(BF16))
            st = state[d]
            stb = st.astype(BF16)
            st_bd = jnp.concatenate([jnp.concatenate([stb[:, 0:LANE], zero_blk], axis=1),
                                     jnp.concatenate([zero_blk, stb[:, LANE:]], axis=1)], axis=0)
            res = _dot_nt(q4, jnp.concatenate([st_bd, k4], axis=0))
            attn = jnp.where(causal[d], res[:, 2 * GLA_DV:], 0.0).astype(BF16)
            v4 = jnp.concatenate([v_ref[0, rows, h * GLA_DV:(h + 1) * GLA_DV] for h in range(GLA_HEADS)], axis=0)
            o4 = _dot(attn, v4)
            for h in range(GLA_HEADS):
                hr = slice(h * C, (h + 1) * C)
                inter = res[hr, (h // 2) * GLA_DV:(h // 2 + 1) * GLA_DV]
                o_ref[0, rows, h * GLA_DV:(h + 1) * GLA_DV] = (o4[hr] + inter).astype(o_ref.dtype)
            state[d] = st * jnp.exp(b_last) + _dot_tn(v4, ks4)
    for d in range(2):
        for pair in range(2):
            s_sc[d, pair] = state[d][:, pair * LANE:(pair + 1) * LANE]

    @pl.when(j == nblk - 1)
    def _():
        sfin_ref[0] = s_sc[...]


def gla_scan(p, wa_p, ba, s0):
    B, T, _ = p.shape
    tb = min(ROW_TILE, T)
    nblk = T // tb
    fwd = lambda b, j: (b, j)
    bwd = lambda b, j: (b, nblk - 1 - j)

    def specs(im):
        return [pl.BlockSpec((1, tb, W_GK), lambda b, j: im(b, j) + (C_QG // W_GK,)),
                pl.BlockSpec((1, tb, W_GK), lambda b, j: im(b, j) + (C_KG // W_GK,)),
                pl.BlockSpec((1, tb, W_GV), lambda b, j: im(b, j) + (C_VG // W_GV,)),
                pl.BlockSpec((1, tb, 2 * LANE), lambda b, j: im(b, j) + (C_GG // (2 * LANE),))]

    return pl.pallas_call(
        _gla_kernel,
        out_shape=(jax.ShapeDtypeStruct((B, T, W_GV), BF16), jax.ShapeDtypeStruct((B, T, W_GV), BF16),
                   jax.ShapeDtypeStruct(s0.shape, F32)),
        grid=(B, nblk),
        in_specs=specs(fwd) + specs(bwd) + [
            pl.BlockSpec((2, LANE, W_GK), lambda b, j: (0, 0, 0)),
            pl.BlockSpec((2, 1, W_GK), lambda b, j: (0, 0, 0)),
            pl.BlockSpec((1, 2, 2, GLA_DV, LANE), lambda b, j: (b, 0, 0, 0, 0))],
        out_specs=(pl.BlockSpec((1, tb, W_GV), lambda b, j: (b, j, 0)),
                   pl.BlockSpec((1, tb, W_GV), lambda b, j: (b, nblk - 1 - j, 0)),
                   pl.BlockSpec((1, 2, 2, GLA_DV, LANE), lambda b, j: (b, 0, 0, 0, 0))),
        scratch_shapes=[pltpu.VMEM((2, 2, GLA_DV, LANE), F32)],
        compiler_params=_cparams(("parallel", "arbitrary")),
        name="gla_scan",
    )(p, p, p, p, p, p, p, p, wa_p, ba, s0)


def _deepnorm(h, gate, y, w, b):
    r = h + (gate * (1.0 / ALPHA)) * y
    mu = jnp.mean(r, axis=-1, keepdims=True)
    xc = r - mu
    var = jnp.mean(xc * xc, axis=-1, keepdims=True)
    return xc * lax.rsqrt(var + LN_EPS / (ALPHA * ALPHA)) * w + b


def _pack_rounded_pairs(xr):
    half = xr.shape[1] // 2
    lo = lax.bitcast_convert_type(xr[:, :half], jnp.uint32)
    hi = lax.bitcast_convert_type(xr[:, half:], jnp.uint32)
    return jnp.bitwise_or(hi, lax.shift_right_logical(lo, jnp.uint32(16)))


def _pack_bf16_pairs(x):
    return _pack_rounded_pairs(x.astype(BF16).astype(F32))


def _unpack_bf16_pairs(p):
    lo = lax.bitcast_convert_type(lax.shift_left(p, jnp.uint32(16)), F32)
    hi = lax.bitcast_convert_type(jnp.bitwise_and(p, jnp.uint32(0xFFFF0000)), F32)
    return lo, hi


def _post_norm_and_route(h, y, mod_ref, lnw_ref, lnb_ref, rwh_ref, rwl_ref, rb_ref, h1_ref, tok_ref, lg_ref):
    h1 = _deepnorm(h, mod_ref[0, 2:3, :], y, lnw_ref[...], lnb_ref[...])
    h1_ref[0] = h1
    tok = h1 * (1.0 + mod_ref[0, 4:5, :]) + mod_ref[0, 3:4, :]
    hi = tok.astype(BF16)
    hi_f = hi.astype(F32)
    tok_ref[0] = _pack_rounded_pairs(hi_f)
    lo = (tok - hi_f).astype(BF16)
    lg = _dot(hi, rwh_ref[...]) + _dot(lo, rwh_ref[...]) + _dot(hi, rwl_ref[...]) + rb_ref[...]
    lg_ref[...] = lg.T[0:ROUTE_ROWS, :]


def _epilogue_specs(tm, mb, b0):
    mod_map = (lambda b, j: (b + b0, 0, 0)) if mb > 1 else (lambda b, j: (0, 0, 0))
    const2 = lambda b, j: (0, 0)
    return [pl.BlockSpec((1, tm, D), lambda b, j: (b + b0, j, 0)),
            pl.BlockSpec((1, 6, D), mod_map),
            pl.BlockSpec((1, D), const2), pl.BlockSpec((1, D), const2),
            pl.BlockSpec((D, ROUTE_W), const2), pl.BlockSpec((D, ROUTE_W), const2),
            pl.BlockSpec((1, ROUTE_W), const2)]


def _epilogue_outs(B, T, tm):
    nj = T // tm
    shapes = (jax.ShapeDtypeStruct((B, T, D), F32), jax.ShapeDtypeStruct((B, T, D // 2), jnp.uint32),
              jax.ShapeDtypeStruct((ROUTE_ROWS, B * T), F32))
    specs = (pl.BlockSpec((1, tm, D), lambda b, j: (b, j, 0)), pl.BlockSpec((1, tm, D // 2), lambda b, j: (b, j, 0)),
             pl.BlockSpec((ROUTE_ROWS, tm), lambda b, j: (0, b * nj + j)))
    return shapes, specs


def _outproj_even_kernel(a_ref, of_ref, ob_ref, rg_ref, nw_ref, wo_ref,
                         h_ref, mod_ref, lnw_ref, lnb_ref, rwh_ref, rwl_ref, rb_ref,
                         h1_ref, tok_ref, lg_ref):
    parts = [a_ref[0]]
    for hd in range(GLA_HEADS):
        cols = slice(hd * GLA_DV, (hd + 1) * GLA_DV)
        o = of_ref[0, :, cols].astype(F32) + ob_ref[0, :, cols].astype(F32)
        o = o * lax.rsqrt(jnp.mean(o * o, axis=-1, keepdims=True) + RMS_EPS)
        parts.append((o * nw_ref[:, cols] * _silu(rg_ref[0, :, cols].astype(F32))).astype(BF16))
    y = _dot(jnp.concatenate(parts, axis=1), wo_ref[...])
    _post_norm_and_route(h_ref[0], y, mod_ref, lnw_ref, lnb_ref, rwh_ref, rwl_ref, rb_ref, h1_ref, tok_ref, lg_ref)


def outproj_even(a, o_f, o_b, p, norm_w, w_out, h, mod, lnw, lnb, rwh, rwl, rb, b0, nb):
    T = h.shape[1]
    tm = min(ROW_TILE, T)
    tile = lambda b, j: (b + b0, j, 0)
    shapes, ospecs = _epilogue_outs(nb, T, tm)
    return pl.pallas_call(
        _outproj_even_kernel,
        out_shape=shapes,
        grid=(nb, T // tm),
        in_specs=[pl.BlockSpec((1, tm, W_AQ), tile), pl.BlockSpec((1, tm, W_GV), tile), pl.BlockSpec((1, tm, W_GV), tile),
                  pl.BlockSpec((1, tm, W_GV), lambda b, j: (b + b0, j, C_RG // W_GV)),
                  pl.BlockSpec((1, W_GV), lambda b, j: (0, 0)),
                  pl.BlockSpec((D, D), lambda b, j: (0, 0))] + _epilogue_specs(tm, mod.shape[0], b0),
        out_specs=ospecs,
        compiler_params=_cparams(("parallel", "arbitrary")),
        name="outproj_even",
    )(a, o_f, o_b, p, norm_w, w_out, h, mod, lnw, lnb, rwh, rwl, rb)


def _inproj_odd_kernel(h_ref, mod_ref, w_ref, o_ref):
    u = (h_ref[0] * (1.0 + mod_ref[0, 1:2, :]) + mod_ref[0, 0:1, :]).astype(BF16)
    o_ref[0, :, 0:D] = _dot(u, w_ref[:, 0:D]).astype(BF16)
    o_ref[0, :, D:2 * D] = (_dot(u, w_ref[:, D:2 * D]) * _dot(u, w_ref[:, 2 * D:3 * D])).astype(BF16)


def inproj_odd(h, mod, w):
    B, T, _ = h.shape
    tm = min(IN_ROW_TILE, T)
    mb = mod.shape[0]
    return pl.pallas_call(
        _inproj_odd_kernel,
        out_shape=jax.ShapeDtypeStruct((B, T, 2 * D), BF16),
        grid=(B, T // tm),
        in_specs=[pl.BlockSpec((1, tm, D), lambda b, j: (b, j, 0)),
                  pl.BlockSpec((1, 6, D), (lambda b, j: (b, 0, 0)) if mb > 1 else (lambda b, j: (0, 0, 0))),
                  pl.BlockSpec((D, 3 * D), lambda b, j: (0, 0))],
        out_specs=pl.BlockSpec((1, tm, 2 * D), lambda b, j: (b, j, 0)),
        compiler_params=_cparams(("parallel", "arbitrary")),
        name="inproj_odd",
    )(h, mod, w)


HALO = 16


def _outproj_odd_kernel(gb_ref, z_ref, zp_ref, zn_ref, cw_ref, cb_ref, wo_ref,
                        h_ref, mod_ref, lnw_ref, lnb_ref, rwh_ref, rwl_ref, rb_ref,
                        h1_ref, tok_ref, lg_ref):
    j = pl.program_id(1)
    tm = z_ref.shape[1]
    z = z_ref[0].astype(F32)
    prev_row = jnp.where(j > 0, zp_ref[0, HALO - 1:HALO, :].astype(F32), 0.0)
    next_row = jnp.where(j < pl.num_programs(1) - 1, zn_ref[0, 0:1, :].astype(F32), 0.0)
    row = lax.broadcasted_iota(jnp.int32, (tm, D), 0)
    z_prev = jnp.where(row == 0, prev_row, pltpu.roll(z, 1, 0))
    z_next = jnp.where(row == tm - 1, next_row, pltpu.roll(z, tm - 1, 0))
    conv = z_prev * cw_ref[0:1, :] + z * cw_ref[1:2, :] + z_next * cw_ref[2:3, :] + cb_ref[...]
    y = _dot((gb_ref[0].astype(F32) * conv).astype(BF16), wo_ref[...])
    _post_norm_and_route(h_ref[0], y, mod_ref, lnw_ref, lnb_ref, rwh_ref, rwl_ref, rb_ref, h1_ref, tok_ref, lg_ref)


def outproj_odd(gz, conv_w, conv_b, w_out, h, mod, lnw, lnb, rwh, rwl, rb, b0, nb):
    T = h.shape[1]
    tm = min(ROW_TILE, T)
    r = tm // HALO
    nh = T // HALO
    shapes, ospecs = _epilogue_outs(nb, T, tm)
    return pl.pallas_call(
        _outproj_odd_kernel,
        out_shape=shapes,
        grid=(nb, T // tm),
        in_specs=[pl.BlockSpec((1, tm, D), lambda b, j: (b + b0, j, 0)),
                  pl.BlockSpec((1, tm, D), lambda b, j: (b + b0, j, 1)),
                  pl.BlockSpec((1, HALO, D), lambda b, j: (b + b0, jnp.maximum(j * r - 1, 0), 1)),
                  pl.BlockSpec((1, HALO, D), lambda b, j: (b + b0, jnp.minimum((j + 1) * r, nh - 1), 1)),
                  pl.BlockSpec((3, D), lambda b, j: (0, 0)),
                  pl.BlockSpec((1, D), lambda b, j: (0, 0)),
                  pl.BlockSpec((D, D), lambda b, j: (0, 0))] + _epilogue_specs(tm, mod.shape[0], b0),
        out_specs=ospecs,
        compiler_params=_cparams(("parallel", "arbitrary")),
        name="outproj_odd",
    )(gz, gz, gz, gz, conv_w, conv_b, w_out, h, mod, lnw, lnb, rwh, rwl, rb)


def _sc_mesh():
    return plsc.VectorSubcoreMesh(core_axis_name="c", subcore_axis_name="s")


def sc_gather_rows(table, idx):
    n = idx.shape[0]
    width = table.shape[1]
    per_w = n // SC_WORKERS
    n_chunks = per_w // SC_CHUNK
    assert n_chunks % 2 == 0

    @functools.partial(
        pl.kernel, mesh=_sc_mesh(),
        out_type=jax.ShapeDtypeStruct((n, width), table.dtype),
        scratch_types=[pltpu.VMEM((n_chunks, SC_CHUNK), jnp.int32),
                       pltpu.VMEM((SC_CHUNK, width), table.dtype), pltpu.VMEM((SC_CHUNK, width), table.dtype),
                       pltpu.SemaphoreType.DMA, pltpu.SemaphoreType.DMA],
    )
    def gather_kernel(table_hbm, idx_hbm, out_hbm, idx_v, buf0, buf1, sem0, sem1):
        wid = lax.axis_index("s") * SC_CORES + lax.axis_index("c")
        pltpu.sync_copy(idx_hbm.at[wid], idx_v)

        def fetch(j, buf, sem):
            return pltpu.make_async_copy(table_hbm.at[idx_v.at[j]], buf, sem)

        def flush(j, buf):
            pltpu.sync_copy(buf, out_hbm.at[pl.ds(wid * per_w + j * SC_CHUNK, SC_CHUNK)])

        fetch(0, buf0, sem0).start()

        @pl.loop(0, n_chunks, step=2)
        def _(j):
            fetch(j + 1, buf1, sem1).start()
            fetch(j, buf0, sem0).wait()
            flush(j, buf0)

            @pl.when(j + 2 < n_chunks)
            def _():
                fetch(j + 2, buf0, sem0).start()

            fetch(j + 1, buf1, sem1).wait()
            flush(j + 1, buf1)

    return gather_kernel(table, idx.reshape(SC_WORKERS, n_chunks, SC_CHUNK))


def sc_scatter_rows(srcs, idxs, n_rows):
    width, dt = srcs[0][0].shape[1], srcs[0][0].dtype
    plans, args = [], []
    for (src, row0, n_src), idx in zip(srcs, idxs):
        per_w = n_src // SC_WORKERS
        chunk = min(SC_CHUNK, per_w // 2)
        assert (per_w // chunk) % 2 == 0
        plans.append((per_w, chunk, per_w // chunk, idx.shape[0], row0))
        args += [src, idx.reshape(idx.shape[0], SC_WORKERS, per_w // chunk, chunk)]
    max_chunk = max(p[1] for p in plans)
    scratch = [pltpu.VMEM((max_chunk, width), dt), pltpu.VMEM((max_chunk, width), dt),
               pltpu.SemaphoreType.DMA, pltpu.SemaphoreType.DMA]
    scratch += [pltpu.VMEM((lists, n_chunks, chunk), jnp.int32) for _, chunk, n_chunks, lists, _ in plans]

    @functools.partial(pl.kernel, mesh=_sc_mesh(), out_type=jax.ShapeDtypeStruct((n_rows, width), dt),
                       scratch_types=scratch)
    def scatter_kernel(*refs):
        ins, out_hbm = refs[:2 * len(plans)], refs[2 * len(plans)]
        rows0, rows1, sem0, sem1 = refs[2 * len(plans) + 1:2 * len(plans) + 5]
        idx_vs = refs[2 * len(plans) + 5:]
        wid = lax.axis_index("s") * SC_CORES + lax.axis_index("c")
        for s, (per_w, chunk, n_chunks, lists, row0) in enumerate(plans):
            src_hbm, idx_hbm, idx_v = ins[2 * s], ins[2 * s + 1], idx_vs[s]
            for k in range(lists):
                pltpu.sync_copy(idx_hbm.at[k, wid], idx_v.at[k])
            buf0 = rows0 if chunk == max_chunk else rows0.at[pl.ds(0, chunk)]
            buf1 = rows1 if chunk == max_chunk else rows1.at[pl.ds(0, chunk)]

            def load(j, buf, sem, src_hbm=src_hbm, per_w=per_w, chunk=chunk, row0=row0):
                return pltpu.make_async_copy(src_hbm.at[pl.ds(row0 + wid * per_w + j * chunk, chunk)], buf, sem)

            def spread(j, buf, idx_v=idx_v, lists=lists):
                for k in range(lists):
                    pltpu.sync_copy(buf, out_hbm.at[idx_v.at[k, j]])

            load(0, buf0, sem0).start()

            @pl.loop(0, n_chunks, step=2)
            def _(j, load=load, spread=spread, buf0=buf0, buf1=buf1, n_chunks=n_chunks):
                load(j + 1, buf1, sem1).start()
                load(j, buf0, sem0).wait()
                spread(j, buf0)

                @pl.when(j + 2 < n_chunks)
                def _():
                    load(j + 2, buf0, sem0).start()

                load(j + 1, buf1, sem1).wait()
                spread(j + 1, buf1)

    return scatter_kernel(*args)


def _ffn_kernel(be_ref, nv_ref, x_ref, w1_ref, w3_ref, w2_ref, y_ref, w1b, w3b, w2b):
    i = pl.program_id(0)
    changed = jnp.logical_or(i == 0, be_ref[i] != be_ref[jnp.maximum(i - 1, 0)])

    @pl.when(changed)
    def _():
        w1b[...] = w1_ref[0, 0].astype(BF16)
        w3b[...] = w3_ref[0, 0].astype(BF16)
        w2b[...] = w2_ref[0, 0].astype(BF16)

    @pl.when(i < nv_ref[0])
    def _():
        x = jnp.concatenate([v.astype(BF16) for v in _unpack_bf16_pairs(x_ref[...])], axis=1)
        y_ref[...] = _pack_bf16_pairs(_dot((_silu(_dot(x, w1b[...])) * _dot(x, w3b[...])).astype(BF16), w2b[...]))

    @pl.when(i >= nv_ref[0])
    def _():
        y_ref[...] = jnp.zeros_like(y_ref)


def moe_ffn(xs, blk_expert, n_valid, w1, w3, w2, layer):
    n_rows = xs.shape[0]
    nb = n_rows // MOE_TM
    return pl.pallas_call(
        _ffn_kernel,
        out_shape=jax.ShapeDtypeStruct((n_rows, D // 2), jnp.uint32),
        grid_spec=pltpu.PrefetchScalarGridSpec(
            num_scalar_prefetch=2,
            grid=(nb,),
            in_specs=[pl.BlockSpec((MOE_TM, D // 2), lambda i, be, nv: (jnp.minimum(i, nv[0] - 1), 0)),
                      pl.BlockSpec((1, 1, D, D_EXPERT), lambda i, be, nv: (layer, be[i], 0, 0)),
                      pl.BlockSpec((1, 1, D, D_EXPERT), lambda i, be, nv: (layer, be[i], 0, 0)),
                      pl.BlockSpec((1, 1, D_EXPERT, D), lambda i, be, nv: (layer, be[i], 0, 0))],
            out_specs=pl.BlockSpec((MOE_TM, D // 2), lambda i, be, nv: (i, 0)),
            scratch_shapes=[pltpu.VMEM((D, D_EXPERT), BF16), pltpu.VMEM((D, D_EXPERT), BF16),
                            pltpu.VMEM((D_EXPERT, D), BF16)]),
        compiler_params=_cparams(("arbitrary",)),
        name="moe_ffn",
    )(blk_expert, n_valid, xs, w1, w3, w2)


def _combine_kernel(h_ref, y0_ref, y1_ref, wt_ref, mod_ref, lnw_ref, lnb_ref, o_ref):
    half = D // 2
    lo0, hi0 = _unpack_bf16_pairs(y0_ref[0])
    lo1, hi1 = _unpack_bf16_pairs(y1_ref[0])
    pick = (lax.broadcasted_iota(jnp.int32, (WT_ROWS, LANE), 0) == lax.broadcasted_iota(jnp.int32, (WT_ROWS, LANE), 1))
    wcols = lax.dot_general(wt_ref[...], pick.astype(F32), (((0,), (0,)), ((), ())), precision=HIGHEST,
                            preferred_element_type=F32)
    w0, w1 = wcols[:, 0:1], wcols[:, 1:2]
    gate = mod_ref[0, 5:6, :] * (1.0 / ALPHA)
    r_lo = h_ref[:, 0:half] + gate[:, 0:half] * (w0 * lo0 + w1 * lo1)
    r_hi = h_ref[:, half:D] + gate[:, half:D] * (w0 * hi0 + w1 * hi1)
    mu = (jnp.sum(r_lo, axis=-1, keepdims=True) + jnp.sum(r_hi, axis=-1, keepdims=True)) * (1.0 / D)
    c_lo, c_hi = r_lo - mu, r_hi - mu
    var = (jnp.sum(c_lo * c_lo, axis=-1, keepdims=True) + jnp.sum(c_hi * c_hi, axis=-1, keepdims=True)) * (1.0 / D)
    inv = lax.rsqrt(var + LN_EPS / (ALPHA * ALPHA))
    o_ref[:, 0:half] = c_lo * inv * lnw_ref[:, 0:half] + lnb_ref[:, 0:half]
    o_ref[:, half:D] = c_hi * inv * lnw_ref[:, half:D] + lnb_ref[:, half:D]


def _combine_into_kernel(*refs):
    _combine_kernel(*refs[:7], refs[8])


def moe_combine(h1, y_rows, wts, mod, lnw, lnb, y_tok0, out_tok0, n_out, prev=None):
    nb, T, _ = h1.shape
    ntok, N = nb * T, n_out
    tm = min(IN_ROW_TILE, ntok)
    per_b = T // tm
    h_off, y_off = out_tok0 // tm, y_tok0 // tm
    mod_map = (lambda i: ((i + h_off) // per_b, 0, 0)) if mod.shape[0] > 1 else (lambda i: (0, 0, 0))
    in_specs = [pl.BlockSpec((tm, D), lambda i: (i, 0)),
                pl.BlockSpec((1, tm, D // 2), lambda i: (0, i + y_off, 0)),
                pl.BlockSpec((1, tm, D // 2), lambda i: (1, i + y_off, 0)),
                pl.BlockSpec((WT_ROWS, tm), lambda i: (0, i + y_off)),
                pl.BlockSpec((1, 6, D), mod_map),
                pl.BlockSpec((1, D), lambda i: (0, 0)), pl.BlockSpec((1, D), lambda i: (0, 0))]
    args = [h1.reshape(ntok, D), y_rows, y_rows, wts, mod, lnw, lnb]
    if prev is not None:
        in_specs.append(pl.BlockSpec(memory_space=pl.ANY))
        args.append(prev)
    return pl.pallas_call(
        _combine_kernel if prev is None else _combine_into_kernel,
        out_shape=jax.ShapeDtypeStruct((N, D), F32),
        grid=(ntok // tm,),
        in_specs=in_specs,
        out_specs=pl.BlockSpec((tm, D), lambda i: (i + h_off, 0)),
        input_output_aliases={} if prev is None else {7: 0},
        compiler_params=_cparams(("parallel",)),
        name="moe_combine",
    )(*args)


def _route_kernel(lg_ref, dest_ref, wt_ref, cnt_ref, tri_sc, start_sc, run_sc):
    ph, i = pl.program_id(0), pl.program_id(1)
    tr = lg_ref.shape[1]

    @pl.when(jnp.logical_and(ph == 0, i == 0))
    def _():
        r = lax.broadcasted_iota(jnp.int32, (LANE, LANE), 0)
        c = lax.broadcasted_iota(jnp.int32, (LANE, LANE), 1)
        tri_sc[...] = (r < c).astype(BF16)
        start_sc[...] = jnp.zeros_like(start_sc)
        run_sc[...] = jnp.zeros_like(run_sc)

    @pl.when(jnp.logical_and(ph == 1, i == 0))
    def _():
        cnt = run_sc[...].astype(jnp.int32)
        cnt_ref[...] = cnt
        padded = jnp.bitwise_and(cnt + (MOE_TM - 1), -MOE_TM)
        row = lax.broadcasted_iota(jnp.int32, padded.shape, 0)
        acc = padded
        for s in (1, 2, 4, 8, 16):
            acc = acc + jnp.where(row >= s, pltpu.roll(acc, s, 0), 0)
        start_sc[...] = (acc - padded).astype(F32)
        run_sc[...] = jnp.zeros_like(run_sc)

    lg = lg_ref[...]
    gl = lg[N_EXPERTS:N_EXPERTS + N_GROUPS]
    gmax = jnp.max(gl, axis=0, keepdims=True)
    sub4 = lax.broadcasted_iota(jnp.int32, gl.shape, 0)
    g_sel = jnp.min(jnp.where(gl == gmax, sub4, N_GROUPS), axis=0, keepdims=True)
    p_group = 1.0 / jnp.sum(jnp.exp(gl - gmax), axis=0, keepdims=True)
    el = lg[0:EXPERTS_PER_GROUP]
    for g in range(1, N_GROUPS):
        el = jnp.where(g_sel == g, lg[g * EXPERTS_PER_GROUP:(g + 1) * EXPERTS_PER_GROUP], el)
    sub8 = lax.broadcasted_iota(jnp.int32, el.shape, 0)
    e1 = jnp.max(el, axis=0, keepdims=True)
    i1 = jnp.min(jnp.where(el == e1, sub8, EXPERTS_PER_GROUP), axis=0, keepdims=True)
    rest = jnp.where(sub8 == i1, -jnp.inf, el)
    e2 = jnp.max(rest, axis=0, keepdims=True)
    i2 = jnp.min(jnp.where(rest == e2, sub8, EXPERTS_PER_GROUP), axis=0, keepdims=True)
    den = jnp.sum(jnp.exp(el - e1), axis=0, keepdims=True)
    p1 = 1.0 / den
    p2 = jnp.exp(e2 - e1) / den
    wt_ref[...] = jnp.zeros_like(wt_ref)
    wt_ref[0:1, :] = p_group * p1 / (p1 + p2)
    wt_ref[1:2, :] = p_group * p2 / (p1 + p2)

    sub32 = lax.broadcasted_iota(jnp.int32, (N_EXPERTS, tr), 0)
    oh = [(sub32 == g_sel * EXPERTS_PER_GROUP + ix).astype(F32) for ix in (i1, i2)]
    cnt = [jnp.sum(o, axis=1, keepdims=True) for o in oh]

    @pl.when(ph == 0)
    def _():
        dest_ref[...] = jnp.zeros_like(dest_ref)

    @pl.when(ph == 1)
    def _():
        before = start_sc[:, 0:1] + run_sc[:, 0:1]
        for k in range(TOP_K):
            subs = [oh[k][:, s * LANE:(s + 1) * LANE] for s in range(tr // LANE)]
            local = _dot(jnp.concatenate(subs, axis=0).astype(BF16), tri_sc[...])
            seen = before + (cnt[0] if k == 1 else 0.0)
            for s, sub in enumerate(subs):
                prior = local[s * N_EXPERTS:(s + 1) * N_EXPERTS] + seen
                dest_ref[k:k + 1, s * LANE:(s + 1) * LANE] = (
                    jnp.sum(sub * prior, axis=0, keepdims=True).astype(jnp.int32))
                seen = seen + jnp.sum(sub, axis=1, keepdims=True)

    run_sc[...] = run_sc[...] + (cnt[0] + cnt[1])


def moe_route(logits_t, col0, N):
    tr = next(t for t in (4096, 2048, 1024, 512, 256) if N % t == 0 and col0 % t == 0)
    t0 = col0 // tr
    return pl.pallas_call(
        _route_kernel,
        out_shape=(jax.ShapeDtypeStruct((TOP_K, N), jnp.int32), jax.ShapeDtypeStruct((WT_ROWS, N), F32),
                   jax.ShapeDtypeStruct((N_EXPERTS, LANE), jnp.int32)),
        grid=(2, N // tr),
        in_specs=[pl.BlockSpec((ROUTE_ROWS, tr), lambda p, i: (0, i + t0))],
        out_specs=(pl.BlockSpec((TOP_K, tr), lambda p, i: (0, i * p)), pl.BlockSpec((WT_ROWS, tr), lambda p, i: (0, i * p)),
                   pl.BlockSpec((N_EXPERTS, LANE), lambda p, i: (0, 0))),
        scratch_shapes=[pltpu.VMEM((LANE, LANE), BF16), pltpu.VMEM((N_EXPERTS, LANE), F32),
                        pltpu.VMEM((N_EXPERTS, LANE), F32)],
        compiler_params=_cparams(("arbitrary", "arbitrary")),
        name="moe_route",
    )(logits_t)


def _block_tables(counts, n_assign):
    padded = (counts + MOE_TM - 1) // MOE_TM * MOE_TM
    pad_end = jnp.cumsum(padded)
    pad_start = pad_end - padded
    nb = -(-n_assign // MOE_TM) + N_EXPERTS
    blk_start = jnp.arange(nb, dtype=jnp.int32) * MOE_TM
    blk_expert = jnp.minimum(jnp.sum((pad_end[None, :] <= blk_start[:, None]).astype(jnp.int32), axis=1), N_EXPERTS - 1)
    n_valid = (pad_end[-1] // MOE_TM).astype(jnp.int32).reshape(1)
    n_fill = nb * MOE_TM - n_assign
    gap = padded - counts
    gap_end = jnp.cumsum(gap)
    k = jnp.arange(n_fill, dtype=jnp.int32)[:, None]
    sel = jnp.logical_and(k >= (gap_end - gap)[None, :], k < gap_end[None, :])
    in_gap = jnp.sum(jnp.where(sel, (pad_start + counts - (gap_end - gap))[None, :] + k, 0), axis=1)
    fill = jnp.where(k[:, 0] < gap_end[-1], in_gap, pad_end[-1] + k[:, 0] - gap_end[-1])
    return blk_expert.astype(jnp.int32), n_valid, fill.astype(jnp.int32), nb * MOE_TM


def hier_moe_and_norm(streams, w1, w3, w2, layer, lnw, lnb):
    lat_parts, ctx = [s for s in streams if s[4]], next((s for s in streams if not s[4]), None)
    n_ctx = ctx[0].shape[0] * ctx[0].shape[1] if ctx else 0
    n_lat_all = sum(s[0].shape[0] * s[0].shape[1] for s in lat_parts)
    routed, sorted_in, tables = [], [], []
    for r, part in enumerate(lat_parts):
        n_lat = part[0].shape[0] * part[0].shape[1]
        with_ctx = ctx is not None and r == len(lat_parts) - 1
        n = n_lat + (n_ctx if with_ctx else 0)
        logits_t = jnp.concatenate([part[2], ctx[2]], 1) if with_ctx else part[2]
        dest, wts, counts = moe_route(logits_t, 0, n)
        blk_expert, n_valid, fill_rows, n_rows = _block_tables(counts[:, 0], TOP_K * n)
        srcs, idxs = [(part[1].reshape(n_lat, D // 2), 0, n_lat)], [dest[:, :n_lat]]
        if with_ctx:
            srcs.append((ctx[1].reshape(n_ctx, D // 2), 0, n_ctx))
            idxs.append(dest[:, n_lat:])
        srcs.append((jnp.zeros((fill_rows.shape[0], D // 2), jnp.uint32), 0, fill_rows.shape[0]))
        idxs.append(fill_rows.reshape(1, -1))
        routed.append((dest, wts))
        sorted_in.append(sc_scatter_rows(srcs, idxs, n_rows))
        tables.append((blk_expert, n_valid, n_lat, n, with_ctx))
    out_lat, out_ctx, tok0 = None, None, 0
    for part, (dest, wts), xs, (blk_expert, n_valid, n_lat, n, with_ctx) in zip(lat_parts, routed, sorted_in, tables):
        y = moe_ffn(xs, blk_expert, n_valid, w1, w3, w2, layer)
        y_rows = sc_gather_rows(y, dest.reshape(-1)).reshape(TOP_K, n, D // 2)
        out_lat = moe_combine(part[0], y_rows, wts, part[3], lnw, lnb, 0, tok0, n_lat_all, out_lat)
        if with_ctx:
            out_ctx = moe_combine(ctx[0], y_rows, wts, ctx[3], lnw, lnb, n_lat, 0, n_ctx)
        tok0 += n_lat
    return out_lat, out_ctx


def _even_projection_weights(w):
    def rope_layout(x, heads):
        return x.reshape(D, heads, 2, 2, HEAD_DIM // 4).transpose(0, 1, 3, 2, 4).reshape(D, heads * HEAD_DIM)

    def twice(x):
        return jnp.concatenate([x[:, :HEAD_DIM], x[:, :HEAD_DIM], x[:, HEAD_DIM:], x[:, HEAD_DIM:]], axis=1)

    def lane_pad(x):
        return jnp.pad(x, ((0, 0), (0, LANE - x.shape[1])))

    qa, ka, va, qg, kg, vg, rg, gg = jnp.split(w, [512, 640, 768, 1024, 1280, 1792, 2304], axis=1)
    cols = [rope_layout(qa, A_Q_HEADS) * (HEAD_DIM ** -0.5 * LOG2_E),
            twice(rope_layout(ka, A_KV_HEADS)), twice(va), vg, rg, qg * GLA_DK ** -0.5, kg,
            lane_pad(gg[:, :GLA_RANK]), lane_pad(gg[:, GLA_RANK:])]
    return jnp.concatenate(cols, axis=1).astype(BF16)


def _rope_tables(S):
    row = jnp.repeat(jnp.arange(S // GRID_W), GRID_W).astype(F32)
    col = jnp.tile(jnp.arange(GRID_W), S // GRID_W).astype(F32)
    axis_dim = HEAD_DIM // 2
    inv_freq = ROPE_BASE ** (-jnp.arange(0, axis_dim, 2, dtype=F32) / axis_dim)
    ang = jnp.concatenate([row[:, None] * inv_freq, col[:, None] * inv_freq], -1)
    cos, sin = jnp.cos(ang), jnp.sin(ang)
    cos_t = jnp.tile(cos, (1, 4))
    sin_t = jnp.tile(jnp.concatenate([-sin, sin], -1), (1, 2))
    return cos_t, sin_t


def _router_weights(wg, bg, we, be):
    w = jnp.zeros((D, ROUTE_W), F32).at[:, :N_EXPERTS].set(we).at[:, N_EXPERTS:N_EXPERTS + N_GROUPS].set(wg)
    b = jnp.zeros((1, ROUTE_W), F32).at[0, :N_EXPERTS].set(be).at[0, N_EXPERTS:N_EXPERTS + N_GROUPS].set(bg)
    hi = w.astype(BF16)
    return hi, (w - hi.astype(F32)).astype(BF16), b


def kernel(x, c, ctx, c_ctx, w_in_even, w_out_even, attn_sink, gla_wa2, gla_ba, gla_norm_w, w_in_odd, conv_w, conv_b, w_out_odd, ada_w, ada_b, ln_w, ln_b, router_wg, router_bg, router_we, router_be, moe_w1, moe_w3, moe_w2):
    B, S, _ = x.shape
    L = ctx.shape[1]
    cos_t, sin_t = _rope_tables(S)
    cos_c, sin_c = jnp.ones((L, LANE), F32), jnp.zeros((L, LANE), F32)

    n_cond = -(-(B + 1) // 8) * 8
    cc = jnp.zeros((n_cond, D), F32).at[:B].set(c).at[B].set(c_ctx)
    mods = ada_modulation_all(cc, ada_w, ada_b).reshape(DEPTH, n_cond, 6, D)

    h_lat, h_ctx = x, ctx
    for l in range(DEPTH):
        i = l // 2
        need_ctx = any(j % 2 == 0 for j in range(l + 1, DEPTH))
        m_lat = mods[l, :B]
        m_ctx = mods[l, B:B + 1]
        lnw0, lnb0 = ln_w[l, 0:1], ln_b[l, 0:1]
        lnw1, lnb1 = ln_w[l, 1:2], ln_b[l, 1:2]
        rwh, rwl, rb = _router_weights(router_wg[l], router_bg[l], router_we[l], router_be[l])
        streams = []
        if l % 2 == 0:
            w_in = _even_projection_weights(w_in_even[i])
            w_out = w_out_even[i].astype(BF16)
            wa_p = jnp.zeros((2, LANE, GLA_HEADS * GLA_DK), F32).at[:, :GLA_RANK].set(gla_wa2[i]).astype(BF16)
            ba = gla_ba[i].reshape(2, 1, -1)
            nw = gla_norm_w[i].reshape(1, -1)
            p_ctx = inproj_even(h_ctx, m_ctx, w_in, cos_c, sin_c)
            p_lat = inproj_even(h_lat, m_lat, w_in, cos_t, sin_t)
            a_lat = attention(p_lat, p_ctx, attn_sink[i], True)
            s0 = jnp.zeros((B, 2, 2, GLA_DV, LANE), F32)
            oc_f, oc_b, s_ctx = gla_scan(p_ctx, wa_p, ba, s0)
            ol_f, ol_b, _ = gla_scan(p_lat, wa_p, ba, s_ctx)
            for b0 in range(0, B, B // 2):
                streams.append(outproj_even(a_lat, ol_f, ol_b, p_lat, nw, w_out, h_lat, m_lat, lnw0, lnb0,
                                            rwh, rwl, rb, b0, B // 2) + (m_lat, True))
            if need_ctx:
                a_ctx = attention(p_ctx, p_ctx, attn_sink[i], False)
                streams.append(outproj_even(a_ctx, oc_f, oc_b, p_ctx, nw, w_out, h_ctx, m_ctx, lnw0, lnb0,
                                            rwh, rwl, rb, 0, B) + (m_ctx, False))
        else:
            w_in = w_in_odd[i].astype(BF16)
            w_out = w_out_odd[i].astype(BF16)
            cb = conv_b[i].reshape(1, D)
            gz = inproj_odd(h_lat, m_lat, w_in)
            for b0 in range(0, B, B // 2):
                streams.append(outproj_odd(gz, conv_w[i], cb, w_out, h_lat, m_lat, lnw0, lnb0, rwh, rwl, rb,
                                           b0, B // 2) + (m_lat, True))
            if need_ctx:
                gz_ctx = inproj_odd(h_ctx, m_ctx, w_in)
                streams.append(outproj_odd(gz_ctx, conv_w[i], cb, w_out, h_ctx, m_ctx, lnw0, lnb0, rwh, rwl, rb,
                                           0, B) + (m_ctx, False))
        out_lat, out_ctx = hier_moe_and_norm(streams, moe_w1, moe_w3, moe_w2, l, lnw1, lnb1)
        h_lat = out_lat.reshape(B, S, D)
        if need_ctx:
            h_ctx = out_ctx.reshape(B, L, D)
    return h_lat
```

```python
import functools

import jax
import jax.numpy as jnp
from jax import lax
from jax.experimental import pallas as pl
from jax.experimental.pallas import tpu as pltpu
from jax.experimental.pallas import tpu_sc as plsc

F32 = jnp.float32
BF16 = jnp.bfloat16
HIGHEST = lax.Precision.HIGHEST

D = 1024
DEPTH = 4
GRID_W = 64
HEAD_DIM = 64
A_Q_HEADS = 8
A_KV_HEADS = 2
WINDOW = 128
ROPE_BASE = 10000.0
GLA_HEADS = 4
GLA_DK = 64
GLA_DV = 128
GLA_RANK = 16
GLA_TAU = 16.0
GLA_CHUNK = 64
N_GROUPS = 4
EXPERTS_PER_GROUP = 8
N_EXPERTS = 32
TOP_K = 2
D_EXPERT = 512
ALPHA = (2.0 * DEPTH) ** 0.25
LN_EPS = 1e-5
RMS_EPS = 1e-6

LANE = 128
VMEM_LIMIT = 48 * 1024 * 1024

ROW_TILE = 512
IN_ROW_TILE = 1024
W_AQ = A_Q_HEADS * HEAD_DIM
W_KV2 = 2 * A_KV_HEADS * HEAD_DIM
W_GK = GLA_HEADS * GLA_DK
W_GV = GLA_HEADS * GLA_DV
C_QA = 0
C_KD = C_QA + W_AQ
C_VD = C_KD + W_KV2
C_VG = C_VD + W_KV2
C_RG = C_VG + W_GV
C_QG = C_RG + W_GV
C_KG = C_QG + W_GK
C_GG = C_KG + W_GK
P_W = C_GG + 2 * LANE
ROUTE_W = 128
WT_ROWS = 8
ROUTE_ROWS = 40
MOE_TM = 512
SC_CORES, SC_SUBCORES = 2, 16
SC_WORKERS = SC_CORES * SC_SUBCORES
SC_CHUNK = 64
NEG = -1e30
LOG2_E = 1.4426950408889634


def _cparams(sem):
    return pltpu.CompilerParams(dimension_semantics=sem, vmem_limit_bytes=VMEM_LIMIT)


def _dot(a, b):
    return jnp.dot(a, b, preferred_element_type=F32)


def _dot_nt(a, b):
    return lax.dot_general(a, b, (((1,), (1,)), ((), ())), preferred_element_type=F32)


def _dot_tn(a, b):
    return lax.dot_general(a, b, (((0,), (0,)), ((), ())), preferred_element_type=F32)


def _silu(x):
    return x * (1.0 / (1.0 + jnp.exp(-x)))


def _ada_kernel(c_ref, w_ref, b_ref, o_ref):
    s = _silu(c_ref[...])
    o_ref[0] = jnp.dot(s, w_ref[0], precision=HIGHEST, preferred_element_type=F32) + b_ref[0]


def ada_modulation_all(cc, ada_w, ada_b):
    R = cc.shape[0]
    tn = 1536
    return pl.pallas_call(
        _ada_kernel,
        out_shape=jax.ShapeDtypeStruct((DEPTH, R, 6 * D), F32),
        grid=(DEPTH, 6 * D // tn),
        in_specs=[pl.BlockSpec((R, D), lambda l, n: (0, 0)),
                  pl.BlockSpec((1, D, tn), lambda l, n: (l, 0, n)),
                  pl.BlockSpec((1, 1, tn), lambda l, n: (l, 0, n))],
        out_specs=pl.BlockSpec((1, R, tn), lambda l, n: (l, 0, n)),
        compiler_params=_cparams(("arbitrary", "arbitrary")),
        name="ada_modulation",
    )(cc, ada_w, ada_b.reshape(DEPTH, 1, 6 * D))


_EVEN_CHUNKS = ((C_QA, C_KD, True), (C_KD, C_VD, True), (C_VD, C_VG, False), (C_VG, C_RG, False),
                (C_RG, C_QG, False), (C_QG, C_GG, False), (C_GG, P_W, False))


def _inproj_even_kernel(h_ref, mod_ref, w_ref, cos_ref, sin_ref, p_ref):
    tm = h_ref.shape[1]
    u = (h_ref[0] * (1.0 + mod_ref[0, 1:2, :]) + mod_ref[0, 0:1, :]).astype(BF16)
    cos = cos_ref[...]
    sin = sin_ref[...]
    lane = lax.broadcasted_iota(jnp.int32, (tm, LANE), 1)
    first_half = (lane % HEAD_DIM) < (HEAD_DIM // 2)
    for c0, c1, rope in _EVEN_CHUNKS:
        acc = _dot(u, w_ref[:, c0:c1])
        if rope:
            for i in range((c1 - c0) // LANE):
                x = acc[:, i * LANE:(i + 1) * LANE]
                partner = jnp.where(first_half, pltpu.roll(x, LANE - 32, 1), pltpu.roll(x, 32, 1))
                p_ref[0, :, c0 + i * LANE:c0 + (i + 1) * LANE] = (x * cos + partner * sin).astype(BF16)
        else:
            p_ref[0, :, c0:c1] = acc.astype(BF16)


def inproj_even(h, mod, w, cos_t, sin_t):
    B, T, _ = h.shape
    tm = min(IN_ROW_TILE, T)
    mb = mod.shape[0]
    return pl.pallas_call(
        _inproj_even_kernel,
        out_shape=jax.ShapeDtypeStruct((B, T, P_W), BF16),
        grid=(B, T // tm),
        in_specs=[pl.BlockSpec((1, tm, D), lambda b, j: (b, j, 0)),
                  pl.BlockSpec((1, 6, D), (lambda b, j: (b, 0, 0)) if mb > 1 else (lambda b, j: (0, 0, 0))),
                  pl.BlockSpec((D, P_W), lambda b, j: (0, 0)),
                  pl.BlockSpec((tm, LANE), lambda b, j: (j, 0)),
                  pl.BlockSpec((tm, LANE), lambda b, j: (j, 0))],
        out_specs=pl.BlockSpec((1, tm, P_W), lambda b, j: (b, j, 0)),
        compiler_params=_cparams(("parallel", "arbitrary")),
        name="inproj_even",
    )(h, mod, w, cos_t, sin_t)


def _attn_kernel(*refs, tq, tiles, has_window):
    if has_window:
        sink_ref, q_ref, kw_ref, vw_ref, kc_ref, vc_ref, o_ref = refs
    else:
        sink_ref, q_ref, kc_ref, vc_ref, o_ref = refs
    group = A_Q_HEADS // A_KV_HEADS
    rows = group * tq
    lo = lax.broadcasted_iota(jnp.int32, (tq, LANE), 1) < HEAD_DIM
    den_lanes = lax.broadcasted_iota(jnp.int32, (rows, LANE), 1) >= HEAD_DIM

    def with_ones(v):
        return jnp.where(lax.broadcasted_iota(jnp.int32, v.shape, 1) < HEAD_DIM, v, jnp.ones_like(v))

    wstart, band = [], []
    if has_window:
        S = kw_ref.shape[1]
        wk = tq + 2 * WINDOW
        for t in range(tiles):
            q0 = (pl.program_id(1) * tiles + t) * tq
            wstart.append(pl.multiple_of(jnp.clip(q0 - WINDOW, 0, S - wk), LANE))
            qpos = q0 + lax.broadcasted_iota(jnp.int32, (tq, wk), 0)
            kpos = wstart[t] + lax.broadcasted_iota(jnp.int32, (tq, wk), 1)
            band.append(jnp.tile(jnp.where(jnp.abs(qpos - kpos) <= WINDOW, 0.0, NEG), (group, 1)))
    chains = [(t, g) for t in range(tiles) for g in range(A_KV_HEADS)]
    rt = [slice(t * tq, (t + 1) * tq) for t in range(tiles)]
    cols = [slice(g * LANE, (g + 1) * LANE) for g in range(A_KV_HEADS)]
    q4, snk, sc, m, outs = [], [], [], [], []
    for t, g in chains:
        qs = []
        for pr in range(group // 2):
            qblk = q_ref[0, rt[t], (2 * g + pr) * LANE:(2 * g + pr + 1) * LANE]
            zero = jnp.zeros_like(qblk)
            qs += [jnp.where(lo, qblk, zero), jnp.where(lo, zero, qblk)]
        q4.append(jnp.concatenate(qs, axis=0))
        snk.append(jnp.concatenate([jnp.full((tq, 1), sink_ref[group * g + i] * LOG2_E, F32) for i in range(group)],
                                   axis=0))
    for c, (t, g) in enumerate(chains):
        if has_window:
            keys = jnp.concatenate([kw_ref[0, pl.ds(wstart[t], wk), cols[g]], kc_ref[0, :, cols[g]]], axis=0)
            s = _dot_nt(q4[c], keys)
            sc.append(jnp.concatenate([s[:, 0:wk] + band[t], s[:, wk:]], axis=1))
        else:
            sc.append(_dot_nt(q4[c], kc_ref[0, :, cols[g]]))
    for c in range(len(chains)):
        m.append(jnp.maximum(jnp.max(sc[c], axis=-1, keepdims=True), snk[c]))
    for c, (t, g) in enumerate(chains):
        if has_window:
            vals = jnp.concatenate([vw_ref[0, pl.ds(wstart[t], wk), cols[g]], vc_ref[0, :, cols[g]]], axis=0)
        else:
            vals = vc_ref[0, :, cols[g]]
        o = _dot(jnp.exp2((sc[c] - m[c]).astype(BF16)), with_ones(vals))
        outs.append(o + jnp.where(den_lanes, jnp.exp2(snk[c] - m[c]), 0.0))
    for c, (t, g) in enumerate(chains):
        o = outs[c]
        swapped = pltpu.roll(o, HEAD_DIM, 1)
        for pr in range(group // 2):
            ev = slice(2 * pr * tq, (2 * pr + 1) * tq)
            od = slice((2 * pr + 1) * tq, (2 * pr + 2) * tq)
            res = jnp.where(lo, o[ev] / swapped[ev], swapped[od] / o[od])
            o_ref[0, rt[t], (2 * g + pr) * LANE:(2 * g + pr + 1) * LANE] = res.astype(BF16)


def attention(p_q, p_ctx, sink, has_window):
    B, T, _ = p_q.shape
    L = p_ctx.shape[1]
    tq = 128
    tiles = next(n for n in (4, 2, 1) if T % (n * tq) == 0)
    in_specs = [pl.BlockSpec(memory_space=pltpu.SMEM),
                pl.BlockSpec((1, tiles * tq, W_AQ), lambda b, j: (b, j, C_QA // W_AQ))]
    args = [sink, p_q]
    if has_window:
        in_specs += [pl.BlockSpec((1, T, W_KV2), lambda b, j: (b, 0, C_KD // W_KV2)),
                     pl.BlockSpec((1, T, W_KV2), lambda b, j: (b, 0, C_VD // W_KV2))]
        args += [p_q, p_q]
    in_specs += [pl.BlockSpec((1, L, W_KV2), lambda b, j: (b, 0, C_KD // W_KV2)),
                 pl.BlockSpec((1, L, W_KV2), lambda b, j: (b, 0, C_VD // W_KV2))]
    args += [p_ctx, p_ctx]
    return pl.pallas_call(
        functools.partial(_attn_kernel, tq=tq, tiles=tiles, has_window=has_window),
        out_shape=jax.ShapeDtypeStruct((B, T, W_AQ), BF16),
        grid=(B, T // (tiles * tq)),
        in_specs=in_specs,
        out_specs=pl.BlockSpec((1, tiles * tq, W_AQ), lambda b, j: (b, j, 0)),
        compiler_params=_cparams(("parallel", "arbitrary")),
        name="window_attention" if has_window else "context_attention",
    )(*args)


def _log_sigmoid(x):
    return jnp.minimum(x, 0.0) - jnp.log(1.0 + jnp.exp(-jnp.abs(x)))


def _gla_kernel(qf_ref, kf_ref, vf_ref, gf_ref, qb_ref, kb_ref, vb_ref, gb_ref, wa_ref, ba_ref, s0_ref,
                of_ref, ob_ref, sfin_ref, s_sc):
    j = pl.program_id(1)
    nblk = pl.num_programs(1)
    tb = qf_ref.shape[1]
    nc = tb // GLA_CHUNK

    @pl.when(j == 0)
    def _():
        s_sc[...] = s0_ref[0]

    C = GLA_CHUNK
    hc = GLA_HEADS * C
    ri = lax.broadcasted_iota(jnp.int32, (hc, hc), 0) % C
    ci = lax.broadcasted_iota(jnp.int32, (hc, hc), 1) % C
    rb = lax.broadcasted_iota(jnp.int32, (tb, tb), 0)
    cb = lax.broadcasted_iota(jnp.int32, (tb, tb), 1)
    same_chunk = (rb // C) == (cb // C)
    lane_head = lax.broadcasted_iota(jnp.int32, (C, GLA_HEADS * GLA_DK), 1) // GLA_DK

    def per_head(x):
        zero = jnp.zeros_like(x)
        return jnp.concatenate([jnp.where(lane_head == h, x, zero) for h in range(GLA_HEADS)], axis=0)

    io = ((qf_ref, kf_ref, vf_ref, gf_ref, of_ref), (qb_ref, kb_ref, vb_ref, gb_ref, ob_ref))
    causal = ((ri >= ci), (ci >= ri))
    tri = (jnp.logical_and(same_chunk, rb >= cb).astype(BF16), jnp.logical_and(same_chunk, cb >= rb).astype(BF16))
    b_all = []
    for d in range(2):
        g = _dot(io[d][3][0, :, d * LANE:(d + 1) * LANE], wa_ref[d]) + ba_ref[d]
        log_a = _log_sigmoid(g) / GLA_TAU
        la1 = log_a.astype(BF16)
        rem = log_a - la1.astype(F32)
        la2 = rem.astype(BF16)
        la3 = (rem - la2.astype(F32)).astype(BF16)
        b_all.append(_dot(tri[d], la1) + _dot(tri[d], la2) + _dot(tri[d], la3))
    state = [jnp.concatenate([s_sc[d, 0], s_sc[d, 1]], axis=1) for d in range(2)]
    zero_blk = jnp.zeros((GLA_DV, LANE), BF16)
    for step in range(nc):
        for d in range(2):
            q_ref, k_ref, v_ref, _, o_ref = io[d]
            c = step if d == 0 else nc - 1 - step
            rows = slice(c * C, (c + 1) * C)
            b = b_all[d][rows]
            b_last = b[C - 1:C] if d == 0 else b[0:1]
            qc = q_ref[0, rows, :].astype(F32)
            kc = k_ref[0, rows, :].astype(F32)
            q4 = per_head((qc * jnp.exp(b)).astype(BF16))
            k4 = per_head((kc * jnp.exp(-b)).astype(BF16))
            ks4 = per_head((kc * jnp.exp(b_last - b)).astype(BF16))
            st = state[d]
            stb = st.astype(BF16)
            st_bd = jnp.concatenate([jnp.concatenate([stb[:, 0:LANE], zero_blk], axis=1),
                                     jnp.concatenate([zero_blk, stb[:, LANE:]], axis=1)], axis=0)
            res = _dot_nt(q4, jnp.concatenate([st_bd, k4], axis=0))
            attn = jnp.where(causal[d], res[:, 2 * GLA_DV:], 0.0).astype(BF16)
            v4 = jnp.concatenate([v_ref[0, rows, h * GLA_DV:(h + 1) * GLA_DV] for h in range(GLA_HEADS)], axis=0)
            o4 = _dot(attn, v4)
            for h in range(GLA_HEADS):
                hr = slice(h * C, (h + 1) * C)
                inter = res[hr, (h // 2) * GLA_DV:(h // 2 + 1) * GLA_DV]
                o_ref[0, rows, h * GLA_DV:(h + 1) * GLA_DV] = (o4[hr] + inter).astype(o_ref.dtype)
            state[d] = st * jnp.exp(b_last) + _dot_tn(v4, ks4)
    for d in range(2):
        for pair in range(2):
            s_sc[d, pair] = state[d][:, pair * LANE:(pair + 1) * LANE]

    @pl.when(j == nblk - 1)
    def _():
        sfin_ref[0] = s_sc[...]


def gla_scan(p, wa_p, ba, s0):
    B, T, _ = p.shape
    tb = min(ROW_TILE, T)
    nblk = T // tb
    fwd = lambda b, j: (b, j)
    bwd = lambda b, j: (b, nblk - 1 - j)

    def specs(im):
        return [pl.BlockSpec((1, tb, W_GK), lambda b, j: im(b, j) + (C_QG // W_GK,)),
                pl.BlockSpec((1, tb, W_GK), lambda b, j: im(b, j) + (C_KG // W_GK,)),
                pl.BlockSpec((1, tb, W_GV), lambda b, j: im(b, j) + (C_VG // W_GV,)),
                pl.BlockSpec((1, tb, 2 * LANE), lambda b, j: im(b, j) + (C_GG // (2 * LANE),))]

    return pl.pallas_call(
        _gla_kernel,
        out_shape=(jax.ShapeDtypeStruct((B, T, W_GV), BF16), jax.ShapeDtypeStruct((B, T, W_GV), BF16),
                   jax.ShapeDtypeStruct(s0.shape, F32)),
        grid=(B, nblk),
        in_specs=specs(fwd) + specs(bwd) + [
            pl.BlockSpec((2, LANE, W_GK), lambda b, j: (0, 0, 0)),
            pl.BlockSpec((2, 1, W_GK), lambda b, j: (0, 0, 0)),
            pl.BlockSpec((1, 2, 2, GLA_DV, LANE), lambda b, j: (b, 0, 0, 0, 0))],
        out_specs=(pl.BlockSpec((1, tb, W_GV), lambda b, j: (b, j, 0)),
                   pl.BlockSpec((1, tb, W_GV), lambda b, j: (b, nblk - 1 - j, 0)),
                   pl.BlockSpec((1, 2, 2, GLA_DV, LANE), lambda b, j: (b, 0, 0, 0, 0))),
        scratch_shapes=[pltpu.VMEM((2, 2, GLA_DV, LANE), F32)],
        compiler_params=_cparams(("parallel", "arbitrary")),
        name="gla_scan",
    )(p, p, p, p, p, p, p, p, wa_p, ba, s0)


def _deepnorm(h, gate, y, w, b):
    r = h + (gate * (1.0 / ALPHA)) * y
    mu = jnp.mean(r, axis=-1, keepdims=True)
    xc = r - mu
    var = jnp.mean(xc * xc, axis=-1, keepdims=True)
    return xc * lax.rsqrt(var + LN_EPS / (ALPHA * ALPHA)) * w + b


def _pack_rounded_pairs(xr):
    half = xr.shape[1] // 2
    lo = lax.bitcast_convert_type(xr[:, :half], jnp.uint32)
    hi = lax.bitcast_convert_type(xr[:, half:], jnp.uint32)
    return jnp.bitwise_or(hi, lax.shift_right_logical(lo, jnp.uint32(16)))


def _pack_bf16_pairs(x):
    return _pack_rounded_pairs(x.astype(BF16).astype(F32))


def _unpack_bf16_pairs(p):
    lo = lax.bitcast_convert_type(lax.shift_left(p, jnp.uint32(16)), F32)
    hi = lax.bitcast_convert_type(jnp.bitwise_and(p, jnp.uint32(0xFFFF0000)), F32)
    return lo, hi


def _post_norm_and_route(h, y, mod_ref, lnw_ref, lnb_ref, rwh_ref, rwl_ref, rb_ref, h1_ref, tok_ref, lg_ref):
    h1 = _deepnorm(h, mod_ref[0, 2:3, :], y, lnw_ref[...], lnb_ref[...])
    h1_ref[0] = h1
    tok = h1 * (1.0 + mod_ref[0, 4:5, :]) + mod_ref[0, 3:4, :]
    hi = tok.astype(BF16)
    hi_f = hi.astype(F32)
    tok_ref[0] = _pack_rounded_pairs(hi_f)
    lo = (tok - hi_f).astype(BF16)
    lg = _dot(hi, rwh_ref[...]) + _dot(lo, rwh_ref[...]) + _dot(hi, rwl_ref[...]) + rb_ref[...]
    lg_ref[...] = lg.T[0:ROUTE_ROWS, :]


def _epilogue_specs(tm, mb, b0):
    mod_map = (lambda b, j: (b + b0, 0, 0)) if mb > 1 else (lambda b, j: (0, 0, 0))
    const2 = lambda b, j: (0, 0)
    return [pl.BlockSpec((1, tm, D), lambda b, j: (b + b0, j, 0)),
            pl.BlockSpec((1, 6, D), mod_map),
            pl.BlockSpec((1, D), const2), pl.BlockSpec((1, D), const2),
            pl.BlockSpec((D, ROUTE_W), const2), pl.BlockSpec((D, ROUTE_W), const2),
            pl.BlockSpec((1, ROUTE_W), const2)]


def _epilogue_outs(B, T, tm):
    nj = T // tm
    shapes = (jax.ShapeDtypeStruct((B, T, D), F32), jax.ShapeDtypeStruct((B, T, D // 2), jnp.uint32),
              jax.ShapeDtypeStruct((ROUTE_ROWS, B * T), F32))
    specs = (pl.BlockSpec((1, tm, D), lambda b, j: (b, j, 0)), pl.BlockSpec((1, tm, D // 2), lambda b, j: (b, j, 0)),
             pl.BlockSpec((ROUTE_ROWS, tm), lambda b, j: (0, b * nj + j)))
    return shapes, specs


def _outproj_even_kernel(a_ref, of_ref, ob_ref, rg_ref, nw_ref, wo_ref,
                         h_ref, mod_ref, lnw_ref, lnb_ref, rwh_ref, rwl_ref, rb_ref,
                         h1_ref, tok_ref, lg_ref):
    parts = [a_ref[0]]
    for hd in range(GLA_HEADS):
        cols = slice(hd * GLA_DV, (hd + 1) * GLA_DV)
        o = of_ref[0, :, cols].astype(F32) + ob_ref[0, :, cols].astype(F32)
        o = o * lax.rsqrt(jnp.mean(o * o, axis=-1, keepdims=True) + RMS_EPS)
        parts.append((o * nw_ref[:, cols] * _silu(rg_ref[0, :, cols].astype(F32))).astype(BF16))
    y = _dot(jnp.concatenate(parts, axis=1), wo_ref[...])
    _post_norm_and_route(h_ref[0], y, mod_ref, lnw_ref, lnb_ref, rwh_ref, rwl_ref, rb_ref, h1_ref, tok_ref, lg_ref)


def outproj_even(a, o_f, o_b, p, norm_w, w_out, h, mod, lnw, lnb, rwh, rwl, rb, b0, nb):
    T = h.shape[1]
    tm = min(IN_ROW_TILE, T)
    tile = lambda b, j: (b + b0, j, 0)
    shapes, ospecs = _epilogue_outs(nb, T, tm)
    return pl.pallas_call(
        _outproj_even_kernel,
        out_shape=shapes,
        grid=(nb, T // tm),
        in_specs=[pl.BlockSpec((1, tm, W_AQ), tile), pl.BlockSpec((1, tm, W_GV), tile), pl.BlockSpec((1, tm, W_GV), tile),
                  pl.BlockSpec((1, tm, W_GV), lambda b, j: (b + b0, j, C_RG // W_GV)),
                  pl.BlockSpec((1, W_GV), lambda b, j: (0, 0)),
                  pl.BlockSpec((D, D), lambda b, j: (0, 0))] + _epilogue_specs(tm, mod.shape[0], b0),
        out_specs=ospecs,
        compiler_params=_cparams(("parallel", "arbitrary")),
        name="outproj_even",
    )(a, o_f, o_b, p, norm_w, w_out, h, mod, lnw, lnb, rwh, rwl, rb)


def _inproj_odd_kernel(h_ref, mod_ref, w_ref, o_ref):
    u = (h_ref[0] * (1.0 + mod_ref[0, 1:2, :]) + mod_ref[0, 0:1, :]).astype(BF16)
    o_ref[0, :, 0:D] = _dot(u, w_ref[:, 0:D]).astype(BF16)
    o_ref[0, :, D:2 * D] = (_dot(u, w_ref[:, D:2 * D]) * _dot(u, w_ref[:, 2 * D:3 * D])).astype(BF16)


def inproj_odd(h, mod, w):
    B, T, _ = h.shape
    tm = min(IN_ROW_TILE, T)
    mb = mod.shape[0]
    return pl.pallas_call(
        _inproj_odd_kernel,
        out_shape=jax.ShapeDtypeStruct((B, T, 2 * D), BF16),
        grid=(B, T // tm),
        in_specs=[pl.BlockSpec((1, tm, D), lambda b, j: (b, j, 0)),
                  pl.BlockSpec((1, 6, D), (lambda b, j: (b, 0, 0)) if mb > 1 else (lambda b, j: (0, 0, 0))),
                  pl.BlockSpec((D, 3 * D), lambda b, j: (0, 0))],
        out_specs=pl.BlockSpec((1, tm, 2 * D), lambda b, j: (b, j, 0)),
        compiler_params=_cparams(("parallel", "arbitrary")),
        name="inproj_odd",
    )(h, mod, w)


HALO = 16


def _outproj_odd_kernel(gb_ref, z_ref, zp_ref, zn_ref, cw_ref, cb_ref, wo_ref,
                        h_ref, mod_ref, lnw_ref, lnb_ref, rwh_ref, rwl_ref, rb_ref,
                        h1_ref, tok_ref, lg_ref):
    j = pl.program_id(1)
    tm = z_ref.shape[1]
    z = z_ref[0].astype(F32)
    prev_row = jnp.where(j > 0, zp_ref[0, HALO - 1:HALO, :].astype(F32), 0.0)
    next_row = jnp.where(j < pl.num_programs(1) - 1, zn_ref[0, 0:1, :].astype(F32), 0.0)
    row = lax.broadcasted_iota(jnp.int32, (tm, D), 0)
    z_prev = jnp.where(row == 0, prev_row, pltpu.roll(z, 1, 0))
    z_next = jnp.where(row == tm - 1, next_row, pltpu.roll(z, tm - 1, 0))
    conv = z_prev * cw_ref[0:1, :] + z * cw_ref[1:2, :] + z_next * cw_ref[2:3, :] + cb_ref[...]
    y = _dot((gb_ref[0].astype(F32) * conv).astype(BF16), wo_ref[...])
    _post_norm_and_route(h_ref[0], y, mod_ref, lnw_ref, lnb_ref, rwh_ref, rwl_ref, rb_ref, h1_ref, tok_ref, lg_ref)


def outproj_odd(gz, conv_w, conv_b, w_out, h, mod, lnw, lnb, rwh, rwl, rb, b0, nb):
    T = h.shape[1]
    tm = min(IN_ROW_TILE, T)
    r = tm // HALO
    nh = T // HALO
    shapes, ospecs = _epilogue_outs(nb, T, tm)
    return pl.pallas_call(
        _outproj_odd_kernel,
        out_shape=shapes,
        grid=(nb, T // tm),
        in_specs=[pl.BlockSpec((1, tm, D), lambda b, j: (b + b0, j, 0)),
                  pl.BlockSpec((1, tm, D), lambda b, j: (b + b0, j, 1)),
                  pl.BlockSpec((1, HALO, D), lambda b, j: (b + b0, jnp.maximum(j * r - 1, 0), 1)),
                  pl.BlockSpec((1, HALO, D), lambda b, j: (b + b0, jnp.minimum((j + 1) * r, nh - 1), 1)),
                  pl.BlockSpec((3, D), lambda b, j: (0, 0)),
                  pl.BlockSpec((1, D), lambda b, j: (0, 0)),
                  pl.BlockSpec((D, D), lambda b, j: (0, 0))] + _epilogue_specs(tm, mod.shape[0], b0),
        out_specs=ospecs,
        compiler_params=_cparams(("parallel", "arbitrary")),
        name="outproj_odd",
    )(gz, gz, gz, gz, conv_w, conv_b, w_out, h, mod, lnw, lnb, rwh, rwl, rb)


def _sc_mesh():
    return plsc.VectorSubcoreMesh(core_axis_name="c", subcore_axis_name="s")


def sc_gather_rows(table, idx):
    n = idx.shape[0]
    width = table.shape[1]
    per_w = n // SC_WORKERS
    n_chunks = per_w // SC_CHUNK
    assert n_chunks % 2 == 0

    @functools.partial(
        pl.kernel, mesh=_sc_mesh(),
        out_type=jax.ShapeDtypeStruct((n, width), table.dtype),
        scratch_types=[pltpu.VMEM((n_chunks, SC_CHUNK), jnp.int32),
                       pltpu.VMEM((SC_CHUNK, width), table.dtype), pltpu.VMEM((SC_CHUNK, width), table.dtype),
                       pltpu.SemaphoreType.DMA, pltpu.SemaphoreType.DMA],
    )
    def gather_kernel(table_hbm, idx_hbm, out_hbm, idx_v, buf0, buf1, sem0, sem1):
        wid = lax.axis_index("s") * SC_CORES + lax.axis_index("c")
        pltpu.sync_copy(idx_hbm.at[wid], idx_v)

        def fetch(j, buf, sem):
            return pltpu.make_async_copy(table_hbm.at[idx_v.at[j]], buf, sem)

        def flush(j, buf):
            pltpu.sync_copy(buf, out_hbm.at[pl.ds(wid * per_w + j * SC_CHUNK, SC_CHUNK)])

        fetch(0, buf0, sem0).start()

        @pl.loop(0, n_chunks, step=2)
        def _(j):
            fetch(j + 1, buf1, sem1).start()
            fetch(j, buf0, sem0).wait()
            flush(j, buf0)

            @pl.when(j + 2 < n_chunks)
            def _():
                fetch(j + 2, buf0, sem0).start()

            fetch(j + 1, buf1, sem1).wait()
            flush(j + 1, buf1)

    return gather_kernel(table, idx.reshape(SC_WORKERS, n_chunks, SC_CHUNK))


def sc_scatter_rows(srcs, idxs, n_rows):
    width, dt = srcs[0][0].shape[1], srcs[0][0].dtype
    plans, args = [], []
    for (src, row0, n_src), idx in zip(srcs, idxs):
        per_w = n_src // SC_WORKERS
        chunk = min(SC_CHUNK, per_w // 2)
        assert (per_w // chunk) % 2 == 0
        plans.append((per_w, chunk, per_w // chunk, idx.shape[0], row0))
        args += [src, idx.reshape(idx.shape[0], SC_WORKERS, per_w // chunk, chunk)]
    max_chunk = max(p[1] for p in plans)
    scratch = [pltpu.VMEM((max_chunk, width), dt), pltpu.VMEM((max_chunk, width), dt),
               pltpu.SemaphoreType.DMA, pltpu.SemaphoreType.DMA]
    scratch += [pltpu.VMEM((lists, n_chunks, chunk), jnp.int32) for _, chunk, n_chunks, lists, _ in plans]

    @functools.partial(pl.kernel, mesh=_sc_mesh(), out_type=jax.ShapeDtypeStruct((n_rows, width), dt),
                       scratch_types=scratch)
    def scatter_kernel(*refs):
        ins, out_hbm = refs[:2 * len(plans)], refs[2 * len(plans)]
        rows0, rows1, sem0, sem1 = refs[2 * len(plans) + 1:2 * len(plans) + 5]
        idx_vs = refs[2 * len(plans) + 5:]
        wid = lax.axis_index("s") * SC_CORES + lax.axis_index("c")
        for s, (per_w, chunk, n_chunks, lists, row0) in enumerate(plans):
            src_hbm, idx_hbm, idx_v = ins[2 * s], ins[2 * s + 1], idx_vs[s]
            for k in range(lists):
                pltpu.sync_copy(idx_hbm.at[k, wid], idx_v.at[k])
            buf0 = rows0 if chunk == max_chunk else rows0.at[pl.ds(0, chunk)]
            buf1 = rows1 if chunk == max_chunk else rows1.at[pl.ds(0, chunk)]

            def load(j, buf, sem, src_hbm=src_hbm, per_w=per_w, chunk=chunk, row0=row0):
                return pltpu.make_async_copy(src_hbm.at[pl.ds(row0 + wid * per_w + j * chunk, chunk)], buf, sem)

            def spread(j, buf, idx_v=idx_v, lists=lists):
                for k in range(lists):
                    pltpu.sync_copy(buf, out_hbm.at[idx_v.at[k, j]])

            load(0, buf0, sem0).start()

            @pl.loop(0, n_chunks, step=2)
            def _(j, load=load, spread=spread, buf0=buf0, buf1=buf1, n_chunks=n_chunks):
                load(j + 1, buf1, sem1).start()
                load(j, buf0, sem0).wait()
                spread(j, buf0)

                @pl.when(j + 2 < n_chunks)
                def _():
                    load(j + 2, buf0, sem0).start()

                load(j + 1, buf1, sem1).wait()
                spread(j + 1, buf1)

    return scatter_kernel(*args)


def _ffn_kernel(be_ref, nv_ref, x_ref, w1_ref, w3_ref, w2_ref, y_ref, w1b, w3b, w2b):
    i = pl.program_id(0)
    changed = jnp.logical_or(i == 0, be_ref[i] != be_ref[jnp.maximum(i - 1, 0)])

    @pl.when(changed)
    def _():
        w1b[...] = w1_ref[0, 0].astype(BF16)
        w3b[...] = w3_ref[0, 0].astype(BF16)
        w2b[...] = w2_ref[0, 0].astype(BF16)

    @pl.when(i < nv_ref[0])
    def _():
        x = jnp.concatenate([v.astype(BF16) for v in _unpack_bf16_pairs(x_ref[...])], axis=1)
        y_ref[...] = _pack_bf16_pairs(_dot((_silu(_dot(x, w1b[...])) * _dot(x, w3b[...])).astype(BF16), w2b[...]))

    @pl.when(i >= nv_ref[0])
    def _():
        y_ref[...] = jnp.zeros_like(y_ref)


def moe_ffn(xs, blk_expert, n_valid, w1, w3, w2, layer):
    n_rows = xs.shape[0]
    nb = n_rows // MOE_TM
    return pl.pallas_call(
        _ffn_kernel,
        out_shape=jax.ShapeDtypeStruct((n_rows, D // 2), jnp.uint32),
        grid_spec=pltpu.PrefetchScalarGridSpec(
            num_scalar_prefetch=2,
            grid=(nb,),
            in_specs=[pl.BlockSpec((MOE_TM, D // 2), lambda i, be, nv: (jnp.minimum(i, nv[0] - 1), 0)),
                      pl.BlockSpec((1, 1, D, D_EXPERT), lambda i, be, nv: (layer, be[i], 0, 0)),
                      pl.BlockSpec((1, 1, D, D_EXPERT), lambda i, be, nv: (layer, be[i], 0, 0)),
                      pl.BlockSpec((1, 1, D_EXPERT, D), lambda i, be, nv: (layer, be[i], 0, 0))],
            out_specs=pl.BlockSpec((MOE_TM, D // 2), lambda i, be, nv: (i, 0)),
            scratch_shapes=[pltpu.VMEM((D, D_EXPERT), BF16), pltpu.VMEM((D, D_EXPERT), BF16),
                            pltpu.VMEM((D_EXPERT, D), BF16)]),
        compiler_params=_cparams(("arbitrary",)),
        name="moe_ffn",
    )(blk_expert, n_valid, xs, w1, w3, w2)


def _combine_kernel(h_ref, y0_ref, y1_ref, wt_ref, mod_ref, lnw_ref, lnb_ref, o_ref):
    half = D // 2
    lo0, hi0 = _unpack_bf16_pairs(y0_ref[0])
    lo1, hi1 = _unpack_bf16_pairs(y1_ref[0])
    pick = (lax.broadcasted_iota(jnp.int32, (WT_ROWS, LANE), 0) == lax.broadcasted_iota(jnp.int32, (WT_ROWS, LANE), 1))
    wcols = lax.dot_general(wt_ref[...], pick.astype(F32), (((0,), (0,)), ((), ())), precision=HIGHEST,
                            preferred_element_type=F32)
    w0, w1 = wcols[:, 0:1], wcols[:, 1:2]
    gate = mod_ref[0, 5:6, :] * (1.0 / ALPHA)
    r_lo = h_ref[:, 0:half] + gate[:, 0:half] * (w0 * lo0 + w1 * lo1)
    r_hi = h_ref[:, half:D] + gate[:, half:D] * (w0 * hi0 + w1 * hi1)
    mu = (jnp.sum(r_lo, axis=-1, keepdims=True) + jnp.sum(r_hi, axis=-1, keepdims=True)) * (1.0 / D)
    c_lo, c_hi = r_lo - mu, r_hi - mu
    var = (jnp.sum(c_lo * c_lo, axis=-1, keepdims=True) + jnp.sum(c_hi * c_hi, axis=-1, keepdims=True)) * (1.0 / D)
    inv = lax.rsqrt(var + LN_EPS / (ALPHA * ALPHA))
    o_ref[:, 0:half] = c_lo * inv * lnw_ref[:, 0:half] + lnb_ref[:, 0:half]
    o_ref[:, half:D] = c_hi * inv * lnw_ref[:, half:D] + lnb_ref[:, half:D]


def _combine_into_kernel(*refs):
    _combine_kernel(*refs[:7], refs[8])


def moe_combine(h1, y_rows, wts, mod, lnw, lnb, y_tok0, out_tok0, n_out, prev=None):
    nb, T, _ = h1.shape
    ntok, N = nb * T, n_out
    tm = min(IN_ROW_TILE, ntok)
    per_b = T // tm
    h_off, y_off = out_tok0 // tm, y_tok0 // tm
    mod_map = (lambda i: ((i + h_off) // per_b, 0, 0)) if mod.shape[0] > 1 else (lambda i: (0, 0, 0))
    in_specs = [pl.BlockSpec((tm, D), lambda i: (i, 0)),
                pl.BlockSpec((1, tm, D // 2), lambda i: (0, i + y_off, 0)),
                pl.BlockSpec((1, tm, D // 2), lambda i: (1, i + y_off, 0)),
                pl.BlockSpec((WT_ROWS, tm), lambda i: (0, i + y_off)),
                pl.BlockSpec((1, 6, D), mod_map),
                pl.BlockSpec((1, D), lambda i: (0, 0)), pl.BlockSpec((1, D), lambda i: (0, 0))]
    args = [h1.reshape(ntok, D), y_rows, y_rows, wts, mod, lnw, lnb]
    if prev is not None:
        in_specs.append(pl.BlockSpec(memory_space=pl.ANY))
        args.append(prev)
    return pl.pallas_call(
        _combine_kernel if prev is None else _combine_into_kernel,
        out_shape=jax.ShapeDtypeStruct((N, D), F32),
        grid=(ntok // tm,),
        in_specs=in_specs,
        out_specs=pl.BlockSpec((tm, D), lambda i: (i + h_off, 0)),
        input_output_aliases={} if prev is None else {7: 0},
        compiler_params=_cparams(("parallel",)),
        name="moe_combine",
    )(*args)


def _route_kernel(lg_ref, dest_ref, wt_ref, cnt_ref, tri_sc, start_sc, run_sc):
    ph, i = pl.program_id(0), pl.program_id(1)
    tr = lg_ref.shape[1]

    @pl.when(jnp.logical_and(ph == 0, i == 0))
    def _():
        r = lax.broadcasted_iota(jnp.int32, (LANE, LANE), 0)
        c = lax.broadcasted_iota(jnp.int32, (LANE, LANE), 1)
        tri_sc[...] = (r < c).astype(BF16)
        start_sc[...] = jnp.zeros_like(start_sc)
        run_sc[...] = jnp.zeros_like(run_sc)

    @pl.when(jnp.logical_and(ph == 1, i == 0))
    def _():
        cnt = run_sc[...].astype(jnp.int32)
        cnt_ref[...] = cnt
        padded = jnp.bitwise_and(cnt + (MOE_TM - 1), -MOE_TM)
        row = lax.broadcasted_iota(jnp.int32, padded.shape, 0)
        acc = padded
        for s in (1, 2, 4, 8, 16):
            acc = acc + jnp.where(row >= s, pltpu.roll(acc, s, 0), 0)
        start_sc[...] = (acc - padded).astype(F32)
        run_sc[...] = jnp.zeros_like(run_sc)

    lg = lg_ref[...]
    gl = lg[N_EXPERTS:N_EXPERTS + N_GROUPS]
    gmax = jnp.max(gl, axis=0, keepdims=True)
    sub4 = lax.broadcasted_iota(jnp.int32, gl.shape, 0)
    g_sel = jnp.min(jnp.where(gl == gmax, sub4, N_GROUPS), axis=0, keepdims=True)
    p_group = 1.0 / jnp.sum(jnp.exp(gl - gmax), axis=0, keepdims=True)
    el = lg[0:EXPERTS_PER_GROUP]
    for g in range(1, N_GROUPS):
        el = jnp.where(g_sel == g, lg[g * EXPERTS_PER_GROUP:(g + 1) * EXPERTS_PER_GROUP], el)
    sub8 = lax.broadcasted_iota(jnp.int32, el.shape, 0)
    e1 = jnp.max(el, axis=0, keepdims=True)
    i1 = jnp.min(jnp.where(el == e1, sub8, EXPERTS_PER_GROUP), axis=0, keepdims=True)
    rest = jnp.where(sub8 == i1, -jnp.inf, el)
    e2 = jnp.max(rest, axis=0, keepdims=True)
    i2 = jnp.min(jnp.where(rest == e2, sub8, EXPERTS_PER_GROUP), axis=0, keepdims=True)
    den = jnp.sum(jnp.exp(el - e1), axis=0, keepdims=True)
    p1 = 1.0 / den
    p2 = jnp.exp(e2 - e1) / den
    wt_ref[...] = jnp.zeros_like(wt_ref)
    wt_ref[0:1, :] = p_group * p1 / (p1 + p2)
    wt_ref[1:2, :] = p_group * p2 / (p1 + p2)

    sub32 = lax.broadcasted_iota(jnp.int32, (N_EXPERTS, tr), 0)
    oh = [(sub32 == g_sel * EXPERTS_PER_GROUP + ix).astype(F32) for ix in (i1, i2)]
    cnt = [jnp.sum(o, axis=1, keepdims=True) for o in oh]

    @pl.when(ph == 0)
    def _():
        dest_ref[...] = jnp.zeros_like(dest_ref)

    @pl.when(ph == 1)
    def _():
        before = start_sc[:, 0:1] + run_sc[:, 0:1]
        for k in range(TOP_K):
            subs = [oh[k][:, s * LANE:(s + 1) * LANE] for s in range(tr // LANE)]
            local = _dot(jnp.concatenate(subs, axis=0).astype(BF16), tri_sc[...])
            seen = before + (cnt[0] if k == 1 else 0.0)
            for s, sub in enumerate(subs):
                prior = local[s * N_EXPERTS:(s + 1) * N_EXPERTS] + seen
                dest_ref[k:k + 1, s * LANE:(s + 1) * LANE] = (
                    jnp.sum(sub * prior, axis=0, keepdims=True).astype(jnp.int32))
                seen = seen + jnp.sum(sub, axis=1, keepdims=True)

    run_sc[...] = run_sc[...] + (cnt[0] + cnt[1])


def moe_route(logits_t, col0, N):
    tr = next(t for t in (4096, 2048, 1024, 512, 256) if N % t == 0 and col0 % t == 0)
    t0 = col0 // tr
    return pl.pallas_call(
        _route_kernel,
        out_shape=(jax.ShapeDtypeStruct((TOP_K, N), jnp.int32), jax.ShapeDtypeStruct((WT_ROWS, N), F32),
                   jax.ShapeDtypeStruct((N_EXPERTS, LANE), jnp.int32)),
        grid=(2, N // tr),
        in_specs=[pl.BlockSpec((ROUTE_ROWS, tr), lambda p, i: (0, i + t0))],
        out_specs=(pl.BlockSpec((TOP_K, tr), lambda p, i: (0, i * p)), pl.BlockSpec((WT_ROWS, tr), lambda p, i: (0, i * p)),
                   pl.BlockSpec((N_EXPERTS, LANE), lambda p, i: (0, 0))),
        scratch_shapes=[pltpu.VMEM((LANE, LANE), BF16), pltpu.VMEM((N_EXPERTS, LANE), F32),
                        pltpu.VMEM((N_EXPERTS, LANE), F32)],
        compiler_params=_cparams(("arbitrary", "arbitrary")),
        name="moe_route",
    )(logits_t)


def _block_tables(counts, n_assign):
    padded = (counts + MOE_TM - 1) // MOE_TM * MOE_TM
    pad_end = jnp.cumsum(padded)
    pad_start = pad_end - padded
    nb = -(-n_assign // MOE_TM) + N_EXPERTS
    blk_start = jnp.arange(nb, dtype=jnp.int32) * MOE_TM
    blk_expert = jnp.minimum(jnp.sum((pad_end[None, :] <= blk_start[:, None]).astype(jnp.int32), axis=1), N_EXPERTS - 1)
    n_valid = (pad_end[-1] // MOE_TM).astype(jnp.int32).reshape(1)
    n_fill = nb * MOE_TM - n_assign
    gap = padded - counts
    gap_end = jnp.cumsum(gap)
    k = jnp.arange(n_fill, dtype=jnp.int32)[:, None]
    sel = jnp.logical_and(k >= (gap_end - gap)[None, :], k < gap_end[None, :])
    in_gap = jnp.sum(jnp.where(sel, (pad_start + counts - (gap_end - gap))[None, :] + k, 0), axis=1)
    fill = jnp.where(k[:, 0] < gap_end[-1], in_gap, pad_end[-1] + k[:, 0] - gap_end[-1])
    return blk_expert.astype(jnp.int32), n_valid, fill.astype(jnp.int32), nb * MOE_TM


def hier_moe_and_norm(streams, w1, w3, w2, layer, lnw, lnb):
    lat_parts, ctx = [s for s in streams if s[4]], next((s for s in streams if not s[4]), None)
    n_ctx = ctx[0].shape[0] * ctx[0].shape[1] if ctx else 0
    n_lat_all = sum(s[0].shape[0] * s[0].shape[1] for s in lat_parts)
    routed, sorted_in, tables = [], [], []
    for r, part in enumerate(lat_parts):
        n_lat = part[0].shape[0] * part[0].shape[1]
        with_ctx = ctx is not None and r == len(lat_parts) - 1
        n = n_lat + (n_ctx if with_ctx else 0)
        logits_t = jnp.concatenate([part[2], ctx[2]], 1) if with_ctx else part[2]
        dest, wts, counts = moe_route(logits_t, 0, n)
        blk_expert, n_valid, fill_rows, n_rows = _block_tables(counts[:, 0], TOP_K * n)
        srcs, idxs = [(part[1].reshape(n_lat, D // 2), 0, n_lat)], [dest[:, :n_lat]]
        if with_ctx:
            srcs.append((ctx[1].reshape(n_ctx, D // 2), 0, n_ctx))
            idxs.append(dest[:, n_lat:])
        srcs.append((jnp.zeros((fill_rows.shape[0], D // 2), jnp.uint32), 0, fill_rows.shape[0]))
        idxs.append(fill_rows.reshape(1, -1))
        routed.append((dest, wts))
        sorted_in.append(sc_scatter_rows(srcs, idxs, n_rows))
        tables.append((blk_expert, n_valid, n_lat, n, with_ctx))
    out_lat, out_ctx, tok0 = None, None, 0
    for part, (dest, wts), xs, (blk_expert, n_valid, n_lat, n, with_ctx) in zip(lat_parts, routed, sorted_in, tables):
        y = moe_ffn(xs, blk_expert, n_valid, w1, w3, w2, layer)
        y_rows = sc_gather_rows(y, dest.reshape(-1)).reshape(TOP_K, n, D // 2)
        out_lat = moe_combine(part[0], y_rows, wts, part[3], lnw, lnb, 0, tok0, n_lat_all, out_lat)
        if with_ctx:
            out_ctx = moe_combine(ctx[0], y_rows, wts, ctx[3], lnw, lnb, n_lat, 0, n_ctx)
        tok0 += n_lat
    return out_lat, out_ctx


def _even_projection_weights(w):
    def rope_layout(x, heads):
        return x.reshape(D, heads, 2, 2, HEAD_DIM // 4).transpose(0, 1, 3, 2, 4).reshape(D, heads * HEAD_DIM)

    def twice(x):
        return jnp.concatenate([x[:, :HEAD_DIM], x[:, :HEAD_DIM], x[:, HEAD_DIM:], x[:, HEAD_DIM:]], axis=1)

    def lane_pad(x):
        return jnp.pad(x, ((0, 0), (0, LANE - x.shape[1])))

    qa, ka, va, qg, kg, vg, rg, gg = jnp.split(w, [512, 640, 768, 1024, 1280, 1792, 2304], axis=1)
    cols = [rope_layout(qa, A_Q_HEADS) * (HEAD_DIM ** -0.5 * LOG2_E),
            twice(rope_layout(ka, A_KV_HEADS)), twice(va), vg, rg, qg * GLA_DK ** -0.5, kg,
            lane_pad(gg[:, :GLA_RANK]), lane_pad(gg[:, GLA_RANK:])]
    return jnp.concatenate(cols, axis=1).astype(BF16)


def _rope_tables(S):
    row = jnp.repeat(jnp.arange(S // GRID_W), GRID_W).astype(F32)
    col = jnp.tile(jnp.arange(GRID_W), S // GRID_W).astype(F32)
    axis_dim = HEAD_DIM // 2
    inv_freq = ROPE_BASE ** (-jnp.arange(0, axis_dim, 2, dtype=F32) / axis_dim)
    ang = jnp.concatenate([row[:, None] * inv_freq, col[:, None] * inv_freq], -1)
    cos, sin = jnp.cos(ang), jnp.sin(ang)
    cos_t = jnp.tile(cos, (1, 4))
    sin_t = jnp.tile(jnp.concatenate([-sin, sin], -1), (1, 2))
    return cos_t, sin_t


def _router_weights(wg, bg, we, be):
    w = jnp.zeros((D, ROUTE_W), F32).at[:, :N_EXPERTS].set(we).at[:, N_EXPERTS:N_EXPERTS + N_GROUPS].set(wg)
    b = jnp.zeros((1, ROUTE_W), F32).at[0, :N_EXPERTS].set(be).at[0, N_EXPERTS:N_EXPERTS + N_GROUPS].set(bg)
    hi = w.astype(BF16)
    return hi, (w - hi.astype(F32)).astype(BF16), b


def kernel(x, c, ctx, c_ctx, w_in_even, w_out_even, attn_sink, gla_wa2, gla_ba, gla_norm_w, w_in_odd, conv_w, conv_b, w_out_odd, ada_w, ada_b, ln_w, ln_b, router_wg, router_bg, router_we, router_be, moe_w1, moe_w3, moe_w2):
    B, S, _ = x.shape
    L = ctx.shape[1]
    cos_t, sin_t = _rope_tables(S)
    cos_c, sin_c = jnp.ones((L, LANE), F32), jnp.zeros((L, LANE), F32)

    n_cond = -(-(B + 1) // 8) * 8
    cc = jnp.zeros((n_cond, D), F32).at[:B].set(c).at[B].set(c_ctx)
    mods = ada_modulation_all(cc, ada_w, ada_b).reshape(DEPTH, n_cond, 6, D)

    h_lat, h_ctx = x, ctx
    for l in range(DEPTH):
        i = l // 2
        need_ctx = any(j % 2 == 0 for j in range(l + 1, DEPTH))
        m_lat = mods[l, :B]
        m_ctx = mods[l, B:B + 1]
        lnw0, lnb0 = ln_w[l, 0:1], ln_b[l, 0:1]
        lnw1, lnb1 = ln_w[l, 1:2], ln_b[l, 1:2]
        rwh, rwl, rb = _router_weights(router_wg[l], router_bg[l], router_we[l], router_be[l])
        streams = []
        if l % 2 == 0:
            w_in = _even_projection_weights(w_in_even[i])
            w_out = w_out_even[i].astype(BF16)
            wa_p = jnp.zeros((2, LANE, GLA_HEADS * GLA_DK), F32).at[:, :GLA_RANK].set(gla_wa2[i]).astype(BF16)
            ba = gla_ba[i].reshape(2, 1, -1)
            nw = gla_norm_w[i].reshape(1, -1)
            p_ctx = inproj_even(h_ctx, m_ctx, w_in, cos_c, sin_c)
            p_lat = inproj_even(h_lat, m_lat, w_in, cos_t, sin_t)
            a_lat = attention(p_lat, p_ctx, attn_sink[i], True)
            s0 = jnp.zeros((B, 2, 2, GLA_DV, LANE), F32)
            oc_f, oc_b, s_ctx = gla_scan(p_ctx, wa_p, ba, s0)
            ol_f, ol_b, _ = gla_scan(p_lat, wa_p, ba, s_ctx)
            for b0 in range(0, B, B // 2):
                streams.append(outproj_even(a_lat, ol_f, ol_b, p_lat, nw, w_out, h_lat, m_lat, lnw0, lnb0,
                                            rwh, rwl, rb, b0, B // 2) + (m_lat, True))
            if need_ctx:
                a_ctx = attention(p_ctx, p_ctx, attn_sink[i], False)
                streams.append(outproj_even(a_ctx, oc_f, oc_b, p_ctx, nw, w_out, h_ctx, m_ctx, lnw0, lnb0,
                                            rwh, rwl, rb, 0, B) + (m_ctx, False))
        else:
            w_in = w_in_odd[i].astype(BF16)
            w_out = w_out_odd[i].astype(BF16)
            cb = conv_b[i].reshape(1, D)
            gz = inproj_odd(h_lat, m_lat, w_in)
            for b0 in range(0, B, B // 2):
                streams.append(outproj_odd(gz, conv_w[i], cb, w_out, h_lat, m_lat, lnw0, lnb0, rwh, rwl, rb,
                                           b0, B // 2) + (m_lat, True))
            if need_ctx:
                gz_ctx = inproj_odd(h_ctx, m_ctx, w_in)
                streams.append(outproj_odd(gz_ctx, conv_w[i], cb, w_out, h_ctx, m_ctx, lnw0, lnb0, rwh, rwl, rb,
                                           0, B) + (m_ctx, False))
        out_lat, out_ctx = hier_moe_and_norm(streams, moe_w1, moe_w3, moe_w2, l, lnw1, lnb1)
        h_lat = out_lat.reshape(B, S, D)
        if need_ctx:
            h_ctx = out_ctx.reshape(B, L, D)
    return h_lat
```
